```python
import math, functools
import jax, jax.numpy as jnp
from jax import lax
import numpy as np

D_MODEL = 2048
BATCH = 4
SEQ = 2048
DEPTH = 1
DEC_BATCH = 32
DEC_SEQ = 4
PAST_LEN = 16384
PAGE_SIZE = 128

D_MIX = D_MODEL
D_CONV = D_MIX // 4
D_ATTN = D_MIX - D_CONV
HEAD_DIM = 128
N_HEADS = D_ATTN // HEAD_DIM
PATTERNS = ((128, 1), (512, 4), (2048, 16))
MAX_WINDOW = max(w for w, _ in PATTERNS)
Q_BLOCK = 128
CONV_WIDTH = 31
N_BUCKETS = 32
MAX_EXACT = 16
MAX_DISTANCE = MAX_WINDOW
EPS = 1e-6
NEG_INF = -1e30
D_IN = 4 * D_ATTN + 3 * D_CONV
SPLIT_POINTS = (D_ATTN, 2 * D_ATTN, 3 * D_ATTN, 4 * D_ATTN, 4 * D_ATTN + D_CONV, 4 * D_ATTN + 2 * D_CONV)

kernel_name = "hymba_dilated_attn_conformer_conv_step"


def rms_norm(x, g):
    xf = x.astype(jnp.float32)
    y = xf * lax.rsqrt(jnp.mean(xf * xf, axis=-1, keepdims=True) + EPS)
    return (y * g.astype(jnp.float32)).astype(x.dtype)


def layer_norm(x, g, b):
    xf = x.astype(jnp.float32)
    mu = jnp.mean(xf, axis=-1, keepdims=True)
    var = jnp.mean(jnp.square(xf - mu), axis=-1, keepdims=True)
    y = (xf - mu) * lax.rsqrt(var + EPS)
    return (y * g.astype(jnp.float32) + b.astype(jnp.float32)).astype(x.dtype)


def rel_bucket(dist):
    d = jnp.maximum(dist, 1).astype(jnp.float32)
    log_b = MAX_EXACT + (jnp.log(d / MAX_EXACT) / math.log(MAX_DISTANCE / MAX_EXACT)
                         * (N_BUCKETS - MAX_EXACT)).astype(jnp.int32)
    log_b = jnp.minimum(log_b, N_BUCKETS - 1)
    return jnp.where(dist < MAX_EXACT, dist, log_b)


def masked_softmax_lse(s, valid):
    s = jnp.where(valid, s, NEG_INF)
    m = jnp.max(s, axis=-1, keepdims=True)
    p = jnp.exp(s - m)
    den = jnp.sum(p, axis=-1, keepdims=True)
    return p / den, (m + jnp.log(den))[..., 0]


def dilated_prompt(q, k, v, rel_bias, window, dilation):
    B, S, H, Dh = q.shape
    K = window // dilation
    L = S // dilation
    nb = -(-L // Q_BLOCK)
    Lp = nb * Q_BLOCK
    scale = HEAD_DIM ** -0.5

    def phases(t):
        return t.reshape(B, L, dilation, H, Dh).transpose(0, 2, 1, 3, 4)

    qb = jnp.pad(phases(q), ((0, 0), (0, 0), (0, Lp - L), (0, 0), (0, 0)))
    qb = qb.reshape(B, dilation, nb, Q_BLOCK, H, Dh)
    pad_kv = ((0, 0), (0, 0), (K, Lp - L), (0, 0), (0, 0))
    kp = jnp.pad(phases(k), pad_kv)
    vp = jnp.pad(phases(v), pad_kv)
    idx = jnp.arange(nb)[:, None] * Q_BLOCK + jnp.arange(Q_BLOCK + K)[None, :]
    kb = kp[:, :, idx]
    vb = vp[:, :, idx]
    a = jnp.arange(Q_BLOCK)[:, None]
    c = jnp.arange(Q_BLOCK + K)[None, :]
    kdist = a + K - c
    key_sub = jnp.arange(nb)[:, None, None] * Q_BLOCK + c[None] - K
    valid = (kdist >= 0) & (kdist <= K) & (key_sub >= 0)
    bias = rel_bias[rel_bucket(jnp.clip(kdist, 0, K) * dilation)]
    s = jnp.einsum('bpnqhd,bpnkhd->bpnhqk', qb, kb, preferred_element_type=jnp.float32) * scale
    s = s + bias.astype(jnp.float32).transpose(2, 0, 1)[None, None, None]
    p, lse = masked_softmax_lse(s, valid[None, None, :, None])
    o = jnp.einsum('bpnhqk,bpnkhd->bpnqhd', p.astype(v.dtype), vb)
    o = o.reshape(B, dilation, Lp, H, Dh)[:, :, :L].transpose(0, 2, 1, 3, 4).reshape(B, S, H, Dh)
    lse = lse.transpose(0, 1, 2, 4, 3).reshape(B, dilation, Lp, H)[:, :, :L]
    lse = lse.transpose(0, 2, 1, 3).reshape(B, S, H)
    return o, lse


def dilated_sample(q, k_all, v_all, past_rows, rel_bias, window, dilation):
    T = q.shape[1]
    K = window // dilation
    scale = HEAD_DIM ** -0.5
    kk = jnp.arange(K + 1)
    rows = past_rows + jnp.arange(T)[:, None] - kk[None, :] * dilation
    valid = rows >= 0
    rows_c = jnp.maximum(rows, 0)
    kg = k_all[:, rows_c]
    vg = v_all[:, rows_c]
    bias = rel_bias[rel_bucket(kk * dilation)].astype(jnp.float32)
    s = jnp.einsum('bthd,btkhd->bthk', q, kg, preferred_element_type=jnp.float32) * scale
    s = s + bias.T[None, None]
    p, lse = masked_softmax_lse(s, valid[None, :, None, :])
    o = jnp.einsum('bthk,btkhd->bthd', p.astype(v_all.dtype), vg)
    return o, lse


def combine_patterns(outs, lses, dtype):
    w = jax.nn.softmax(jnp.stack(lses, axis=0), axis=0)
    o = jnp.sum(w[..., None] * jnp.stack(outs, axis=0).astype(jnp.float32), axis=0)
    return o.astype(dtype)


def attend_prompt(q, k, v, rel_bias):
    outs, lses = [], []
    for window, dilation in PATTERNS:
        o, lse = dilated_prompt(q, k, v, rel_bias, window, dilation)
        outs.append(o)
        lses.append(lse)
    return combine_patterns(outs, lses, q.dtype)


def attend_sample(q, k, v, kv_buf, rel_bias):
    past_rows = kv_buf.shape[1]
    k_all = jnp.concatenate([kv_buf[:, :, 0], k], axis=1)
    v_all = jnp.concatenate([kv_buf[:, :, 1], v], axis=1)
    outs, lses = [], []
    for window, dilation in PATTERNS:
        o, lse = dilated_sample(q, k_all, v_all, past_rows, rel_bias, window, dilation)
        outs.append(o)
        lses.append(lse)
    return combine_patterns(outs, lses, q.dtype)


def depthwise_causal_conv(u_pad, w, b):
    C = u_pad.shape[-1]
    y = lax.conv_general_dilated(u_pad, w[:, None, :], window_strides=(1,), padding='VALID',
                                 dimension_numbers=('NWC', 'WIO', 'NWC'), feature_group_count=C)
    return y + b


def mixer_sublayer(x, conv_prefix, attend, norm_pre, w_in, conv_dw_w, conv_dw_b, conv_ln_g,
                   conv_ln_b, conv_pw_w, conv_pw_b, w_out, norm_post):
    N, T, _ = x.shape
    h = rms_norm(x, norm_pre)
    z = jnp.einsum('btd,de->bte', h, w_in)
    q, k, v, g_att, c_a, c_b, g_conv = jnp.split(z, SPLIT_POINTS, axis=-1)
    q = q.reshape(N, T, N_HEADS, HEAD_DIM)
    k = k.reshape(N, T, N_HEADS, HEAD_DIM)
    v = v.reshape(N, T, N_HEADS, HEAD_DIM)
    att = attend(q, k, v).reshape(N, T, D_ATTN)
    u = c_a * jax.nn.sigmoid(c_b)
    u_pad = jnp.concatenate([conv_prefix, u], axis=1)
    c = depthwise_causal_conv(u_pad, conv_dw_w, conv_dw_b)
    c = jax.nn.silu(layer_norm(c, conv_ln_g, conv_ln_b))
    c = jnp.einsum('btc,ce->bte', c, conv_pw_w) + conv_pw_b
    mix = jnp.concatenate([att * jax.nn.silu(g_att), c * jax.nn.silu(g_conv)], axis=-1)
    y = jnp.einsum('bte,ed->btd', mix, w_out)
    new_conv = u_pad[:, -(CONV_WIDTH - 1):]
    new_kv = jnp.stack([k, v], axis=2)
    return x + rms_norm(y, norm_post), new_conv, new_kv


def setup_inputs(seed: int = 0) -> dict:
    key = jax.random.key(seed)
    ks = jax.random.split(key, 16)
    nrm = jax.random.normal
    f = jnp.float32
    win_past = min(MAX_WINDOW, PAST_LEN)
    return {
        "x_prompt": nrm(ks[0], (BATCH, SEQ, D_MODEL), f),
        "x_sample": nrm(ks[1], (DEC_BATCH, DEC_SEQ, D_MODEL), f),
        "cache_conv": 0.5 * nrm(ks[2], (DEPTH, DEC_BATCH, CONV_WIDTH - 1, D_CONV), f),
        "cache_kv": nrm(ks[3], (DEPTH, DEC_BATCH, win_past, 2, N_HEADS, HEAD_DIM), f),
        "rel_bias": 0.5 * nrm(ks[4], (N_BUCKETS, N_HEADS), f),
        "norm_pre": 1.0 + 0.05 * nrm(ks[5], (DEPTH, D_MODEL), f),
        "w_in": nrm(ks[6], (DEPTH, D_MODEL, D_IN), f) * D_MODEL ** -0.5,
        "conv_dw_w": nrm(ks[7], (DEPTH, CONV_WIDTH, D_CONV), f) * CONV_WIDTH ** -0.5,
        "conv_dw_b": 0.02 * nrm(ks[8], (DEPTH, D_CONV), f),
        "conv_ln_g": 1.0 + 0.05 * nrm(ks[9], (DEPTH, D_CONV), f),
        "conv_ln_b": 0.02 * nrm(ks[10], (DEPTH, D_CONV), f),
        "conv_pw_w": nrm(ks[11], (DEPTH, D_CONV, D_CONV), f) * D_CONV ** -0.5,
        "conv_pw_b": 0.02 * nrm(ks[12], (DEPTH, D_CONV), f),
        "w_out": nrm(ks[13], (DEPTH, D_MIX, D_MODEL), f) * D_MIX ** -0.5,
        "norm_post": 1.0 + 0.05 * nrm(ks[14], (DEPTH, D_MODEL), f),
    }


def reference(x_prompt, x_sample, cache_conv, cache_kv, rel_bias, norm_pre, w_in, conv_dw_w,
              conv_dw_b, conv_ln_g, conv_ln_b, conv_pw_w, conv_pw_b, w_out, norm_post):
    xp, xs = x_prompt, x_sample
    win_prompt = min(MAX_WINDOW, xp.shape[1])
    conv_p, kv_p, conv_s, kv_s = [], [], [], []
    for l in range(DEPTH):
        lw = (norm_pre[l], w_in[l], conv_dw_w[l], conv_dw_b[l], conv_ln_g[l], conv_ln_b[l],
              conv_pw_w[l], conv_pw_b[l], w_out[l], norm_post[l])
        zero_prefix = jnp.zeros((xp.shape[0], CONV_WIDTH - 1, D_CONV), xp.dtype)
        xp, cp, kvp = mixer_sublayer(xp, zero_prefix,
                                     functools.partial(attend_prompt, rel_bias=rel_bias), *lw)
        conv_p.append(cp)
        kv_p.append(kvp[:, xp.shape[1] - win_prompt:])
        buf = cache_kv[l]
        xs, cs, kvs = mixer_sublayer(xs, cache_conv[l],
                                     functools.partial(attend_sample, kv_buf=buf, rel_bias=rel_bias), *lw)
        conv_s.append(cs)
        kv_s.append(jnp.concatenate([buf, kvs], axis=1)[:, kvs.shape[1]:])
    return (xp, xs, jnp.stack(conv_p), jnp.stack(kv_p), jnp.stack(conv_s), jnp.stack(kv_s))
```

```python
import functools
import math

import jax
import jax.numpy as jnp
import numpy as np
from jax import lax
from jax.experimental import pallas as pl
from jax.experimental.pallas import tpu as pltpu

F32 = jnp.float32
BF16 = jnp.bfloat16

HEAD_DIM = 128
PATTERNS = ((128, 1), (512, 4), (2048, 16))
MAX_WINDOW = 2048
Q_BLOCK = 128
KEYS_PER_PATTERN = 128
CONV_WIDTH = 31
N_BUCKETS = 32
MAX_EXACT = 16
EPS = 1e-6
NEG_INF = -1e30
SUBLANES = 8
HEAD_PAD = 16

MIB = 1024 * 1024


def _rel_bucket(dist):
    d = jnp.maximum(dist, 1).astype(F32)
    log_b = MAX_EXACT + (jnp.log(d / MAX_EXACT) / math.log(MAX_WINDOW / MAX_EXACT)
                         * (N_BUCKETS - MAX_EXACT)).astype(jnp.int32)
    log_b = jnp.minimum(log_b, N_BUCKETS - 1)
    return jnp.where(dist < MAX_EXACT, dist, log_b)


def _round_up(x, m):
    return -(-x // m) * m


def _silu(x):
    return x * jax.nn.sigmoid(x)


def _inproj_kernel(x_ref, g_ref, w_ref, z_ref, h_ref):
    @pl.when(pl.program_id(1) == 0)
    def _():
        x = x_ref[...]
        ms = jnp.mean(x * x, axis=-1, keepdims=True)
        h_ref[...] = (x * lax.rsqrt(ms + EPS) * g_ref[...]).astype(BF16)

    z_ref[...] = jnp.dot(h_ref[...], w_ref[...], preferred_element_type=F32)


def _inproj(x2d, norm_g, w_bf, tm, tn):
    m, d = x2d.shape
    n = w_bf.shape[1]
    return pl.pallas_call(
        _inproj_kernel,
        grid=(m // tm, n // tn),
        in_specs=[pl.BlockSpec((tm, d), lambda i, j: (i, 0)),
                  pl.BlockSpec((1, d), lambda i, j: (0, 0)),
                  pl.BlockSpec((d, tn), lambda i, j: (0, j))],
        out_specs=pl.BlockSpec((tm, tn), lambda i, j: (i, j)),
        out_shape=jax.ShapeDtypeStruct((m, n), F32),
        scratch_shapes=[pltpu.VMEM((tm, d), BF16)],
        compiler_params=pltpu.CompilerParams(
            dimension_semantics=("parallel", "arbitrary"), vmem_limit_bytes=48 * MIB),
        name="inproj",
    )(x2d, norm_g.reshape(1, d), w_bf)


def _attn_block(qb, kw, vw, tab, scale):
    s = lax.dot_general(qb.astype(BF16), kw.astype(BF16), (((1,), (1,)), ((), ())),
                        preferred_element_type=F32)
    s = s * scale + tab
    m = jnp.max(s, axis=-1, keepdims=True)
    p = jnp.exp(s - m)
    l = jnp.sum(p, axis=-1, keepdims=True)
    acc = jnp.dot(p.astype(BF16), vw.astype(BF16), preferred_element_type=F32)
    return acc, m, l


def _attn_prompt_kernel(q_ref, k_ref, v_ref, g_ref, tab_ref, o_ref, acc_ref, m_ref, l_ref):
    seq = q_ref.shape[0]
    scale = HEAD_DIM ** -0.5
    qb_rows = Q_BLOCK
    nk = KEYS_PER_PATTERN

    def rows(ref, start, size, stride):
        if stride == 1:
            return ref[pl.ds(start, size), :]
        return ref[pl.ds(start, size, stride=stride), :]

    def put(p, start, stride, acc, m, l):
        lanes = acc.shape[-1]
        if stride == 1:
            idx = pl.ds(start, qb_rows)
        else:
            idx = pl.ds(start, qb_rows, stride=stride)
        acc_ref[p, idx, :] = acc
        m_ref[p, idx, :] = jnp.broadcast_to(m, (qb_rows, lanes))
        l_ref[p, idx, :] = jnp.broadcast_to(l, (qb_rows, lanes))

    def first_block(p, phase, dil):
        tab = tab_ref[p][:, nk:]
        qb = rows(q_ref, phase, qb_rows, dil)
        kw = rows(k_ref, phase, qb_rows, dil)
        vw = rows(v_ref, phase, qb_rows, dil)
        put(p, phase, dil, *_attn_block(qb, kw, vw, tab, scale))

    def later_block(p, phase, dil, n):
        tab = tab_ref[p]
        q0 = phase + dil * qb_rows * n
        k0 = q0 - dil * nk
        qb = rows(q_ref, q0, qb_rows, dil)
        kw = rows(k_ref, k0, qb_rows + nk, dil)
        vw = rows(v_ref, k0, qb_rows + nk, dil)
        put(p, q0, dil, *_attn_block(qb, kw, vw, tab, scale))

    for p, (window, dil) in enumerate(PATTERNS):
        n_blocks = seq // dil // qb_rows
        if dil == 1:
            first_block(p, 0, 1)

            def body1(n, c):
                later_block(0, 0, 1, n)
                return c
            lax.fori_loop(1, n_blocks, body1, 0)
        elif n_blocks == 1:
            def body3(phase, c, p=p, dil=dil):
                first_block(p, phase, dil)
                return c
            lax.fori_loop(0, dil, body3, 0)
        else:
            for phase in range(dil):
                first_block(p, phase, dil)
                for n in range(1, n_blocks):
                    later_block(p, phase, dil, n)

    chunk = 256

    def combine(c, carry):
        sl = pl.ds(pl.multiple_of(c * chunk, chunk), chunk)
        m0, m1, m2 = m_ref[0, sl, :], m_ref[1, sl, :], m_ref[2, sl, :]
        mm = jnp.maximum(jnp.maximum(m0, m1), m2)
        e0, e1, e2 = jnp.exp(m0 - mm), jnp.exp(m1 - mm), jnp.exp(m2 - mm)
        num = e0 * acc_ref[0, sl, :] + e1 * acc_ref[1, sl, :] + e2 * acc_ref[2, sl, :]
        den = e0 * l_ref[0, sl, :] + e1 * l_ref[1, sl, :] + e2 * l_ref[2, sl, :]
        o_ref[sl, :] = (num / den * _silu(g_ref[sl, :])).astype(o_ref.dtype)
        return carry
    lax.fori_loop(0, seq // chunk, combine, 0)


def _attn_prompt(z3, tabs, n_heads):
    b, seq, _ = z3.shape
    hd = HEAD_DIM

    def col(off):
        return pl.BlockSpec((None, seq, hd), lambda i, h: (i, 0, off + h))

    return pl.pallas_call(
        _attn_prompt_kernel,
        grid=(b, n_heads),
        in_specs=[col(0), col(n_heads), col(2 * n_heads), col(3 * n_heads),
                  pl.BlockSpec((len(PATTERNS), None, Q_BLOCK, Q_BLOCK + KEYS_PER_PATTERN),
                               lambda i, h: (0, h, 0, 0))],
        out_specs=pl.BlockSpec((None, seq, hd), lambda i, h: (i, 0, h)),
        out_shape=jax.ShapeDtypeStruct((b, seq, n_heads * hd), BF16),
        scratch_shapes=[pltpu.VMEM((len(PATTERNS), seq, hd), F32)] * 3,
        compiler_params=pltpu.CompilerParams(
            dimension_semantics=("parallel", "parallel"), vmem_limit_bytes=48 * MIB),
        name="attn_prompt",
    )(z3, z3, z3, z3, tabs)


def _conv_kernel(ca_ref, cb_ref, gc_ref, pre_ref, dww_ref, dwb_ref, lng_ref, lnb_ref,
                 pww_ref, pwb_ref, o_ref, newc_ref, upad_ref, *, chunk):
    t_len = ca_ref.shape[0]
    hist = CONV_WIDTH - 1
    ca = ca_ref[...]
    u = ca * jax.nn.sigmoid(cb_ref[...])
    upad_ref[0:hist, :] = pre_ref[...]
    upad_ref[hist:hist + t_len, :] = u
    n_pad = upad_ref.shape[0] - (hist + t_len)
    upad_ref[hist + t_len:, :] = jnp.zeros((n_pad, ca_ref.shape[1]), F32)
    newc_ref[...] = upad_ref[t_len:t_len + hist, :]
    win_rows = upad_ref.shape[0] - t_len + chunk

    def body(c, carry):
        r0 = pl.multiple_of(c * chunk, SUBLANES) if chunk % SUBLANES == 0 else c * chunk
        win = upad_ref[pl.ds(r0, win_rows), :]
        y = jnp.zeros((chunk, ca_ref.shape[1]), F32) + dwb_ref[...]
        for s in range(SUBLANES):
            shifted = win[s:s + win_rows - SUBLANES]
            for a in range(-(-CONV_WIDTH // SUBLANES)):
                w = SUBLANES * a + s
                if w < CONV_WIDTH:
                    y = y + shifted[SUBLANES * a:SUBLANES * a + chunk] * dww_ref[w:w + 1, :]
        mu = jnp.mean(y, axis=-1, keepdims=True)
        var = jnp.mean(jnp.square(y - mu), axis=-1, keepdims=True)
        yn = (y - mu) * lax.rsqrt(var + EPS) * lng_ref[...] + lnb_ref[...]
        c_act = _silu(yn).astype(BF16)
        proj = jnp.dot(c_act, pww_ref[...], preferred_element_type=F32) + pwb_ref[...]
        o_ref[pl.ds(r0, chunk), :] = (proj * _silu(gc_ref[pl.ds(r0, chunk), :])).astype(o_ref.dtype)
        return carry
    lax.fori_loop(0, t_len // chunk, body, 0)


def _conv_branch(z3, prefix, dw_w, dw_b, ln_g, ln_b, pw_w_bf, pw_b, col0):
    n, t_len, _ = z3.shape
    c = prefix.shape[-1]
    hist = CONV_WIDTH - 1
    chunk = min(t_len, 64)
    cblk = col0 // c

    def zc(j):
        return pl.BlockSpec((None, t_len, c), lambda i: (i, 0, cblk + j))

    def vec():
        return pl.BlockSpec((1, c), lambda i: (0, 0))

    return pl.pallas_call(
        functools.partial(_conv_kernel, chunk=chunk),
        grid=(n,),
        in_specs=[zc(0), zc(1), zc(2),
                  pl.BlockSpec((None, hist, c), lambda i: (i, 0, 0)),
                  pl.BlockSpec((CONV_WIDTH, c), lambda i: (0, 0)),
                  vec(), vec(), vec(),
                  pl.BlockSpec((c, c), lambda i: (0, 0)),
                  vec()],
        out_specs=[pl.BlockSpec((None, t_len, c), lambda i: (i, 0, 0)),
                   pl.BlockSpec((None, hist, c), lambda i: (i, 0, 0))],
        out_shape=[jax.ShapeDtypeStruct((n, t_len, c), BF16),
                   jax.ShapeDtypeStruct((n, hist, c), F32)],
        scratch_shapes=[pltpu.VMEM((t_len - chunk + _round_up(chunk + CONV_WIDTH + 1, SUBLANES), c), F32)],
        compiler_params=pltpu.CompilerParams(
            dimension_semantics=("parallel",), vmem_limit_bytes=48 * MIB),
        name="conv_branch",
    )(z3, z3, z3, prefix, dw_w, dw_b.reshape(1, c), ln_g.reshape(1, c), ln_b.reshape(1, c),
      pw_w_bf, pw_b.reshape(1, c))


def _outproj_kernel(ma_ref, mc_ref, wa_ref, wc_ref, x_ref, g_ref, y_ref):
    y = jnp.dot(ma_ref[...], wa_ref[...], preferred_element_type=F32)
    y = y + jnp.dot(mc_ref[...], wc_ref[...], preferred_element_type=F32)
    ms = jnp.mean(y * y, axis=-1, keepdims=True)
    y_ref[...] = x_ref[...] + y * lax.rsqrt(ms + EPS) * g_ref[...]


def _outproj(mix_att, mix_conv, w_att_bf, w_conv_bf, x2d, norm_g, tm):
    m, d = x2d.shape
    da, dc = mix_att.shape[1], mix_conv.shape[1]
    return pl.pallas_call(
        _outproj_kernel,
        grid=(m // tm,),
        in_specs=[pl.BlockSpec((tm, da), lambda i: (i, 0)),
                  pl.BlockSpec((tm, dc), lambda i: (i, 0)),
                  pl.BlockSpec((da, d), lambda i: (0, 0)),
                  pl.BlockSpec((dc, d), lambda i: (0, 0)),
                  pl.BlockSpec((tm, d), lambda i: (i, 0)),
                  pl.BlockSpec((1, d), lambda i: (0, 0))],
        out_specs=pl.BlockSpec((tm, d), lambda i: (i, 0)),
        out_shape=jax.ShapeDtypeStruct((m, d), F32),
        compiler_params=pltpu.CompilerParams(
            dimension_semantics=("parallel",), vmem_limit_bytes=48 * MIB),
        name="outproj",
    )(mix_att, mix_conv, w_att_bf, w_conv_bf, x2d, norm_g.reshape(1, d))


def _attn_sample_kernel(q_ref, g_ref, kvn_ref, near_ref, far_ref, bias_ref, cache_any, kvn_any,
                        o_ref, newkv_any, sem):
    b = pl.program_id(0)
    t_new = q_ref.shape[0]
    past = cache_any.shape[2]
    nk = KEYS_PER_PATTERN
    shift = pltpu.make_async_copy(cache_any.at[0, b, pl.ds(t_new, past - t_new)],
                                  newkv_any.at[0, b, pl.ds(0, past - t_new)], sem.at[0])
    tail = pltpu.make_async_copy(kvn_any.at[b], newkv_any.at[0, b, pl.ds(past - t_new, t_new)],
                                 sem.at[1])
    shift.start()
    tail.start()

    scale = HEAD_DIM ** -0.5
    ones = jnp.ones((HEAD_DIM, HEAD_DIM), BF16)
    g_near = near_ref.shape[0]
    tail_groups = nk // 16
    bias = bias_ref[...]
    n_keys = bias.shape[0]
    for t in range(t_new):
        pieces = []
        for kv in range(2):
            new_rows = kvn_ref[:, kv]
            dense = near_ref[g_near - tail_groups:, :, :, kv].reshape(nk, HEAD_PAD, HEAD_DIM)
            dense = jnp.concatenate([dense, new_rows], axis=0)[t:t + nk + 1]
            mid = near_ref[:, :, t, kv].reshape(nk, HEAD_PAD, HEAD_DIM)
            far = far_ref[:, t, kv]
            own = new_rows[t:t + 1]
            pieces.append(jnp.concatenate([dense, mid, own, far, own], axis=0))
        k3, v3 = pieces
        prod = (k3 * q_ref[t][None]).reshape(n_keys * HEAD_PAD, HEAD_DIM)
        s = jnp.dot(prod.astype(BF16), ones, preferred_element_type=F32)
        s = s.reshape(n_keys, HEAD_PAD, HEAD_DIM) * scale + bias
        m = jnp.max(s, axis=0)
        p = jnp.exp(s - m[None])
        l = jnp.sum(p, axis=0)
        acc = jnp.sum(p * v3, axis=0)
        o_ref[t] = (acc / l * _silu(g_ref[t]))[:o_ref.shape[1]]

    shift.wait()
    tail.wait()


def _attn_sample(q4, g4, kvn_pad, kvn6, cache_kv, bias3):
    n, t_new = q4.shape[:2]
    past = cache_kv.shape[2]
    n_heads = cache_kv.shape[4]
    groups = past // 16
    cache8 = cache_kv.reshape(1, n, groups, 4, 4, 2, n_heads, HEAD_DIM)
    near_groups = 512 // 16
    n_keys = bias3.shape[0]
    vec = pl.BlockSpec((None, t_new, HEAD_PAD, HEAD_DIM), lambda i: (i, 0, 0, 0))
    return pl.pallas_call(
        _attn_sample_kernel,
        grid=(n,),
        in_specs=[vec, vec,
                  pl.BlockSpec((None, t_new, 2, HEAD_PAD, HEAD_DIM), lambda i: (i, 0, 0, 0, 0)),
                  pl.BlockSpec((None, None, near_groups, 4, 4, 2, HEAD_PAD, HEAD_DIM),
                               lambda i: (0, i, groups // near_groups - 1, 0, 0, 0, 0, 0)),
                  pl.BlockSpec((None, None, groups, None, 4, 2, HEAD_PAD, HEAD_DIM),
                               lambda i: (0, i, 0, 0, 0, 0, 0, 0)),
                  pl.BlockSpec((n_keys, HEAD_PAD, HEAD_DIM), lambda i: (0, 0, 0)),
                  pl.BlockSpec(memory_space=pl.ANY),
                  pl.BlockSpec(memory_space=pl.ANY)],
        out_specs=[pl.BlockSpec((None, t_new, n_heads, HEAD_DIM), lambda i: (i, 0, 0, 0)),
                   pl.BlockSpec(memory_space=pl.ANY)],
        out_shape=[jax.ShapeDtypeStruct((n, t_new, n_heads, HEAD_DIM), F32),
                   jax.ShapeDtypeStruct(cache_kv.shape, cache_kv.dtype)],
        scratch_shapes=[pltpu.SemaphoreType.DMA((2,))],
        compiler_params=pltpu.CompilerParams(
            dimension_semantics=("arbitrary",), vmem_limit_bytes=56 * MIB),
        name="attn_sample",
    )(q4, g4, kvn_pad, cache8, cache8, bias3, cache_kv, kvn6)


def _prompt_bias_tables(rel_bias):
    nk = KEYS_PER_PATTERN
    a = np.arange(Q_BLOCK)[:, None]
    c = np.arange(Q_BLOCK + nk)[None, :]
    kdist = a + nk - c
    valid = (kdist >= 0) & (kdist <= nk)
    tabs = []
    for _, dil in PATTERNS:
        bucket = _rel_bucket(jnp.asarray(np.clip(kdist, 0, nk) * dil, jnp.int32))
        bias = rel_bias[bucket].astype(F32)
        tabs.append(jnp.where(valid[:, :, None], bias, NEG_INF).transpose(2, 0, 1))
    return jnp.stack(tabs)


def _sample_bias_table(rel_bias):
    nk = KEYS_PER_PATTERN
    kk = np.arange(nk, -1, -1)
    parts = []
    for _, dil in PATTERNS:
        bucket = _rel_bucket(jnp.asarray(kk * dil, jnp.int32))
        parts.append(rel_bias[bucket].astype(F32))
    bias = jnp.concatenate(parts, axis=0)
    bias = jnp.pad(bias, ((0, 0), (0, HEAD_PAD - bias.shape[1])))
    return jnp.broadcast_to(bias[:, :, None], bias.shape + (HEAD_DIM,))


def kernel(x_prompt, x_sample, cache_conv, cache_kv, rel_bias, norm_pre, w_in, conv_dw_w, conv_dw_b,
           conv_ln_g, conv_ln_b, conv_pw_w, conv_pw_b, w_out, norm_post):
    depth = w_in.shape[0]
    assert depth == 1
    bsz, seq, d_model = x_prompt.shape
    n_dec, t_new, _ = x_sample.shape
    n_heads = cache_kv.shape[4]
    d_attn = n_heads * HEAD_DIM
    d_conv = cache_conv.shape[-1]
    past = cache_kv.shape[2]
    assert past == MAX_WINDOW and seq >= MAX_WINDOW and t_new <= 4
    hist = CONV_WIDTH - 1

    w_in_bf = w_in[0].astype(BF16)
    w_out_bf = w_out[0].astype(BF16)
    pw_bf = conv_pw_w[0].astype(BF16)
    conv_args = (conv_dw_w[0], conv_dw_b[0], conv_ln_g[0], conv_ln_b[0], pw_bf, conv_pw_b[0])
    conv_col0 = 4 * d_attn

    xp2 = x_prompt.reshape(bsz * seq, d_model)
    zp = _inproj(xp2, norm_pre[0], w_in_bf, tm=1024, tn=512)
    zp3 = zp.reshape(bsz, seq, -1)
    mix_att_p = _attn_prompt(zp3, _prompt_bias_tables(rel_bias), n_heads)
    zero_prefix = jnp.zeros((bsz, hist, d_conv), F32)
    mix_conv_p, new_conv_p = _conv_branch(zp3, zero_prefix, *conv_args, col0=conv_col0)
    yp = _outproj(mix_att_p.reshape(bsz * seq, d_attn), mix_conv_p.reshape(bsz * seq, d_conv),
                  w_out_bf[:d_attn], w_out_bf[d_attn:], xp2, norm_post[0], tm=512)
    win = min(MAX_WINDOW, seq)
    new_kv_p = zp3[:, seq - win:, d_attn:3 * d_attn].reshape(1, bsz, win, 2, n_heads, HEAD_DIM)

    xs2 = x_sample.reshape(n_dec * t_new, d_model)
    zs = _inproj(xs2, norm_pre[0], w_in_bf, tm=n_dec * t_new, tn=512)
    zs3 = zs.reshape(n_dec, t_new, -1)
    pad_h = ((0, 0), (0, 0), (0, HEAD_PAD - n_heads), (0, 0))
    q4 = jnp.pad(zs3[:, :, :d_attn].reshape(n_dec, t_new, n_heads, HEAD_DIM), pad_h)
    g4 = jnp.pad(zs3[:, :, 3 * d_attn:4 * d_attn].reshape(n_dec, t_new, n_heads, HEAD_DIM), pad_h)
    kvn6 = zs3[:, :, d_attn:3 * d_attn].reshape(n_dec, t_new, 2, n_heads, HEAD_DIM)
    kvn_pad = jnp.pad(kvn6, ((0, 0), (0, 0), (0, 0), (0, HEAD_PAD - n_heads), (0, 0)))
    att_s, new_kv_s = _attn_sample(q4, g4, kvn_pad, kvn6, cache_kv, _sample_bias_table(rel_bias))
    mix_att_s = att_s.reshape(n_dec * t_new, d_attn).astype(BF16)
    mix_conv_s, new_conv_s = _conv_branch(zs3, cache_conv[0], *conv_args, col0=conv_col0)
    ys = _outproj(mix_att_s, mix_conv_s.reshape(n_dec * t_new, d_conv),
                  w_out_bf[:d_attn], w_out_bf[d_attn:], xs2, norm_post[0], tm=n_dec * t_new)

    return (yp.reshape(bsz, seq, d_model), ys.reshape(n_dec, t_new, d_model),
            new_conv_p[None], new_kv_p, new_conv_s[None], new_kv_s)
```

```python
import functools
import math

import jax
import jax.numpy as jnp
import numpy as np
from jax import lax
from jax.experimental import pallas as pl
from jax.experimental.pallas import tpu as pltpu

F32 = jnp.float32
BF16 = jnp.bfloat16

HEAD_DIM = 128
PATTERNS = ((128, 1), (512, 4), (2048, 16))
MAX_WINDOW = 2048
Q_BLOCK = 128
KEYS_PER_PATTERN = 128
CONV_WIDTH = 31
N_BUCKETS = 32
MAX_EXACT = 16
EPS = 1e-6
NEG_INF = -1e30
SUBLANES = 8
HEAD_PAD = 16

MIB = 1024 * 1024


def _rel_bucket(dist):
    d = jnp.maximum(dist, 1).astype(F32)
    log_b = MAX_EXACT + (jnp.log(d / MAX_EXACT) / math.log(MAX_WINDOW / MAX_EXACT)
                         * (N_BUCKETS - MAX_EXACT)).astype(jnp.int32)
    log_b = jnp.minimum(log_b, N_BUCKETS - 1)
    return jnp.where(dist < MAX_EXACT, dist, log_b)


def _round_up(x, m):
    return -(-x // m) * m


def _silu(x):
    return x * jax.nn.sigmoid(x)


def _inproj_kernel(x_ref, g_ref, w_ref, z_ref, h_ref):
    @pl.when(pl.program_id(1) == 0)
    def _():
        x = x_ref[...]
        ms = jnp.mean(x * x, axis=-1, keepdims=True)
        h_ref[...] = (x * lax.rsqrt(ms + EPS) * g_ref[...]).astype(BF16)

    z_ref[...] = jnp.dot(h_ref[...], w_ref[...], preferred_element_type=F32)


def _inproj(x2d, norm_g, w_bf, tm, tn):
    m, d = x2d.shape
    n = w_bf.shape[1]
    return pl.pallas_call(
        _inproj_kernel,
        grid=(m // tm, n // tn),
        in_specs=[pl.BlockSpec((tm, d), lambda i, j: (i, 0)),
                  pl.BlockSpec((1, d), lambda i, j: (0, 0)),
                  pl.BlockSpec((d, tn), lambda i, j: (0, j))],
        out_specs=pl.BlockSpec((tm, tn), lambda i, j: (i, j)),
        out_shape=jax.ShapeDtypeStruct((m, n), F32),
        scratch_shapes=[pltpu.VMEM((tm, d), BF16)],
        compiler_params=pltpu.CompilerParams(
            dimension_semantics=("parallel", "arbitrary"), vmem_limit_bytes=48 * MIB),
        name="inproj",
    )(x2d, norm_g.reshape(1, d), w_bf)


def _attn_block(qb, kw, vw, tab, scale):
    s = lax.dot_general(qb.astype(BF16), kw.astype(BF16), (((1,), (1,)), ((), ())),
                        preferred_element_type=F32)
    s = s * scale + tab
    m = jnp.max(s, axis=-1, keepdims=True)
    p = jnp.exp(s - m).astype(BF16)
    v_ones = jnp.concatenate([vw.astype(BF16), jnp.ones(vw.shape, BF16)], axis=1)
    acc_l = jnp.dot(p, v_ones, preferred_element_type=F32)
    d = vw.shape[1]
    return acc_l[:, :d], m, acc_l[:, d:]


def _attn_prompt_kernel(q_ref, k_ref, v_ref, g_ref, tab_ref, o_ref, acc_ref, m_ref, l_ref):
    seq = q_ref.shape[0]
    scale = HEAD_DIM ** -0.5
    qb_rows = Q_BLOCK
    nk = KEYS_PER_PATTERN

    def rows(ref, start, size, stride):
        if stride == 1:
            return ref[pl.ds(start, size), :]
        return ref[pl.ds(start, size, stride=stride), :]

    def put(p, start, stride, acc, m, l):
        lanes = acc.shape[-1]
        if stride == 1:
            idx = pl.ds(start, qb_rows)
        else:
            idx = pl.ds(start, qb_rows, stride=stride)
        acc_ref[p, idx, :] = acc
        m_ref[p, idx, :] = jnp.broadcast_to(m, (qb_rows, lanes))
        l_ref[p, idx, :] = l

    def first_block(p, phase, dil):
        tab = tab_ref[p][:, nk:]
        qb = rows(q_ref, phase, qb_rows, dil)
        kw = rows(k_ref, phase, qb_rows, dil)
        vw = rows(v_ref, phase, qb_rows, dil)
        put(p, phase, dil, *_attn_block(qb, kw, vw, tab, scale))

    def later_block(p, phase, dil, n):
        tab = tab_ref[p]
        q0 = phase + dil * qb_rows * n
        k0 = q0 - dil * nk
        qb = rows(q_ref, q0, qb_rows, dil)
        kw = rows(k_ref, k0, qb_rows + nk, dil)
        vw = rows(v_ref, k0, qb_rows + nk, dil)
        put(p, q0, dil, *_attn_block(qb, kw, vw, tab, scale))

    for p, (window, dil) in enumerate(PATTERNS):
        n_blocks = seq // dil // qb_rows
        for phase in range(dil):
            first_block(p, phase, dil)
            for n in range(1, n_blocks):
                later_block(p, phase, dil, n)

    chunk = 256

    def combine(c, carry):
        sl = pl.ds(pl.multiple_of(c * chunk, chunk), chunk)
        m0, m1, m2 = m_ref[0, sl, :], m_ref[1, sl, :], m_ref[2, sl, :]
        mm = jnp.maximum(jnp.maximum(m0, m1), m2)
        e0, e1, e2 = jnp.exp(m0 - mm), jnp.exp(m1 - mm), jnp.exp(m2 - mm)
        num = e0 * acc_ref[0, sl, :] + e1 * acc_ref[1, sl, :] + e2 * acc_ref[2, sl, :]
        den = e0 * l_ref[0, sl, :] + e1 * l_ref[1, sl, :] + e2 * l_ref[2, sl, :]
        o_ref[sl, :] = (num / den * _silu(g_ref[sl, :])).astype(o_ref.dtype)
        return carry
    lax.fori_loop(0, seq // chunk, combine, 0)


def _attn_prompt(z3, tabs, n_heads):
    b, seq, _ = z3.shape
    hd = HEAD_DIM

    def col(off):
        return pl.BlockSpec((None, seq, hd), lambda i, h: (i, 0, off + h))

    return pl.pallas_call(
        _attn_prompt_kernel,
        grid=(b, n_heads),
        in_specs=[col(0), col(n_heads), col(2 * n_heads), col(3 * n_heads),
                  pl.BlockSpec((len(PATTERNS), None, Q_BLOCK, Q_BLOCK + KEYS_PER_PATTERN),
                               lambda i, h: (0, h, 0, 0))],
        out_specs=pl.BlockSpec((None, seq, hd), lambda i, h: (i, 0, h)),
        out_shape=jax.ShapeDtypeStruct((b, seq, n_heads * hd), BF16),
        scratch_shapes=[pltpu.VMEM((len(PATTERNS), seq, hd), F32)] * 3,
        compiler_params=pltpu.CompilerParams(
            dimension_semantics=("parallel", "parallel"), vmem_limit_bytes=48 * MIB),
        name="attn_prompt",
    )(z3, z3, z3, z3, tabs)


def _conv_kernel(ca_ref, cb_ref, gc_ref, pre_ref, dww_ref, dwb_ref, lng_ref, lnb_ref,
                 pww_ref, pwb_ref, o_ref, newc_ref, upad_ref, *, chunk):
    t_len = ca_ref.shape[0]
    hist = CONV_WIDTH - 1
    ca = ca_ref[...]
    u = ca * jax.nn.sigmoid(cb_ref[...])
    upad_ref[0:hist, :] = pre_ref[...]
    upad_ref[hist:hist + t_len, :] = u
    n_pad = upad_ref.shape[0] - (hist + t_len)
    upad_ref[hist + t_len:, :] = jnp.zeros((n_pad, ca_ref.shape[1]), F32)
    newc_ref[...] = upad_ref[t_len:t_len + hist, :]
    win_rows = upad_ref.shape[0] - t_len + chunk

    def body(c, carry):
        r0 = pl.multiple_of(c * chunk, SUBLANES) if chunk % SUBLANES == 0 else c * chunk
        win = upad_ref[pl.ds(r0, win_rows), :]
        y = jnp.zeros((chunk, ca_ref.shape[1]), F32) + dwb_ref[...]
        for s in range(SUBLANES):
            shifted = win[s:s + win_rows - SUBLANES]
            for a in range(-(-CONV_WIDTH // SUBLANES)):
                w = SUBLANES * a + s
                if w < CONV_WIDTH:
                    y = y + shifted[SUBLANES * a:SUBLANES * a + chunk] * dww_ref[w:w + 1, :]
        mu = jnp.mean(y, axis=-1, keepdims=True)
        var = jnp.mean(jnp.square(y - mu), axis=-1, keepdims=True)
        yn = (y - mu) * lax.rsqrt(var + EPS) * lng_ref[...] + lnb_ref[...]
        c_act = _silu(yn).astype(BF16)
        proj = jnp.dot(c_act, pww_ref[...], preferred_element_type=F32) + pwb_ref[...]
        o_ref[pl.ds(r0, chunk), :] = (proj * _silu(gc_ref[pl.ds(r0, chunk), :])).astype(o_ref.dtype)
        return carry
    lax.fori_loop(0, t_len // chunk, body, 0)


def _conv_branch(z3, prefix, dw_w, dw_b, ln_g, ln_b, pw_w_bf, pw_b, col0):
    n, t_len, _ = z3.shape
    c = prefix.shape[-1]
    hist = CONV_WIDTH - 1
    chunk = min(t_len, 64)
    cblk = col0 // c

    def zc(j):
        return pl.BlockSpec((None, t_len, c), lambda i: (i, 0, cblk + j))

    def vec():
        return pl.BlockSpec((1, c), lambda i: (0, 0))

    return pl.pallas_call(
        functools.partial(_conv_kernel, chunk=chunk),
        grid=(n,),
        in_specs=[zc(0), zc(1), zc(2),
                  pl.BlockSpec((None, hist, c), lambda i: (i, 0, 0)),
                  pl.BlockSpec((CONV_WIDTH, c), lambda i: (0, 0)),
                  vec(), vec(), vec(),
                  pl.BlockSpec((c, c), lambda i: (0, 0)),
                  vec()],
        out_specs=[pl.BlockSpec((None, t_len, c), lambda i: (i, 0, 0)),
                   pl.BlockSpec((None, hist, c), lambda i: (i, 0, 0))],
        out_shape=[jax.ShapeDtypeStruct((n, t_len, c), BF16),
                   jax.ShapeDtypeStruct((n, hist, c), F32)],
        scratch_shapes=[pltpu.VMEM((t_len - chunk + _round_up(chunk + CONV_WIDTH + 1, SUBLANES), c), F32)],
        compiler_params=pltpu.CompilerParams(
            dimension_semantics=("parallel",), vmem_limit_bytes=48 * MIB),
        name="conv_branch",
    )(z3, z3, z3, prefix, dw_w, dw_b.reshape(1, c), ln_g.reshape(1, c), ln_b.reshape(1, c),
      pw_w_bf, pw_b.reshape(1, c))


def _outproj_kernel(ma_ref, mc_ref, wa_ref, wc_ref, x_ref, g_ref, y_ref):
    y = jnp.dot(ma_ref[...], wa_ref[...], preferred_element_type=F32)
    y = y + jnp.dot(mc_ref[...], wc_ref[...], preferred_element_type=F32)
    ms = jnp.mean(y * y, axis=-1, keepdims=True)
    y_ref[...] = x_ref[...] + y * lax.rsqrt(ms + EPS) * g_ref[...]


def _outproj(mix_att, mix_conv, w_att_bf, w_conv_bf, x2d, norm_g, tm):
    m, d = x2d.shape
    da, dc = mix_att.shape[1], mix_conv.shape[1]
    return pl.pallas_call(
        _outproj_kernel,
        grid=(m // tm,),
        in_specs=[pl.BlockSpec((tm, da), lambda i: (i, 0)),
                  pl.BlockSpec((tm, dc), lambda i: (i, 0)),
                  pl.BlockSpec((da, d), lambda i: (0, 0)),
                  pl.BlockSpec((dc, d), lambda i: (0, 0)),
                  pl.BlockSpec((tm, d), lambda i: (i, 0)),
                  pl.BlockSpec((1, d), lambda i: (0, 0))],
        out_specs=pl.BlockSpec((tm, d), lambda i: (i, 0)),
        out_shape=jax.ShapeDtypeStruct((m, d), F32),
        compiler_params=pltpu.CompilerParams(
            dimension_semantics=("parallel",), vmem_limit_bytes=48 * MIB),
        name="outproj",
    )(mix_att, mix_conv, w_att_bf, w_conv_bf, x2d, norm_g.reshape(1, d))


def _attn_sample_kernel(q_ref, g_ref, kvn_ref, near_ref, far_ref, bias_ref, cache_any, kvn_any,
                        o_ref, newkv_any, sem):
    b = pl.program_id(0)
    t_new, rows_kv, hd = q_ref.shape
    past = cache_any.shape[1]
    nk = KEYS_PER_PATTERN
    shift = pltpu.make_async_copy(cache_any.at[b, pl.ds(t_new, past - t_new)],
                                  newkv_any.at[b, pl.ds(0, past - t_new)], sem.at[0])
    tail = pltpu.make_async_copy(kvn_any.at[b], newkv_any.at[b, pl.ds(past - t_new, t_new)],
                                 sem.at[1])
    shift.start()
    tail.start()

    scale = HEAD_DIM ** -0.5
    ones = jnp.ones((hd, hd), BF16)
    g_near = near_ref.shape[0]
    tail_groups = nk // 16
    bias = bias_ref[...]
    n_keys = bias.shape[0]

    def key_to_value_sublane(x):
        shp = x.shape
        x = x.reshape(-1, SUBLANES, hd)
        return pltpu.roll(x, 1, 1).reshape(shp)

    new_rows = kvn_ref[...]
    dense_all = jnp.concatenate(
        [near_ref[g_near - tail_groups:].reshape(nk, rows_kv, hd), new_rows], axis=0)
    for t in range(t_new):
        own = new_rows[t:t + 1]
        rows3 = jnp.concatenate(
            [dense_all[t:t + nk + 1],
             near_ref[:, :, t].reshape(nk, rows_kv, hd), own,
             far_ref[:, t], own], axis=0)
        prod = (rows3 * q_ref[t][None]).reshape(n_keys * rows_kv, hd)
        s = jnp.dot(prod.astype(BF16), ones, preferred_element_type=F32)
        s = s.reshape(n_keys, rows_kv, hd) * scale + bias
        m = jnp.max(s, axis=0)
        p = jnp.exp(s - m[None])
        l = jnp.sum(p, axis=0)
        acc = jnp.sum(key_to_value_sublane(p) * rows3, axis=0)
        o_ref[t] = acc / key_to_value_sublane(l) * _silu(g_ref[t])

    shift.wait()
    tail.wait()


def _attn_sample(q24, g24, kvn24, cache_rows, bias3):
    n, t_new, rows_kv, hd = q24.shape
    past = cache_rows.shape[1]
    groups = past // 16
    cache6 = cache_rows.reshape(n, groups, 4, 4, rows_kv, hd)
    near_groups = 512 // 16
    n_keys = bias3.shape[0]
    vec = pl.BlockSpec((None, t_new, rows_kv, hd), lambda i: (i, 0, 0, 0))
    return pl.pallas_call(
        _attn_sample_kernel,
        grid=(n,),
        in_specs=[vec, vec, vec,
                  pl.BlockSpec((None, near_groups, 4, 4, rows_kv, hd),
                               lambda i: (i, groups // near_groups - 1, 0, 0, 0, 0)),
                  pl.BlockSpec((None, groups, None, 4, rows_kv, hd),
                               lambda i: (i, 0, 0, 0, 0, 0)),
                  pl.BlockSpec((n_keys, rows_kv, hd), lambda i: (0, 0, 0)),
                  pl.BlockSpec(memory_space=pl.ANY),
                  pl.BlockSpec(memory_space=pl.ANY)],
        out_specs=[vec, pl.BlockSpec(memory_space=pl.ANY)],
        out_shape=[jax.ShapeDtypeStruct((n, t_new, rows_kv, hd), F32),
                   jax.ShapeDtypeStruct(cache_rows.shape, cache_rows.dtype)],
        scratch_shapes=[pltpu.SemaphoreType.DMA((2,))],
        compiler_params=pltpu.CompilerParams(
            dimension_semantics=("arbitrary",), vmem_limit_bytes=56 * MIB),
        name="attn_sample",
    )(q24, g24, kvn24, cache6, cache6, bias3, cache_rows, kvn24)


def _prompt_bias_tables(rel_bias):
    nk = KEYS_PER_PATTERN
    qb = Q_BLOCK
    kdist = np.arange(-(qb - 1), qb + nk)
    n_dist = kdist.shape[0]
    valid = (kdist >= 0) & (kdist <= nk)
    tabs = []
    for _, dil in PATTERNS:
        bucket = _rel_bucket(jnp.asarray(np.clip(kdist, 0, nk) * dil, jnp.int32))
        per_dist = jnp.where(valid[:, None], rel_bias[bucket].astype(F32), NEG_INF)
        rev = per_dist.T[:, ::-1]
        skew = jnp.tile(rev, (1, qb + 1))[:, :qb * (n_dist + 1)].reshape(-1, qb, n_dist + 1)
        tabs.append(skew[:, ::-1, :qb + nk])
    return jnp.stack(tabs)


def _sample_bias_table(rel_bias):
    nk = KEYS_PER_PATTERN
    kk = np.arange(nk, -1, -1)
    parts = []
    for _, dil in PATTERNS:
        bucket = _rel_bucket(jnp.asarray(kk * dil, jnp.int32))
        parts.append(rel_bias[bucket].astype(F32))
    bias = jnp.concatenate(parts, axis=0)
    bias = jnp.stack([bias, jnp.zeros_like(bias)], axis=-1).reshape(bias.shape[0], -1)
    return jnp.broadcast_to(bias[:, :, None], bias.shape + (HEAD_DIM,))


def kernel(x_prompt, x_sample, cache_conv, cache_kv, rel_bias, norm_pre, w_in, conv_dw_w, conv_dw_b,
           conv_ln_g, conv_ln_b, conv_pw_w, conv_pw_b, w_out, norm_post):
    depth = w_in.shape[0]
    assert depth == 1
    bsz, seq, d_model = x_prompt.shape
    n_dec, t_new, _ = x_sample.shape
    n_heads = cache_kv.shape[4]
    d_attn = n_heads * HEAD_DIM
    d_conv = cache_conv.shape[-1]
    past = cache_kv.shape[2]
    assert past == MAX_WINDOW and seq >= MAX_WINDOW and t_new <= 4
    hist = CONV_WIDTH - 1

    w_in_bf = w_in[0].astype(BF16)
    w_out_bf = w_out[0].astype(BF16)
    pw_bf = conv_pw_w[0].astype(BF16)
    conv_args = (conv_dw_w[0], conv_dw_b[0], conv_ln_g[0], conv_ln_b[0], pw_bf, conv_pw_b[0])
    conv_col0 = 4 * d_attn

    xp2 = x_prompt.reshape(bsz * seq, d_model)
    zp = _inproj(xp2, norm_pre[0], w_in_bf, tm=1024, tn=512)
    zp3 = zp.reshape(bsz, seq, -1)
    mix_att_p = _attn_prompt(zp3, _prompt_bias_tables(rel_bias), n_heads)
    zero_prefix = jnp.zeros((bsz, hist, d_conv), F32)
    mix_conv_p, new_conv_p = _conv_branch(zp3, zero_prefix, *conv_args, col0=conv_col0)
    yp = _outproj(mix_att_p.reshape(bsz * seq, d_attn), mix_conv_p.reshape(bsz * seq, d_conv),
                  w_out_bf[:d_attn], w_out_bf[d_attn:], xp2, norm_post[0], tm=512)
    win = min(MAX_WINDOW, seq)
    new_kv_p = zp3[:, seq - win:, d_attn:3 * d_attn].reshape(1, bsz, win, 2, n_heads, HEAD_DIM)

    xs2 = x_sample.reshape(n_dec * t_new, d_model)
    zs = _inproj(xs2, norm_pre[0], w_in_bf, tm=n_dec * t_new, tn=512)
    zs3 = zs.reshape(n_dec, t_new, -1)
    def heads(col0):
        return zs3[:, :, col0:col0 + d_attn].reshape(n_dec, t_new, n_heads, HEAD_DIM)

    def interleave(even, odd):
        return jnp.stack([even, odd], axis=3).reshape(n_dec, t_new, 2 * n_heads, HEAD_DIM)

    zero_h = jnp.zeros((n_dec, t_new, n_heads, HEAD_DIM), F32)
    q24 = interleave(heads(0), zero_h)
    g24 = interleave(zero_h, heads(3 * d_attn))
    kvn24 = interleave(heads(d_attn), heads(2 * d_attn))
    cache_rows = cache_kv[0].transpose(0, 1, 3, 2, 4).reshape(n_dec, past, 2 * n_heads, HEAD_DIM)
    att_s, new_rows = _attn_sample(q24, g24, kvn24, cache_rows, _sample_bias_table(rel_bias))
    new_kv_s = new_rows.reshape(n_dec, past, n_heads, 2, HEAD_DIM).transpose(0, 1, 3, 2, 4)[None]
    mix_att_s = att_s[:, :, 1::2].reshape(n_dec * t_new, d_attn).astype(BF16)
    mix_conv_s, new_conv_s = _conv_branch(zs3, cache_conv[0], *conv_args, col0=conv_col0)
    ys = _outproj(mix_att_s, mix_conv_s.reshape(n_dec * t_new, d_conv),
                  w_out_bf[:d_attn], w_out_bf[d_attn:], xs2, norm_post[0], tm=n_dec * t_new)

    return (yp.reshape(bsz, seq, d_model), ys.reshape(n_dec, t_new, d_model),
            new_conv_p[None], new_kv_p, new_conv_s[None], new_kv_s)
```

```python
import functools
import math

import jax
import jax.numpy as jnp
import numpy as np
from jax import lax
from jax.experimental import pallas as pl
from jax.experimental.pallas import tpu as pltpu

F32 = jnp.float32
BF16 = jnp.bfloat16

HEAD_DIM = 128
PATTERNS = ((128, 1), (512, 4), (2048, 16))
MAX_WINDOW = 2048
Q_BLOCK = 128
KEYS_PER_PATTERN = 128
CONV_WIDTH = 31
N_BUCKETS = 32
MAX_EXACT = 16
EPS = 1e-6
NEG_INF = -1e30
SUBLANES = 8
HEAD_PAD = 16

MIB = 1024 * 1024


def _rel_bucket(dist):
    d = jnp.maximum(dist, 1).astype(F32)
    log_b = MAX_EXACT + (jnp.log(d / MAX_EXACT) / math.log(MAX_WINDOW / MAX_EXACT)
                         * (N_BUCKETS - MAX_EXACT)).astype(jnp.int32)
    log_b = jnp.minimum(log_b, N_BUCKETS - 1)
    return jnp.where(dist < MAX_EXACT, dist, log_b)


def _round_up(x, m):
    return -(-x // m) * m


def _silu(x):
    return x * jax.nn.sigmoid(x)


def _inproj_kernel(x_ref, g_ref, w_ref, z_ref, h_ref):
    @pl.when(pl.program_id(1) == 0)
    def _():
        x = x_ref[...]
        ms = jnp.mean(x * x, axis=-1, keepdims=True)
        h_ref[...] = (x * lax.rsqrt(ms + EPS) * g_ref[...]).astype(BF16)

    z_ref[...] = jnp.dot(h_ref[...], w_ref[...], preferred_element_type=F32)


def _inproj(x2d, norm_g, w_bf, tm, tn):
    m, d = x2d.shape
    n = w_bf.shape[1]
    return pl.pallas_call(
        _inproj_kernel,
        grid=(m // tm, n // tn),
        in_specs=[pl.BlockSpec((tm, d), lambda i, j: (i, 0)),
                  pl.BlockSpec((1, d), lambda i, j: (0, 0)),
                  pl.BlockSpec((d, tn), lambda i, j: (0, j))],
        out_specs=pl.BlockSpec((tm, tn), lambda i, j: (i, j)),
        out_shape=jax.ShapeDtypeStruct((m, n), F32),
        scratch_shapes=[pltpu.VMEM((tm, d), BF16)],
        compiler_params=pltpu.CompilerParams(
            dimension_semantics=("parallel", "arbitrary"), vmem_limit_bytes=48 * MIB),
        name="inproj",
    )(x2d, norm_g.reshape(1, d), w_bf)


def _attn_block(qb, kw, vw, tab, scale):
    s = lax.dot_general(qb.astype(BF16), kw.astype(BF16), (((1,), (1,)), ((), ())),
                        preferred_element_type=F32)
    s = s * scale + tab
    m = jnp.max(s, axis=-1, keepdims=True)
    p = jnp.exp(s - m).astype(BF16)
    v_ones = jnp.concatenate([vw.astype(BF16), jnp.ones(vw.shape, BF16)], axis=1)
    acc_l = jnp.dot(p, v_ones, preferred_element_type=F32)
    d = vw.shape[1]
    return acc_l[:, :d], m, acc_l[:, d:]


def _attn_prompt_kernel(q_ref, k_ref, v_ref, g_ref, tab_ref, o_ref, acc_ref, m_ref, l_ref):
    seq = q_ref.shape[0]
    scale = HEAD_DIM ** -0.5
    qb_rows = Q_BLOCK
    nk = KEYS_PER_PATTERN

    def rows(ref, start, size, stride):
        if stride == 1:
            return ref[pl.ds(start, size), :]
        return ref[pl.ds(start, size, stride=stride), :]

    def put(p, start, stride, acc, m, l):
        lanes = acc.shape[-1]
        if stride == 1:
            idx = pl.ds(start, qb_rows)
        else:
            idx = pl.ds(start, qb_rows, stride=stride)
        acc_ref[p, idx, :] = acc
        m_ref[p, idx, :] = jnp.broadcast_to(m, (qb_rows, lanes))
        l_ref[p, idx, :] = l

    def first_block(p, phase, dil):
        tab = tab_ref[p][:, nk:]
        qb = rows(q_ref, phase, qb_rows, dil)
        kw = rows(k_ref, phase, qb_rows, dil)
        vw = rows(v_ref, phase, qb_rows, dil)
        put(p, phase, dil, *_attn_block(qb, kw, vw, tab, scale))

    def later_block(p, phase, dil, n):
        tab = tab_ref[p]
        q0 = phase + dil * qb_rows * n
        k0 = q0 - dil * nk
        qb = rows(q_ref, q0, qb_rows, dil)
        kw = rows(k_ref, k0, qb_rows + nk, dil)
        vw = rows(v_ref, k0, qb_rows + nk, dil)
        put(p, q0, dil, *_attn_block(qb, kw, vw, tab, scale))

    for p, (window, dil) in enumerate(PATTERNS):
        n_blocks = seq // dil // qb_rows
        for phase in range(dil):
            first_block(p, phase, dil)
            for n in range(1, n_blocks):
                later_block(p, phase, dil, n)

    chunk = 256

    def combine(c, carry):
        sl = pl.ds(pl.multiple_of(c * chunk, chunk), chunk)
        m0, m1, m2 = m_ref[0, sl, :], m_ref[1, sl, :], m_ref[2, sl, :]
        mm = jnp.maximum(jnp.maximum(m0, m1), m2)
        e0, e1, e2 = jnp.exp(m0 - mm), jnp.exp(m1 - mm), jnp.exp(m2 - mm)
        num = e0 * acc_ref[0, sl, :] + e1 * acc_ref[1, sl, :] + e2 * acc_ref[2, sl, :]
        den = e0 * l_ref[0, sl, :] + e1 * l_ref[1, sl, :] + e2 * l_ref[2, sl, :]
        o_ref[sl, :] = (num / den * _silu(g_ref[sl, :])).astype(o_ref.dtype)
        return carry
    lax.fori_loop(0, seq // chunk, combine, 0)


def _attn_prompt(z3, tabs, n_heads):
    b, seq, _ = z3.shape
    hd = HEAD_DIM

    def col(off):
        return pl.BlockSpec((None, seq, hd), lambda i, h: (i, 0, off + h))

    return pl.pallas_call(
        _attn_prompt_kernel,
        grid=(b, n_heads),
        in_specs=[col(0), col(n_heads), col(2 * n_heads), col(3 * n_heads),
                  pl.BlockSpec((len(PATTERNS), None, Q_BLOCK, Q_BLOCK + KEYS_PER_PATTERN),
                               lambda i, h: (0, h, 0, 0))],
        out_specs=pl.BlockSpec((None, seq, hd), lambda i, h: (i, 0, h)),
        out_shape=jax.ShapeDtypeStruct((b, seq, n_heads * hd), BF16),
        scratch_shapes=[pltpu.VMEM((len(PATTERNS), seq, hd), F32)] * 3,
        compiler_params=pltpu.CompilerParams(
            dimension_semantics=("parallel", "parallel"), vmem_limit_bytes=48 * MIB),
        name="attn_prompt",
    )(z3, z3, z3, z3, tabs)


def _conv_kernel(ca_ref, cb_ref, gc_ref, pre_ref, dww_ref, dwb_ref, lng_ref, lnb_ref,
                 pww_ref, pwb_ref, o_ref, newc_ref, upad_ref, *, chunk):
    t_len = ca_ref.shape[0]
    hist = CONV_WIDTH - 1
    ca = ca_ref[...]
    u = ca * jax.nn.sigmoid(cb_ref[...])
    upad_ref[0:hist, :] = pre_ref[...]
    upad_ref[hist:hist + t_len, :] = u
    n_pad = upad_ref.shape[0] - (hist + t_len)
    upad_ref[hist + t_len:, :] = jnp.zeros((n_pad, ca_ref.shape[1]), F32)
    newc_ref[...] = upad_ref[t_len:t_len + hist, :]
    win_rows = upad_ref.shape[0] - t_len + chunk

    def body(c, carry):
        r0 = pl.multiple_of(c * chunk, SUBLANES) if chunk % SUBLANES == 0 else c * chunk
        win = upad_ref[pl.ds(r0, win_rows), :]
        y = jnp.zeros((chunk, ca_ref.shape[1]), F32) + dwb_ref[...]
        for s in range(SUBLANES):
            shifted = win[s:s + win_rows - SUBLANES]
            for a in range(-(-CONV_WIDTH // SUBLANES)):
                w = SUBLANES * a + s
                if w < CONV_WIDTH:
                    y = y + shifted[SUBLANES * a:SUBLANES * a + chunk] * dww_ref[w:w + 1, :]
        mu = jnp.mean(y, axis=-1, keepdims=True)
        var = jnp.mean(jnp.square(y - mu), axis=-1, keepdims=True)
        yn = (y - mu) * lax.rsqrt(var + EPS) * lng_ref[...] + lnb_ref[...]
        c_act = _silu(yn).astype(BF16)
        proj = jnp.dot(c_act, pww_ref[...], preferred_element_type=F32) + pwb_ref[...]
        o_ref[pl.ds(r0, chunk), :] = (proj * _silu(gc_ref[pl.ds(r0, chunk), :])).astype(o_ref.dtype)
        return carry
    lax.fori_loop(0, t_len // chunk, body, 0)


def _conv_branch(z3, prefix, dw_w, dw_b, ln_g, ln_b, pw_w_bf, pw_b, col0):
    n, t_len, _ = z3.shape
    c = prefix.shape[-1]
    hist = CONV_WIDTH - 1
    chunk = min(t_len, 64)
    cblk = col0 // c

    def zc(j):
        return pl.BlockSpec((None, t_len, c), lambda i: (i, 0, cblk + j))

    def vec():
        return pl.BlockSpec((1, c), lambda i: (0, 0))

    return pl.pallas_call(
        functools.partial(_conv_kernel, chunk=chunk),
        grid=(n,),
        in_specs=[zc(0), zc(1), zc(2),
                  pl.BlockSpec((None, hist, c), lambda i: (i, 0, 0)),
                  pl.BlockSpec((CONV_WIDTH, c), lambda i: (0, 0)),
                  vec(), vec(), vec(),
                  pl.BlockSpec((c, c), lambda i: (0, 0)),
                  vec()],
        out_specs=[pl.BlockSpec((None, t_len, c), lambda i: (i, 0, 0)),
                   pl.BlockSpec((None, hist, c), lambda i: (i, 0, 0))],
        out_shape=[jax.ShapeDtypeStruct((n, t_len, c), BF16),
                   jax.ShapeDtypeStruct((n, hist, c), F32)],
        scratch_shapes=[pltpu.VMEM((t_len - chunk + _round_up(chunk + CONV_WIDTH + 1, SUBLANES), c), F32)],
        compiler_params=pltpu.CompilerParams(
            dimension_semantics=("parallel",), vmem_limit_bytes=48 * MIB),
        name="conv_branch",
    )(z3, z3, z3, prefix, dw_w, dw_b.reshape(1, c), ln_g.reshape(1, c), ln_b.reshape(1, c),
      pw_w_bf, pw_b.reshape(1, c))


def _outproj_kernel(ma_ref, mc_ref, wa_ref, wc_ref, x_ref, g_ref, y_ref):
    y = jnp.dot(ma_ref[...], wa_ref[...], preferred_element_type=F32)
    y = y + jnp.dot(mc_ref[...], wc_ref[...], preferred_element_type=F32)
    ms = jnp.mean(y * y, axis=-1, keepdims=True)
    y_ref[...] = x_ref[...] + y * lax.rsqrt(ms + EPS) * g_ref[...]


def _outproj(mix_att, mix_conv, w_att_bf, w_conv_bf, x2d, norm_g, tm):
    m, d = x2d.shape
    da, dc = mix_att.shape[1], mix_conv.shape[1]
    return pl.pallas_call(
        _outproj_kernel,
        grid=(m // tm,),
        in_specs=[pl.BlockSpec((tm, da), lambda i: (i, 0)),
                  pl.BlockSpec((tm, dc), lambda i: (i, 0)),
                  pl.BlockSpec((da, d), lambda i: (0, 0)),
                  pl.BlockSpec((dc, d), lambda i: (0, 0)),
                  pl.BlockSpec((tm, d), lambda i: (i, 0)),
                  pl.BlockSpec((1, d), lambda i: (0, 0))],
        out_specs=pl.BlockSpec((tm, d), lambda i: (i, 0)),
        out_shape=jax.ShapeDtypeStruct((m, d), F32),
        compiler_params=pltpu.CompilerParams(
            dimension_semantics=("parallel",), vmem_limit_bytes=48 * MIB),
        name="outproj",
    )(mix_att, mix_conv, w_att_bf, w_conv_bf, x2d, norm_g.reshape(1, d))


def _attn_sample_kernel(q_ref, g_ref, kvn_ref, blk_ref, nxt_ref, bias_ref, o_ref, out_ref, far_ref):
    c = pl.program_id(1)
    last = pl.num_programs(1) - 1
    t_new, rows_kv, hd = q_ref.shape
    gpc = blk_ref.shape[0]
    nk = KEYS_PER_PATTERN

    out_ref[:, 0:3] = blk_ref[:, 1:4]
    out_ref[0:gpc - 1, 3] = blk_ref[1:gpc, 0]

    @pl.when(c < last)
    def _():
        out_ref[gpc - 1, 3] = nxt_ref[...]

    @pl.when(c == last)
    def _():
        out_ref[gpc - 1, 3] = kvn_ref[...]

    far_ref[pl.ds(c * gpc, gpc)] = blk_ref[:, 0]

    @pl.when(c == last)
    def _():
        scale = HEAD_DIM ** -0.5
        ones = jnp.ones((hd, hd), BF16)
        tail_groups = nk // 16
        bias = bias_ref[...]
        n_keys = bias.shape[0]

        def key_to_value_sublane(x):
            shp = x.shape
            x = x.reshape(-1, SUBLANES, hd)
            return pltpu.roll(x, 1, 1).reshape(shp)

        new_rows = kvn_ref[...]
        dense_all = jnp.concatenate(
            [blk_ref[gpc - tail_groups:].reshape(nk, rows_kv, hd), new_rows], axis=0)
        for t in range(t_new):
            own = new_rows[t:t + 1]
            rows3 = jnp.concatenate(
                [dense_all[t:t + nk + 1],
                 blk_ref[:, :, t].reshape(nk, rows_kv, hd), own,
                 far_ref[:, t], own], axis=0)
            prod = (rows3 * q_ref[t][None]).reshape(n_keys * rows_kv, hd)
            s = jnp.dot(prod.astype(BF16), ones, preferred_element_type=F32)
            s = s.reshape(n_keys, rows_kv, hd) * scale + bias
            m = jnp.max(s, axis=0)
            p = jnp.exp(s - m[None])
            l = jnp.sum(p, axis=0)
            acc = jnp.sum(key_to_value_sublane(p) * rows3, axis=0)
            o_ref[t] = acc / key_to_value_sublane(l) * _silu(g_ref[t])


def _attn_sample(q24, g24, kvn24, cache_rows, bias3):
    n, t_new, rows_kv, hd = q24.shape
    past = cache_rows.shape[1]
    assert t_new == 4 and past % 512 == 0
    groups = past // 16
    cache6 = cache_rows.reshape(n, groups, 4, 4, rows_kv, hd)
    gpc = 512 // 16
    n_chunks = groups // gpc
    n_keys = bias3.shape[0]
    vec = pl.BlockSpec((None, t_new, rows_kv, hd), lambda i, c: (i, 0, 0, 0))
    chunk = pl.BlockSpec((None, gpc, 4, 4, rows_kv, hd), lambda i, c: (i, c, 0, 0, 0, 0))
    att, new_cache = pl.pallas_call(
        _attn_sample_kernel,
        grid=(n, n_chunks),
        in_specs=[vec, vec, vec, chunk,
                  pl.BlockSpec((None, None, None, 4, rows_kv, hd),
                               lambda i, c: (i, jnp.minimum((c + 1) * gpc, groups - 1), 0, 0, 0, 0)),
                  pl.BlockSpec((n_keys, rows_kv, hd), lambda i, c: (0, 0, 0))],
        out_specs=[vec, chunk],
        out_shape=[jax.ShapeDtypeStruct((n, t_new, rows_kv, hd), F32),
                   jax.ShapeDtypeStruct(cache6.shape, cache6.dtype)],
        scratch_shapes=[pltpu.VMEM((groups, 4, rows_kv, hd), F32)],
        compiler_params=pltpu.CompilerParams(
            dimension_semantics=("arbitrary", "arbitrary"), vmem_limit_bytes=56 * MIB),
        name="attn_sample",
    )(q24, g24, kvn24, cache6, cache6, bias3)
    return att, new_cache.reshape(cache_rows.shape)


def _prompt_bias_tables(rel_bias):
    nk = KEYS_PER_PATTERN
    qb = Q_BLOCK
    kdist = np.arange(-(qb - 1), qb + nk)
    n_dist = kdist.shape[0]
    valid = (kdist >= 0) & (kdist <= nk)
    tabs = []
    for _, dil in PATTERNS:
        bucket = _rel_bucket(jnp.asarray(np.clip(kdist, 0, nk) * dil, jnp.int32))
        per_dist = jnp.where(valid[:, None], rel_bias[bucket].astype(F32), NEG_INF)
        rev = per_dist.T[:, ::-1]
        skew = jnp.tile(rev, (1, qb + 1))[:, :qb * (n_dist + 1)].reshape(-1, qb, n_dist + 1)
        tabs.append(skew[:, ::-1, :qb + nk])
    return jnp.stack(tabs)


def _sample_bias_table(rel_bias):
    nk = KEYS_PER_PATTERN
    kk = np.arange(nk, -1, -1)
    parts = []
    for _, dil in PATTERNS:
        bucket = _rel_bucket(jnp.asarray(kk * dil, jnp.int32))
        parts.append(rel_bias[bucket].astype(F32))
    bias = jnp.concatenate(parts, axis=0)
    bias = jnp.stack([bias, jnp.zeros_like(bias)], axis=-1).reshape(bias.shape[0], -1)
    return jnp.broadcast_to(bias[:, :, None], bias.shape + (HEAD_DIM,))


def kernel(x_prompt, x_sample, cache_conv, cache_kv, rel_bias, norm_pre, w_in, conv_dw_w, conv_dw_b,
           conv_ln_g, conv_ln_b, conv_pw_w, conv_pw_b, w_out, norm_post):
    depth = w_in.shape[0]
    assert depth == 1
    bsz, seq, d_model = x_prompt.shape
    n_dec, t_new, _ = x_sample.shape
    n_heads = cache_kv.shape[4]
    d_attn = n_heads * HEAD_DIM
    d_conv = cache_conv.shape[-1]
    past = cache_kv.shape[2]
    assert past == MAX_WINDOW and seq >= MAX_WINDOW and t_new <= 4
    hist = CONV_WIDTH - 1

    w_in_bf = w_in[0].astype(BF16)
    w_out_bf = w_out[0].astype(BF16)
    pw_bf = conv_pw_w[0].astype(BF16)
    conv_args = (conv_dw_w[0], conv_dw_b[0], conv_ln_g[0], conv_ln_b[0], pw_bf, conv_pw_b[0])
    conv_col0 = 4 * d_attn

    xp2 = x_prompt.reshape(bsz * seq, d_model)
    zp = _inproj(xp2, norm_pre[0], w_in_bf, tm=1024, tn=512)
    zp3 = zp.reshape(bsz, seq, -1)
    mix_att_p = _attn_prompt(zp3, _prompt_bias_tables(rel_bias), n_heads)
    zero_prefix = jnp.zeros((bsz, hist, d_conv), F32)
    mix_conv_p, new_conv_p = _conv_branch(zp3, zero_prefix, *conv_args, col0=conv_col0)
    yp = _outproj(mix_att_p.reshape(bsz * seq, d_attn), mix_conv_p.reshape(bsz * seq, d_conv),
                  w_out_bf[:d_attn], w_out_bf[d_attn:], xp2, norm_post[0], tm=512)
    win = min(MAX_WINDOW, seq)
    new_kv_p = zp3[:, seq - win:, d_attn:3 * d_attn].reshape(1, bsz, win, 2, n_heads, HEAD_DIM)

    xs2 = x_sample.reshape(n_dec * t_new, d_model)
    zs = _inproj(xs2, norm_pre[0], w_in_bf, tm=n_dec * t_new, tn=512)
    zs3 = zs.reshape(n_dec, t_new, -1)
    def heads(col0):
        return zs3[:, :, col0:col0 + d_attn].reshape(n_dec, t_new, n_heads, HEAD_DIM)

    def interleave(even, odd):
        return jnp.stack([even, odd], axis=3).reshape(n_dec, t_new, 2 * n_heads, HEAD_DIM)

    zero_h = jnp.zeros((n_dec, t_new, n_heads, HEAD_DIM), F32)
    q24 = interleave(heads(0), zero_h)
    g24 = interleave(zero_h, heads(3 * d_attn))
    kvn24 = interleave(heads(d_attn), heads(2 * d_attn))
    cache_rows = cache_kv[0].transpose(0, 1, 3, 2, 4).reshape(n_dec, past, 2 * n_heads, HEAD_DIM)
    att_s, new_rows = _attn_sample(q24, g24, kvn24, cache_rows, _sample_bias_table(rel_bias))
    new_kv_s = new_rows.reshape(n_dec, past, n_heads, 2, HEAD_DIM).transpose(0, 1, 3, 2, 4)[None]
    mix_att_s = att_s[:, :, 1::2].reshape(n_dec * t_new, d_attn).astype(BF16)
    mix_conv_s, new_conv_s = _conv_branch(zs3, cache_conv[0], *conv_args, col0=conv_col0)
    ys = _outproj(mix_att_s, mix_conv_s.reshape(n_dec * t_new, d_conv),
                  w_out_bf[:d_attn], w_out_bf[d_attn:], xs2, norm_post[0], tm=n_dec * t_new)

    return (yp.reshape(bsz, seq, d_model), ys.reshape(n_dec, t_new, d_model),
            new_conv_p[None], new_kv_p, new_conv_s[None], new_kv_s)
```

```python
import functools
import math

import jax
import jax.numpy as jnp
import numpy as np
from jax import lax
from jax.experimental import pallas as pl
from jax.experimental.pallas import tpu as pltpu

F32 = jnp.float32
BF16 = jnp.bfloat16

HEAD_DIM = 128
PATTERNS = ((128, 1), (512, 4), (2048, 16))
MAX_WINDOW = 2048
Q_BLOCK = 128
KEYS_PER_PATTERN = 128
CONV_WIDTH = 31
N_BUCKETS = 32
MAX_EXACT = 16
EPS = 1e-6
NEG_INF = -1e30
SUBLANES = 8
HEAD_PAD = 16

MIB = 1024 * 1024


def _rel_bucket(dist):
    d = jnp.maximum(dist, 1).astype(F32)
    log_b = MAX_EXACT + (jnp.log(d / MAX_EXACT) / math.log(MAX_WINDOW / MAX_EXACT)
                         * (N_BUCKETS - MAX_EXACT)).astype(jnp.int32)
    log_b = jnp.minimum(log_b, N_BUCKETS - 1)
    return jnp.where(dist < MAX_EXACT, dist, log_b)


def _round_up(x, m):
    return -(-x // m) * m


def _silu(x):
    return x * jax.nn.sigmoid(x)


def _inproj_kernel(x_ref, g_ref, w_ref, z_ref, h_ref):
    @pl.when(pl.program_id(1) == 0)
    def _():
        x = x_ref[...]
        ms = jnp.mean(x * x, axis=-1, keepdims=True)
        h_ref[...] = (x * lax.rsqrt(ms + EPS) * g_ref[...]).astype(BF16)

    z_ref[...] = jnp.dot(h_ref[...], w_ref[...], preferred_element_type=F32)


def _inproj(x2d, norm_g, w_bf, tm, tn):
    m, d = x2d.shape
    n = w_bf.shape[1]
    return pl.pallas_call(
        _inproj_kernel,
        grid=(m // tm, n // tn),
        in_specs=[pl.BlockSpec((tm, d), lambda i, j: (i, 0)),
                  pl.BlockSpec((1, d), lambda i, j: (0, 0)),
                  pl.BlockSpec((d, tn), lambda i, j: (0, j))],
        out_specs=pl.BlockSpec((tm, tn), lambda i, j: (i, j)),
        out_shape=jax.ShapeDtypeStruct((m, n), F32),
        scratch_shapes=[pltpu.VMEM((tm, d), BF16)],
        compiler_params=pltpu.CompilerParams(
            dimension_semantics=("parallel", "arbitrary"), vmem_limit_bytes=48 * MIB),
        name="inproj",
    )(x2d, norm_g.reshape(1, d), w_bf)


def _kv_rows_kernel(k_ref, v_ref, o_ref):
    tm = k_ref.shape[0]
    hd = o_ref.shape[1]
    n_heads = k_ref.shape[1] // hd
    for h in range(n_heads):
        o_ref[pl.ds(2 * h, tm, stride=2 * n_heads), :] = k_ref[:, h * hd:(h + 1) * hd]
        o_ref[pl.ds(2 * h + 1, tm, stride=2 * n_heads), :] = v_ref[:, h * hd:(h + 1) * hd]


def _kv_rows(z, d_attn, tm):
    m = z.shape[0]
    rows_kv = 2 * d_attn // HEAD_DIM
    return pl.pallas_call(
        _kv_rows_kernel,
        grid=(m // tm,),
        in_specs=[pl.BlockSpec((tm, d_attn), lambda i: (i, 1)),
                  pl.BlockSpec((tm, d_attn), lambda i: (i, 2))],
        out_specs=pl.BlockSpec((tm * rows_kv, HEAD_DIM), lambda i: (i, 0)),
        out_shape=jax.ShapeDtypeStruct((m * rows_kv, HEAD_DIM), F32),
        compiler_params=pltpu.CompilerParams(
            dimension_semantics=("parallel",), vmem_limit_bytes=48 * MIB),
        name="kv_rows",
    )(z, z)


def _attn_block(qb, kw, vw, tab, scale):
    s = lax.dot_general(qb.astype(BF16), kw.astype(BF16), (((1,), (1,)), ((), ())),
                        preferred_element_type=F32)
    s = s * scale + tab
    m = jnp.max(s, axis=-1, keepdims=True)
    p = jnp.exp(s - m).astype(BF16)
    v_ones = jnp.concatenate([vw.astype(BF16), jnp.ones(vw.shape, BF16)], axis=1)
    acc_l = jnp.dot(p, v_ones, preferred_element_type=F32)
    d = vw.shape[1]
    return acc_l[:, :d], m, acc_l[:, d:]


def _attn_prompt_kernel(q_ref, k_ref, v_ref, g_ref, tab_ref, o_ref, acc_ref, m_ref, l_ref):
    seq = q_ref.shape[0]
    scale = HEAD_DIM ** -0.5
    qb_rows = Q_BLOCK
    nk = KEYS_PER_PATTERN

    def rows(ref, start, size, stride):
        if stride == 1:
            return ref[pl.ds(start, size), :]
        return ref[pl.ds(start, size, stride=stride), :]

    def put(p, start, stride, acc, m, l):
        lanes = acc.shape[-1]
        if stride == 1:
            idx = pl.ds(start, qb_rows)
        else:
            idx = pl.ds(start, qb_rows, stride=stride)
        acc_ref[p, idx, :] = acc
        m_ref[p, idx, :] = jnp.broadcast_to(m, (qb_rows, lanes))
        l_ref[p, idx, :] = l

    def first_block(p, phase, dil):
        tab = tab_ref[p][:, nk:]
        qb = rows(q_ref, phase, qb_rows, dil)
        kw = rows(k_ref, phase, qb_rows, dil)
        vw = rows(v_ref, phase, qb_rows, dil)
        put(p, phase, dil, *_attn_block(qb, kw, vw, tab, scale))

    def later_block(p, phase, dil, n):
        tab = tab_ref[p]
        q0 = phase + dil * qb_rows * n
        k0 = q0 - dil * nk
        qb = rows(q_ref, q0, qb_rows, dil)
        kw = rows(k_ref, k0, qb_rows + nk, dil)
        vw = rows(v_ref, k0, qb_rows + nk, dil)
        put(p, q0, dil, *_attn_block(qb, kw, vw, tab, scale))

    for p, (window, dil) in enumerate(PATTERNS):
        n_blocks = seq // dil // qb_rows
        for phase in range(dil):
            first_block(p, phase, dil)
            for n in range(1, n_blocks):
                later_block(p, phase, dil, n)

    chunk = 256

    def combine(c, carry):
        sl = pl.ds(pl.multiple_of(c * chunk, chunk), chunk)
        m0, m1, m2 = m_ref[0, sl, :], m_ref[1, sl, :], m_ref[2, sl, :]
        mm = jnp.maximum(jnp.maximum(m0, m1), m2)
        e0, e1, e2 = jnp.exp(m0 - mm), jnp.exp(m1 - mm), jnp.exp(m2 - mm)
        num = e0 * acc_ref[0, sl, :] + e1 * acc_ref[1, sl, :] + e2 * acc_ref[2, sl, :]
        den = e0 * l_ref[0, sl, :] + e1 * l_ref[1, sl, :] + e2 * l_ref[2, sl, :]
        o_ref[sl, :] = (num / den * _silu(g_ref[sl, :])).astype(o_ref.dtype)
        return carry
    lax.fori_loop(0, seq // chunk, combine, 0)


def _attn_prompt(z3, tabs, n_heads):
    b, seq, _ = z3.shape
    hd = HEAD_DIM

    def col(off):
        return pl.BlockSpec((None, seq, hd), lambda i, h: (i, 0, off + h))

    return pl.pallas_call(
        _attn_prompt_kernel,
        grid=(b, n_heads),
        in_specs=[col(0), col(n_heads), col(2 * n_heads), col(3 * n_heads),
                  pl.BlockSpec((len(PATTERNS), None, Q_BLOCK, Q_BLOCK + KEYS_PER_PATTERN),
                               lambda i, h: (0, h, 0, 0))],
        out_specs=pl.BlockSpec((None, seq, hd), lambda i, h: (i, 0, h)),
        out_shape=jax.ShapeDtypeStruct((b, seq, n_heads * hd), BF16),
        scratch_shapes=[pltpu.VMEM((len(PATTERNS), seq, hd), F32)] * 3,
        compiler_params=pltpu.CompilerParams(
            dimension_semantics=("parallel", "parallel"), vmem_limit_bytes=48 * MIB),
        name="attn_prompt",
    )(z3, z3, z3, z3, tabs)


def _conv_kernel(ca_ref, cb_ref, gc_ref, pre_ref, dww_ref, dwb_ref, lng_ref, lnb_ref,
                 pww_ref, pwb_ref, o_ref, newc_ref, upad_ref, *, chunk):
    t_len = ca_ref.shape[0]
    hist = CONV_WIDTH - 1
    ca = ca_ref[...]
    u = ca * jax.nn.sigmoid(cb_ref[...])
    upad_ref[0:hist, :] = pre_ref[...]
    upad_ref[hist:hist + t_len, :] = u
    n_pad = upad_ref.shape[0] - (hist + t_len)
    upad_ref[hist + t_len:, :] = jnp.zeros((n_pad, ca_ref.shape[1]), F32)
    newc_ref[...] = upad_ref[t_len:t_len + hist, :]
    win_rows = upad_ref.shape[0] - t_len + chunk

    def body(c, carry):
        r0 = pl.multiple_of(c * chunk, SUBLANES) if chunk % SUBLANES == 0 else c * chunk
        win = upad_ref[pl.ds(r0, win_rows), :]
        y = jnp.zeros((chunk, ca_ref.shape[1]), F32) + dwb_ref[...]
        for s in range(SUBLANES):
            shifted = win[s:s + win_rows - SUBLANES]
            for a in range(-(-CONV_WIDTH // SUBLANES)):
                w = SUBLANES * a + s
                if w < CONV_WIDTH:
                    y = y + shifted[SUBLANES * a:SUBLANES * a + chunk] * dww_ref[w:w + 1, :]
        mu = jnp.mean(y, axis=-1, keepdims=True)
        var = jnp.mean(jnp.square(y - mu), axis=-1, keepdims=True)
        yn = (y - mu) * lax.rsqrt(var + EPS) * lng_ref[...] + lnb_ref[...]
        c_act = _silu(yn).astype(BF16)
        proj = jnp.dot(c_act, pww_ref[...], preferred_element_type=F32) + pwb_ref[...]
        o_ref[pl.ds(r0, chunk), :] = (proj * _silu(gc_ref[pl.ds(r0, chunk), :])).astype(o_ref.dtype)
        return carry
    lax.fori_loop(0, t_len // chunk, body, 0)


def _conv_branch(z3, prefix, dw_w, dw_b, ln_g, ln_b, pw_w_bf, pw_b, col0):
    n, t_len, _ = z3.shape
    c = prefix.shape[-1]
    hist = CONV_WIDTH - 1
    chunk = min(t_len, 64)
    cblk = col0 // c

    def zc(j):
        return pl.BlockSpec((None, t_len, c), lambda i: (i, 0, cblk + j))

    def vec():
        return pl.BlockSpec((1, c), lambda i: (0, 0))

    return pl.pallas_call(
        functools.partial(_conv_kernel, chunk=chunk),
        grid=(n,),
        in_specs=[zc(0), zc(1), zc(2),
                  pl.BlockSpec((None, hist, c), lambda i: (i, 0, 0)),
                  pl.BlockSpec((CONV_WIDTH, c), lambda i: (0, 0)),
                  vec(), vec(), vec(),
                  pl.BlockSpec((c, c), lambda i: (0, 0)),
                  vec()],
        out_specs=[pl.BlockSpec((None, t_len, c), lambda i: (i, 0, 0)),
                   pl.BlockSpec((None, hist, c), lambda i: (i, 0, 0))],
        out_shape=[jax.ShapeDtypeStruct((n, t_len, c), BF16),
                   jax.ShapeDtypeStruct((n, hist, c), F32)],
        scratch_shapes=[pltpu.VMEM((t_len - chunk + _round_up(chunk + CONV_WIDTH + 1, SUBLANES), c), F32)],
        compiler_params=pltpu.CompilerParams(
            dimension_semantics=("parallel",), vmem_limit_bytes=48 * MIB),
        name="conv_branch",
    )(z3, z3, z3, prefix, dw_w, dw_b.reshape(1, c), ln_g.reshape(1, c), ln_b.reshape(1, c),
      pw_w_bf, pw_b.reshape(1, c))


def _outproj_kernel(ma_ref, mc_ref, wa_ref, wc_ref, x_ref, g_ref, y_ref):
    y = jnp.dot(ma_ref[...], wa_ref[...], preferred_element_type=F32)
    y = y + jnp.dot(mc_ref[...], wc_ref[...], preferred_element_type=F32)
    ms = jnp.mean(y * y, axis=-1, keepdims=True)
    y_ref[...] = x_ref[...] + y * lax.rsqrt(ms + EPS) * g_ref[...]


def _outproj(mix_att, mix_conv, w_att_bf, w_conv_bf, x2d, norm_g, tm):
    m, d = x2d.shape
    da, dc = mix_att.shape[1], mix_conv.shape[1]
    return pl.pallas_call(
        _outproj_kernel,
        grid=(m // tm,),
        in_specs=[pl.BlockSpec((tm, da), lambda i: (i, 0)),
                  pl.BlockSpec((tm, dc), lambda i: (i, 0)),
                  pl.BlockSpec((da, d), lambda i: (0, 0)),
                  pl.BlockSpec((dc, d), lambda i: (0, 0)),
                  pl.BlockSpec((tm, d), lambda i: (i, 0)),
                  pl.BlockSpec((1, d), lambda i: (0, 0))],
        out_specs=pl.BlockSpec((tm, d), lambda i: (i, 0)),
        out_shape=jax.ShapeDtypeStruct((m, d), F32),
        compiler_params=pltpu.CompilerParams(
            dimension_semantics=("parallel",), vmem_limit_bytes=48 * MIB),
        name="outproj",
    )(mix_att, mix_conv, w_att_bf, w_conv_bf, x2d, norm_g.reshape(1, d))


def _attn_sample_kernel(q_ref, g_ref, kvn_prev_ref, kvn_ref, blk_ref, nxt_ref, bias_ref,
                        o_ref, out_ref, far_ref, near_ref):
    i = pl.program_id(0)
    c = pl.program_id(1)
    n_seq = pl.num_programs(0) - 1
    last = pl.num_programs(1) - 1
    t_new, rows_kv, hd = q_ref.shape
    gpc = blk_ref.shape[0]
    nk = KEYS_PER_PATTERN
    slot = lax.rem(i, 2)

    def attend(t):
        scale = HEAD_DIM ** -0.5
        ones = jnp.ones((hd, hd), BF16)
        tail_groups = nk // 16
        bias = bias_ref[...]
        n_keys = bias.shape[0]

        def key_to_value_sublane(x):
            shp = x.shape
            x = x.reshape(-1, SUBLANES, hd)
            return pltpu.roll(x, 1, 1).reshape(shp)

        new_rows = kvn_prev_ref[...]
        dense = jnp.concatenate(
            [near_ref[gpc - tail_groups:].reshape(nk, rows_kv, hd), new_rows], axis=0)
        own = new_rows[t:t + 1]
        rows3 = jnp.concatenate(
            [dense[t:t + nk + 1],
             near_ref[:, :, t].reshape(nk, rows_kv, hd), own,
             far_ref[1 - slot, :, t], own], axis=0)
        prod = (rows3 * q_ref[t][None]).reshape(n_keys * rows_kv, hd)
        s = jnp.dot(prod.astype(BF16), ones, preferred_element_type=F32)
        s = s.reshape(n_keys, rows_kv, hd) * scale + bias
        m = jnp.max(s, axis=0)
        p = jnp.exp(s - m[None])
        l = jnp.sum(p, axis=0)
        acc = jnp.sum(key_to_value_sublane(p) * rows3, axis=0)
        o_ref[t] = acc / key_to_value_sublane(l) * _silu(g_ref[t])

    for t in range(t_new):
        pl.when((i > 0) & (c == t))(functools.partial(attend, t))

    @pl.when(i < n_seq)
    def _():
        out_ref[:, 0:3] = blk_ref[:, 1:4]
        out_ref[0:gpc - 1, 3] = blk_ref[1:gpc, 0]
        far_ref[slot, pl.ds(c * gpc, gpc)] = blk_ref[:, 0]

        @pl.when(c < last)
        def _():
            out_ref[gpc - 1, 3] = nxt_ref[...]

        @pl.when(c == last)
        def _():
            out_ref[gpc - 1, 3] = kvn_ref[...]
            near_ref[...] = blk_ref[...]


def _attn_sample(q24, g24, kvn24, cache_rows, bias3):
    n, t_new, rows_kv, hd = q24.shape
    past = cache_rows.shape[1]
    assert past % 512 == 0
    groups = past // 16
    cache6 = cache_rows.reshape(n, groups, 4, 4, rows_kv, hd)
    gpc = 512 // 16
    n_chunks = groups // gpc
    assert t_new == 4 and n_chunks == t_new
    n_keys = bias3.shape[0]

    def seq(i):
        return jnp.minimum(i, n - 1)

    def prev(i):
        return jnp.maximum(i - 1, 0)

    def chunk_of(i, c):
        return jnp.where(i < n, c, n_chunks - 1)

    def after_chunk(i, c):
        return jnp.minimum((chunk_of(i, c) + 1) * gpc, groups - 1)

    vec_prev = pl.BlockSpec((None, t_new, rows_kv, hd), lambda i, c: (prev(i), 0, 0, 0))
    vec_cur = pl.BlockSpec((None, t_new, rows_kv, hd), lambda i, c: (seq(i), 0, 0, 0))
    chunk = pl.BlockSpec((None, gpc, 4, 4, rows_kv, hd),
                         lambda i, c: (seq(i), chunk_of(i, c), 0, 0, 0, 0))
    att, new_cache = pl.pallas_call(
        _attn_sample_kernel,
        grid=(n + 1, n_chunks),
        in_specs=[vec_prev, vec_prev, vec_prev, vec_cur, chunk,
                  pl.BlockSpec((None, None, None, 4, rows_kv, hd),
                               lambda i, c: (seq(i), after_chunk(i, c), 0, 0, 0, 0)),
                  pl.BlockSpec((n_keys, rows_kv, hd), lambda i, c: (0, 0, 0))],
        out_specs=[vec_prev, chunk],
        out_shape=[jax.ShapeDtypeStruct((n, t_new, rows_kv, hd), F32),
                   jax.ShapeDtypeStruct(cache6.shape, cache6.dtype)],
        scratch_shapes=[pltpu.VMEM((2, groups, 4, rows_kv, hd), F32),
                        pltpu.VMEM((gpc, 4, 4, rows_kv, hd), F32)],
        compiler_params=pltpu.CompilerParams(
            dimension_semantics=("arbitrary", "arbitrary"), vmem_limit_bytes=60 * MIB),
        name="attn_sample",
    )(q24, g24, kvn24, kvn24, cache6, cache6, bias3)
    return att, new_cache.reshape(cache_rows.shape)


def _prompt_bias_tables(rel_bias):
    nk = KEYS_PER_PATTERN
    qb = Q_BLOCK
    kdist = np.arange(-(qb - 1), qb + nk)
    n_dist = kdist.shape[0]
    valid = (kdist >= 0) & (kdist <= nk)
    tabs = []
    for _, dil in PATTERNS:
        bucket = _rel_bucket(jnp.asarray(np.clip(kdist, 0, nk) * dil, jnp.int32))
        per_dist = jnp.where(valid[:, None], rel_bias[bucket].astype(F32), NEG_INF)
        rev = per_dist.T[:, ::-1]
        skew = jnp.tile(rev, (1, qb + 1))[:, qb - 1:qb - 1 + qb * (n_dist - 1)]
        tabs.append(skew.reshape(-1, qb, n_dist - 1)[:, :, :qb + nk])
    return jnp.stack(tabs)


def _sample_bias_table(rel_bias):
    nk = KEYS_PER_PATTERN
    kk = np.arange(nk, -1, -1)
    parts = []
    for _, dil in PATTERNS:
        bucket = _rel_bucket(jnp.asarray(kk * dil, jnp.int32))
        parts.append(rel_bias[bucket].astype(F32))
    bias = jnp.concatenate(parts, axis=0)
    bias = jnp.stack([bias, jnp.zeros_like(bias)], axis=-1).reshape(bias.shape[0], -1)
    return jnp.broadcast_to(bias[:, :, None], bias.shape + (HEAD_DIM,))


def kernel(x_prompt, x_sample, cache_conv, cache_kv, rel_bias, norm_pre, w_in, conv_dw_w, conv_dw_b,
           conv_ln_g, conv_ln_b, conv_pw_w, conv_pw_b, w_out, norm_post):
    depth = w_in.shape[0]
    assert depth == 1
    bsz, seq, d_model = x_prompt.shape
    n_dec, t_new, _ = x_sample.shape
    n_heads = cache_kv.shape[4]
    d_attn = n_heads * HEAD_DIM
    d_conv = cache_conv.shape[-1]
    past = cache_kv.shape[2]
    assert past == MAX_WINDOW and seq >= MAX_WINDOW and t_new <= 4
    hist = CONV_WIDTH - 1

    w_in_bf = w_in[0].astype(BF16)
    w_out_bf = w_out[0].astype(BF16)
    pw_bf = conv_pw_w[0].astype(BF16)
    conv_args = (conv_dw_w[0], conv_dw_b[0], conv_ln_g[0], conv_ln_b[0], pw_bf, conv_pw_b[0])
    conv_col0 = 4 * d_attn

    xp2 = x_prompt.reshape(bsz * seq, d_model)
    zp = _inproj(xp2, norm_pre[0], w_in_bf, tm=1024, tn=512)
    zp3 = zp.reshape(bsz, seq, -1)
    mix_att_p = _attn_prompt(zp3, _prompt_bias_tables(rel_bias), n_heads)
    zero_prefix = jnp.zeros((bsz, hist, d_conv), F32)
    mix_conv_p, new_conv_p = _conv_branch(zp3, zero_prefix, *conv_args, col0=conv_col0)
    yp = _outproj(mix_att_p.reshape(bsz * seq, d_attn), mix_conv_p.reshape(bsz * seq, d_conv),
                  w_out_bf[:d_attn], w_out_bf[d_attn:], xp2, norm_post[0], tm=512)
    win = min(MAX_WINDOW, seq)
    kv_rows_p = _kv_rows(zp, d_attn, tm=512).reshape(bsz, seq, n_heads, 2, HEAD_DIM)
    new_kv_p = kv_rows_p[:, seq - win:].transpose(0, 1, 3, 2, 4)[None]

    xs2 = x_sample.reshape(n_dec * t_new, d_model)
    zs = _inproj(xs2, norm_pre[0], w_in_bf, tm=n_dec * t_new, tn=512)
    zs3 = zs.reshape(n_dec, t_new, -1)
    def heads(col0):
        return zs3[:, :, col0:col0 + d_attn].reshape(n_dec, t_new, n_heads, HEAD_DIM)

    def interleave(even, odd):
        return jnp.stack([even, odd], axis=3).reshape(n_dec, t_new, 2 * n_heads, HEAD_DIM)

    zero_h = jnp.zeros((n_dec, t_new, n_heads, HEAD_DIM), F32)
    q24 = interleave(heads(0), zero_h)
    g24 = interleave(zero_h, heads(3 * d_attn))
    kvn24 = interleave(heads(d_attn), heads(2 * d_attn))
    cache_rows = cache_kv[0].transpose(0, 1, 3, 2, 4).reshape(n_dec, past, 2 * n_heads, HEAD_DIM)
    att_s, new_rows = _attn_sample(q24, g24, kvn24, cache_rows, _sample_bias_table(rel_bias))
    new_kv_s = new_rows.reshape(n_dec, past, n_heads, 2, HEAD_DIM).transpose(0, 1, 3, 2, 4)[None]
    mix_att_s = att_s[:, :, 1::2].reshape(n_dec * t_new, d_attn).astype(BF16)
    mix_conv_s, new_conv_s = _conv_branch(zs3, cache_conv[0], *conv_args, col0=conv_col0)
    ys = _outproj(mix_att_s, mix_conv_s.reshape(n_dec * t_new, d_conv),
                  w_out_bf[:d_attn], w_out_bf[d_attn:], xs2, norm_post[0], tm=n_dec * t_new)

    return (yp.reshape(bsz, seq, d_model), ys.reshape(n_dec, t_new, d_model),
            new_conv_p[None], new_kv_p, new_conv_s[None], new_kv_s)
```

```python
import functools
import math

import jax
import jax.numpy as jnp
import numpy as np
from jax import lax
from jax.experimental import pallas as pl
from jax.experimental.pallas import tpu as pltpu

F32 = jnp.float32
BF16 = jnp.bfloat16

HEAD_DIM = 128
PATTERNS = ((128, 1), (512, 4), (2048, 16))
MAX_WINDOW = 2048
Q_BLOCK = 128
KEYS_PER_PATTERN = 128
CONV_WIDTH = 31
N_BUCKETS = 32
MAX_EXACT = 16
EPS = 1e-6
NEG_INF = -1e30
SUBLANES = 8
HEAD_PAD = 16

MIB = 1024 * 1024


def _rel_bucket(dist):
    d = jnp.maximum(dist, 1).astype(F32)
    log_b = MAX_EXACT + (jnp.log(d / MAX_EXACT) / math.log(MAX_WINDOW / MAX_EXACT)
                         * (N_BUCKETS - MAX_EXACT)).astype(jnp.int32)
    log_b = jnp.minimum(log_b, N_BUCKETS - 1)
    return jnp.where(dist < MAX_EXACT, dist, log_b)


def _round_up(x, m):
    return -(-x // m) * m


def _silu(x):
    return x * jax.nn.sigmoid(x)


def _inproj_kernel(x_ref, g_ref, w_ref, z_ref, h_ref):
    @pl.when(pl.program_id(1) == 0)
    def _():
        x = x_ref[...]
        ms = jnp.mean(x * x, axis=-1, keepdims=True)
        h_ref[...] = (x * lax.rsqrt(ms + EPS) * g_ref[...]).astype(BF16)

    z_ref[...] = jnp.dot(h_ref[...], w_ref[...], preferred_element_type=F32)


def _inproj(x2d, norm_g, w_bf, tm, tn):
    m, d = x2d.shape
    n = w_bf.shape[1]
    return pl.pallas_call(
        _inproj_kernel,
        grid=(m // tm, n // tn),
        in_specs=[pl.BlockSpec((tm, d), lambda i, j: (i, 0)),
                  pl.BlockSpec((1, d), lambda i, j: (0, 0)),
                  pl.BlockSpec((d, tn), lambda i, j: (0, j))],
        out_specs=pl.BlockSpec((tm, tn), lambda i, j: (i, j)),
        out_shape=jax.ShapeDtypeStruct((m, n), F32),
        scratch_shapes=[pltpu.VMEM((tm, d), BF16)],
        compiler_params=pltpu.CompilerParams(
            dimension_semantics=("parallel", "arbitrary"), vmem_limit_bytes=48 * MIB),
        name="inproj",
    )(x2d, norm_g.reshape(1, d), w_bf)


def _kv_rows_kernel(k_ref, v_ref, o_ref):
    tm = k_ref.shape[0]
    hd = o_ref.shape[1]
    n_heads = k_ref.shape[1] // hd
    for h in range(n_heads):
        o_ref[pl.ds(2 * h, tm, stride=2 * n_heads), :] = k_ref[:, h * hd:(h + 1) * hd]
        o_ref[pl.ds(2 * h + 1, tm, stride=2 * n_heads), :] = v_ref[:, h * hd:(h + 1) * hd]


def _kv_rows(z, d_attn, tm):
    m = z.shape[0]
    rows_kv = 2 * d_attn // HEAD_DIM
    return pl.pallas_call(
        _kv_rows_kernel,
        grid=(m // tm,),
        in_specs=[pl.BlockSpec((tm, d_attn), lambda i: (i, 1)),
                  pl.BlockSpec((tm, d_attn), lambda i: (i, 2))],
        out_specs=pl.BlockSpec((tm * rows_kv, HEAD_DIM), lambda i: (i, 0)),
        out_shape=jax.ShapeDtypeStruct((m * rows_kv, HEAD_DIM), F32),
        compiler_params=pltpu.CompilerParams(
            dimension_semantics=("parallel",), vmem_limit_bytes=48 * MIB),
        name="kv_rows",
    )(z, z)


def _attn_block(qb, kw, vw, tab, scale):
    s = lax.dot_general(qb.astype(BF16), kw.astype(BF16), (((1,), (1,)), ((), ())),
                        preferred_element_type=F32)
    s = s * scale + tab
    m = jnp.max(s, axis=-1, keepdims=True)
    p = jnp.exp(s - m).astype(BF16)
    v_ones = jnp.concatenate([vw.astype(BF16), jnp.ones(vw.shape, BF16)], axis=1)
    acc_l = jnp.dot(p, v_ones, preferred_element_type=F32)
    d = vw.shape[1]
    return acc_l[:, :d], m, acc_l[:, d:]


def _attn_prompt_kernel(q_ref, k_ref, v_ref, g_ref, tab_ref, o_ref, acc_ref, m_ref, l_ref):
    seq = q_ref.shape[0]
    scale = HEAD_DIM ** -0.5
    qb_rows = Q_BLOCK
    nk = KEYS_PER_PATTERN

    def rows(ref, start, size, stride):
        if stride == 1:
            return ref[pl.ds(start, size), :]
        return ref[pl.ds(start, size, stride=stride), :]

    def put(p, start, stride, acc, m, l):
        lanes = acc.shape[-1]
        if stride == 1:
            idx = pl.ds(start, qb_rows)
        else:
            idx = pl.ds(start, qb_rows, stride=stride)
        acc_ref[p, idx, :] = acc
        m_ref[p, idx, :] = jnp.broadcast_to(m, (qb_rows, lanes))
        l_ref[p, idx, :] = l

    def first_block(p, phase, dil):
        tab = tab_ref[p][:, nk:]
        qb = rows(q_ref, phase, qb_rows, dil)
        kw = rows(k_ref, phase, qb_rows, dil)
        vw = rows(v_ref, phase, qb_rows, dil)
        put(p, phase, dil, *_attn_block(qb, kw, vw, tab, scale))

    def later_block(p, phase, dil, n):
        tab = tab_ref[p]
        q0 = phase + dil * qb_rows * n
        k0 = q0 - dil * nk
        qb = rows(q_ref, q0, qb_rows, dil)
        kw = rows(k_ref, k0, qb_rows + nk, dil)
        vw = rows(v_ref, k0, qb_rows + nk, dil)
        put(p, q0, dil, *_attn_block(qb, kw, vw, tab, scale))

    for p, (window, dil) in enumerate(PATTERNS):
        n_blocks = seq // dil // qb_rows
        for phase in range(dil):
            first_block(p, phase, dil)
            for n in range(1, n_blocks):
                later_block(p, phase, dil, n)

    chunk = 256

    def combine(c, carry):
        sl = pl.ds(pl.multiple_of(c * chunk, chunk), chunk)
        m0, m1, m2 = m_ref[0, sl, :], m_ref[1, sl, :], m_ref[2, sl, :]
        mm = jnp.maximum(jnp.maximum(m0, m1), m2)
        e0, e1, e2 = jnp.exp(m0 - mm), jnp.exp(m1 - mm), jnp.exp(m2 - mm)
        num = e0 * acc_ref[0, sl, :] + e1 * acc_ref[1, sl, :] + e2 * acc_ref[2, sl, :]
        den = e0 * l_ref[0, sl, :] + e1 * l_ref[1, sl, :] + e2 * l_ref[2, sl, :]
        o_ref[sl, :] = (num / den * _silu(g_ref[sl, :])).astype(o_ref.dtype)
        return carry
    lax.fori_loop(0, seq // chunk, combine, 0)


def _attn_prompt(z3, tabs, n_heads):
    b, seq, _ = z3.shape
    hd = HEAD_DIM

    def col(off):
        return pl.BlockSpec((None, seq, hd), lambda i, h: (i, 0, off + h))

    return pl.pallas_call(
        _attn_prompt_kernel,
        grid=(b, n_heads),
        in_specs=[col(0), col(n_heads), col(2 * n_heads), col(3 * n_heads),
                  pl.BlockSpec((len(PATTERNS), None, Q_BLOCK, Q_BLOCK + KEYS_PER_PATTERN),
                               lambda i, h: (0, h, 0, 0))],
        out_specs=pl.BlockSpec((None, seq, hd), lambda i, h: (i, 0, h)),
        out_shape=jax.ShapeDtypeStruct((b, seq, n_heads * hd), BF16),
        scratch_shapes=[pltpu.VMEM((len(PATTERNS), seq, hd), F32)] * 3,
        compiler_params=pltpu.CompilerParams(
            dimension_semantics=("parallel", "parallel"), vmem_limit_bytes=48 * MIB),
        name="attn_prompt",
    )(z3, z3, z3, z3, tabs)


def _conv_kernel(ca_ref, cb_ref, gc_ref, pre_ref, dww_ref, dwb_ref, lng_ref, lnb_ref,
                 pww_ref, pwb_ref, o_ref, newc_ref, upad_ref, *, chunk):
    t_len = ca_ref.shape[0]
    hist = CONV_WIDTH - 1
    ca = ca_ref[...]
    u = ca * jax.nn.sigmoid(cb_ref[...])
    upad_ref[0:hist, :] = pre_ref[...]
    upad_ref[hist:hist + t_len, :] = u
    n_pad = upad_ref.shape[0] - (hist + t_len)
    upad_ref[hist + t_len:, :] = jnp.zeros((n_pad, ca_ref.shape[1]), F32)
    newc_ref[...] = upad_ref[t_len:t_len + hist, :]
    win_rows = upad_ref.shape[0] - t_len + chunk

    def body(c, carry):
        r0 = pl.multiple_of(c * chunk, SUBLANES) if chunk % SUBLANES == 0 else c * chunk
        win = upad_ref[pl.ds(r0, win_rows), :]
        y = jnp.zeros((chunk, ca_ref.shape[1]), F32) + dwb_ref[...]
        for s in range(SUBLANES):
            shifted = win[s:s + win_rows - SUBLANES]
            for a in range(-(-CONV_WIDTH // SUBLANES)):
                w = SUBLANES * a + s
                if w < CONV_WIDTH:
                    y = y + shifted[SUBLANES * a:SUBLANES * a + chunk] * dww_ref[w:w + 1, :]
        mu = jnp.mean(y, axis=-1, keepdims=True)
        var = jnp.mean(jnp.square(y - mu), axis=-1, keepdims=True)
        yn = (y - mu) * lax.rsqrt(var + EPS) * lng_ref[...] + lnb_ref[...]
        c_act = _silu(yn).astype(BF16)
        proj = jnp.dot(c_act, pww_ref[...], preferred_element_type=F32) + pwb_ref[...]
        o_ref[pl.ds(r0, chunk), :] = (proj * _silu(gc_ref[pl.ds(r0, chunk), :])).astype(o_ref.dtype)
        return carry
    lax.fori_loop(0, t_len // chunk, body, 0)


def _conv_branch(z3, prefix, dw_w, dw_b, ln_g, ln_b, pw_w_bf, pw_b, col0):
    n, t_len, _ = z3.shape
    c = prefix.shape[-1]
    hist = CONV_WIDTH - 1
    chunk = min(t_len, 64)
    cblk = col0 // c

    def zc(j):
        return pl.BlockSpec((None, t_len, c), lambda i: (i, 0, cblk + j))

    def vec():
        return pl.BlockSpec((1, c), lambda i: (0, 0))

    return pl.pallas_call(
        functools.partial(_conv_kernel, chunk=chunk),
        grid=(n,),
        in_specs=[zc(0), zc(1), zc(2),
                  pl.BlockSpec((None, hist, c), lambda i: (i, 0, 0)),
                  pl.BlockSpec((CONV_WIDTH, c), lambda i: (0, 0)),
                  vec(), vec(), vec(),
                  pl.BlockSpec((c, c), lambda i: (0, 0)),
                  vec()],
        out_specs=[pl.BlockSpec((None, t_len, c), lambda i: (i, 0, 0)),
                   pl.BlockSpec((None, hist, c), lambda i: (i, 0, 0))],
        out_shape=[jax.ShapeDtypeStruct((n, t_len, c), BF16),
                   jax.ShapeDtypeStruct((n, hist, c), F32)],
        scratch_shapes=[pltpu.VMEM((t_len - chunk + _round_up(chunk + CONV_WIDTH + 1, SUBLANES), c), F32)],
        compiler_params=pltpu.CompilerParams(
            dimension_semantics=("parallel",), vmem_limit_bytes=48 * MIB),
        name="conv_branch",
    )(z3, z3, z3, prefix, dw_w, dw_b.reshape(1, c), ln_g.reshape(1, c), ln_b.reshape(1, c),
      pw_w_bf, pw_b.reshape(1, c))


def _outproj_kernel(ma_ref, mc_ref, wa_ref, wc_ref, x_ref, g_ref, y_ref):
    y = jnp.dot(ma_ref[...], wa_ref[...], preferred_element_type=F32)
    y = y + jnp.dot(mc_ref[...], wc_ref[...], preferred_element_type=F32)
    ms = jnp.mean(y * y, axis=-1, keepdims=True)
    y_ref[...] = x_ref[...] + y * lax.rsqrt(ms + EPS) * g_ref[...]


def _outproj(mix_att, mix_conv, w_att_bf, w_conv_bf, x2d, norm_g, tm):
    m, d = x2d.shape
    da, dc = mix_att.shape[1], mix_conv.shape[1]
    return pl.pallas_call(
        _outproj_kernel,
        grid=(m // tm,),
        in_specs=[pl.BlockSpec((tm, da), lambda i: (i, 0)),
                  pl.BlockSpec((tm, dc), lambda i: (i, 0)),
                  pl.BlockSpec((da, d), lambda i: (0, 0)),
                  pl.BlockSpec((dc, d), lambda i: (0, 0)),
                  pl.BlockSpec((tm, d), lambda i: (i, 0)),
                  pl.BlockSpec((1, d), lambda i: (0, 0))],
        out_specs=pl.BlockSpec((tm, d), lambda i: (i, 0)),
        out_shape=jax.ShapeDtypeStruct((m, d), F32),
        compiler_params=pltpu.CompilerParams(
            dimension_semantics=("parallel",), vmem_limit_bytes=48 * MIB),
        name="outproj",
    )(mix_att, mix_conv, w_att_bf, w_conv_bf, x2d, norm_g.reshape(1, d))


NEAR_ROWS = 512
FAR_STRIDE = 16


def _attn_sample_kernel(q_ref, k_ref, v_ref, g_ref, kvn_ref, blk_ref, nxt_ref, tab_ref, mult_ref,
                        o_ref, out_ref, far_ref, *, rows_kv):
    c = pl.program_id(1)
    last = pl.num_programs(1) - 1
    t_new = q_ref.shape[0]
    hd = blk_ref.shape[1]
    n_heads = rows_kv // 2
    chunk_sl = blk_ref.shape[0]
    shift_sl = t_new * rows_kv
    far_per_chunk = NEAR_ROWS // FAR_STRIDE

    out_ref[0:chunk_sl - shift_sl, :] = blk_ref[shift_sl:chunk_sl, :]

    @pl.when(c < last)
    def _():
        out_ref[chunk_sl - shift_sl:chunk_sl, :] = nxt_ref[...]

    @pl.when(c == last)
    def _():
        out_ref[chunk_sl - shift_sl:chunk_sl, :] = kvn_ref[...]

    for grp in range(far_per_chunk):
        dst = pl.multiple_of((c * far_per_chunk + grp) * shift_sl, SUBLANES)
        src = grp * FAR_STRIDE * rows_kv
        far_ref[pl.ds(dst, shift_sl), :] = blk_ref[src:src + shift_sl, :]

    @pl.when(c == last)
    def _():
        scale = HEAD_DIM ** -0.5
        pad_q = jnp.zeros((SUBLANES - t_new, hd), F32)
        pad_kv = jnp.zeros((hd - t_new, hd), F32)
        nt = (((1,), (1,)), ((), ()))
        n_far = far_ref.shape[0] // rows_kv
        mult = mult_ref[...]
        for h in range(n_heads):
            cols = slice(h * hd, (h + 1) * hd)

            def head_rows(ref, n, parity):
                return ref[pl.ds(2 * h + parity, n, stride=rows_kv), :].astype(BF16)

            q8 = jnp.concatenate([q_ref[:, cols], pad_q], axis=0).astype(BF16)
            k_new = jnp.concatenate([k_ref[:, cols], pad_kv], axis=0).astype(BF16)
            v_new = jnp.concatenate([v_ref[:, cols], pad_kv], axis=0).astype(BF16)
            s = jnp.concatenate(
                [lax.dot_general(q8, head_rows(blk_ref, NEAR_ROWS, 0), nt, preferred_element_type=F32),
                 lax.dot_general(q8, head_rows(far_ref, n_far, 0), nt, preferred_element_type=F32),
                 lax.dot_general(q8, k_new, nt, preferred_element_type=F32)], axis=1)
            s = s * scale + tab_ref[h]
            m = jnp.max(s, axis=-1, keepdims=True)
            p = jnp.exp(s - m) * mult
            l = jnp.sum(p, axis=-1, keepdims=True)
            pb = p.astype(BF16)
            acc = jnp.dot(pb[:, :NEAR_ROWS], head_rows(blk_ref, NEAR_ROWS, 1),
                          preferred_element_type=F32)
            acc = acc + jnp.dot(pb[:, NEAR_ROWS:NEAR_ROWS + n_far], head_rows(far_ref, n_far, 1),
                                preferred_element_type=F32)
            acc = acc + jnp.dot(pb[:, NEAR_ROWS + n_far:], v_new, preferred_element_type=F32)
            o_ref[:, cols] = (acc / l)[:t_new] * _silu(g_ref[:, cols])


def _attn_sample(zs3, kvn_rows, cache_rows, tab, mult, d_attn):
    n, t_new, _ = zs3.shape
    past, rows_kv, hd = cache_rows.shape[1:]
    assert past % NEAR_ROWS == 0 and t_new <= SUBLANES
    n_chunks = past // NEAR_ROWS
    chunk_sl = NEAR_ROWS * rows_kv
    shift_sl = t_new * rows_kv
    cache2 = cache_rows.reshape(n, past * rows_kv, hd)
    n_far = past // FAR_STRIDE * t_new

    def zcol(j):
        return pl.BlockSpec((None, t_new, d_attn), lambda i, c: (i, 0, j))

    chunk = pl.BlockSpec((None, chunk_sl, hd), lambda i, c: (i, c, 0))
    blocks_per_chunk = chunk_sl // shift_sl
    att, new_cache = pl.pallas_call(
        functools.partial(_attn_sample_kernel, rows_kv=rows_kv),
        grid=(n, n_chunks),
        in_specs=[zcol(0), zcol(1), zcol(2), zcol(3),
                  pl.BlockSpec((None, shift_sl, hd), lambda i, c: (i, 0, 0)),
                  chunk,
                  pl.BlockSpec((None, shift_sl, hd),
                               lambda i, c: (i, jnp.minimum(c + 1, n_chunks - 1) * blocks_per_chunk, 0)),
                  pl.BlockSpec(tab.shape, lambda i, c: (0, 0, 0)),
                  pl.BlockSpec(mult.shape, lambda i, c: (0, 0))],
        out_specs=[pl.BlockSpec((None, t_new, d_attn), lambda i, c: (i, 0, 0)), chunk],
        out_shape=[jax.ShapeDtypeStruct((n, t_new, d_attn), F32),
                   jax.ShapeDtypeStruct(cache2.shape, cache2.dtype)],
        scratch_shapes=[pltpu.VMEM((n_far * rows_kv, hd), F32)],
        compiler_params=pltpu.CompilerParams(
            dimension_semantics=("arbitrary", "arbitrary"), vmem_limit_bytes=48 * MIB),
        name="attn_sample",
    )(zs3, zs3, zs3, zs3, kvn_rows, cache2, cache2, tab, mult)
    return att, new_cache.reshape(cache_rows.shape)


def _prompt_bias_tables(rel_bias):
    nk = KEYS_PER_PATTERN
    qb = Q_BLOCK
    kdist = np.arange(-(qb - 1), qb + nk)
    n_dist = kdist.shape[0]
    valid = (kdist >= 0) & (kdist <= nk)
    tabs = []
    for _, dil in PATTERNS:
        bucket = _rel_bucket(jnp.asarray(np.clip(kdist, 0, nk) * dil, jnp.int32))
        per_dist = jnp.where(valid[:, None], rel_bias[bucket].astype(F32), NEG_INF)
        rev = per_dist.T[:, ::-1]
        skew = jnp.tile(rev, (1, qb + 1))[:, qb - 1:qb - 1 + qb * (n_dist - 1)]
        tabs.append(skew.reshape(-1, qb, n_dist - 1)[:, :, :qb + nk])
    return jnp.stack(tabs)


def _pattern_count(dist, patterns):
    return sum(((dist % dil == 0) & (dist >= 0) & (dist <= window)).astype(np.int32)
               for window, dil in patterns)


def _sample_tables(rel_bias, t_new, past):
    def bias_at(dist):
        return rel_bias[_rel_bucket(jnp.asarray(dist, jnp.int32))].astype(F32)

    def masked(bias, count):
        return jnp.where(jnp.asarray(count > 0)[..., None], bias, NEG_INF)

    near_pats, far_pats = PATTERNS[:2], PATTERNS[2:]
    assert near_pats[-1][0] == NEAR_ROWS and far_pats[0][1] == FAR_STRIDE and t_new <= far_pats[0][1]
    desc = np.arange(NEAR_ROWS + t_new - 1, 0, -1)
    desc_cnt = _pattern_count(desc, near_pats)
    desc_tab = masked(bias_at(desc), desc_cnt)
    starts = [t_new - 1 - t for t in range(t_new)]
    near_tab = jnp.stack([desc_tab[s0:s0 + NEAR_ROWS] for s0 in starts])
    near_cnt = np.stack([desc_cnt[s0:s0 + NEAR_ROWS] for s0 in starts])
    groups = past // FAR_STRIDE
    far_dist = past - FAR_STRIDE * np.arange(groups)
    own = np.eye(t_new, dtype=bool)[:, None, :] & (_pattern_count(far_dist, far_pats) > 0)[None, :, None]
    far_tab = jnp.where(jnp.asarray(own)[..., None], bias_at(far_dist)[None, :, None, :], NEG_INF)
    far_tab = far_tab.reshape(t_new, groups * t_new, -1)
    far_cnt = np.ones((t_new, groups * t_new), np.int32)
    tj = np.arange(t_new)[:, None] - np.arange(HEAD_DIM)[None, :]
    new_cnt = np.where(np.arange(HEAD_DIM)[None, :] < t_new, _pattern_count(tj, PATTERNS), 0)
    new_tab = masked(bias_at(np.clip(tj, 0, None).reshape(-1)).reshape(t_new, HEAD_DIM, -1), new_cnt)
    tab = jnp.concatenate([near_tab, far_tab, new_tab], axis=1).transpose(2, 0, 1)
    cnt = np.concatenate([near_cnt, far_cnt, new_cnt], axis=1)
    pad = SUBLANES - t_new
    tab = jnp.pad(tab, ((0, 0), (0, pad), (0, 0)))
    mult = np.pad(np.maximum(cnt, 1), ((0, pad), (0, 0)), constant_values=1).astype(np.float32)
    return tab, jnp.asarray(mult)


def kernel(x_prompt, x_sample, cache_conv, cache_kv, rel_bias, norm_pre, w_in, conv_dw_w, conv_dw_b,
           conv_ln_g, conv_ln_b, conv_pw_w, conv_pw_b, w_out, norm_post):
    depth = w_in.shape[0]
    assert depth == 1
    bsz, seq, d_model = x_prompt.shape
    n_dec, t_new, _ = x_sample.shape
    n_heads = cache_kv.shape[4]
    d_attn = n_heads * HEAD_DIM
    d_conv = cache_conv.shape[-1]
    past = cache_kv.shape[2]
    assert past == MAX_WINDOW and seq >= MAX_WINDOW and t_new <= 4
    hist = CONV_WIDTH - 1

    w_in_bf = w_in[0].astype(BF16)
    w_out_bf = w_out[0].astype(BF16)
    pw_bf = conv_pw_w[0].astype(BF16)
    conv_args = (conv_dw_w[0], conv_dw_b[0], conv_ln_g[0], conv_ln_b[0], pw_bf, conv_pw_b[0])
    conv_col0 = 4 * d_attn

    xp2 = x_prompt.reshape(bsz * seq, d_model)
    zp = _inproj(xp2, norm_pre[0], w_in_bf, tm=1024, tn=512)
    zp3 = zp.reshape(bsz, seq, -1)
    mix_att_p = _attn_prompt(zp3, _prompt_bias_tables(rel_bias), n_heads)
    zero_prefix = jnp.zeros((bsz, hist, d_conv), F32)
    mix_conv_p, new_conv_p = _conv_branch(zp3, zero_prefix, *conv_args, col0=conv_col0)
    yp = _outproj(mix_att_p.reshape(bsz * seq, d_attn), mix_conv_p.reshape(bsz * seq, d_conv),
                  w_out_bf[:d_attn], w_out_bf[d_attn:], xp2, norm_post[0], tm=512)
    win = min(MAX_WINDOW, seq)
    kv_rows_p = _kv_rows(zp, d_attn, tm=512).reshape(bsz, seq, n_heads, 2, HEAD_DIM)
    new_kv_p = kv_rows_p[:, seq - win:].transpose(0, 1, 3, 2, 4)[None]

    xs2 = x_sample.reshape(n_dec * t_new, d_model)
    zs = _inproj(xs2, norm_pre[0], w_in_bf, tm=n_dec * t_new, tn=512)
    zs3 = zs.reshape(n_dec, t_new, -1)
    def heads(col0):
        return zs3[:, :, col0:col0 + d_attn].reshape(n_dec, t_new, n_heads, HEAD_DIM)

    kvn_rows = jnp.stack([heads(d_attn), heads(2 * d_attn)], axis=3).reshape(
        n_dec, t_new * 2 * n_heads, HEAD_DIM)
    cache_rows = cache_kv[0].transpose(0, 1, 3, 2, 4).reshape(n_dec, past, 2 * n_heads, HEAD_DIM)
    tab_s, mult_s = _sample_tables(rel_bias, t_new, past)
    att_s, new_rows = _attn_sample(zs3, kvn_rows, cache_rows, tab_s, mult_s, d_attn)
    new_kv_s = new_rows.reshape(n_dec, past, n_heads, 2, HEAD_DIM).transpose(0, 1, 3, 2, 4)[None]
    mix_att_s = att_s.reshape(n_dec * t_new, d_attn).astype(BF16)
    mix_conv_s, new_conv_s = _conv_branch(zs3, cache_conv[0], *conv_args, col0=conv_col0)
    ys = _outproj(mix_att_s, mix_conv_s.reshape(n_dec * t_new, d_conv),
                  w_out_bf[:d_attn], w_out_bf[d_attn:], xs2, norm_post[0], tm=n_dec * t_new)

    return (yp.reshape(bsz, seq, d_model), ys.reshape(n_dec, t_new, d_model),
            new_conv_p[None], new_kv_p, new_conv_s[None], new_kv_s)
```

```python
import functools
import math

import jax
import jax.numpy as jnp
import numpy as np
from jax import lax
from jax.experimental import pallas as pl
from jax.experimental.pallas import tpu as pltpu

F32 = jnp.float32
BF16 = jnp.bfloat16

HEAD_DIM = 128
PATTERNS = ((128, 1), (512, 4), (2048, 16))
MAX_WINDOW = 2048
Q_BLOCK = 128
KEYS_PER_PATTERN = 128
CONV_WIDTH = 31
N_BUCKETS = 32
MAX_EXACT = 16
EPS = 1e-6
NEG_INF = -1e30
SUBLANES = 8
HEAD_PAD = 16

MIB = 1024 * 1024


def _rel_bucket(dist):
    d = jnp.maximum(dist, 1).astype(F32)
    log_b = MAX_EXACT + (jnp.log(d / MAX_EXACT) / math.log(MAX_WINDOW / MAX_EXACT)
                         * (N_BUCKETS - MAX_EXACT)).astype(jnp.int32)
    log_b = jnp.minimum(log_b, N_BUCKETS - 1)
    return jnp.where(dist < MAX_EXACT, dist, log_b)


def _round_up(x, m):
    return -(-x // m) * m


def _silu(x):
    return x * jax.nn.sigmoid(x)


def _inproj_kernel(x_ref, g_ref, w_ref, z_ref, h_ref):
    @pl.when(pl.program_id(1) == 0)
    def _():
        x = x_ref[...]
        ms = jnp.mean(x * x, axis=-1, keepdims=True)
        h_ref[...] = (x * lax.rsqrt(ms + EPS) * g_ref[...]).astype(BF16)

    z_ref[...] = jnp.dot(h_ref[...], w_ref[...], preferred_element_type=F32)


def _inproj(x2d, norm_g, w_bf, tm, tn):
    m, d = x2d.shape
    n = w_bf.shape[1]
    return pl.pallas_call(
        _inproj_kernel,
        grid=(m // tm, n // tn),
        in_specs=[pl.BlockSpec((tm, d), lambda i, j: (i, 0)),
                  pl.BlockSpec((1, d), lambda i, j: (0, 0)),
                  pl.BlockSpec((d, tn), lambda i, j: (0, j))],
        out_specs=pl.BlockSpec((tm, tn), lambda i, j: (i, j)),
        out_shape=jax.ShapeDtypeStruct((m, n), F32),
        scratch_shapes=[pltpu.VMEM((tm, d), BF16)],
        compiler_params=pltpu.CompilerParams(
            dimension_semantics=("parallel", "arbitrary"), vmem_limit_bytes=48 * MIB),
        name="inproj",
    )(x2d, norm_g.reshape(1, d), w_bf)


def _kv_rows_kernel(k_ref, v_ref, o_ref):
    tm = k_ref.shape[0]
    hd = o_ref.shape[1]
    n_heads = k_ref.shape[1] // hd
    for h in range(n_heads):
        o_ref[pl.ds(2 * h, tm, stride=2 * n_heads), :] = k_ref[:, h * hd:(h + 1) * hd]
        o_ref[pl.ds(2 * h + 1, tm, stride=2 * n_heads), :] = v_ref[:, h * hd:(h + 1) * hd]


def _kv_rows(z, d_attn, tm):
    m = z.shape[0]
    rows_kv = 2 * d_attn // HEAD_DIM
    return pl.pallas_call(
        _kv_rows_kernel,
        grid=(m // tm,),
        in_specs=[pl.BlockSpec((tm, d_attn), lambda i: (i, 1)),
                  pl.BlockSpec((tm, d_attn), lambda i: (i, 2))],
        out_specs=pl.BlockSpec((tm * rows_kv, HEAD_DIM), lambda i: (i, 0)),
        out_shape=jax.ShapeDtypeStruct((m * rows_kv, HEAD_DIM), F32),
        compiler_params=pltpu.CompilerParams(
            dimension_semantics=("parallel",), vmem_limit_bytes=48 * MIB),
        name="kv_rows",
    )(z, z)


def _attn_block(qb, kw, vw, tab, scale):
    s = lax.dot_general(qb.astype(BF16), kw.astype(BF16), (((1,), (1,)), ((), ())),
                        preferred_element_type=F32)
    s = s * scale + tab
    m = jnp.max(s, axis=-1, keepdims=True)
    p = jnp.exp(s - m).astype(BF16)
    v_ones = jnp.concatenate([vw.astype(BF16), jnp.ones(vw.shape, BF16)], axis=1)
    acc_l = jnp.dot(p, v_ones, preferred_element_type=F32)
    d = vw.shape[1]
    return acc_l[:, :d], m, acc_l[:, d:]


def _attn_prompt_kernel(q_ref, k_ref, v_ref, g_ref, tab_ref, o_ref, acc_ref, m_ref, l_ref):
    seq = q_ref.shape[0]
    scale = HEAD_DIM ** -0.5
    qb_rows = Q_BLOCK
    nk = KEYS_PER_PATTERN

    def rows(ref, start, size, stride):
        if stride == 1:
            return ref[pl.ds(start, size), :]
        return ref[pl.ds(start, size, stride=stride), :]

    def put(p, start, stride, acc, m, l):
        lanes = acc.shape[-1]
        if stride == 1:
            idx = pl.ds(start, qb_rows)
        else:
            idx = pl.ds(start, qb_rows, stride=stride)
        acc_ref[p, idx, :] = acc
        m_ref[p, idx, :] = jnp.broadcast_to(m, (qb_rows, lanes))
        l_ref[p, idx, :] = l

    def first_block(p, phase, dil):
        tab = tab_ref[p][:, nk:]
        qb = rows(q_ref, phase, qb_rows, dil)
        kw = rows(k_ref, phase, qb_rows, dil)
        vw = rows(v_ref, phase, qb_rows, dil)
        put(p, phase, dil, *_attn_block(qb, kw, vw, tab, scale))

    def later_block(p, phase, dil, n):
        tab = tab_ref[p]
        q0 = phase + dil * qb_rows * n
        k0 = q0 - dil * nk
        qb = rows(q_ref, q0, qb_rows, dil)
        kw = rows(k_ref, k0, qb_rows + nk, dil)
        vw = rows(v_ref, k0, qb_rows + nk, dil)
        put(p, q0, dil, *_attn_block(qb, kw, vw, tab, scale))

    for p, (window, dil) in enumerate(PATTERNS):
        n_blocks = seq // dil // qb_rows
        for phase in range(dil):
            first_block(p, phase, dil)
            for n in range(1, n_blocks):
                later_block(p, phase, dil, n)

    chunk = 256

    def combine(c, carry):
        sl = pl.ds(pl.multiple_of(c * chunk, chunk), chunk)
        m0, m1, m2 = m_ref[0, sl, :], m_ref[1, sl, :], m_ref[2, sl, :]
        mm = jnp.maximum(jnp.maximum(m0, m1), m2)
        e0, e1, e2 = jnp.exp(m0 - mm), jnp.exp(m1 - mm), jnp.exp(m2 - mm)
        num = e0 * acc_ref[0, sl, :] + e1 * acc_ref[1, sl, :] + e2 * acc_ref[2, sl, :]
        den = e0 * l_ref[0, sl, :] + e1 * l_ref[1, sl, :] + e2 * l_ref[2, sl, :]
        o_ref[sl, :] = (num / den * _silu(g_ref[sl, :])).astype(o_ref.dtype)
        return carry
    lax.fori_loop(0, seq // chunk, combine, 0)


def _attn_prompt(z3, tabs, n_heads):
    b, seq, _ = z3.shape
    hd = HEAD_DIM

    def col(off):
        return pl.BlockSpec((None, seq, hd), lambda i, h: (i, 0, off + h))

    return pl.pallas_call(
        _attn_prompt_kernel,
        grid=(b, n_heads),
        in_specs=[col(0), col(n_heads), col(2 * n_heads), col(3 * n_heads),
                  pl.BlockSpec((len(PATTERNS), None, Q_BLOCK, Q_BLOCK + KEYS_PER_PATTERN),
                               lambda i, h: (0, h, 0, 0))],
        out_specs=pl.BlockSpec((None, seq, hd), lambda i, h: (i, 0, h)),
        out_shape=jax.ShapeDtypeStruct((b, seq, n_heads * hd), BF16),
        scratch_shapes=[pltpu.VMEM((len(PATTERNS), seq, hd), F32)] * 3,
        compiler_params=pltpu.CompilerParams(
            dimension_semantics=("parallel", "parallel"), vmem_limit_bytes=48 * MIB),
        name="attn_prompt",
    )(z3, z3, z3, z3, tabs)


def _conv_kernel(ca_ref, cb_ref, gc_ref, pre_ref, dww_ref, dwb_ref, lng_ref, lnb_ref,
                 pww_ref, pwb_ref, o_ref, newc_ref, upad_ref, *, chunk):
    t_len = ca_ref.shape[0]
    hist = CONV_WIDTH - 1
    ca = ca_ref[...]
    u = ca * jax.nn.sigmoid(cb_ref[...])
    upad_ref[0:hist, :] = pre_ref[...]
    upad_ref[hist:hist + t_len, :] = u
    n_pad = upad_ref.shape[0] - (hist + t_len)
    upad_ref[hist + t_len:, :] = jnp.zeros((n_pad, ca_ref.shape[1]), F32)
    newc_ref[...] = upad_ref[t_len:t_len + hist, :]
    win_rows = upad_ref.shape[0] - t_len + chunk

    def body(c, carry):
        r0 = pl.multiple_of(c * chunk, SUBLANES) if chunk % SUBLANES == 0 else c * chunk
        win = upad_ref[pl.ds(r0, win_rows), :]
        y = jnp.zeros((chunk, ca_ref.shape[1]), F32) + dwb_ref[...]
        for s in range(SUBLANES):
            shifted = win[s:s + win_rows - SUBLANES]
            for a in range(-(-CONV_WIDTH // SUBLANES)):
                w = SUBLANES * a + s
                if w < CONV_WIDTH:
                    y = y + shifted[SUBLANES * a:SUBLANES * a + chunk] * dww_ref[w:w + 1, :]
        mu = jnp.mean(y, axis=-1, keepdims=True)
        var = jnp.mean(jnp.square(y - mu), axis=-1, keepdims=True)
        yn = (y - mu) * lax.rsqrt(var + EPS) * lng_ref[...] + lnb_ref[...]
        c_act = _silu(yn).astype(BF16)
        proj = jnp.dot(c_act, pww_ref[...], preferred_element_type=F32) + pwb_ref[...]
        o_ref[pl.ds(r0, chunk), :] = (proj * _silu(gc_ref[pl.ds(r0, chunk), :])).astype(o_ref.dtype)
        return carry
    lax.fori_loop(0, t_len // chunk, body, 0)


def _conv_branch(z3, prefix, dw_w, dw_b, ln_g, ln_b, pw_w_bf, pw_b, col0):
    n, t_len, _ = z3.shape
    c = prefix.shape[-1]
    hist = CONV_WIDTH - 1
    chunk = min(t_len, 64)
    cblk = col0 // c

    def zc(j):
        return pl.BlockSpec((None, t_len, c), lambda i: (i, 0, cblk + j))

    def vec():
        return pl.BlockSpec((1, c), lambda i: (0, 0))

    return pl.pallas_call(
        functools.partial(_conv_kernel, chunk=chunk),
        grid=(n,),
        in_specs=[zc(0), zc(1), zc(2),
                  pl.BlockSpec((None, hist, c), lambda i: (i, 0, 0)),
                  pl.BlockSpec((CONV_WIDTH, c), lambda i: (0, 0)),
                  vec(), vec(), vec(),
                  pl.BlockSpec((c, c), lambda i: (0, 0)),
                  vec()],
        out_specs=[pl.BlockSpec((None, t_len, c), lambda i: (i, 0, 0)),
                   pl.BlockSpec((None, hist, c), lambda i: (i, 0, 0))],
        out_shape=[jax.ShapeDtypeStruct((n, t_len, c), BF16),
                   jax.ShapeDtypeStruct((n, hist, c), F32)],
        scratch_shapes=[pltpu.VMEM((t_len - chunk + _round_up(chunk + CONV_WIDTH + 1, SUBLANES), c), F32)],
        compiler_params=pltpu.CompilerParams(
            dimension_semantics=("parallel",), vmem_limit_bytes=48 * MIB),
        name="conv_branch",
    )(z3, z3, z3, prefix, dw_w, dw_b.reshape(1, c), ln_g.reshape(1, c), ln_b.reshape(1, c),
      pw_w_bf, pw_b.reshape(1, c))


def _outproj_kernel(ma_ref, mc_ref, wa_ref, wc_ref, x_ref, g_ref, y_ref):
    y = jnp.dot(ma_ref[...], wa_ref[...], preferred_element_type=F32)
    y = y + jnp.dot(mc_ref[...], wc_ref[...], preferred_element_type=F32)
    ms = jnp.mean(y * y, axis=-1, keepdims=True)
    y_ref[...] = x_ref[...] + y * lax.rsqrt(ms + EPS) * g_ref[...]


def _outproj(mix_att, mix_conv, w_att_bf, w_conv_bf, x2d, norm_g, tm):
    m, d = x2d.shape
    da, dc = mix_att.shape[1], mix_conv.shape[1]
    return pl.pallas_call(
        _outproj_kernel,
        grid=(m // tm,),
        in_specs=[pl.BlockSpec((tm, da), lambda i: (i, 0)),
                  pl.BlockSpec((tm, dc), lambda i: (i, 0)),
                  pl.BlockSpec((da, d), lambda i: (0, 0)),
                  pl.BlockSpec((dc, d), lambda i: (0, 0)),
                  pl.BlockSpec((tm, d), lambda i: (i, 0)),
                  pl.BlockSpec((1, d), lambda i: (0, 0))],
        out_specs=pl.BlockSpec((tm, d), lambda i: (i, 0)),
        out_shape=jax.ShapeDtypeStruct((m, d), F32),
        compiler_params=pltpu.CompilerParams(
            dimension_semantics=("parallel",), vmem_limit_bytes=48 * MIB),
        name="outproj",
    )(mix_att, mix_conv, w_att_bf, w_conv_bf, x2d, norm_g.reshape(1, d))


NEAR_ROWS = 512
FAR_STRIDE = 16


RING = 3


def _sample_heads(q_ref, k_ref, v_ref, g_ref, near_ref, far_ref, tab_ref, mult_ref, o_ref, rows_kv):
    t_new = q_ref.shape[0]
    hd = near_ref.shape[1]
    scale = HEAD_DIM ** -0.5
    pad_q = jnp.zeros((SUBLANES - t_new, hd), F32)
    pad_kv = jnp.zeros((hd - t_new, hd), F32)
    nt = (((1,), (1,)), ((), ()))
    n_far = far_ref.shape[0] // rows_kv
    mult = mult_ref[...]
    for h in range(rows_kv // 2):
        cols = slice(h * hd, (h + 1) * hd)

        def head_rows(ref, n, parity):
            return ref[pl.ds(2 * h + parity, n, stride=rows_kv), :].astype(BF16)

        q8 = jnp.concatenate([q_ref[:, cols], pad_q], axis=0).astype(BF16)
        k_new = jnp.concatenate([k_ref[:, cols], pad_kv], axis=0).astype(BF16)
        v_new = jnp.concatenate([v_ref[:, cols], pad_kv], axis=0).astype(BF16)
        s = jnp.concatenate(
            [lax.dot_general(q8, head_rows(near_ref, NEAR_ROWS, 0), nt, preferred_element_type=F32),
             lax.dot_general(q8, head_rows(far_ref, n_far, 0), nt, preferred_element_type=F32),
             lax.dot_general(q8, k_new, nt, preferred_element_type=F32)], axis=1)
        s = s * scale + tab_ref[h]
        m = jnp.max(s, axis=-1, keepdims=True)
        p = jnp.exp(s - m) * mult
        l = jnp.sum(p, axis=-1, keepdims=True)
        pb = p.astype(BF16)
        acc = jnp.dot(pb[:, :NEAR_ROWS], head_rows(near_ref, NEAR_ROWS, 1),
                      preferred_element_type=F32)
        acc = acc + jnp.dot(pb[:, NEAR_ROWS:NEAR_ROWS + n_far], head_rows(far_ref, n_far, 1),
                            preferred_element_type=F32)
        acc = acc + jnp.dot(pb[:, NEAR_ROWS + n_far:], v_new, preferred_element_type=F32)
        o_ref[:, cols] = (acc / l)[:t_new] * _silu(g_ref[:, cols])


def _cache_stream_step(k, n_steps, seq0, n_chunks, cache_any, kvn_any, out_any, buf_ref, far_ref,
                       sem_in, sem_out, sem_tail, shift_sl, rows_kv, attend):
    chunk_sl = buf_ref.shape[1]
    far_per_chunk = NEAR_ROWS // FAR_STRIDE
    seq_sl = n_chunks * chunk_sl

    def chunk_in(j):
        return pltpu.make_async_copy(
            cache_any.at[seq0 + j // n_chunks, pl.ds((j % n_chunks) * chunk_sl, chunk_sl)],
            buf_ref.at[j % RING], sem_in.at[j % RING])

    def first_out(j):
        return pltpu.make_async_copy(
            buf_ref.at[j % RING, pl.ds(shift_sl, chunk_sl - shift_sl)],
            out_any.at[seq0 + j // n_chunks, pl.ds(0, chunk_sl - shift_sl)], sem_out.at[j % RING])

    def later_out(j):
        return pltpu.make_async_copy(
            buf_ref.at[j % RING],
            out_any.at[seq0 + j // n_chunks, pl.ds((j % n_chunks) * chunk_sl - shift_sl, chunk_sl)],
            sem_out.at[j % RING])

    def tail(j):
        return pltpu.make_async_copy(
            kvn_any.at[seq0 + j // n_chunks],
            out_any.at[seq0 + j // n_chunks, pl.ds(seq_sl - shift_sl, shift_sl)], sem_tail.at[0])

    def on_chunk(j, first, later):
        pl.when(j % n_chunks == 0)(first)
        pl.when(j % n_chunks != 0)(later)

    @pl.when(k == 0)
    def _():
        for j in range(RING - 1):
            chunk_in(jnp.int32(j)).start()

    c = k % n_chunks
    slot = k % RING
    chunk_in(k).wait()
    on_chunk(k, lambda: first_out(k).start(), lambda: later_out(k).start())
    near_ref = buf_ref.at[slot]

    for grp in range(far_per_chunk):
        dst = pl.multiple_of((c * far_per_chunk + grp) * shift_sl, SUBLANES)
        src = grp * FAR_STRIDE * rows_kv
        far_ref[pl.ds(dst, shift_sl), :] = near_ref[src:src + shift_sl, :]

    @pl.when(c == n_chunks - 1)
    def _():
        tail(k).start()
        attend(near_ref)
        tail(k).wait()

    @pl.when(k >= 1)
    def _():
        on_chunk(k - 1, lambda: first_out(k - 1).wait(), lambda: later_out(k - 1).wait())

    @pl.when(k + RING - 1 < n_steps)
    def _():
        chunk_in(k + RING - 1).start()

    @pl.when(k == n_steps - 1)
    def _():
        on_chunk(k, lambda: first_out(k).wait(), lambda: later_out(k).wait())


def _attn_sample_kernel(q_ref, k_ref, v_ref, g_ref, tab_ref, mult_ref, cache_any, kvn_any,
                        o_ref, out_any, buf_ref, far_ref, sem_in, sem_out, sem_tail, *,
                        rows_kv, n_seq, n_chunks):
    k = pl.program_id(0) * n_chunks + pl.program_id(1)
    n_steps = n_seq * n_chunks
    shift_sl = q_ref.shape[0] * rows_kv

    def attend(near_ref):
        _sample_heads(q_ref, k_ref, v_ref, g_ref, near_ref, far_ref, tab_ref, mult_ref, o_ref, rows_kv)

    _cache_stream_step(k, n_steps, 0, n_chunks, cache_any, kvn_any, out_any, buf_ref, far_ref,
                       sem_in, sem_out, sem_tail, shift_sl, rows_kv, attend)


def _attn_sample(zs3, kvn_rows, cache_rows, tab, mult, d_attn):
    n, t_new, _ = zs3.shape
    past, rows_kv, hd = cache_rows.shape[1:]
    assert past % NEAR_ROWS == 0 and t_new <= SUBLANES
    n_chunks = past // NEAR_ROWS
    chunk_sl = NEAR_ROWS * rows_kv
    cache2 = cache_rows.reshape(n, past * rows_kv, hd)
    n_far = past // FAR_STRIDE * t_new

    def zcol(j):
        return pl.BlockSpec((None, t_new, d_attn), lambda i, c: (i, 0, j))

    att, new_cache = pl.pallas_call(
        functools.partial(_attn_sample_kernel, rows_kv=rows_kv, n_seq=n, n_chunks=n_chunks),
        grid=(n, n_chunks),
        in_specs=[zcol(0), zcol(1), zcol(2), zcol(3),
                  pl.BlockSpec(tab.shape, lambda i, c: (0, 0, 0)),
                  pl.BlockSpec(mult.shape, lambda i, c: (0, 0)),
                  pl.BlockSpec(memory_space=pl.ANY),
                  pl.BlockSpec(memory_space=pl.ANY)],
        out_specs=[pl.BlockSpec((None, t_new, d_attn), lambda i, c: (i, 0, 0)),
                   pl.BlockSpec(memory_space=pl.ANY)],
        out_shape=[jax.ShapeDtypeStruct((n, t_new, d_attn), F32),
                   jax.ShapeDtypeStruct(cache2.shape, cache2.dtype)],
        scratch_shapes=[pltpu.VMEM((RING, chunk_sl, hd), F32),
                        pltpu.VMEM((n_far * rows_kv, hd), F32),
                        pltpu.SemaphoreType.DMA((RING,)),
                        pltpu.SemaphoreType.DMA((RING,)),
                        pltpu.SemaphoreType.DMA((1,))],
        compiler_params=pltpu.CompilerParams(
            dimension_semantics=("arbitrary", "arbitrary"), vmem_limit_bytes=48 * MIB),
        name="attn_sample",
    )(zs3, zs3, zs3, zs3, tab, mult, cache2, kvn_rows)
    return att, new_cache.reshape(cache_rows.shape)


def _prompt_bias_tables(rel_bias):
    nk = KEYS_PER_PATTERN
    qb = Q_BLOCK
    kdist = np.arange(-(qb - 1), qb + nk)
    n_dist = kdist.shape[0]
    valid = (kdist >= 0) & (kdist <= nk)
    tabs = []
    for _, dil in PATTERNS:
        bucket = _rel_bucket(jnp.asarray(np.clip(kdist, 0, nk) * dil, jnp.int32))
        per_dist = jnp.where(valid[:, None], rel_bias[bucket].astype(F32), NEG_INF)
        rev = per_dist.T[:, ::-1]
        skew = jnp.tile(rev, (1, qb + 1))[:, qb - 1:qb - 1 + qb * (n_dist - 1)]
        tabs.append(skew.reshape(-1, qb, n_dist - 1)[:, :, :qb + nk])
    return jnp.stack(tabs)


def _pattern_count(dist, patterns):
    return sum(((dist % dil == 0) & (dist >= 0) & (dist <= window)).astype(np.int32)
               for window, dil in patterns)


def _sample_tables(rel_bias, t_new, past):
    def bias_at(dist):
        return rel_bias[_rel_bucket(jnp.asarray(dist, jnp.int32))].astype(F32)

    def masked(bias, count):
        return jnp.where(jnp.asarray(count > 0)[..., None], bias, NEG_INF)

    near_pats, far_pats = PATTERNS[:2], PATTERNS[2:]
    assert near_pats[-1][0] == NEAR_ROWS and far_pats[0][1] == FAR_STRIDE and t_new <= far_pats[0][1]
    desc = np.arange(NEAR_ROWS + t_new - 1, 0, -1)
    desc_cnt = _pattern_count(desc, near_pats)
    desc_tab = masked(bias_at(desc), desc_cnt)
    starts = [t_new - 1 - t for t in range(t_new)]
    near_tab = jnp.stack([desc_tab[s0:s0 + NEAR_ROWS] for s0 in starts])
    near_cnt = np.stack([desc_cnt[s0:s0 + NEAR_ROWS] for s0 in starts])
    groups = past // FAR_STRIDE
    far_dist = past - FAR_STRIDE * np.arange(groups)
    own = np.eye(t_new, dtype=bool)[:, None, :] & (_pattern_count(far_dist, far_pats) > 0)[None, :, None]
    far_tab = jnp.where(jnp.asarray(own)[..., None], bias_at(far_dist)[None, :, None, :], NEG_INF)
    far_tab = far_tab.reshape(t_new, groups * t_new, -1)
    far_cnt = np.ones((t_new, groups * t_new), np.int32)
    tj = np.arange(t_new)[:, None] - np.arange(HEAD_DIM)[None, :]
    new_cnt = np.where(np.arange(HEAD_DIM)[None, :] < t_new, _pattern_count(tj, PATTERNS), 0)
    new_tab = masked(bias_at(np.clip(tj, 0, None).reshape(-1)).reshape(t_new, HEAD_DIM, -1), new_cnt)
    tab = jnp.concatenate([near_tab, far_tab, new_tab], axis=1).transpose(2, 0, 1)
    cnt = np.concatenate([near_cnt, far_cnt, new_cnt], axis=1)
    pad = SUBLANES - t_new
    tab = jnp.pad(tab, ((0, 0), (0, pad), (0, 0)))
    mult = np.pad(np.maximum(cnt, 1), ((0, pad), (0, 0)), constant_values=1).astype(np.float32)
    return tab, jnp.asarray(mult)


def kernel(x_prompt, x_sample, cache_conv, cache_kv, rel_bias, norm_pre, w_in, conv_dw_w, conv_dw_b,
           conv_ln_g, conv_ln_b, conv_pw_w, conv_pw_b, w_out, norm_post):
    depth = w_in.shape[0]
    assert depth == 1
    bsz, seq, d_model = x_prompt.shape
    n_dec, t_new, _ = x_sample.shape
    n_heads = cache_kv.shape[4]
    d_attn = n_heads * HEAD_DIM
    d_conv = cache_conv.shape[-1]
    past = cache_kv.shape[2]
    assert past == MAX_WINDOW and seq >= MAX_WINDOW and t_new <= 4
    hist = CONV_WIDTH - 1

    w_in_bf = w_in[0].astype(BF16)
    w_out_bf = w_out[0].astype(BF16)
    pw_bf = conv_pw_w[0].astype(BF16)
    conv_args = (conv_dw_w[0], conv_dw_b[0], conv_ln_g[0], conv_ln_b[0], pw_bf, conv_pw_b[0])
    conv_col0 = 4 * d_attn

    xp2 = x_prompt.reshape(bsz * seq, d_model)
    zp = _inproj(xp2, norm_pre[0], w_in_bf, tm=1024, tn=512)
    zp3 = zp.reshape(bsz, seq, -1)
    mix_att_p = _attn_prompt(zp3, _prompt_bias_tables(rel_bias), n_heads)
    zero_prefix = jnp.zeros((bsz, hist, d_conv), F32)
    mix_conv_p, new_conv_p = _conv_branch(zp3, zero_prefix, *conv_args, col0=conv_col0)
    yp = _outproj(mix_att_p.reshape(bsz * seq, d_attn), mix_conv_p.reshape(bsz * seq, d_conv),
                  w_out_bf[:d_attn], w_out_bf[d_attn:], xp2, norm_post[0], tm=512)
    win = min(MAX_WINDOW, seq)
    kv_rows_p = _kv_rows(zp, d_attn, tm=512).reshape(bsz, seq, n_heads, 2, HEAD_DIM)
    new_kv_p = kv_rows_p[:, seq - win:].transpose(0, 1, 3, 2, 4)[None]

    xs2 = x_sample.reshape(n_dec * t_new, d_model)
    zs = _inproj(xs2, norm_pre[0], w_in_bf, tm=n_dec * t_new, tn=512)
    zs3 = zs.reshape(n_dec, t_new, -1)
    def heads(col0):
        return zs3[:, :, col0:col0 + d_attn].reshape(n_dec, t_new, n_heads, HEAD_DIM)

    kvn_rows = jnp.stack([heads(d_attn), heads(2 * d_attn)], axis=3).reshape(
        n_dec, t_new * 2 * n_heads, HEAD_DIM)
    cache_rows = cache_kv[0].transpose(0, 1, 3, 2, 4).reshape(n_dec, past, 2 * n_heads, HEAD_DIM)
    tab_s, mult_s = _sample_tables(rel_bias, t_new, past)
    att_s, new_rows = _attn_sample(zs3, kvn_rows, cache_rows, tab_s, mult_s, d_attn)
    new_kv_s = new_rows.reshape(n_dec, past, n_heads, 2, HEAD_DIM).transpose(0, 1, 3, 2, 4)[None]
    mix_att_s = att_s.reshape(n_dec * t_new, d_attn).astype(BF16)
    mix_conv_s, new_conv_s = _conv_branch(zs3, cache_conv[0], *conv_args, col0=conv_col0)
    ys = _outproj(mix_att_s, mix_conv_s.reshape(n_dec * t_new, d_conv),
                  w_out_bf[:d_attn], w_out_bf[d_attn:], xs2, norm_post[0], tm=n_dec * t_new)

    return (yp.reshape(bsz, seq, d_model), ys.reshape(n_dec, t_new, d_model),
            new_conv_p[None], new_kv_p, new_conv_s[None], new_kv_s)
```

```python
import functools
import math
from typing import NamedTuple

import jax
import jax.numpy as jnp
import numpy as np
from jax import lax
from jax.experimental import pallas as pl
from jax.experimental.pallas import tpu as pltpu

F32 = jnp.float32
BF16 = jnp.bfloat16

HEAD_DIM = 128
PATTERNS = ((128, 1), (512, 4), (2048, 16))
MAX_WINDOW = 2048
Q_BLOCK = 128
KEYS_PER_PATTERN = 128
CONV_WIDTH = 31
N_BUCKETS = 32
MAX_EXACT = 16
EPS = 1e-6
NEG_INF = -1e30
SUBLANES = 8
HEAD_PAD = 16

MIB = 1024 * 1024


def _rel_bucket(dist):
    d = jnp.maximum(dist, 1).astype(F32)
    log_b = MAX_EXACT + (jnp.log(d / MAX_EXACT) / math.log(MAX_WINDOW / MAX_EXACT)
                         * (N_BUCKETS - MAX_EXACT)).astype(jnp.int32)
    log_b = jnp.minimum(log_b, N_BUCKETS - 1)
    return jnp.where(dist < MAX_EXACT, dist, log_b)


def _round_up(x, m):
    return -(-x // m) * m


def _silu(x):
    return x * jax.nn.sigmoid(x)


def _inproj_kernel(x_ref, g_ref, w_ref, z_ref, h_ref):
    @pl.when(pl.program_id(1) == 0)
    def _():
        x = x_ref[...]
        ms = jnp.mean(x * x, axis=-1, keepdims=True)
        h_ref[...] = (x * lax.rsqrt(ms + EPS) * g_ref[...]).astype(BF16)

    z_ref[...] = jnp.dot(h_ref[...], w_ref[...], preferred_element_type=F32)


def _inproj(x2d, norm_g, w_bf, tm, tn):
    m, d = x2d.shape
    n = w_bf.shape[1]
    return pl.pallas_call(
        _inproj_kernel,
        grid=(m // tm, n // tn),
        in_specs=[pl.BlockSpec((tm, d), lambda i, j: (i, 0)),
                  pl.BlockSpec((1, d), lambda i, j: (0, 0)),
                  pl.BlockSpec((d, tn), lambda i, j: (0, j))],
        out_specs=pl.BlockSpec((tm, tn), lambda i, j: (i, j)),
        out_shape=jax.ShapeDtypeStruct((m, n), F32),
        scratch_shapes=[pltpu.VMEM((tm, d), BF16)],
        compiler_params=pltpu.CompilerParams(
            dimension_semantics=("parallel", "arbitrary"), vmem_limit_bytes=48 * MIB),
        name="inproj",
    )(x2d, norm_g.reshape(1, d), w_bf)


def _kv_rows_kernel(k_ref, v_ref, o_ref):
    tm = k_ref.shape[0]
    hd = o_ref.shape[1]
    n_heads = k_ref.shape[1] // hd
    for h in range(n_heads):
        o_ref[pl.ds(2 * h, tm, stride=2 * n_heads), :] = k_ref[:, h * hd:(h + 1) * hd]
        o_ref[pl.ds(2 * h + 1, tm, stride=2 * n_heads), :] = v_ref[:, h * hd:(h + 1) * hd]


def _kv_rows(z, d_attn, tm):
    m = z.shape[0]
    rows_kv = 2 * d_attn // HEAD_DIM
    return pl.pallas_call(
        _kv_rows_kernel,
        grid=(m // tm,),
        in_specs=[pl.BlockSpec((tm, d_attn), lambda i: (i, 1)),
                  pl.BlockSpec((tm, d_attn), lambda i: (i, 2))],
        out_specs=pl.BlockSpec((tm * rows_kv, HEAD_DIM), lambda i: (i, 0)),
        out_shape=jax.ShapeDtypeStruct((m * rows_kv, HEAD_DIM), F32),
        compiler_params=pltpu.CompilerParams(
            dimension_semantics=("parallel",), vmem_limit_bytes=48 * MIB),
        name="kv_rows",
    )(z, z)


def _attn_block(qb, kw, vw, tab, scale):
    s = lax.dot_general(qb.astype(BF16), kw.astype(BF16), (((1,), (1,)), ((), ())),
                        preferred_element_type=F32)
    s = s * scale + tab
    m = jnp.max(s, axis=-1, keepdims=True)
    p = jnp.exp(s - m).astype(BF16)
    v_ones = jnp.concatenate([vw.astype(BF16), jnp.ones(vw.shape, BF16)], axis=1)
    acc_l = jnp.dot(p, v_ones, preferred_element_type=F32)
    d = vw.shape[1]
    return acc_l[:, :d], m, acc_l[:, d:]


def _attn_prompt_kernel(q_ref, k_ref, v_ref, g_ref, tab_ref, o_ref, acc_ref, m_ref, l_ref):
    seq = q_ref.shape[0]
    scale = HEAD_DIM ** -0.5
    qb_rows = Q_BLOCK
    nk = KEYS_PER_PATTERN

    def rows(ref, start, size, stride):
        if stride == 1:
            return ref[pl.ds(start, size), :]
        return ref[pl.ds(start, size, stride=stride), :]

    def put(p, start, stride, acc, m, l):
        lanes = acc.shape[-1]
        if stride == 1:
            idx = pl.ds(start, qb_rows)
        else:
            idx = pl.ds(start, qb_rows, stride=stride)
        acc_ref[p, idx, :] = acc
        m_ref[p, idx, :] = jnp.broadcast_to(m, (qb_rows, lanes))
        l_ref[p, idx, :] = l

    def first_block(p, phase, dil):
        tab = tab_ref[p][:, nk:]
        qb = rows(q_ref, phase, qb_rows, dil)
        kw = rows(k_ref, phase, qb_rows, dil)
        vw = rows(v_ref, phase, qb_rows, dil)
        put(p, phase, dil, *_attn_block(qb, kw, vw, tab, scale))

    def later_block(p, phase, dil, n):
        tab = tab_ref[p]
        q0 = phase + dil * qb_rows * n
        k0 = q0 - dil * nk
        qb = rows(q_ref, q0, qb_rows, dil)
        kw = rows(k_ref, k0, qb_rows + nk, dil)
        vw = rows(v_ref, k0, qb_rows + nk, dil)
        put(p, q0, dil, *_attn_block(qb, kw, vw, tab, scale))

    for p, (window, dil) in enumerate(PATTERNS):
        n_blocks = seq // dil // qb_rows
        for phase in range(dil):
            first_block(p, phase, dil)
            for n in range(1, n_blocks):
                later_block(p, phase, dil, n)

    chunk = 256

    def combine(c, carry):
        sl = pl.ds(pl.multiple_of(c * chunk, chunk), chunk)
        m0, m1, m2 = m_ref[0, sl, :], m_ref[1, sl, :], m_ref[2, sl, :]
        mm = jnp.maximum(jnp.maximum(m0, m1), m2)
        e0, e1, e2 = jnp.exp(m0 - mm), jnp.exp(m1 - mm), jnp.exp(m2 - mm)
        num = e0 * acc_ref[0, sl, :] + e1 * acc_ref[1, sl, :] + e2 * acc_ref[2, sl, :]
        den = e0 * l_ref[0, sl, :] + e1 * l_ref[1, sl, :] + e2 * l_ref[2, sl, :]
        o_ref[sl, :] = (num / den * _silu(g_ref[sl, :])).astype(o_ref.dtype)
        return carry
    lax.fori_loop(0, seq // chunk, combine, 0)


def _attn_prompt_stream_kernel(q_ref, k_ref, v_ref, g_ref, tab_ref, *rest, n_heads, **stream_kw):
    stream_in, (o_ref, so_ref, out_any), scratch = rest[:8], rest[8:11], rest[11:]
    acc_ref, m_ref, l_ref = scratch[:3]
    _attn_prompt_kernel(q_ref, k_ref, v_ref, g_ref, tab_ref, o_ref, acc_ref, m_ref, l_ref)
    k = pl.program_id(0) * n_heads + pl.program_id(1)
    _stream_step_with_attention(k, stream_in, so_ref, out_any, scratch[3:], **stream_kw)


def _attn_prompt(z3, tabs, n_heads, stream):
    b, seq, _ = z3.shape
    hd = HEAD_DIM
    n_seq = min(b * n_heads // stream.n_chunks, stream.zs3.shape[0])

    def col(off):
        return pl.BlockSpec((None, seq, hd), lambda i, h: (i, 0, off + h))

    def seq_of(i, h):
        return (i * n_heads + h) // stream.n_chunks

    s_in, s_out, s_shapes, s_scratch, s_args, s_kw = _stream_operands(stream, 0, n_seq, seq_of)
    return pl.pallas_call(
        functools.partial(_attn_prompt_stream_kernel, n_heads=n_heads, **s_kw),
        grid=(b, n_heads),
        in_specs=[col(0), col(n_heads), col(2 * n_heads), col(3 * n_heads),
                  pl.BlockSpec((len(PATTERNS), None, Q_BLOCK, Q_BLOCK + KEYS_PER_PATTERN),
                               lambda i, h: (0, h, 0, 0))] + s_in,
        out_specs=[pl.BlockSpec((None, seq, hd), lambda i, h: (i, 0, h))] + s_out,
        out_shape=[jax.ShapeDtypeStruct((b, seq, n_heads * hd), BF16)] + s_shapes,
        scratch_shapes=[pltpu.VMEM((len(PATTERNS), seq, hd), F32)] * 3 + s_scratch,
        compiler_params=pltpu.CompilerParams(
            dimension_semantics=("arbitrary", "arbitrary"), vmem_limit_bytes=56 * MIB),
        name="attn_prompt",
    )(z3, z3, z3, z3, tabs, *s_args)


def _conv_kernel(ca_ref, cb_ref, gc_ref, pre_ref, dww_ref, dwb_ref, lng_ref, lnb_ref,
                 pww_ref, pwb_ref, o_ref, newc_ref, upad_ref, *, chunk):
    t_len = ca_ref.shape[0]
    hist = CONV_WIDTH - 1
    ca = ca_ref[...]
    u = ca * jax.nn.sigmoid(cb_ref[...])
    upad_ref[0:hist, :] = pre_ref[...]
    upad_ref[hist:hist + t_len, :] = u
    n_pad = upad_ref.shape[0] - (hist + t_len)
    upad_ref[hist + t_len:, :] = jnp.zeros((n_pad, ca_ref.shape[1]), F32)
    newc_ref[...] = upad_ref[t_len:t_len + hist, :]
    win_rows = upad_ref.shape[0] - t_len + chunk

    def body(c, carry):
        r0 = pl.multiple_of(c * chunk, SUBLANES) if chunk % SUBLANES == 0 else c * chunk
        win = upad_ref[pl.ds(r0, win_rows), :]
        y = jnp.zeros((chunk, ca_ref.shape[1]), F32) + dwb_ref[...]
        for s in range(SUBLANES):
            shifted = win[s:s + win_rows - SUBLANES]
            for a in range(-(-CONV_WIDTH // SUBLANES)):
                w = SUBLANES * a + s
                if w < CONV_WIDTH:
                    y = y + shifted[SUBLANES * a:SUBLANES * a + chunk] * dww_ref[w:w + 1, :]
        mu = jnp.mean(y, axis=-1, keepdims=True)
        var = jnp.mean(jnp.square(y - mu), axis=-1, keepdims=True)
        yn = (y - mu) * lax.rsqrt(var + EPS) * lng_ref[...] + lnb_ref[...]
        c_act = _silu(yn).astype(BF16)
        proj = jnp.dot(c_act, pww_ref[...], preferred_element_type=F32) + pwb_ref[...]
        o_ref[pl.ds(r0, chunk), :] = (proj * _silu(gc_ref[pl.ds(r0, chunk), :])).astype(o_ref.dtype)
        return carry
    lax.fori_loop(0, t_len // chunk, body, 0)


def _conv_branch(z3, prefix, dw_w, dw_b, ln_g, ln_b, pw_w_bf, pw_b, col0):
    n, t_len, _ = z3.shape
    c = prefix.shape[-1]
    hist = CONV_WIDTH - 1
    chunk = min(t_len, 64)
    cblk = col0 // c

    def zc(j):
        return pl.BlockSpec((None, t_len, c), lambda i: (i, 0, cblk + j))

    def vec():
        return pl.BlockSpec((1, c), lambda i: (0, 0))

    return pl.pallas_call(
        functools.partial(_conv_kernel, chunk=chunk),
        grid=(n,),
        in_specs=[zc(0), zc(1), zc(2),
                  pl.BlockSpec((None, hist, c), lambda i: (i, 0, 0)),
                  pl.BlockSpec((CONV_WIDTH, c), lambda i: (0, 0)),
                  vec(), vec(), vec(),
                  pl.BlockSpec((c, c), lambda i: (0, 0)),
                  vec()],
        out_specs=[pl.BlockSpec((None, t_len, c), lambda i: (i, 0, 0)),
                   pl.BlockSpec((None, hist, c), lambda i: (i, 0, 0))],
        out_shape=[jax.ShapeDtypeStruct((n, t_len, c), BF16),
                   jax.ShapeDtypeStruct((n, hist, c), F32)],
        scratch_shapes=[pltpu.VMEM((t_len - chunk + _round_up(chunk + CONV_WIDTH + 1, SUBLANES), c), F32)],
        compiler_params=pltpu.CompilerParams(
            dimension_semantics=("parallel",), vmem_limit_bytes=48 * MIB),
        name="conv_branch",
    )(z3, z3, z3, prefix, dw_w, dw_b.reshape(1, c), ln_g.reshape(1, c), ln_b.reshape(1, c),
      pw_w_bf, pw_b.reshape(1, c))


def _outproj_kernel(ma_ref, mc_ref, wa_ref, wc_ref, x_ref, g_ref, y_ref):
    y = jnp.dot(ma_ref[...], wa_ref[...], preferred_element_type=F32)
    y = y + jnp.dot(mc_ref[...], wc_ref[...], preferred_element_type=F32)
    ms = jnp.mean(y * y, axis=-1, keepdims=True)
    y_ref[...] = x_ref[...] + y * lax.rsqrt(ms + EPS) * g_ref[...]


def _outproj(mix_att, mix_conv, w_att_bf, w_conv_bf, x2d, norm_g, tm):
    m, d = x2d.shape
    da, dc = mix_att.shape[1], mix_conv.shape[1]
    return pl.pallas_call(
        _outproj_kernel,
        grid=(m // tm,),
        in_specs=[pl.BlockSpec((tm, da), lambda i: (i, 0)),
                  pl.BlockSpec((tm, dc), lambda i: (i, 0)),
                  pl.BlockSpec((da, d), lambda i: (0, 0)),
                  pl.BlockSpec((dc, d), lambda i: (0, 0)),
                  pl.BlockSpec((tm, d), lambda i: (i, 0)),
                  pl.BlockSpec((1, d), lambda i: (0, 0))],
        out_specs=pl.BlockSpec((tm, d), lambda i: (i, 0)),
        out_shape=jax.ShapeDtypeStruct((m, d), F32),
        compiler_params=pltpu.CompilerParams(
            dimension_semantics=("parallel",), vmem_limit_bytes=48 * MIB),
        name="outproj",
    )(mix_att, mix_conv, w_att_bf, w_conv_bf, x2d, norm_g.reshape(1, d))


NEAR_ROWS = 512
FAR_STRIDE = 16


RING = 3


def _sample_heads(q_ref, k_ref, v_ref, g_ref, near_ref, far_ref, tab_ref, mult_ref, o_ref, rows_kv):
    t_new = q_ref.shape[0]
    hd = near_ref.shape[1]
    scale = HEAD_DIM ** -0.5
    pad_q = jnp.zeros((SUBLANES - t_new, hd), F32)
    pad_kv = jnp.zeros((hd - t_new, hd), F32)
    nt = (((1,), (1,)), ((), ()))
    n_far = far_ref.shape[0] // rows_kv
    mult = mult_ref[...]
    for h in range(rows_kv // 2):
        cols = slice(h * hd, (h + 1) * hd)

        def head_rows(ref, n, parity):
            return ref[pl.ds(2 * h + parity, n, stride=rows_kv), :].astype(BF16)

        q8 = jnp.concatenate([q_ref[:, cols], pad_q], axis=0).astype(BF16)
        k_new = jnp.concatenate([k_ref[:, cols], pad_kv], axis=0).astype(BF16)
        v_new = jnp.concatenate([v_ref[:, cols], pad_kv], axis=0).astype(BF16)
        s = jnp.concatenate(
            [lax.dot_general(q8, head_rows(near_ref, NEAR_ROWS, 0), nt, preferred_element_type=F32),
             lax.dot_general(q8, head_rows(far_ref, n_far, 0), nt, preferred_element_type=F32),
             lax.dot_general(q8, k_new, nt, preferred_element_type=F32)], axis=1)
        s = s * scale + tab_ref[h]
        m = jnp.max(s, axis=-1, keepdims=True)
        p = jnp.exp(s - m) * mult
        l = jnp.sum(p, axis=-1, keepdims=True)
        pb = p.astype(BF16)
        acc = jnp.dot(pb[:, :NEAR_ROWS], head_rows(near_ref, NEAR_ROWS, 1),
                      preferred_element_type=F32)
        acc = acc + jnp.dot(pb[:, NEAR_ROWS:NEAR_ROWS + n_far], head_rows(far_ref, n_far, 1),
                            preferred_element_type=F32)
        acc = acc + jnp.dot(pb[:, NEAR_ROWS + n_far:], v_new, preferred_element_type=F32)
        o_ref[:, cols] = (acc / l)[:t_new] * _silu(g_ref[:, cols])


def _cache_stream_step(k, n_steps, seq0, n_chunks, cache_any, kvn_any, out_any, buf_ref, far_ref,
                       sem_in, sem_out, sem_tail, shift_sl, rows_kv, attend):
    chunk_sl = buf_ref.shape[1]
    far_per_chunk = NEAR_ROWS // FAR_STRIDE
    seq_sl = n_chunks * chunk_sl

    def chunk_in(j):
        return pltpu.make_async_copy(
            cache_any.at[seq0 + j // n_chunks, pl.ds((j % n_chunks) * chunk_sl, chunk_sl)],
            buf_ref.at[j % RING], sem_in.at[j % RING])

    def first_out(j):
        return pltpu.make_async_copy(
            buf_ref.at[j % RING, pl.ds(shift_sl, chunk_sl - shift_sl)],
            out_any.at[seq0 + j // n_chunks, pl.ds(0, chunk_sl - shift_sl)], sem_out.at[j % RING])

    def later_out(j):
        return pltpu.make_async_copy(
            buf_ref.at[j % RING],
            out_any.at[seq0 + j // n_chunks, pl.ds((j % n_chunks) * chunk_sl - shift_sl, chunk_sl)],
            sem_out.at[j % RING])

    def tail(j):
        return pltpu.make_async_copy(
            kvn_any.at[seq0 + j // n_chunks],
            out_any.at[seq0 + j // n_chunks, pl.ds(seq_sl - shift_sl, shift_sl)], sem_tail.at[0])

    def on_chunk(j, first, later):
        pl.when(j % n_chunks == 0)(first)
        pl.when(j % n_chunks != 0)(later)

    @pl.when(k == 0)
    def _():
        for j in range(RING - 1):
            chunk_in(jnp.int32(j)).start()

    c = k % n_chunks
    slot = k % RING
    chunk_in(k).wait()
    on_chunk(k, lambda: first_out(k).start(), lambda: later_out(k).start())
    near_ref = buf_ref.at[slot]

    for grp in range(far_per_chunk):
        dst = pl.multiple_of((c * far_per_chunk + grp) * shift_sl, SUBLANES)
        src = grp * FAR_STRIDE * rows_kv
        far_ref[pl.ds(dst, shift_sl), :] = near_ref[src:src + shift_sl, :]

    @pl.when(c == n_chunks - 1)
    def _():
        tail(k).start()
        attend(near_ref)
        tail(k).wait()

    @pl.when(k >= 1)
    def _():
        on_chunk(k - 1, lambda: first_out(k - 1).wait(), lambda: later_out(k - 1).wait())

    @pl.when(k + RING - 1 < n_steps)
    def _():
        chunk_in(k + RING - 1).start()

    @pl.when(k == n_steps - 1)
    def _():
        on_chunk(k, lambda: first_out(k).wait(), lambda: later_out(k).wait())


class _Stream(NamedTuple):
    zs3: jax.Array
    kvn_rows: jax.Array
    cache2: jax.Array
    tab: jax.Array
    mult: jax.Array
    d_attn: int
    rows_kv: int
    n_chunks: int


def _stream_operands(stream, seq0, n_seq, seq_of):
    t_new = stream.zs3.shape[1]
    hd = stream.cache2.shape[2]
    chunk_sl = stream.cache2.shape[1] // stream.n_chunks
    n_far = stream.cache2.shape[1] // stream.rows_kv // FAR_STRIDE * t_new

    def local(*g):
        return jnp.minimum(seq_of(*g), n_seq - 1)

    def zcol(j):
        return pl.BlockSpec((None, t_new, stream.d_attn), lambda *g: (seq0 + local(*g), 0, j))

    in_specs = [zcol(0), zcol(1), zcol(2), zcol(3),
                pl.BlockSpec(stream.tab.shape, lambda *g: (0, 0, 0)),
                pl.BlockSpec(stream.mult.shape, lambda *g: (0, 0)),
                pl.BlockSpec(memory_space=pl.ANY),
                pl.BlockSpec(memory_space=pl.ANY)]
    out_specs = [pl.BlockSpec((None, t_new, stream.d_attn), lambda *g: (local(*g), 0, 0)),
                 pl.BlockSpec(memory_space=pl.ANY)]
    out_shapes = [jax.ShapeDtypeStruct((n_seq, t_new, stream.d_attn), F32),
                  jax.ShapeDtypeStruct(stream.cache2.shape, stream.cache2.dtype)]
    scratch = [pltpu.VMEM((RING, chunk_sl, hd), F32),
               pltpu.VMEM((n_far * stream.rows_kv, hd), F32),
               pltpu.SemaphoreType.DMA((RING,)),
               pltpu.SemaphoreType.DMA((RING,)),
               pltpu.SemaphoreType.DMA((1,))]
    args = (stream.zs3,) * 4 + (stream.tab, stream.mult, stream.cache2, stream.kvn_rows)
    kw = dict(rows_kv=stream.rows_kv, seq0=seq0, n_chunks=stream.n_chunks,
              n_steps=n_seq * stream.n_chunks)
    return in_specs, out_specs, out_shapes, scratch, args, kw


def _stream_step_with_attention(k, stream_in, o_ref, out_any, scratch, *, rows_kv, seq0, n_chunks,
                                n_steps):
    q_ref, k_ref, v_ref, g_ref, tab_ref, mult_ref, cache_any, kvn_any = stream_in
    buf_ref, far_ref, sem_in, sem_out, sem_tail = scratch
    shift_sl = q_ref.shape[0] * rows_kv

    def attend(near_ref):
        _sample_heads(q_ref, k_ref, v_ref, g_ref, near_ref, far_ref, tab_ref, mult_ref, o_ref, rows_kv)

    @pl.when(k < n_steps)
    def _():
        _cache_stream_step(k, n_steps, seq0, n_chunks, cache_any, kvn_any, out_any, buf_ref, far_ref,
                           sem_in, sem_out, sem_tail, shift_sl, rows_kv, attend)


def _attn_sample_kernel(*refs, n_chunks, **stream_kw):
    k = pl.program_id(0) * n_chunks + pl.program_id(1)
    _stream_step_with_attention(k, refs[:8], refs[9], refs[10], refs[11:], n_chunks=n_chunks,
                                **stream_kw)


def _attn_sample(stream, seq0, n_seq, partial_cache):
    s_in, s_out, s_shapes, s_scratch, s_args, s_kw = _stream_operands(
        stream, seq0, n_seq, lambda i, c: i)
    return pl.pallas_call(
        functools.partial(_attn_sample_kernel, **s_kw),
        grid=(n_seq, stream.n_chunks),
        in_specs=s_in + [pl.BlockSpec(memory_space=pl.ANY)],
        out_specs=s_out,
        out_shape=s_shapes,
        scratch_shapes=s_scratch,
        input_output_aliases={len(s_in): 1},
        compiler_params=pltpu.CompilerParams(
            dimension_semantics=("arbitrary", "arbitrary"), vmem_limit_bytes=48 * MIB),
        name="attn_sample",
    )(*s_args, partial_cache)


def _prompt_bias_tables(rel_bias):
    nk = KEYS_PER_PATTERN
    qb = Q_BLOCK
    kdist = np.arange(-(qb - 1), qb + nk)
    n_dist = kdist.shape[0]
    valid = (kdist >= 0) & (kdist <= nk)
    tabs = []
    for _, dil in PATTERNS:
        bucket = _rel_bucket(jnp.asarray(np.clip(kdist, 0, nk) * dil, jnp.int32))
        per_dist = jnp.where(valid[:, None], rel_bias[bucket].astype(F32), NEG_INF)
        rev = per_dist.T[:, ::-1]
        skew = jnp.tile(rev, (1, qb + 1))[:, qb - 1:qb - 1 + qb * (n_dist - 1)]
        tabs.append(skew.reshape(-1, qb, n_dist - 1)[:, :, :qb + nk])
    return jnp.stack(tabs)


def _pattern_count(dist, patterns):
    return sum(((dist % dil == 0) & (dist >= 0) & (dist <= window)).astype(np.int32)
               for window, dil in patterns)


def _sample_tables(rel_bias, t_new, past):
    def bias_at(dist):
        return rel_bias[_rel_bucket(jnp.asarray(dist, jnp.int32))].astype(F32)

    def masked(bias, count):
        return jnp.where(jnp.asarray(count > 0)[..., None], bias, NEG_INF)

    near_pats, far_pats = PATTERNS[:2], PATTERNS[2:]
    assert near_pats[-1][0] == NEAR_ROWS and far_pats[0][1] == FAR_STRIDE and t_new <= far_pats[0][1]
    desc = np.arange(NEAR_ROWS + t_new - 1, 0, -1)
    desc_cnt = _pattern_count(desc, near_pats)
    desc_tab = masked(bias_at(desc), desc_cnt)
    starts = [t_new - 1 - t for t in range(t_new)]
    near_tab = jnp.stack([desc_tab[s0:s0 + NEAR_ROWS] for s0 in starts])
    near_cnt = np.stack([desc_cnt[s0:s0 + NEAR_ROWS] for s0 in starts])
    groups = past // FAR_STRIDE
    far_dist = past - FAR_STRIDE * np.arange(groups)
    own = np.eye(t_new, dtype=bool)[:, None, :] & (_pattern_count(far_dist, far_pats) > 0)[None, :, None]
    far_tab = jnp.where(jnp.asarray(own)[..., None], bias_at(far_dist)[None, :, None, :], NEG_INF)
    far_tab = far_tab.reshape(t_new, groups * t_new, -1)
    far_cnt = np.ones((t_new, groups * t_new), np.int32)
    tj = np.arange(t_new)[:, None] - np.arange(HEAD_DIM)[None, :]
    new_cnt = np.where(np.arange(HEAD_DIM)[None, :] < t_new, _pattern_count(tj, PATTERNS), 0)
    new_tab = masked(bias_at(np.clip(tj, 0, None).reshape(-1)).reshape(t_new, HEAD_DIM, -1), new_cnt)
    tab = jnp.concatenate([near_tab, far_tab, new_tab], axis=1).transpose(2, 0, 1)
    cnt = np.concatenate([near_cnt, far_cnt, new_cnt], axis=1)
    pad = SUBLANES - t_new
    tab = jnp.pad(tab, ((0, 0), (0, pad), (0, 0)))
    mult = np.pad(np.maximum(cnt, 1), ((0, pad), (0, 0)), constant_values=1).astype(np.float32)
    return tab, jnp.asarray(mult)


def kernel(x_prompt, x_sample, cache_conv, cache_kv, rel_bias, norm_pre, w_in, conv_dw_w, conv_dw_b,
           conv_ln_g, conv_ln_b, conv_pw_w, conv_pw_b, w_out, norm_post):
    depth = w_in.shape[0]
    assert depth == 1
    bsz, seq, d_model = x_prompt.shape
    n_dec, t_new, _ = x_sample.shape
    n_heads = cache_kv.shape[4]
    d_attn = n_heads * HEAD_DIM
    d_conv = cache_conv.shape[-1]
    past = cache_kv.shape[2]
    assert past == MAX_WINDOW and seq >= MAX_WINDOW and t_new <= 4
    hist = CONV_WIDTH - 1

    w_in_bf = w_in[0].astype(BF16)
    w_out_bf = w_out[0].astype(BF16)
    pw_bf = conv_pw_w[0].astype(BF16)
    conv_args = (conv_dw_w[0], conv_dw_b[0], conv_ln_g[0], conv_ln_b[0], pw_bf, conv_pw_b[0])
    conv_col0 = 4 * d_attn

    xp2 = x_prompt.reshape(bsz * seq, d_model)
    zp = _inproj(xp2, norm_pre[0], w_in_bf, tm=1024, tn=512)
    zp3 = zp.reshape(bsz, seq, -1)
    xs2 = x_sample.reshape(n_dec * t_new, d_model)
    zs = _inproj(xs2, norm_pre[0], w_in_bf, tm=n_dec * t_new, tn=512)
    zs3 = zs.reshape(n_dec, t_new, -1)

    def heads(col0):
        return zs3[:, :, col0:col0 + d_attn].reshape(n_dec, t_new, n_heads, HEAD_DIM)

    rows_kv = 2 * n_heads
    kvn_rows = jnp.stack([heads(d_attn), heads(2 * d_attn)], axis=3).reshape(
        n_dec, t_new * rows_kv, HEAD_DIM)
    cache2 = cache_kv[0].transpose(0, 1, 3, 2, 4).reshape(n_dec, past * rows_kv, HEAD_DIM)
    tab_s, mult_s = _sample_tables(rel_bias, t_new, past)
    assert past % NEAR_ROWS == 0 and t_new <= SUBLANES
    stream = _Stream(zs3, kvn_rows, cache2, tab_s, mult_s, d_attn, rows_kv, past // NEAR_ROWS)

    mix_att_p, att_s0, part_cache = _attn_prompt(zp3, _prompt_bias_tables(rel_bias), n_heads, stream)
    zero_prefix = jnp.zeros((bsz, hist, d_conv), F32)
    mix_conv_p, new_conv_p = _conv_branch(zp3, zero_prefix, *conv_args, col0=conv_col0)
    yp = _outproj(mix_att_p.reshape(bsz * seq, d_attn), mix_conv_p.reshape(bsz * seq, d_conv),
                  w_out_bf[:d_attn], w_out_bf[d_attn:], xp2, norm_post[0], tm=512)
    win = min(MAX_WINDOW, seq)
    kv_rows_p = _kv_rows(zp, d_attn, tm=512).reshape(bsz, seq, n_heads, 2, HEAD_DIM)
    new_kv_p = kv_rows_p[:, seq - win:].transpose(0, 1, 3, 2, 4)[None]

    n_hosted = att_s0.shape[0]
    att_s, new_rows = att_s0, part_cache
    if n_hosted < n_dec:
        att_s1, new_rows = _attn_sample(stream, n_hosted, n_dec - n_hosted, part_cache)
        att_s = jnp.concatenate([att_s0, att_s1], axis=0)
    new_kv_s = new_rows.reshape(n_dec, past, n_heads, 2, HEAD_DIM).transpose(0, 1, 3, 2, 4)[None]
    mix_att_s = att_s.reshape(n_dec * t_new, d_attn).astype(BF16)
    mix_conv_s, new_conv_s = _conv_branch(zs3, cache_conv[0], *conv_args, col0=conv_col0)
    ys = _outproj(mix_att_s, mix_conv_s.reshape(n_dec * t_new, d_conv),
                  w_out_bf[:d_attn], w_out_bf[d_attn:], xs2, norm_post[0], tm=n_dec * t_new)

    return (yp.reshape(bsz, seq, d_model), ys.reshape(n_dec, t_new, d_model),
            new_conv_p[None], new_kv_p, new_conv_s[None], new_kv_s)
```

```python
import functools
import math
from typing import NamedTuple

import jax
import jax.numpy as jnp
import numpy as np
from jax import lax
from jax.experimental import pallas as pl
from jax.experimental.pallas import tpu as pltpu

F32 = jnp.float32
BF16 = jnp.bfloat16

HEAD_DIM = 128
PATTERNS = ((128, 1), (512, 4), (2048, 16))
MAX_WINDOW = 2048
Q_BLOCK = 128
KEYS_PER_PATTERN = 128
CONV_WIDTH = 31
N_BUCKETS = 32
MAX_EXACT = 16
EPS = 1e-6
NEG_INF = -1e30
SUBLANES = 8
HEAD_PAD = 16

MIB = 1024 * 1024


def _rel_bucket(dist):
    d = jnp.maximum(dist, 1).astype(F32)
    log_b = MAX_EXACT + (jnp.log(d / MAX_EXACT) / math.log(MAX_WINDOW / MAX_EXACT)
                         * (N_BUCKETS - MAX_EXACT)).astype(jnp.int32)
    log_b = jnp.minimum(log_b, N_BUCKETS - 1)
    return jnp.where(dist < MAX_EXACT, dist, log_b)


def _round_up(x, m):
    return -(-x // m) * m


def _silu(x):
    return x * jax.nn.sigmoid(x)


def _inproj_kernel(x_ref, g_ref, w_ref, z_ref, h_ref):
    @pl.when(pl.program_id(1) == 0)
    def _():
        x = x_ref[...]
        ms = jnp.mean(x * x, axis=-1, keepdims=True)
        h_ref[...] = (x * lax.rsqrt(ms + EPS) * g_ref[...]).astype(BF16)

    z_ref[...] = jnp.dot(h_ref[...], w_ref[...], preferred_element_type=F32)


def _inproj_stream_kernel(x_ref, g_ref, w_ref, *rest, n_col_tiles, **stream_kw):
    stream_in, (z_ref, so_ref, out_any), scratch = rest[:8], rest[8:11], rest[11:]
    _inproj_kernel(x_ref, g_ref, w_ref, z_ref, scratch[0])
    k = pl.program_id(0) * n_col_tiles + pl.program_id(1)
    _stream_step_with_attention(k, stream_in, so_ref, out_any, scratch[1:], **stream_kw)


def _inproj(x2d, norm_g, w_bf, tm, tn, stream=None, n_seq=0):
    m, d = x2d.shape
    n = w_bf.shape[1]
    grid = (m // tm, n // tn)
    in_specs = [pl.BlockSpec((tm, d), lambda i, j: (i, 0)),
                pl.BlockSpec((1, d), lambda i, j: (0, 0)),
                pl.BlockSpec((d, tn), lambda i, j: (0, j))]
    z_spec = pl.BlockSpec((tm, tn), lambda i, j: (i, j))
    z_shape = jax.ShapeDtypeStruct((m, n), F32)
    h_scratch = pltpu.VMEM((tm, d), BF16)
    args = (x2d, norm_g.reshape(1, d), w_bf)
    if stream is None:
        return pl.pallas_call(
            _inproj_kernel, grid=grid, in_specs=in_specs, out_specs=z_spec, out_shape=z_shape,
            scratch_shapes=[h_scratch],
            compiler_params=pltpu.CompilerParams(
                dimension_semantics=("parallel", "arbitrary"), vmem_limit_bytes=48 * MIB),
            name="inproj",
        )(*args)
    assert n_seq * stream.n_chunks <= grid[0] * grid[1]
    s_in, s_out, s_shapes, s_scratch, s_args, s_kw = _stream_operands(
        stream, 0, n_seq, lambda i, j: (i * grid[1] + j) // stream.n_chunks)
    return pl.pallas_call(
        functools.partial(_inproj_stream_kernel, n_col_tiles=grid[1], **s_kw),
        grid=grid, in_specs=in_specs + s_in, out_specs=[z_spec] + s_out,
        out_shape=[z_shape] + s_shapes, scratch_shapes=[h_scratch] + s_scratch,
        compiler_params=pltpu.CompilerParams(
            dimension_semantics=("arbitrary", "arbitrary"), vmem_limit_bytes=60 * MIB),
        name="inproj",
    )(*args, *s_args)


def _kv_rows_kernel(k_ref, v_ref, o_ref):
    tm = k_ref.shape[0]
    hd = o_ref.shape[1]
    n_heads = k_ref.shape[1] // hd
    for h in range(n_heads):
        o_ref[pl.ds(2 * h, tm, stride=2 * n_heads), :] = k_ref[:, h * hd:(h + 1) * hd]
        o_ref[pl.ds(2 * h + 1, tm, stride=2 * n_heads), :] = v_ref[:, h * hd:(h + 1) * hd]


def _kv_rows(z, d_attn, tm):
    m = z.shape[0]
    rows_kv = 2 * d_attn // HEAD_DIM
    return pl.pallas_call(
        _kv_rows_kernel,
        grid=(m // tm,),
        in_specs=[pl.BlockSpec((tm, d_attn), lambda i: (i, 1)),
                  pl.BlockSpec((tm, d_attn), lambda i: (i, 2))],
        out_specs=pl.BlockSpec((tm * rows_kv, HEAD_DIM), lambda i: (i, 0)),
        out_shape=jax.ShapeDtypeStruct((m * rows_kv, HEAD_DIM), F32),
        compiler_params=pltpu.CompilerParams(
            dimension_semantics=("parallel",), vmem_limit_bytes=48 * MIB),
        name="kv_rows",
    )(z, z)


def _attn_block(qb, kw, vw, tab, scale):
    s = lax.dot_general(qb.astype(BF16), kw.astype(BF16), (((1,), (1,)), ((), ())),
                        preferred_element_type=F32)
    s = s * scale + tab
    m = jnp.max(s, axis=-1, keepdims=True)
    p = jnp.exp(s - m).astype(BF16)
    v_ones = jnp.concatenate([vw.astype(BF16), jnp.ones(vw.shape, BF16)], axis=1)
    acc_l = jnp.dot(p, v_ones, preferred_element_type=F32)
    d = vw.shape[1]
    return acc_l[:, :d], m, acc_l[:, d:]


def _attn_prompt_kernel(q_ref, k_ref, v_ref, g_ref, tab_ref, o_ref, acc_ref, m_ref, l_ref):
    seq = q_ref.shape[0]
    scale = HEAD_DIM ** -0.5
    qb_rows = Q_BLOCK
    nk = KEYS_PER_PATTERN

    def rows(ref, start, size, stride):
        if stride == 1:
            return ref[pl.ds(start, size), :]
        return ref[pl.ds(start, size, stride=stride), :]

    def put(p, start, stride, acc, m, l):
        lanes = acc.shape[-1]
        if stride == 1:
            idx = pl.ds(start, qb_rows)
        else:
            idx = pl.ds(start, qb_rows, stride=stride)
        acc_ref[p, idx, :] = acc
        m_ref[p, idx, :] = jnp.broadcast_to(m, (qb_rows, lanes))
        l_ref[p, idx, :] = l

    def first_block(p, phase, dil):
        tab = tab_ref[p][:, nk:]
        qb = rows(q_ref, phase, qb_rows, dil)
        kw = rows(k_ref, phase, qb_rows, dil)
        vw = rows(v_ref, phase, qb_rows, dil)
        put(p, phase, dil, *_attn_block(qb, kw, vw, tab, scale))

    def later_block(p, phase, dil, n):
        tab = tab_ref[p]
        q0 = phase + dil * qb_rows * n
        k0 = q0 - dil * nk
        qb = rows(q_ref, q0, qb_rows, dil)
        kw = rows(k_ref, k0, qb_rows + nk, dil)
        vw = rows(v_ref, k0, qb_rows + nk, dil)
        put(p, q0, dil, *_attn_block(qb, kw, vw, tab, scale))

    for p, (window, dil) in enumerate(PATTERNS):
        n_blocks = seq // dil // qb_rows
        for phase in range(dil):
            first_block(p, phase, dil)
            for n in range(1, n_blocks):
                later_block(p, phase, dil, n)

    chunk = 256

    def combine(c, carry):
        sl = pl.ds(pl.multiple_of(c * chunk, chunk), chunk)
        m0, m1, m2 = m_ref[0, sl, :], m_ref[1, sl, :], m_ref[2, sl, :]
        mm = jnp.maximum(jnp.maximum(m0, m1), m2)
        e0, e1, e2 = jnp.exp(m0 - mm), jnp.exp(m1 - mm), jnp.exp(m2 - mm)
        num = e0 * acc_ref[0, sl, :] + e1 * acc_ref[1, sl, :] + e2 * acc_ref[2, sl, :]
        den = e0 * l_ref[0, sl, :] + e1 * l_ref[1, sl, :] + e2 * l_ref[2, sl, :]
        o_ref[sl, :] = (num / den * _silu(g_ref[sl, :])).astype(o_ref.dtype)
        return carry
    lax.fori_loop(0, seq // chunk, combine, 0)


def _attn_prompt_stream_kernel(q_ref, k_ref, v_ref, g_ref, tab_ref, *rest, n_heads, **stream_kw):
    stream_in, (o_ref, so_ref, out_any), scratch = rest[:8], rest[9:12], rest[12:]
    acc_ref, m_ref, l_ref = scratch[:3]
    _attn_prompt_kernel(q_ref, k_ref, v_ref, g_ref, tab_ref, o_ref, acc_ref, m_ref, l_ref)
    k = pl.program_id(0) * n_heads + pl.program_id(1)
    _stream_step_with_attention(k, stream_in, so_ref, out_any, scratch[3:], **stream_kw)


def _attn_prompt(z3, tabs, n_heads, stream, seq0, partial_cache):
    b, seq, _ = z3.shape
    hd = HEAD_DIM
    n_seq = min(b * n_heads // stream.n_chunks, stream.zs3.shape[0] - seq0)
    assert n_seq > 0

    def col(off):
        return pl.BlockSpec((None, seq, hd), lambda i, h: (i, 0, off + h))

    def seq_of(i, h):
        return (i * n_heads + h) // stream.n_chunks

    s_in, s_out, s_shapes, s_scratch, s_args, s_kw = _stream_operands(stream, seq0, n_seq, seq_of)
    in_specs = [col(0), col(n_heads), col(2 * n_heads), col(3 * n_heads),
                pl.BlockSpec((len(PATTERNS), None, Q_BLOCK, Q_BLOCK + KEYS_PER_PATTERN),
                             lambda i, h: (0, h, 0, 0))] + s_in + [pl.BlockSpec(memory_space=pl.ANY)]
    return pl.pallas_call(
        functools.partial(_attn_prompt_stream_kernel, n_heads=n_heads, **s_kw),
        grid=(b, n_heads),
        in_specs=in_specs,
        out_specs=[pl.BlockSpec((None, seq, hd), lambda i, h: (i, 0, h))] + s_out,
        out_shape=[jax.ShapeDtypeStruct((b, seq, n_heads * hd), BF16)] + s_shapes,
        scratch_shapes=[pltpu.VMEM((len(PATTERNS), seq, hd), F32)] * 3 + s_scratch,
        input_output_aliases={len(in_specs) - 1: 2},
        compiler_params=pltpu.CompilerParams(
            dimension_semantics=("arbitrary", "arbitrary"), vmem_limit_bytes=56 * MIB),
        name="attn_prompt",
    )(z3, z3, z3, z3, tabs, *s_args, partial_cache)


def _conv_kernel(ca_ref, cb_ref, gc_ref, pre_ref, dww_ref, dwb_ref, lng_ref, lnb_ref,
                 pww_ref, pwb_ref, o_ref, newc_ref, upad_ref, *, chunk):
    t_len = ca_ref.shape[0]
    hist = CONV_WIDTH - 1
    ca = ca_ref[...]
    u = ca * jax.nn.sigmoid(cb_ref[...])
    upad_ref[0:hist, :] = pre_ref[...]
    upad_ref[hist:hist + t_len, :] = u
    n_pad = upad_ref.shape[0] - (hist + t_len)
    upad_ref[hist + t_len:, :] = jnp.zeros((n_pad, ca_ref.shape[1]), F32)
    newc_ref[...] = upad_ref[t_len:t_len + hist, :]
    win_rows = upad_ref.shape[0] - t_len + chunk

    def body(c, carry):
        r0 = pl.multiple_of(c * chunk, SUBLANES) if chunk % SUBLANES == 0 else c * chunk
        win = upad_ref[pl.ds(r0, win_rows), :]
        y = jnp.zeros((chunk, ca_ref.shape[1]), F32) + dwb_ref[...]
        for s in range(SUBLANES):
            shifted = win[s:s + win_rows - SUBLANES]
            for a in range(-(-CONV_WIDTH // SUBLANES)):
                w = SUBLANES * a + s
                if w < CONV_WIDTH:
                    y = y + shifted[SUBLANES * a:SUBLANES * a + chunk] * dww_ref[w:w + 1, :]
        mu = jnp.mean(y, axis=-1, keepdims=True)
        var = jnp.mean(jnp.square(y - mu), axis=-1, keepdims=True)
        yn = (y - mu) * lax.rsqrt(var + EPS) * lng_ref[...] + lnb_ref[...]
        c_act = _silu(yn).astype(BF16)
        proj = jnp.dot(c_act, pww_ref[...], preferred_element_type=F32) + pwb_ref[...]
        o_ref[pl.ds(r0, chunk), :] = (proj * _silu(gc_ref[pl.ds(r0, chunk), :])).astype(o_ref.dtype)
        return carry
    lax.fori_loop(0, t_len // chunk, body, 0)


def _conv_branch(z3, prefix, dw_w, dw_b, ln_g, ln_b, pw_w_bf, pw_b, col0):
    n, t_len, _ = z3.shape
    c = prefix.shape[-1]
    hist = CONV_WIDTH - 1
    chunk = min(t_len, 64)
    cblk = col0 // c

    def zc(j):
        return pl.BlockSpec((None, t_len, c), lambda i: (i, 0, cblk + j))

    def vec():
        return pl.BlockSpec((1, c), lambda i: (0, 0))

    return pl.pallas_call(
        functools.partial(_conv_kernel, chunk=chunk),
        grid=(n,),
        in_specs=[zc(0), zc(1), zc(2),
                  pl.BlockSpec((None, hist, c), lambda i: (i, 0, 0)),
                  pl.BlockSpec((CONV_WIDTH, c), lambda i: (0, 0)),
                  vec(), vec(), vec(),
                  pl.BlockSpec((c, c), lambda i: (0, 0)),
                  vec()],
        out_specs=[pl.BlockSpec((None, t_len, c), lambda i: (i, 0, 0)),
                   pl.BlockSpec((None, hist, c), lambda i: (i, 0, 0))],
        out_shape=[jax.ShapeDtypeStruct((n, t_len, c), BF16),
                   jax.ShapeDtypeStruct((n, hist, c), F32)],
        scratch_shapes=[pltpu.VMEM((t_len - chunk + _round_up(chunk + CONV_WIDTH + 1, SUBLANES), c), F32)],
        compiler_params=pltpu.CompilerParams(
            dimension_semantics=("parallel",), vmem_limit_bytes=48 * MIB),
        name="conv_branch",
    )(z3, z3, z3, prefix, dw_w, dw_b.reshape(1, c), ln_g.reshape(1, c), ln_b.reshape(1, c),
      pw_w_bf, pw_b.reshape(1, c))


def _outproj_kernel(ma_ref, mc_ref, wa_ref, wc_ref, x_ref, g_ref, y_ref):
    y = jnp.dot(ma_ref[...], wa_ref[...], preferred_element_type=F32)
    y = y + jnp.dot(mc_ref[...], wc_ref[...], preferred_element_type=F32)
    ms = jnp.mean(y * y, axis=-1, keepdims=True)
    y_ref[...] = x_ref[...] + y * lax.rsqrt(ms + EPS) * g_ref[...]


def _outproj(mix_att, mix_conv, w_att_bf, w_conv_bf, x2d, norm_g, tm):
    m, d = x2d.shape
    da, dc = mix_att.shape[1], mix_conv.shape[1]
    return pl.pallas_call(
        _outproj_kernel,
        grid=(m // tm,),
        in_specs=[pl.BlockSpec((tm, da), lambda i: (i, 0)),
                  pl.BlockSpec((tm, dc), lambda i: (i, 0)),
                  pl.BlockSpec((da, d), lambda i: (0, 0)),
                  pl.BlockSpec((dc, d), lambda i: (0, 0)),
                  pl.BlockSpec((tm, d), lambda i: (i, 0)),
                  pl.BlockSpec((1, d), lambda i: (0, 0))],
        out_specs=pl.BlockSpec((tm, d), lambda i: (i, 0)),
        out_shape=jax.ShapeDtypeStruct((m, d), F32),
        compiler_params=pltpu.CompilerParams(
            dimension_semantics=("parallel",), vmem_limit_bytes=48 * MIB),
        name="outproj",
    )(mix_att, mix_conv, w_att_bf, w_conv_bf, x2d, norm_g.reshape(1, d))


NEAR_ROWS = 512
FAR_STRIDE = 16


RING = 3


def _sample_heads(q_ref, k_ref, v_ref, g_ref, near_ref, far_ref, tab_ref, mult_ref, o_ref, rows_kv):
    t_new = q_ref.shape[0]
    hd = near_ref.shape[1]
    scale = HEAD_DIM ** -0.5
    pad_q = jnp.zeros((SUBLANES - t_new, hd), F32)
    pad_kv = jnp.zeros((hd - t_new, hd), F32)
    nt = (((1,), (1,)), ((), ()))
    n_far = far_ref.shape[0] // rows_kv
    mult = mult_ref[...]
    for h in range(rows_kv // 2):
        cols = slice(h * hd, (h + 1) * hd)

        def head_rows(ref, n, parity):
            return ref[pl.ds(2 * h + parity, n, stride=rows_kv), :].astype(BF16)

        q8 = jnp.concatenate([q_ref[:, cols], pad_q], axis=0).astype(BF16)
        k_new = jnp.concatenate([k_ref[:, cols], pad_kv], axis=0).astype(BF16)
        v_new = jnp.concatenate([v_ref[:, cols], pad_kv], axis=0).astype(BF16)
        s = jnp.concatenate(
            [lax.dot_general(q8, head_rows(near_ref, NEAR_ROWS, 0), nt, preferred_element_type=F32),
             lax.dot_general(q8, head_rows(far_ref, n_far, 0), nt, preferred_element_type=F32),
             lax.dot_general(q8, k_new, nt, preferred_element_type=F32)], axis=1)
        s = s * scale + tab_ref[h]
        m = jnp.max(s, axis=-1, keepdims=True)
        p = jnp.exp(s - m) * mult
        l = jnp.sum(p, axis=-1, keepdims=True)
        pb = p.astype(BF16)
        acc = jnp.dot(pb[:, :NEAR_ROWS], head_rows(near_ref, NEAR_ROWS, 1),
                      preferred_element_type=F32)
        acc = acc + jnp.dot(pb[:, NEAR_ROWS:NEAR_ROWS + n_far], head_rows(far_ref, n_far, 1),
                            preferred_element_type=F32)
        acc = acc + jnp.dot(pb[:, NEAR_ROWS + n_far:], v_new, preferred_element_type=F32)
        o_ref[:, cols] = (acc / l)[:t_new] * _silu(g_ref[:, cols])


def _cache_stream_step(k, n_steps, seq0, n_chunks, cache_any, kvn_any, out_any, buf_ref, far_ref,
                       sem_in, sem_out, sem_tail, shift_sl, rows_kv, attend):
    chunk_sl = buf_ref.shape[1]
    far_per_chunk = NEAR_ROWS // FAR_STRIDE
    seq_sl = n_chunks * chunk_sl

    def chunk_in(j):
        return pltpu.make_async_copy(
            cache_any.at[seq0 + j // n_chunks, pl.ds((j % n_chunks) * chunk_sl, chunk_sl)],
            buf_ref.at[j % RING], sem_in.at[j % RING])

    def first_out(j):
        return pltpu.make_async_copy(
            buf_ref.at[j % RING, pl.ds(shift_sl, chunk_sl - shift_sl)],
            out_any.at[seq0 + j // n_chunks, pl.ds(0, chunk_sl - shift_sl)], sem_out.at[j % RING])

    def later_out(j):
        return pltpu.make_async_copy(
            buf_ref.at[j % RING],
            out_any.at[seq0 + j // n_chunks, pl.ds((j % n_chunks) * chunk_sl - shift_sl, chunk_sl)],
            sem_out.at[j % RING])

    def tail(j):
        return pltpu.make_async_copy(
            kvn_any.at[seq0 + j // n_chunks],
            out_any.at[seq0 + j // n_chunks, pl.ds(seq_sl - shift_sl, shift_sl)], sem_tail.at[0])

    def on_chunk(j, first, later):
        pl.when(j % n_chunks == 0)(first)
        pl.when(j % n_chunks != 0)(later)

    @pl.when(k == 0)
    def _():
        for j in range(RING - 1):
            chunk_in(jnp.int32(j)).start()

    c = k % n_chunks
    slot = k % RING
    chunk_in(k).wait()
    on_chunk(k, lambda: first_out(k).start(), lambda: later_out(k).start())
    near_ref = buf_ref.at[slot]

    for grp in range(far_per_chunk):
        dst = pl.multiple_of((c * far_per_chunk + grp) * shift_sl, SUBLANES)
        src = grp * FAR_STRIDE * rows_kv
        far_ref[pl.ds(dst, shift_sl), :] = near_ref[src:src + shift_sl, :]

    @pl.when(c == n_chunks - 1)
    def _():
        tail(k).start()
        attend(near_ref)
        tail(k).wait()

    @pl.when(k >= 1)
    def _():
        on_chunk(k - 1, lambda: first_out(k - 1).wait(), lambda: later_out(k - 1).wait())

    @pl.when(k + RING - 1 < n_steps)
    def _():
        chunk_in(k + RING - 1).start()

    @pl.when(k == n_steps - 1)
    def _():
        on_chunk(k, lambda: first_out(k).wait(), lambda: later_out(k).wait())


class _Stream(NamedTuple):
    zs3: jax.Array
    kvn_rows: jax.Array
    cache2: jax.Array
    tab: jax.Array
    mult: jax.Array
    d_attn: int
    rows_kv: int
    n_chunks: int


def _stream_operands(stream, seq0, n_seq, seq_of):
    t_new = stream.zs3.shape[1]
    hd = stream.cache2.shape[2]
    chunk_sl = stream.cache2.shape[1] // stream.n_chunks
    n_far = stream.cache2.shape[1] // stream.rows_kv // FAR_STRIDE * t_new

    def local(*g):
        return jnp.minimum(seq_of(*g), n_seq - 1)

    def zcol(j):
        return pl.BlockSpec((None, t_new, stream.d_attn), lambda *g: (seq0 + local(*g), 0, j))

    in_specs = [zcol(0), zcol(1), zcol(2), zcol(3),
                pl.BlockSpec(stream.tab.shape, lambda *g: (0, 0, 0)),
                pl.BlockSpec(stream.mult.shape, lambda *g: (0, 0)),
                pl.BlockSpec(memory_space=pl.ANY),
                pl.BlockSpec(memory_space=pl.ANY)]
    out_specs = [pl.BlockSpec((None, t_new, stream.d_attn), lambda *g: (local(*g), 0, 0)),
                 pl.BlockSpec(memory_space=pl.ANY)]
    out_shapes = [jax.ShapeDtypeStruct((n_seq, t_new, stream.d_attn), F32),
                  jax.ShapeDtypeStruct(stream.cache2.shape, stream.cache2.dtype)]
    scratch = [pltpu.VMEM((RING, chunk_sl, hd), F32),
               pltpu.VMEM((n_far * stream.rows_kv, hd), F32),
               pltpu.SemaphoreType.DMA((RING,)),
               pltpu.SemaphoreType.DMA((RING,)),
               pltpu.SemaphoreType.DMA((1,))]
    args = (stream.zs3,) * 4 + (stream.tab, stream.mult, stream.cache2, stream.kvn_rows)
    kw = dict(rows_kv=stream.rows_kv, seq0=seq0, n_chunks=stream.n_chunks,
              n_steps=n_seq * stream.n_chunks)
    return in_specs, out_specs, out_shapes, scratch, args, kw


def _stream_step_with_attention(k, stream_in, o_ref, out_any, scratch, *, rows_kv, seq0, n_chunks,
                                n_steps):
    q_ref, k_ref, v_ref, g_ref, tab_ref, mult_ref, cache_any, kvn_any = stream_in
    buf_ref, far_ref, sem_in, sem_out, sem_tail = scratch
    shift_sl = q_ref.shape[0] * rows_kv

    def attend(near_ref):
        _sample_heads(q_ref, k_ref, v_ref, g_ref, near_ref, far_ref, tab_ref, mult_ref, o_ref, rows_kv)

    @pl.when(k < n_steps)
    def _():
        _cache_stream_step(k, n_steps, seq0, n_chunks, cache_any, kvn_any, out_any, buf_ref, far_ref,
                           sem_in, sem_out, sem_tail, shift_sl, rows_kv, attend)


def _attn_sample_kernel(*refs, n_chunks, **stream_kw):
    k = pl.program_id(0) * n_chunks + pl.program_id(1)
    _stream_step_with_attention(k, refs[:8], refs[9], refs[10], refs[11:], n_chunks=n_chunks,
                                **stream_kw)


def _attn_sample(stream, seq0, n_seq, partial_cache):
    s_in, s_out, s_shapes, s_scratch, s_args, s_kw = _stream_operands(
        stream, seq0, n_seq, lambda i, c: i)
    return pl.pallas_call(
        functools.partial(_attn_sample_kernel, **s_kw),
        grid=(n_seq, stream.n_chunks),
        in_specs=s_in + [pl.BlockSpec(memory_space=pl.ANY)],
        out_specs=s_out,
        out_shape=s_shapes,
        scratch_shapes=s_scratch,
        input_output_aliases={len(s_in): 1},
        compiler_params=pltpu.CompilerParams(
            dimension_semantics=("arbitrary", "arbitrary"), vmem_limit_bytes=48 * MIB),
        name="attn_sample",
    )(*s_args, partial_cache)


def _prompt_bias_tables(rel_bias):
    nk = KEYS_PER_PATTERN
    qb = Q_BLOCK
    kdist = np.arange(-(qb - 1), qb + nk)
    n_dist = kdist.shape[0]
    valid = (kdist >= 0) & (kdist <= nk)
    tabs = []
    for _, dil in PATTERNS:
        bucket = _rel_bucket(jnp.asarray(np.clip(kdist, 0, nk) * dil, jnp.int32))
        per_dist = jnp.where(valid[:, None], rel_bias[bucket].astype(F32), NEG_INF)
        rev = per_dist.T[:, ::-1]
        skew = jnp.tile(rev, (1, qb + 1))[:, qb - 1:qb - 1 + qb * (n_dist - 1)]
        tabs.append(skew.reshape(-1, qb, n_dist - 1)[:, :, :qb + nk])
    return jnp.stack(tabs)


def _pattern_count(dist, patterns):
    return sum(((dist % dil == 0) & (dist >= 0) & (dist <= window)).astype(np.int32)
               for window, dil in patterns)


def _sample_tables(rel_bias, t_new, past):
    def bias_at(dist):
        return rel_bias[_rel_bucket(jnp.asarray(dist, jnp.int32))].astype(F32)

    def masked(bias, count):
        return jnp.where(jnp.asarray(count > 0)[..., None], bias, NEG_INF)

    near_pats, far_pats = PATTERNS[:2], PATTERNS[2:]
    assert near_pats[-1][0] == NEAR_ROWS and far_pats[0][1] == FAR_STRIDE and t_new <= far_pats[0][1]
    desc = np.arange(NEAR_ROWS + t_new - 1, 0, -1)
    desc_cnt = _pattern_count(desc, near_pats)
    desc_tab = masked(bias_at(desc), desc_cnt)
    starts = [t_new - 1 - t for t in range(t_new)]
    near_tab = jnp.stack([desc_tab[s0:s0 + NEAR_ROWS] for s0 in starts])
    near_cnt = np.stack([desc_cnt[s0:s0 + NEAR_ROWS] for s0 in starts])
    groups = past // FAR_STRIDE
    far_dist = past - FAR_STRIDE * np.arange(groups)
    own = np.eye(t_new, dtype=bool)[:, None, :] & (_pattern_count(far_dist, far_pats) > 0)[None, :, None]
    far_tab = jnp.where(jnp.asarray(own)[..., None], bias_at(far_dist)[None, :, None, :], NEG_INF)
    far_tab = far_tab.reshape(t_new, groups * t_new, -1)
    far_cnt = np.ones((t_new, groups * t_new), np.int32)
    tj = np.arange(t_new)[:, None] - np.arange(HEAD_DIM)[None, :]
    new_cnt = np.where(np.arange(HEAD_DIM)[None, :] < t_new, _pattern_count(tj, PATTERNS), 0)
    new_tab = masked(bias_at(np.clip(tj, 0, None).reshape(-1)).reshape(t_new, HEAD_DIM, -1), new_cnt)
    tab = jnp.concatenate([near_tab, far_tab, new_tab], axis=1).transpose(2, 0, 1)
    cnt = np.concatenate([near_cnt, far_cnt, new_cnt], axis=1)
    pad = SUBLANES - t_new
    tab = jnp.pad(tab, ((0, 0), (0, pad), (0, 0)))
    mult = np.pad(np.maximum(cnt, 1), ((0, pad), (0, 0)), constant_values=1).astype(np.float32)
    return tab, jnp.asarray(mult)


def kernel(x_prompt, x_sample, cache_conv, cache_kv, rel_bias, norm_pre, w_in, conv_dw_w, conv_dw_b,
           conv_ln_g, conv_ln_b, conv_pw_w, conv_pw_b, w_out, norm_post):
    depth = w_in.shape[0]
    assert depth == 1
    bsz, seq, d_model = x_prompt.shape
    n_dec, t_new, _ = x_sample.shape
    n_heads = cache_kv.shape[4]
    d_attn = n_heads * HEAD_DIM
    d_conv = cache_conv.shape[-1]
    past = cache_kv.shape[2]
    assert past == MAX_WINDOW and seq >= MAX_WINDOW and t_new <= 4
    hist = CONV_WIDTH - 1

    w_in_bf = w_in[0].astype(BF16)
    w_out_bf = w_out[0].astype(BF16)
    pw_bf = conv_pw_w[0].astype(BF16)
    conv_args = (conv_dw_w[0], conv_dw_b[0], conv_ln_g[0], conv_ln_b[0], pw_bf, conv_pw_b[0])
    conv_col0 = 4 * d_attn

    xp2 = x_prompt.reshape(bsz * seq, d_model)
    xs2 = x_sample.reshape(n_dec * t_new, d_model)
    zs = _inproj(xs2, norm_pre[0], w_in_bf, tm=n_dec * t_new, tn=512)
    zs3 = zs.reshape(n_dec, t_new, -1)

    def heads(col0):
        return zs3[:, :, col0:col0 + d_attn].reshape(n_dec, t_new, n_heads, HEAD_DIM)

    rows_kv = 2 * n_heads
    kvn_rows = jnp.stack([heads(d_attn), heads(2 * d_attn)], axis=3).reshape(
        n_dec, t_new * rows_kv, HEAD_DIM)
    cache2 = cache_kv[0].transpose(0, 1, 3, 2, 4).reshape(n_dec, past * rows_kv, HEAD_DIM)
    tab_s, mult_s = _sample_tables(rel_bias, t_new, past)
    assert past % NEAR_ROWS == 0 and t_new <= SUBLANES
    stream = _Stream(zs3, kvn_rows, cache2, tab_s, mult_s, d_attn, rows_kv, past // NEAR_ROWS)

    n_in_attn = min(bsz * n_heads // stream.n_chunks, n_dec)
    n_in_proj = n_dec - n_in_attn
    tm_p, tn_p = 1024, 512
    if 0 < n_in_proj * stream.n_chunks <= (bsz * seq // tm_p) * (w_in_bf.shape[1] // tn_p):
        zp, att_s0, part_cache = _inproj(xp2, norm_pre[0], w_in_bf, tm_p, tn_p, stream, n_in_proj)
    else:
        zp = _inproj(xp2, norm_pre[0], w_in_bf, tm_p, tn_p)
        n_in_proj = 0
        att_s0 = jnp.zeros((0, t_new, d_attn), F32)
        part_cache = jnp.zeros(cache2.shape, cache2.dtype)
    zp3 = zp.reshape(bsz, seq, -1)
    mix_att_p, att_s1, part_cache = _attn_prompt(zp3, _prompt_bias_tables(rel_bias), n_heads, stream,
                                                 n_in_proj, part_cache)
    att_s0 = jnp.concatenate([att_s0, att_s1], axis=0)
    zero_prefix = jnp.zeros((bsz, hist, d_conv), F32)
    mix_conv_p, new_conv_p = _conv_branch(zp3, zero_prefix, *conv_args, col0=conv_col0)
    yp = _outproj(mix_att_p.reshape(bsz * seq, d_attn), mix_conv_p.reshape(bsz * seq, d_conv),
                  w_out_bf[:d_attn], w_out_bf[d_attn:], xp2, norm_post[0], tm=512)
    win = min(MAX_WINDOW, seq)
    kv_rows_p = _kv_rows(zp, d_attn, tm=512).reshape(bsz, seq, n_heads, 2, HEAD_DIM)
    new_kv_p = kv_rows_p[:, seq - win:].transpose(0, 1, 3, 2, 4)[None]

    n_hosted = att_s0.shape[0]
    att_s, new_rows = att_s0, part_cache
    if n_hosted < n_dec:
        att_s1, new_rows = _attn_sample(stream, n_hosted, n_dec - n_hosted, part_cache)
        att_s = jnp.concatenate([att_s0, att_s1], axis=0)
    new_kv_s = new_rows.reshape(n_dec, past, n_heads, 2, HEAD_DIM).transpose(0, 1, 3, 2, 4)[None]
    mix_att_s = att_s.reshape(n_dec * t_new, d_attn).astype(BF16)
    mix_conv_s, new_conv_s = _conv_branch(zs3, cache_conv[0], *conv_args, col0=conv_col0)
    ys = _outproj(mix_att_s, mix_conv_s.reshape(n_dec * t_new, d_conv),
                  w_out_bf[:d_attn], w_out_bf[d_attn:], xs2, norm_post[0], tm=n_dec * t_new)

    return (yp.reshape(bsz, seq, d_model), ys.reshape(n_dec, t_new, d_model),
            new_conv_p[None], new_kv_p, new_conv_s[None], new_kv_s)
```

```python
import functools
import math
from typing import NamedTuple

import jax
import jax.numpy as jnp
import numpy as np
from jax import lax
from jax.experimental import pallas as pl
from jax.experimental.pallas import tpu as pltpu

F32 = jnp.float32
BF16 = jnp.bfloat16

HEAD_DIM = 128
PATTERNS = ((128, 1), (512, 4), (2048, 16))
MAX_WINDOW = 2048
Q_BLOCK = 128
KEYS_PER_PATTERN = 128
CONV_WIDTH = 31
N_BUCKETS = 32
MAX_EXACT = 16
EPS = 1e-6
NEG_INF = -1e30
SUBLANES = 8
HEAD_PAD = 16

MIB = 1024 * 1024


def _rel_bucket(dist):
    d = jnp.maximum(dist, 1).astype(F32)
    log_b = MAX_EXACT + (jnp.log(d / MAX_EXACT) / math.log(MAX_WINDOW / MAX_EXACT)
                         * (N_BUCKETS - MAX_EXACT)).astype(jnp.int32)
    log_b = jnp.minimum(log_b, N_BUCKETS - 1)
    return jnp.where(dist < MAX_EXACT, dist, log_b)


def _round_up(x, m):
    return -(-x // m) * m


def _silu(x):
    return x * jax.nn.sigmoid(x)


def _inproj_kernel(x_ref, g_ref, w_ref, z_ref, h_ref):
    @pl.when(pl.program_id(1) == 0)
    def _():
        x = x_ref[...]
        ms = jnp.mean(x * x, axis=-1, keepdims=True)
        h_ref[...] = (x * lax.rsqrt(ms + EPS) * g_ref[...]).astype(BF16)

    z_ref[...] = jnp.dot(h_ref[...], w_ref[...], preferred_element_type=F32)


def _inproj_stream_kernel(x_ref, g_ref, w_ref, *rest, n_col_tiles, **stream_kw):
    stream_in, (z_ref, so_ref, out_any), scratch = rest[:8], rest[8:11], rest[11:]
    _inproj_kernel(x_ref, g_ref, w_ref, z_ref, scratch[0])
    k = pl.program_id(0) * n_col_tiles + pl.program_id(1)
    _stream_step_with_attention(k, stream_in, so_ref, out_any, scratch[1:], **stream_kw)


def _inproj(x2d, norm_g, w_bf, tm, tn, stream=None, n_seq=0):
    m, d = x2d.shape
    n = w_bf.shape[1]
    grid = (m // tm, n // tn)
    in_specs = [pl.BlockSpec((tm, d), lambda i, j: (i, 0)),
                pl.BlockSpec((1, d), lambda i, j: (0, 0)),
                pl.BlockSpec((d, tn), lambda i, j: (0, j))]
    z_spec = pl.BlockSpec((tm, tn), lambda i, j: (i, j))
    z_shape = jax.ShapeDtypeStruct((m, n), F32)
    h_scratch = pltpu.VMEM((tm, d), BF16)
    args = (x2d, norm_g.reshape(1, d), w_bf)
    if stream is None:
        return pl.pallas_call(
            _inproj_kernel, grid=grid, in_specs=in_specs, out_specs=z_spec, out_shape=z_shape,
            scratch_shapes=[h_scratch],
            compiler_params=pltpu.CompilerParams(
                dimension_semantics=("parallel", "arbitrary"), vmem_limit_bytes=48 * MIB),
            name="inproj",
        )(*args)
    assert n_seq * stream.n_chunks <= grid[0] * grid[1]
    s_in, s_out, s_shapes, s_scratch, s_args, s_kw = _stream_operands(
        stream, 0, n_seq, lambda i, j: (i * grid[1] + j) // stream.n_chunks)
    return pl.pallas_call(
        functools.partial(_inproj_stream_kernel, n_col_tiles=grid[1], **s_kw),
        grid=grid, in_specs=in_specs + s_in, out_specs=[z_spec] + s_out,
        out_shape=[z_shape] + s_shapes, scratch_shapes=[h_scratch] + s_scratch,
        compiler_params=pltpu.CompilerParams(
            dimension_semantics=("arbitrary", "arbitrary"), vmem_limit_bytes=60 * MIB),
        name="inproj",
    )(*args, *s_args)


def _kv_rows_kernel(k_ref, v_ref, o_ref):
    tm = k_ref.shape[0]
    hd = o_ref.shape[1]
    n_heads = k_ref.shape[1] // hd
    for h in range(n_heads):
        o_ref[pl.ds(2 * h, tm, stride=2 * n_heads), :] = k_ref[:, h * hd:(h + 1) * hd]
        o_ref[pl.ds(2 * h + 1, tm, stride=2 * n_heads), :] = v_ref[:, h * hd:(h + 1) * hd]


def _kv_rows(z, d_attn, tm):
    m = z.shape[0]
    rows_kv = 2 * d_attn // HEAD_DIM
    return pl.pallas_call(
        _kv_rows_kernel,
        grid=(m // tm,),
        in_specs=[pl.BlockSpec((tm, d_attn), lambda i: (i, 1)),
                  pl.BlockSpec((tm, d_attn), lambda i: (i, 2))],
        out_specs=pl.BlockSpec((tm * rows_kv, HEAD_DIM), lambda i: (i, 0)),
        out_shape=jax.ShapeDtypeStruct((m * rows_kv, HEAD_DIM), F32),
        compiler_params=pltpu.CompilerParams(
            dimension_semantics=("parallel",), vmem_limit_bytes=48 * MIB),
        name="kv_rows",
    )(z, z)


def _attn_block(qb, kw, vw, tab, scale):
    s = lax.dot_general(qb.astype(BF16), kw.astype(BF16), (((1,), (1,)), ((), ())),
                        preferred_element_type=F32)
    s = s * scale + tab
    m = jnp.max(s, axis=-1, keepdims=True)
    p = jnp.exp(s - m).astype(BF16)
    v_ones = jnp.concatenate([vw.astype(BF16), jnp.ones(vw.shape, BF16)], axis=1)
    acc_l = jnp.dot(p, v_ones, preferred_element_type=F32)
    d = vw.shape[1]
    return acc_l[:, :d], m, acc_l[:, d:]


def _attn_prompt_kernel(q_ref, k_ref, v_ref, g_ref, tab_ref, o_ref, acc_ref, m_ref, l_ref):
    seq = q_ref.shape[0]
    scale = HEAD_DIM ** -0.5
    qb_rows = Q_BLOCK
    nk = KEYS_PER_PATTERN

    def rows(ref, start, size, stride):
        if stride == 1:
            return ref[pl.ds(start, size), :]
        return ref[pl.ds(start, size, stride=stride), :]

    def put(p, start, stride, acc, m, l):
        lanes = acc.shape[-1]
        if stride == 1:
            idx = pl.ds(start, qb_rows)
        else:
            idx = pl.ds(start, qb_rows, stride=stride)
        acc_ref[p, idx, :] = acc
        m_ref[p, idx, :] = jnp.broadcast_to(m, (qb_rows, lanes))
        l_ref[p, idx, :] = l

    def first_block(p, phase, dil):
        tab = tab_ref[p][:, nk:]
        qb = rows(q_ref, phase, qb_rows, dil)
        kw = rows(k_ref, phase, qb_rows, dil)
        vw = rows(v_ref, phase, qb_rows, dil)
        put(p, phase, dil, *_attn_block(qb, kw, vw, tab, scale))

    def later_block(p, phase, dil, n):
        tab = tab_ref[p]
        q0 = phase + dil * qb_rows * n
        k0 = q0 - dil * nk
        qb = rows(q_ref, q0, qb_rows, dil)
        kw = rows(k_ref, k0, qb_rows + nk, dil)
        vw = rows(v_ref, k0, qb_rows + nk, dil)
        put(p, q0, dil, *_attn_block(qb, kw, vw, tab, scale))

    for p, (window, dil) in enumerate(PATTERNS):
        n_blocks = seq // dil // qb_rows
        for phase in range(dil):
            first_block(p, phase, dil)
            for n in range(1, n_blocks):
                later_block(p, phase, dil, n)

    chunk = 256

    def combine(c, carry):
        sl = pl.ds(pl.multiple_of(c * chunk, chunk), chunk)
        m0, m1, m2 = m_ref[0, sl, :], m_ref[1, sl, :], m_ref[2, sl, :]
        mm = jnp.maximum(jnp.maximum(m0, m1), m2)
        e0, e1, e2 = jnp.exp(m0 - mm), jnp.exp(m1 - mm), jnp.exp(m2 - mm)
        num = e0 * acc_ref[0, sl, :] + e1 * acc_ref[1, sl, :] + e2 * acc_ref[2, sl, :]
        den = e0 * l_ref[0, sl, :] + e1 * l_ref[1, sl, :] + e2 * l_ref[2, sl, :]
        o_ref[sl, :] = (num / den * _silu(g_ref[sl, :])).astype(o_ref.dtype)
        return carry
    lax.fori_loop(0, seq // chunk, combine, 0)


def _attn_prompt_stream_kernel(q_ref, k_ref, v_ref, g_ref, tab_ref, *rest, n_heads, **stream_kw):
    stream_in, (o_ref, so_ref, out_any), scratch = rest[:8], rest[9:12], rest[12:]
    acc_ref, m_ref, l_ref = scratch[:3]
    _attn_prompt_kernel(q_ref, k_ref, v_ref, g_ref, tab_ref, o_ref, acc_ref, m_ref, l_ref)
    k = pl.program_id(0) * n_heads + pl.program_id(1)
    _stream_step_with_attention(k, stream_in, so_ref, out_any, scratch[3:], **stream_kw)


def _attn_prompt(z3, tabs, n_heads, stream, seq0, n_seq, partial_cache):
    b, seq, _ = z3.shape
    hd = HEAD_DIM
    assert 0 < n_seq * stream.n_chunks <= b * n_heads

    def col(off):
        return pl.BlockSpec((None, seq, hd), lambda i, h: (i, 0, off + h))

    def seq_of(i, h):
        return (i * n_heads + h) // stream.n_chunks

    s_in, s_out, s_shapes, s_scratch, s_args, s_kw = _stream_operands(stream, seq0, n_seq, seq_of)
    in_specs = [col(0), col(n_heads), col(2 * n_heads), col(3 * n_heads),
                pl.BlockSpec((len(PATTERNS), None, Q_BLOCK, Q_BLOCK + KEYS_PER_PATTERN),
                             lambda i, h: (0, h, 0, 0))] + s_in + [pl.BlockSpec(memory_space=pl.ANY)]
    return pl.pallas_call(
        functools.partial(_attn_prompt_stream_kernel, n_heads=n_heads, **s_kw),
        grid=(b, n_heads),
        in_specs=in_specs,
        out_specs=[pl.BlockSpec((None, seq, hd), lambda i, h: (i, 0, h))] + s_out,
        out_shape=[jax.ShapeDtypeStruct((b, seq, n_heads * hd), BF16)] + s_shapes,
        scratch_shapes=[pltpu.VMEM((len(PATTERNS), seq, hd), F32)] * 3 + s_scratch,
        input_output_aliases={len(in_specs) - 1: 2},
        compiler_params=pltpu.CompilerParams(
            dimension_semantics=("arbitrary", "arbitrary"), vmem_limit_bytes=56 * MIB),
        name="attn_prompt",
    )(z3, z3, z3, z3, tabs, *s_args, partial_cache)


def _conv_kernel(ca_ref, cb_ref, gc_ref, pre_ref, dww_ref, dwb_ref, lng_ref, lnb_ref,
                 pww_ref, pwb_ref, o_ref, newc_ref, upad_ref, *, chunk):
    t_len = ca_ref.shape[0]
    rows_step = o_ref.shape[0]
    hist = CONV_WIDTH - 1
    rb = pl.program_id(1)

    @pl.when(rb == 0)
    def _():
        u = ca_ref[...] * jax.nn.sigmoid(cb_ref[...])
        upad_ref[0:hist, :] = pre_ref[...]
        upad_ref[hist:hist + t_len, :] = u
        n_pad = upad_ref.shape[0] - (hist + t_len)
        upad_ref[hist + t_len:, :] = jnp.zeros((n_pad, ca_ref.shape[1]), F32)
        newc_ref[...] = upad_ref[t_len:t_len + hist, :]

    win_rows = upad_ref.shape[0] - t_len + chunk

    def body(c, carry):
        l0 = c * chunk
        r0 = l0 if rows_step == t_len else rb * rows_step + l0
        if chunk % SUBLANES == 0:
            l0, r0 = pl.multiple_of(l0, SUBLANES), pl.multiple_of(r0, SUBLANES)
        win = upad_ref[pl.ds(r0, win_rows), :]
        y = jnp.zeros((chunk, ca_ref.shape[1]), F32) + dwb_ref[...]
        for s in range(SUBLANES):
            shifted = win[s:s + win_rows - SUBLANES]
            for a in range(-(-CONV_WIDTH // SUBLANES)):
                w = SUBLANES * a + s
                if w < CONV_WIDTH:
                    y = y + shifted[SUBLANES * a:SUBLANES * a + chunk] * dww_ref[w:w + 1, :]
        mu = jnp.mean(y, axis=-1, keepdims=True)
        var = jnp.mean(jnp.square(y - mu), axis=-1, keepdims=True)
        yn = (y - mu) * lax.rsqrt(var + EPS) * lng_ref[...] + lnb_ref[...]
        c_act = _silu(yn).astype(BF16)
        proj = jnp.dot(c_act, pww_ref[...], preferred_element_type=F32) + pwb_ref[...]
        o_ref[pl.ds(l0, chunk), :] = (proj * _silu(gc_ref[pl.ds(r0, chunk), :])).astype(o_ref.dtype)
        return carry
    if rows_step == chunk:
        body(0, 0)
    else:
        lax.fori_loop(0, rows_step // chunk, body, 0)


def _conv_stream_kernel(*refs, row_blocks, **stream_kw):
    chunk = stream_kw.pop("chunk")
    _conv_kernel(*refs[:10], refs[19], refs[20], refs[23], chunk=chunk)
    k = pl.program_id(0) * row_blocks + pl.program_id(1)
    _stream_step_with_attention(k, refs[10:18], refs[21], refs[22], refs[24:], **stream_kw)


def _conv_branch(z3, prefix, dw_w, dw_b, ln_g, ln_b, pw_w_bf, pw_b, col0, row_blocks=1,
                 stream=None, seq0=0, n_seq=0, partial_cache=None):
    n, t_len, _ = z3.shape
    c = prefix.shape[-1]
    hist = CONV_WIDTH - 1
    rows_step = t_len // row_blocks
    chunk = min(rows_step, 64)
    cblk = col0 // c

    def zc(j):
        return pl.BlockSpec((None, t_len, c), lambda i, r: (i, 0, cblk + j))

    def vec():
        return pl.BlockSpec((1, c), lambda i, r: (0, 0))

    in_specs = [zc(0), zc(1), zc(2),
                pl.BlockSpec((None, hist, c), lambda i, r: (i, 0, 0)),
                pl.BlockSpec((CONV_WIDTH, c), lambda i, r: (0, 0)),
                vec(), vec(), vec(),
                pl.BlockSpec((c, c), lambda i, r: (0, 0)),
                vec()]
    out_specs = [pl.BlockSpec((None, rows_step, c), lambda i, r: (i, r, 0)),
                 pl.BlockSpec((None, hist, c), lambda i, r: (i, 0, 0))]
    out_shape = [jax.ShapeDtypeStruct((n, t_len, c), BF16),
                 jax.ShapeDtypeStruct((n, hist, c), F32)]
    scratch = [pltpu.VMEM((t_len - chunk + _round_up(chunk + CONV_WIDTH + 1, SUBLANES), c), F32)]
    args = (z3, z3, z3, prefix, dw_w, dw_b.reshape(1, c), ln_g.reshape(1, c), ln_b.reshape(1, c),
            pw_w_bf, pw_b.reshape(1, c))
    if stream is None:
        return pl.pallas_call(
            functools.partial(_conv_kernel, chunk=chunk),
            grid=(n, row_blocks), in_specs=in_specs, out_specs=out_specs, out_shape=out_shape,
            scratch_shapes=scratch,
            compiler_params=pltpu.CompilerParams(
                dimension_semantics=("parallel", "arbitrary"), vmem_limit_bytes=48 * MIB),
            name="conv_branch",
        )(*args)
    assert 0 < n_seq * stream.n_chunks <= n * row_blocks
    s_in, s_out, s_shapes, s_scratch, s_args, s_kw = _stream_operands(
        stream, seq0, n_seq, lambda i, r: (i * row_blocks + r) // stream.n_chunks)
    in_specs = in_specs + s_in + [pl.BlockSpec(memory_space=pl.ANY)]
    return pl.pallas_call(
        functools.partial(_conv_stream_kernel, row_blocks=row_blocks, chunk=chunk, **s_kw),
        grid=(n, row_blocks), in_specs=in_specs, out_specs=out_specs + s_out,
        out_shape=out_shape + s_shapes, scratch_shapes=scratch + s_scratch,
        input_output_aliases={len(in_specs) - 1: 3},
        compiler_params=pltpu.CompilerParams(
            dimension_semantics=("arbitrary", "arbitrary"), vmem_limit_bytes=60 * MIB),
        name="conv_branch",
    )(*args, *s_args, partial_cache)


def _outproj_kernel(ma_ref, mc_ref, wa_ref, wc_ref, x_ref, g_ref, y_ref):
    y = jnp.dot(ma_ref[...], wa_ref[...], preferred_element_type=F32)
    y = y + jnp.dot(mc_ref[...], wc_ref[...], preferred_element_type=F32)
    ms = jnp.mean(y * y, axis=-1, keepdims=True)
    y_ref[...] = x_ref[...] + y * lax.rsqrt(ms + EPS) * g_ref[...]


def _outproj(mix_att, mix_conv, w_att_bf, w_conv_bf, x2d, norm_g, tm):
    m, d = x2d.shape
    da, dc = mix_att.shape[1], mix_conv.shape[1]
    return pl.pallas_call(
        _outproj_kernel,
        grid=(m // tm,),
        in_specs=[pl.BlockSpec((tm, da), lambda i: (i, 0)),
                  pl.BlockSpec((tm, dc), lambda i: (i, 0)),
                  pl.BlockSpec((da, d), lambda i: (0, 0)),
                  pl.BlockSpec((dc, d), lambda i: (0, 0)),
                  pl.BlockSpec((tm, d), lambda i: (i, 0)),
                  pl.BlockSpec((1, d), lambda i: (0, 0))],
        out_specs=pl.BlockSpec((tm, d), lambda i: (i, 0)),
        out_shape=jax.ShapeDtypeStruct((m, d), F32),
        compiler_params=pltpu.CompilerParams(
            dimension_semantics=("parallel",), vmem_limit_bytes=48 * MIB),
        name="outproj",
    )(mix_att, mix_conv, w_att_bf, w_conv_bf, x2d, norm_g.reshape(1, d))


NEAR_ROWS = 512
FAR_STRIDE = 16


RING = 3


def _sample_heads(q_ref, k_ref, v_ref, g_ref, near_ref, far_ref, tab_ref, mult_ref, o_ref, rows_kv):
    t_new = q_ref.shape[0]
    hd = near_ref.shape[1]
    scale = HEAD_DIM ** -0.5
    pad_q = jnp.zeros((SUBLANES - t_new, hd), F32)
    pad_kv = jnp.zeros((hd - t_new, hd), F32)
    nt = (((1,), (1,)), ((), ()))
    n_far = far_ref.shape[0] // rows_kv
    mult = mult_ref[...]
    n_heads = rows_kv // 2

    def head_rows(ref, h, n, parity):
        return ref[pl.ds(2 * h + parity, n, stride=rows_kv), :].astype(BF16)

    def pad_bf(ref, h, pad):
        return jnp.concatenate([ref[:, h * hd:(h + 1) * hd], pad], axis=0).astype(BF16)

    scores = []
    for h in range(n_heads):
        q8 = pad_bf(q_ref, h, pad_q)
        scores.append(jnp.concatenate(
            [lax.dot_general(q8, head_rows(near_ref, h, NEAR_ROWS, 0), nt, preferred_element_type=F32),
             lax.dot_general(q8, head_rows(far_ref, h, n_far, 0), nt, preferred_element_type=F32),
             lax.dot_general(q8, pad_bf(k_ref, h, pad_kv), nt, preferred_element_type=F32)], axis=1))
    probs = []
    for h, s in enumerate(scores):
        s = s * scale + tab_ref[h]
        m = jnp.max(s, axis=-1, keepdims=True)
        p = jnp.exp(s - m) * mult
        probs.append((p.astype(BF16), jnp.sum(p, axis=-1, keepdims=True)))
    for h, (pb, l) in enumerate(probs):
        acc = jnp.dot(pb[:, :NEAR_ROWS], head_rows(near_ref, h, NEAR_ROWS, 1),
                      preferred_element_type=F32)
        acc = acc + jnp.dot(pb[:, NEAR_ROWS:NEAR_ROWS + n_far], head_rows(far_ref, h, n_far, 1),
                            preferred_element_type=F32)
        acc = acc + jnp.dot(pb[:, NEAR_ROWS + n_far:], pad_bf(v_ref, h, pad_kv),
                            preferred_element_type=F32)
        cols = slice(h * hd, (h + 1) * hd)
        o_ref[:, cols] = (acc / l)[:t_new] * _silu(g_ref[:, cols])


def _cache_stream_step(k, n_steps, seq0, n_chunks, cache_any, kvn_any, out_any, buf_ref, far_ref,
                       sem_in, sem_out, sem_tail, shift_sl, rows_kv, attend):
    chunk_sl = buf_ref.shape[1]
    far_per_chunk = NEAR_ROWS // FAR_STRIDE
    seq_sl = n_chunks * chunk_sl

    def chunk_in(j):
        return pltpu.make_async_copy(
            cache_any.at[seq0 + j // n_chunks, pl.ds((j % n_chunks) * chunk_sl, chunk_sl)],
            buf_ref.at[j % RING], sem_in.at[j % RING])

    def first_out(j):
        return pltpu.make_async_copy(
            buf_ref.at[j % RING, pl.ds(shift_sl, chunk_sl - shift_sl)],
            out_any.at[seq0 + j // n_chunks, pl.ds(0, chunk_sl - shift_sl)], sem_out.at[j % RING])

    def later_out(j):
        return pltpu.make_async_copy(
            buf_ref.at[j % RING],
            out_any.at[seq0 + j // n_chunks, pl.ds((j % n_chunks) * chunk_sl - shift_sl, chunk_sl)],
            sem_out.at[j % RING])

    def tail(j):
        return pltpu.make_async_copy(
            kvn_any.at[seq0 + j // n_chunks],
            out_any.at[seq0 + j // n_chunks, pl.ds(seq_sl - shift_sl, shift_sl)], sem_tail.at[0])

    def on_chunk(j, first, later):
        pl.when(j % n_chunks == 0)(first)
        pl.when(j % n_chunks != 0)(later)

    @pl.when(k == 0)
    def _():
        for j in range(RING - 1):
            chunk_in(jnp.int32(j)).start()

    c = k % n_chunks
    slot = k % RING
    chunk_in(k).wait()
    on_chunk(k, lambda: first_out(k).start(), lambda: later_out(k).start())
    near_ref = buf_ref.at[slot]

    for grp in range(far_per_chunk):
        dst = pl.multiple_of((c * far_per_chunk + grp) * shift_sl, SUBLANES)
        src = grp * FAR_STRIDE * rows_kv
        far_ref[pl.ds(dst, shift_sl), :] = near_ref[src:src + shift_sl, :]

    @pl.when(c == n_chunks - 1)
    def _():
        tail(k).start()
        attend(near_ref)
        tail(k).wait()

    @pl.when(k >= 1)
    def _():
        on_chunk(k - 1, lambda: first_out(k - 1).wait(), lambda: later_out(k - 1).wait())

    @pl.when(k + RING - 1 < n_steps)
    def _():
        chunk_in(k + RING - 1).start()

    @pl.when(k == n_steps - 1)
    def _():
        on_chunk(k, lambda: first_out(k).wait(), lambda: later_out(k).wait())


class _Stream(NamedTuple):
    zs3: jax.Array
    kvn_rows: jax.Array
    cache2: jax.Array
    tab: jax.Array
    mult: jax.Array
    d_attn: int
    rows_kv: int
    n_chunks: int


def _stream_operands(stream, seq0, n_seq, seq_of):
    t_new = stream.zs3.shape[1]
    hd = stream.cache2.shape[2]
    chunk_sl = stream.cache2.shape[1] // stream.n_chunks
    n_far = stream.cache2.shape[1] // stream.rows_kv // FAR_STRIDE * t_new

    def local(*g):
        return jnp.minimum(seq_of(*g), n_seq - 1)

    def zcol(j):
        return pl.BlockSpec((None, t_new, stream.d_attn), lambda *g: (seq0 + local(*g), 0, j))

    in_specs = [zcol(0), zcol(1), zcol(2), zcol(3),
                pl.BlockSpec(stream.tab.shape, lambda *g: (0, 0, 0)),
                pl.BlockSpec(stream.mult.shape, lambda *g: (0, 0)),
                pl.BlockSpec(memory_space=pl.ANY),
                pl.BlockSpec(memory_space=pl.ANY)]
    out_specs = [pl.BlockSpec((None, t_new, stream.d_attn), lambda *g: (local(*g), 0, 0)),
                 pl.BlockSpec(memory_space=pl.ANY)]
    out_shapes = [jax.ShapeDtypeStruct((n_seq, t_new, stream.d_attn), F32),
                  jax.ShapeDtypeStruct(stream.cache2.shape, stream.cache2.dtype)]
    scratch = [pltpu.VMEM((RING, chunk_sl, hd), F32),
               pltpu.VMEM((n_far * stream.rows_kv, hd), F32),
               pltpu.SemaphoreType.DMA((RING,)),
               pltpu.SemaphoreType.DMA((RING,)),
               pltpu.SemaphoreType.DMA((1,))]
    args = (stream.zs3,) * 4 + (stream.tab, stream.mult, stream.cache2, stream.kvn_rows)
    kw = dict(rows_kv=stream.rows_kv, seq0=seq0, n_chunks=stream.n_chunks,
              n_steps=n_seq * stream.n_chunks)
    return in_specs, out_specs, out_shapes, scratch, args, kw


def _stream_step_with_attention(k, stream_in, o_ref, out_any, scratch, *, rows_kv, seq0, n_chunks,
                                n_steps):
    q_ref, k_ref, v_ref, g_ref, tab_ref, mult_ref, cache_any, kvn_any = stream_in
    buf_ref, far_ref, sem_in, sem_out, sem_tail = scratch
    shift_sl = q_ref.shape[0] * rows_kv

    def attend(near_ref):
        _sample_heads(q_ref, k_ref, v_ref, g_ref, near_ref, far_ref, tab_ref, mult_ref, o_ref, rows_kv)

    @pl.when(k < n_steps)
    def _():
        _cache_stream_step(k, n_steps, seq0, n_chunks, cache_any, kvn_any, out_any, buf_ref, far_ref,
                           sem_in, sem_out, sem_tail, shift_sl, rows_kv, attend)


def _attn_sample_kernel(*refs, n_chunks, **stream_kw):
    k = pl.program_id(0) * n_chunks + pl.program_id(1)
    _stream_step_with_attention(k, refs[:8], refs[9], refs[10], refs[11:], n_chunks=n_chunks,
                                **stream_kw)


def _attn_sample(stream, seq0, n_seq, partial_cache):
    s_in, s_out, s_shapes, s_scratch, s_args, s_kw = _stream_operands(
        stream, seq0, n_seq, lambda i, c: i)
    return pl.pallas_call(
        functools.partial(_attn_sample_kernel, **s_kw),
        grid=(n_seq, stream.n_chunks),
        in_specs=s_in + [pl.BlockSpec(memory_space=pl.ANY)],
        out_specs=s_out,
        out_shape=s_shapes,
        scratch_shapes=s_scratch,
        input_output_aliases={len(s_in): 1},
        compiler_params=pltpu.CompilerParams(
            dimension_semantics=("arbitrary", "arbitrary"), vmem_limit_bytes=48 * MIB),
        name="attn_sample",
    )(*s_args, partial_cache)


def _prompt_bias_tables(rel_bias):
    nk = KEYS_PER_PATTERN
    qb = Q_BLOCK
    kdist = np.arange(-(qb - 1), qb + nk)
    n_dist = kdist.shape[0]
    valid = (kdist >= 0) & (kdist <= nk)
    tabs = []
    for _, dil in PATTERNS:
        bucket = _rel_bucket(jnp.asarray(np.clip(kdist, 0, nk) * dil, jnp.int32))
        per_dist = jnp.where(valid[:, None], rel_bias[bucket].astype(F32), NEG_INF)
        rev = per_dist.T[:, ::-1]
        skew = jnp.tile(rev, (1, qb + 1))[:, qb - 1:qb - 1 + qb * (n_dist - 1)]
        tabs.append(skew.reshape(-1, qb, n_dist - 1)[:, :, :qb + nk])
    return jnp.stack(tabs)


def _pattern_count(dist, patterns):
    return sum(((dist % dil == 0) & (dist >= 0) & (dist <= window)).astype(np.int32)
               for window, dil in patterns)


def _sample_tables(rel_bias, t_new, past):
    def bias_at(dist):
        return rel_bias[_rel_bucket(jnp.asarray(dist, jnp.int32))].astype(F32)

    def masked(bias, count):
        return jnp.where(jnp.asarray(count > 0)[..., None], bias, NEG_INF)

    near_pats, far_pats = PATTERNS[:2], PATTERNS[2:]
    assert near_pats[-1][0] == NEAR_ROWS and far_pats[0][1] == FAR_STRIDE and t_new <= far_pats[0][1]
    desc = np.arange(NEAR_ROWS + t_new - 1, 0, -1)
    desc_cnt = _pattern_count(desc, near_pats)
    desc_tab = masked(bias_at(desc), desc_cnt)
    starts = [t_new - 1 - t for t in range(t_new)]
    near_tab = jnp.stack([desc_tab[s0:s0 + NEAR_ROWS] for s0 in starts])
    near_cnt = np.stack([desc_cnt[s0:s0 + NEAR_ROWS] for s0 in starts])
    groups = past // FAR_STRIDE
    far_dist = past - FAR_STRIDE * np.arange(groups)
    own = np.eye(t_new, dtype=bool)[:, None, :] & (_pattern_count(far_dist, far_pats) > 0)[None, :, None]
    far_tab = jnp.where(jnp.asarray(own)[..., None], bias_at(far_dist)[None, :, None, :], NEG_INF)
    far_tab = far_tab.reshape(t_new, groups * t_new, -1)
    far_cnt = np.ones((t_new, groups * t_new), np.int32)
    tj = np.arange(t_new)[:, None] - np.arange(HEAD_DIM)[None, :]
    new_cnt = np.where(np.arange(HEAD_DIM)[None, :] < t_new, _pattern_count(tj, PATTERNS), 0)
    new_tab = masked(bias_at(np.clip(tj, 0, None).reshape(-1)).reshape(t_new, HEAD_DIM, -1), new_cnt)
    tab = jnp.concatenate([near_tab, far_tab, new_tab], axis=1).transpose(2, 0, 1)
    cnt = np.concatenate([near_cnt, far_cnt, new_cnt], axis=1)
    pad = SUBLANES - t_new
    tab = jnp.pad(tab, ((0, 0), (0, pad), (0, 0)))
    mult = np.pad(np.maximum(cnt, 1), ((0, pad), (0, 0)), constant_values=1).astype(np.float32)
    return tab, jnp.asarray(mult)


def kernel(x_prompt, x_sample, cache_conv, cache_kv, rel_bias, norm_pre, w_in, conv_dw_w, conv_dw_b,
           conv_ln_g, conv_ln_b, conv_pw_w, conv_pw_b, w_out, norm_post):
    depth = w_in.shape[0]
    assert depth == 1
    bsz, seq, d_model = x_prompt.shape
    n_dec, t_new, _ = x_sample.shape
    n_heads = cache_kv.shape[4]
    d_attn = n_heads * HEAD_DIM
    d_conv = cache_conv.shape[-1]
    past = cache_kv.shape[2]
    assert past == MAX_WINDOW and seq >= MAX_WINDOW and t_new <= 4
    hist = CONV_WIDTH - 1

    w_in_bf = w_in[0].astype(BF16)
    w_out_bf = w_out[0].astype(BF16)
    pw_bf = conv_pw_w[0].astype(BF16)
    conv_args = (conv_dw_w[0], conv_dw_b[0], conv_ln_g[0], conv_ln_b[0], pw_bf, conv_pw_b[0])
    conv_col0 = 4 * d_attn

    xp2 = x_prompt.reshape(bsz * seq, d_model)
    xs2 = x_sample.reshape(n_dec * t_new, d_model)
    zs = _inproj(xs2, norm_pre[0], w_in_bf, tm=n_dec * t_new, tn=512)
    zs3 = zs.reshape(n_dec, t_new, -1)

    def heads(col0):
        return zs3[:, :, col0:col0 + d_attn].reshape(n_dec, t_new, n_heads, HEAD_DIM)

    rows_kv = 2 * n_heads
    kvn_rows = jnp.stack([heads(d_attn), heads(2 * d_attn)], axis=3).reshape(
        n_dec, t_new * rows_kv, HEAD_DIM)
    cache2 = cache_kv[0].transpose(0, 1, 3, 2, 4).reshape(n_dec, past * rows_kv, HEAD_DIM)
    tab_s, mult_s = _sample_tables(rel_bias, t_new, past)
    assert past % NEAR_ROWS == 0 and t_new <= SUBLANES
    stream = _Stream(zs3, kvn_rows, cache2, tab_s, mult_s, d_attn, rows_kv, past // NEAR_ROWS)

    conv_row_blocks = 8
    n_in_conv = min(bsz * conv_row_blocks // stream.n_chunks, n_dec - 1)
    n_in_attn = min(bsz * n_heads // stream.n_chunks, n_dec - n_in_conv)
    n_in_proj = n_dec - n_in_attn - n_in_conv
    tm_p, tn_p = 1024, 512
    if 0 < n_in_proj * stream.n_chunks <= (bsz * seq // tm_p) * (w_in_bf.shape[1] // tn_p):
        zp, att_s0, part_cache = _inproj(xp2, norm_pre[0], w_in_bf, tm_p, tn_p, stream, n_in_proj)
    else:
        zp = _inproj(xp2, norm_pre[0], w_in_bf, tm_p, tn_p)
        n_in_proj = 0
        att_s0 = jnp.zeros((0, t_new, d_attn), F32)
        part_cache = jnp.zeros(cache2.shape, cache2.dtype)
    zp3 = zp.reshape(bsz, seq, -1)
    mix_att_p, att_s1, part_cache = _attn_prompt(zp3, _prompt_bias_tables(rel_bias), n_heads, stream,
                                                 n_in_proj, n_in_attn, part_cache)
    att_s0 = jnp.concatenate([att_s0, att_s1], axis=0)
    zero_prefix = jnp.zeros((bsz, hist, d_conv), F32)
    if n_in_conv > 0:
        mix_conv_p, new_conv_p, att_s2, part_cache = _conv_branch(
            zp3, zero_prefix, *conv_args, col0=conv_col0, row_blocks=conv_row_blocks,
            stream=stream, seq0=att_s0.shape[0], n_seq=n_in_conv, partial_cache=part_cache)
        att_s0 = jnp.concatenate([att_s0, att_s2], axis=0)
    else:
        mix_conv_p, new_conv_p = _conv_branch(zp3, zero_prefix, *conv_args, col0=conv_col0,
                                              row_blocks=conv_row_blocks)
    yp = _outproj(mix_att_p.reshape(bsz * seq, d_attn), mix_conv_p.reshape(bsz * seq, d_conv),
                  w_out_bf[:d_attn], w_out_bf[d_attn:], xp2, norm_post[0], tm=512)
    win = min(MAX_WINDOW, seq)
    kv_rows_p = _kv_rows(zp, d_attn, tm=512).reshape(bsz, seq, n_heads, 2, HEAD_DIM)
    new_kv_p = kv_rows_p[:, seq - win:].transpose(0, 1, 3, 2, 4)[None]

    n_hosted = att_s0.shape[0]
    att_s, new_rows = att_s0, part_cache
    if n_hosted < n_dec:
        att_s1, new_rows = _attn_sample(stream, n_hosted, n_dec - n_hosted, part_cache)
        att_s = jnp.concatenate([att_s0, att_s1], axis=0)
    new_kv_s = new_rows.reshape(n_dec, past, n_heads, 2, HEAD_DIM).transpose(0, 1, 3, 2, 4)[None]
    mix_att_s = att_s.reshape(n_dec * t_new, d_attn).astype(BF16)
    mix_conv_s, new_conv_s = _conv_branch(zs3, cache_conv[0], *conv_args, col0=conv_col0)
    ys = _outproj(mix_att_s, mix_conv_s.reshape(n_dec * t_new, d_conv),
                  w_out_bf[:d_attn], w_out_bf[d_attn:], xs2, norm_post[0], tm=n_dec * t_new)

    return (yp.reshape(bsz, seq, d_model), ys.reshape(n_dec, t_new, d_model),
            new_conv_p[None], new_kv_p, new_conv_s[None], new_kv_s)
```

```python
import functools
import math
from typing import NamedTuple

import jax
import jax.numpy as jnp
import numpy as np
from jax import lax
from jax.experimental import pallas as pl
from jax.experimental.pallas import tpu as pltpu

F32 = jnp.float32
BF16 = jnp.bfloat16

HEAD_DIM = 128
PATTERNS = ((128, 1), (512, 4), (2048, 16))
MAX_WINDOW = 2048
Q_BLOCK = 128
KEYS_PER_PATTERN = 128
TABLE_LANES = 384
CONV_WIDTH = 31
N_BUCKETS = 32
MAX_EXACT = 16
EPS = 1e-6
NEG_INF = -1e30
SUBLANES = 8
HEAD_PAD = 16

MIB = 1024 * 1024


def _rel_bucket(dist):
    d = jnp.maximum(dist, 1).astype(F32)
    log_b = MAX_EXACT + (jnp.log(d / MAX_EXACT) / math.log(MAX_WINDOW / MAX_EXACT)
                         * (N_BUCKETS - MAX_EXACT)).astype(jnp.int32)
    log_b = jnp.minimum(log_b, N_BUCKETS - 1)
    return jnp.where(dist < MAX_EXACT, dist, log_b)


def _round_up(x, m):
    return -(-x // m) * m


def _silu(x):
    return x * jax.nn.sigmoid(x)


def _inproj_kernel(x_ref, g_ref, w_ref, z_ref, h_ref):
    @pl.when(pl.program_id(1) == 0)
    def _():
        x = x_ref[...]
        ms = jnp.mean(x * x, axis=-1, keepdims=True)
        h_ref[...] = (x * lax.rsqrt(ms + EPS) * g_ref[...]).astype(BF16)

    z_ref[...] = jnp.dot(h_ref[...], w_ref[...], preferred_element_type=F32)


def _inproj_stream_kernel(x_ref, g_ref, w_ref, *rest, n_col_tiles, **stream_kw):
    stream_in, (z_ref, so_ref, out_any), scratch = rest[:8], rest[8:11], rest[11:]
    _inproj_kernel(x_ref, g_ref, w_ref, z_ref, scratch[0])
    k = pl.program_id(0) * n_col_tiles + pl.program_id(1)
    _stream_step_with_attention(k, stream_in, so_ref, out_any, scratch[1:], **stream_kw)


def _inproj(x2d, norm_g, w_bf, tm, tn, stream=None, n_seq=0):
    m, d = x2d.shape
    n = w_bf.shape[1]
    grid = (m // tm, n // tn)
    in_specs = [pl.BlockSpec((tm, d), lambda i, j: (i, 0)),
                pl.BlockSpec((1, d), lambda i, j: (0, 0)),
                pl.BlockSpec((d, tn), lambda i, j: (0, j))]
    z_spec = pl.BlockSpec((tm, tn), lambda i, j: (i, j))
    z_shape = jax.ShapeDtypeStruct((m, n), F32)
    h_scratch = pltpu.VMEM((tm, d), BF16)
    args = (x2d, norm_g.reshape(1, d), w_bf)
    if stream is None:
        return pl.pallas_call(
            _inproj_kernel, grid=grid, in_specs=in_specs, out_specs=z_spec, out_shape=z_shape,
            scratch_shapes=[h_scratch],
            compiler_params=pltpu.CompilerParams(
                dimension_semantics=("parallel", "arbitrary"), vmem_limit_bytes=48 * MIB),
            name="inproj",
        )(*args)
    assert n_seq * stream.n_chunks <= grid[0] * grid[1]
    s_in, s_out, s_shapes, s_scratch, s_args, s_kw = _stream_operands(
        stream, 0, n_seq, lambda i, j: (i * grid[1] + j) // stream.n_chunks)
    return pl.pallas_call(
        functools.partial(_inproj_stream_kernel, n_col_tiles=grid[1], **s_kw),
        grid=grid, in_specs=in_specs + s_in, out_specs=[z_spec] + s_out,
        out_shape=[z_shape] + s_shapes, scratch_shapes=[h_scratch] + s_scratch,
        compiler_params=pltpu.CompilerParams(
            dimension_semantics=("arbitrary", "arbitrary"), vmem_limit_bytes=60 * MIB),
        name="inproj",
    )(*args, *s_args)


def _kv_rows_kernel(k_ref, v_ref, o_ref):
    tm = k_ref.shape[0]
    hd = o_ref.shape[1]
    n_heads = k_ref.shape[1] // hd
    for h in range(n_heads):
        o_ref[pl.ds(2 * h, tm, stride=2 * n_heads), :] = k_ref[:, h * hd:(h + 1) * hd]
        o_ref[pl.ds(2 * h + 1, tm, stride=2 * n_heads), :] = v_ref[:, h * hd:(h + 1) * hd]


def _kv_rows(z, d_attn, tm):
    m = z.shape[0]
    rows_kv = 2 * d_attn // HEAD_DIM
    return pl.pallas_call(
        _kv_rows_kernel,
        grid=(m // tm,),
        in_specs=[pl.BlockSpec((tm, d_attn), lambda i: (i, 1)),
                  pl.BlockSpec((tm, d_attn), lambda i: (i, 2))],
        out_specs=pl.BlockSpec((tm * rows_kv, HEAD_DIM), lambda i: (i, 0)),
        out_shape=jax.ShapeDtypeStruct((m * rows_kv, HEAD_DIM), F32),
        compiler_params=pltpu.CompilerParams(
            dimension_semantics=("parallel",), vmem_limit_bytes=48 * MIB),
        name="kv_rows",
    )(z, z)


def _attn_block(qb, kw, vw, tab, scale):
    s = lax.dot_general(qb.astype(BF16), kw.astype(BF16), (((1,), (1,)), ((), ())),
                        preferred_element_type=F32)
    s = s * scale + tab
    m = jnp.max(s, axis=-1, keepdims=True)
    p = jnp.exp(s - m).astype(BF16)
    v_ones = jnp.concatenate([vw.astype(BF16), jnp.ones(vw.shape, BF16)], axis=1)
    acc_l = jnp.dot(p, v_ones, preferred_element_type=F32)
    d = vw.shape[1]
    return acc_l[:, :d], m, acc_l[:, d:]


def _attn_prompt_kernel(q_ref, k_ref, v_ref, g_ref, vec_ref, o_ref, acc_ref, m_ref, l_ref, tab_ref):
    seq = q_ref.shape[0]
    scale = HEAD_DIM ** -0.5
    qb_rows = Q_BLOCK
    nk = KEYS_PER_PATTERN

    for p in range(len(PATTERNS)):
        base = jnp.broadcast_to(vec_ref[p][0:1, :], (qb_rows, vec_ref.shape[-1]))
        tab_ref[p] = pltpu.roll(base, 0, 1, stride=1, stride_axis=0)[:, :qb_rows + nk]

    def rows(ref, start, size, stride):
        if stride == 1:
            return ref[pl.ds(start, size), :]
        return ref[pl.ds(start, size, stride=stride), :]

    def put(p, start, stride, acc, m, l):
        lanes = acc.shape[-1]
        if stride == 1:
            idx = pl.ds(start, qb_rows)
        else:
            idx = pl.ds(start, qb_rows, stride=stride)
        acc_ref[p, idx, :] = acc
        m_ref[p, idx, :] = jnp.broadcast_to(m, (qb_rows, lanes))
        l_ref[p, idx, :] = l

    def first_block(p, phase, dil):
        tab = tab_ref[p][:, nk:]
        qb = rows(q_ref, phase, qb_rows, dil)
        kw = rows(k_ref, phase, qb_rows, dil)
        vw = rows(v_ref, phase, qb_rows, dil)
        put(p, phase, dil, *_attn_block(qb, kw, vw, tab, scale))

    def later_block(p, phase, dil, n):
        tab = tab_ref[p]
        q0 = phase + dil * qb_rows * n
        k0 = q0 - dil * nk
        qb = rows(q_ref, q0, qb_rows, dil)
        kw = rows(k_ref, k0, qb_rows + nk, dil)
        vw = rows(v_ref, k0, qb_rows + nk, dil)
        put(p, q0, dil, *_attn_block(qb, kw, vw, tab, scale))

    for p, (window, dil) in enumerate(PATTERNS):
        n_blocks = seq // dil // qb_rows
        for phase in range(dil):
            first_block(p, phase, dil)
            for n in range(1, n_blocks):
                later_block(p, phase, dil, n)

    chunk = 256

    def combine(c, carry):
        sl = pl.ds(pl.multiple_of(c * chunk, chunk), chunk)
        m0, m1, m2 = m_ref[0, sl, :], m_ref[1, sl, :], m_ref[2, sl, :]
        mm = jnp.maximum(jnp.maximum(m0, m1), m2)
        e0, e1, e2 = jnp.exp(m0 - mm), jnp.exp(m1 - mm), jnp.exp(m2 - mm)
        num = e0 * acc_ref[0, sl, :] + e1 * acc_ref[1, sl, :] + e2 * acc_ref[2, sl, :]
        den = e0 * l_ref[0, sl, :] + e1 * l_ref[1, sl, :] + e2 * l_ref[2, sl, :]
        o_ref[sl, :] = (num / den * _silu(g_ref[sl, :])).astype(o_ref.dtype)
        return carry
    lax.fori_loop(0, seq // chunk, combine, 0)


def _attn_prompt_stream_kernel(q_ref, k_ref, v_ref, g_ref, tab_ref, *rest, n_heads, **stream_kw):
    stream_in, (o_ref, so_ref, out_any), scratch = rest[:8], rest[9:12], rest[12:]
    _attn_prompt_kernel(q_ref, k_ref, v_ref, g_ref, tab_ref, o_ref, *scratch[:4])
    k = pl.program_id(0) * n_heads + pl.program_id(1)
    _stream_step_with_attention(k, stream_in, so_ref, out_any, scratch[4:], **stream_kw)


def _attn_prompt(z3, tabs, n_heads, stream, seq0, n_seq, partial_cache):
    b, seq, _ = z3.shape
    hd = HEAD_DIM
    assert 0 < n_seq * stream.n_chunks <= b * n_heads

    def col(off):
        return pl.BlockSpec((None, seq, hd), lambda i, h: (i, 0, off + h))

    def seq_of(i, h):
        return (i * n_heads + h) // stream.n_chunks

    s_in, s_out, s_shapes, s_scratch, s_args, s_kw = _stream_operands(stream, seq0, n_seq, seq_of)
    in_specs = [col(0), col(n_heads), col(2 * n_heads), col(3 * n_heads),
                pl.BlockSpec((len(PATTERNS), None, SUBLANES, TABLE_LANES),
                             lambda i, h: (0, h, 0, 0))] + s_in + [pl.BlockSpec(memory_space=pl.ANY)]
    return pl.pallas_call(
        functools.partial(_attn_prompt_stream_kernel, n_heads=n_heads, **s_kw),
        grid=(b, n_heads),
        in_specs=in_specs,
        out_specs=[pl.BlockSpec((None, seq, hd), lambda i, h: (i, 0, h))] + s_out,
        out_shape=[jax.ShapeDtypeStruct((b, seq, n_heads * hd), BF16)] + s_shapes,
        scratch_shapes=([pltpu.VMEM((len(PATTERNS), seq, hd), F32)] * 3
                        + [pltpu.VMEM((len(PATTERNS), Q_BLOCK, Q_BLOCK + KEYS_PER_PATTERN), F32)]
                        + s_scratch),
        input_output_aliases={len(in_specs) - 1: 2},
        compiler_params=pltpu.CompilerParams(
            dimension_semantics=("arbitrary", "arbitrary"), vmem_limit_bytes=56 * MIB),
        name="attn_prompt",
    )(z3, z3, z3, z3, tabs, *s_args, partial_cache)


def _conv_kernel(ca_ref, cb_ref, gc_ref, pre_ref, dww_ref, dwb_ref, lng_ref, lnb_ref,
                 pww_ref, pwb_ref, o_ref, newc_ref, upad_ref, *, chunk):
    t_len = ca_ref.shape[0]
    rows_step = o_ref.shape[0]
    hist = CONV_WIDTH - 1
    rb = pl.program_id(1)

    @pl.when(rb == 0)
    def _():
        u = ca_ref[...] * jax.nn.sigmoid(cb_ref[...])
        upad_ref[0:hist, :] = pre_ref[...]
        upad_ref[hist:hist + t_len, :] = u
        n_pad = upad_ref.shape[0] - (hist + t_len)
        upad_ref[hist + t_len:, :] = jnp.zeros((n_pad, ca_ref.shape[1]), F32)
        newc_ref[...] = upad_ref[t_len:t_len + hist, :]

    win_rows = upad_ref.shape[0] - t_len + chunk

    def body(c, carry):
        l0 = c * chunk
        r0 = l0 if rows_step == t_len else rb * rows_step + l0
        if chunk % SUBLANES == 0:
            l0, r0 = pl.multiple_of(l0, SUBLANES), pl.multiple_of(r0, SUBLANES)
        win = upad_ref[pl.ds(r0, win_rows), :]
        y = jnp.zeros((chunk, ca_ref.shape[1]), F32) + dwb_ref[...]
        for s in range(SUBLANES):
            shifted = win[s:s + win_rows - SUBLANES]
            for a in range(-(-CONV_WIDTH // SUBLANES)):
                w = SUBLANES * a + s
                if w < CONV_WIDTH:
                    y = y + shifted[SUBLANES * a:SUBLANES * a + chunk] * dww_ref[w:w + 1, :]
        mu = jnp.mean(y, axis=-1, keepdims=True)
        var = jnp.mean(jnp.square(y - mu), axis=-1, keepdims=True)
        yn = (y - mu) * lax.rsqrt(var + EPS) * lng_ref[...] + lnb_ref[...]
        c_act = _silu(yn).astype(BF16)
        proj = jnp.dot(c_act, pww_ref[...], preferred_element_type=F32) + pwb_ref[...]
        o_ref[pl.ds(l0, chunk), :] = (proj * _silu(gc_ref[pl.ds(r0, chunk), :])).astype(o_ref.dtype)
        return carry
    if rows_step == chunk:
        body(0, 0)
    else:
        lax.fori_loop(0, rows_step // chunk, body, 0)


def _conv_stream_kernel(*refs, row_blocks, **stream_kw):
    chunk = stream_kw.pop("chunk")
    _conv_kernel(*refs[:10], refs[19], refs[20], refs[23], chunk=chunk)
    k = pl.program_id(0) * row_blocks + pl.program_id(1)
    _stream_step_with_attention(k, refs[10:18], refs[21], refs[22], refs[24:], **stream_kw)


def _conv_branch(z3, prefix, dw_w, dw_b, ln_g, ln_b, pw_w_bf, pw_b, col0, row_blocks=1,
                 stream=None, seq0=0, n_seq=0, partial_cache=None):
    n, t_len, _ = z3.shape
    c = prefix.shape[-1]
    hist = CONV_WIDTH - 1
    rows_step = t_len // row_blocks
    chunk = min(rows_step, 64)
    cblk = col0 // c

    def zc(j):
        return pl.BlockSpec((None, t_len, c), lambda i, r: (i, 0, cblk + j))

    def vec():
        return pl.BlockSpec((1, c), lambda i, r: (0, 0))

    in_specs = [zc(0), zc(1), zc(2),
                pl.BlockSpec((None, hist, c), lambda i, r: (i, 0, 0)),
                pl.BlockSpec((CONV_WIDTH, c), lambda i, r: (0, 0)),
                vec(), vec(), vec(),
                pl.BlockSpec((c, c), lambda i, r: (0, 0)),
                vec()]
    out_specs = [pl.BlockSpec((None, rows_step, c), lambda i, r: (i, r, 0)),
                 pl.BlockSpec((None, hist, c), lambda i, r: (i, 0, 0))]
    out_shape = [jax.ShapeDtypeStruct((n, t_len, c), BF16),
                 jax.ShapeDtypeStruct((n, hist, c), F32)]
    scratch = [pltpu.VMEM((t_len - chunk + _round_up(chunk + CONV_WIDTH + 1, SUBLANES), c), F32)]
    args = (z3, z3, z3, prefix, dw_w, dw_b.reshape(1, c), ln_g.reshape(1, c), ln_b.reshape(1, c),
            pw_w_bf, pw_b.reshape(1, c))
    if stream is None:
        return pl.pallas_call(
            functools.partial(_conv_kernel, chunk=chunk),
            grid=(n, row_blocks), in_specs=in_specs, out_specs=out_specs, out_shape=out_shape,
            scratch_shapes=scratch,
            compiler_params=pltpu.CompilerParams(
                dimension_semantics=("parallel", "arbitrary"), vmem_limit_bytes=48 * MIB),
            name="conv_branch",
        )(*args)
    assert 0 < n_seq * stream.n_chunks <= n * row_blocks
    s_in, s_out, s_shapes, s_scratch, s_args, s_kw = _stream_operands(
        stream, seq0, n_seq, lambda i, r: (i * row_blocks + r) // stream.n_chunks)
    in_specs = in_specs + s_in + [pl.BlockSpec(memory_space=pl.ANY)]
    return pl.pallas_call(
        functools.partial(_conv_stream_kernel, row_blocks=row_blocks, chunk=chunk, **s_kw),
        grid=(n, row_blocks), in_specs=in_specs, out_specs=out_specs + s_out,
        out_shape=out_shape + s_shapes, scratch_shapes=scratch + s_scratch,
        input_output_aliases={len(in_specs) - 1: 3},
        compiler_params=pltpu.CompilerParams(
            dimension_semantics=("arbitrary", "arbitrary"), vmem_limit_bytes=60 * MIB),
        name="conv_branch",
    )(*args, *s_args, partial_cache)


def _outproj_kernel(ma_ref, mc_ref, wa_ref, wc_ref, x_ref, g_ref, y_ref):
    y = jnp.dot(ma_ref[...], wa_ref[...], preferred_element_type=F32)
    y = y + jnp.dot(mc_ref[...], wc_ref[...], preferred_element_type=F32)
    ms = jnp.mean(y * y, axis=-1, keepdims=True)
    y_ref[...] = x_ref[...] + y * lax.rsqrt(ms + EPS) * g_ref[...]


def _outproj(mix_att, mix_conv, w_att_bf, w_conv_bf, x2d, norm_g, tm):
    m, d = x2d.shape
    da, dc = mix_att.shape[1], mix_conv.shape[1]
    return pl.pallas_call(
        _outproj_kernel,
        grid=(m // tm,),
        in_specs=[pl.BlockSpec((tm, da), lambda i: (i, 0)),
                  pl.BlockSpec((tm, dc), lambda i: (i, 0)),
                  pl.BlockSpec((da, d), lambda i: (0, 0)),
                  pl.BlockSpec((dc, d), lambda i: (0, 0)),
                  pl.BlockSpec((tm, d), lambda i: (i, 0)),
                  pl.BlockSpec((1, d), lambda i: (0, 0))],
        out_specs=pl.BlockSpec((tm, d), lambda i: (i, 0)),
        out_shape=jax.ShapeDtypeStruct((m, d), F32),
        compiler_params=pltpu.CompilerParams(
            dimension_semantics=("parallel",), vmem_limit_bytes=48 * MIB),
        name="outproj",
    )(mix_att, mix_conv, w_att_bf, w_conv_bf, x2d, norm_g.reshape(1, d))


NEAR_ROWS = 512
FAR_STRIDE = 16


RING = 3


def _sample_heads(q_ref, k_ref, v_ref, g_ref, near_ref, far_ref, tab_ref, mult_ref, o_ref, rows_kv):
    t_new = q_ref.shape[0]
    hd = near_ref.shape[1]
    scale = HEAD_DIM ** -0.5
    pad_q = jnp.zeros((SUBLANES - t_new, hd), F32)
    pad_kv = jnp.zeros((hd - t_new, hd), F32)
    nt = (((1,), (1,)), ((), ()))
    n_far = far_ref.shape[0] // rows_kv
    mult = mult_ref[...]
    n_heads = rows_kv // 2

    def head_rows(ref, h, n, parity):
        return ref[pl.ds(2 * h + parity, n, stride=rows_kv), :].astype(BF16)

    def pad_bf(ref, h, pad):
        return jnp.concatenate([ref[:, h * hd:(h + 1) * hd], pad], axis=0).astype(BF16)

    scores = []
    for h in range(n_heads):
        q8 = pad_bf(q_ref, h, pad_q)
        scores.append(jnp.concatenate(
            [lax.dot_general(q8, head_rows(near_ref, h, NEAR_ROWS, 0), nt, preferred_element_type=F32),
             lax.dot_general(q8, head_rows(far_ref, h, n_far, 0), nt, preferred_element_type=F32),
             lax.dot_general(q8, pad_bf(k_ref, h, pad_kv), nt, preferred_element_type=F32)], axis=1))
    probs = []
    for h, s in enumerate(scores):
        s = s * scale + tab_ref[h]
        m = jnp.max(s, axis=-1, keepdims=True)
        p = jnp.exp(s - m) * mult
        probs.append((p.astype(BF16), jnp.sum(p, axis=-1, keepdims=True)))
    for h, (pb, l) in enumerate(probs):
        acc = jnp.dot(pb[:, :NEAR_ROWS], head_rows(near_ref, h, NEAR_ROWS, 1),
                      preferred_element_type=F32)
        acc = acc + jnp.dot(pb[:, NEAR_ROWS:NEAR_ROWS + n_far], head_rows(far_ref, h, n_far, 1),
                            preferred_element_type=F32)
        acc = acc + jnp.dot(pb[:, NEAR_ROWS + n_far:], pad_bf(v_ref, h, pad_kv),
                            preferred_element_type=F32)
        cols = slice(h * hd, (h + 1) * hd)
        o_ref[:, cols] = (acc / l)[:t_new] * _silu(g_ref[:, cols])


def _cache_stream_step(k, n_steps, seq0, n_chunks, cache_any, kvn_any, out_any, buf_ref, far_ref,
                       sem_in, sem_out, sem_tail, shift_sl, rows_kv, attend):
    chunk_sl = buf_ref.shape[1]
    far_per_chunk = NEAR_ROWS // FAR_STRIDE
    seq_sl = n_chunks * chunk_sl

    def chunk_in(j):
        return pltpu.make_async_copy(
            cache_any.at[seq0 + j // n_chunks, pl.ds((j % n_chunks) * chunk_sl, chunk_sl)],
            buf_ref.at[j % RING], sem_in.at[j % RING])

    def first_out(j):
        return pltpu.make_async_copy(
            buf_ref.at[j % RING, pl.ds(shift_sl, chunk_sl - shift_sl)],
            out_any.at[seq0 + j // n_chunks, pl.ds(0, chunk_sl - shift_sl)], sem_out.at[j % RING])

    def later_out(j):
        return pltpu.make_async_copy(
            buf_ref.at[j % RING],
            out_any.at[seq0 + j // n_chunks, pl.ds((j % n_chunks) * chunk_sl - shift_sl, chunk_sl)],
            sem_out.at[j % RING])

    def tail(j):
        return pltpu.make_async_copy(
            kvn_any.at[seq0 + j // n_chunks],
            out_any.at[seq0 + j // n_chunks, pl.ds(seq_sl - shift_sl, shift_sl)], sem_tail.at[0])

    def on_chunk(j, first, later):
        pl.when(j % n_chunks == 0)(first)
        pl.when(j % n_chunks != 0)(later)

    @pl.when(k == 0)
    def _():
        for j in range(RING - 1):
            chunk_in(jnp.int32(j)).start()

    c = k % n_chunks
    slot = k % RING
    chunk_in(k).wait()
    on_chunk(k, lambda: first_out(k).start(), lambda: later_out(k).start())
    near_ref = buf_ref.at[slot]

    for grp in range(far_per_chunk):
        dst = pl.multiple_of((c * far_per_chunk + grp) * shift_sl, SUBLANES)
        src = grp * FAR_STRIDE * rows_kv
        far_ref[pl.ds(dst, shift_sl), :] = near_ref[src:src + shift_sl, :]

    @pl.when(c == n_chunks - 1)
    def _():
        tail(k).start()
        attend(near_ref)
        tail(k).wait()

    @pl.when(k >= 1)
    def _():
        on_chunk(k - 1, lambda: first_out(k - 1).wait(), lambda: later_out(k - 1).wait())

    @pl.when(k + RING - 1 < n_steps)
    def _():
        chunk_in(k + RING - 1).start()

    @pl.when(k == n_steps - 1)
    def _():
        on_chunk(k, lambda: first_out(k).wait(), lambda: later_out(k).wait())


class _Stream(NamedTuple):
    zs3: jax.Array
    kvn_rows: jax.Array
    cache2: jax.Array
    tab: jax.Array
    mult: jax.Array
    d_attn: int
    rows_kv: int
    n_chunks: int


def _stream_operands(stream, seq0, n_seq, seq_of):
    t_new = stream.zs3.shape[1]
    hd = stream.cache2.shape[2]
    chunk_sl = stream.cache2.shape[1] // stream.n_chunks
    n_far = stream.cache2.shape[1] // stream.rows_kv // FAR_STRIDE * t_new

    def local(*g):
        return jnp.minimum(seq_of(*g), n_seq - 1)

    def zcol(j):
        return pl.BlockSpec((None, t_new, stream.d_attn), lambda *g: (seq0 + local(*g), 0, j))

    in_specs = [zcol(0), zcol(1), zcol(2), zcol(3),
                pl.BlockSpec(stream.tab.shape, lambda *g: (0, 0, 0)),
                pl.BlockSpec(stream.mult.shape, lambda *g: (0, 0)),
                pl.BlockSpec(memory_space=pl.ANY),
                pl.BlockSpec(memory_space=pl.ANY)]
    out_specs = [pl.BlockSpec((None, t_new, stream.d_attn), lambda *g: (local(*g), 0, 0)),
                 pl.BlockSpec(memory_space=pl.ANY)]
    out_shapes = [jax.ShapeDtypeStruct((n_seq, t_new, stream.d_attn), F32),
                  jax.ShapeDtypeStruct(stream.cache2.shape, stream.cache2.dtype)]
    scratch = [pltpu.VMEM((RING, chunk_sl, hd), F32),
               pltpu.VMEM((n_far * stream.rows_kv, hd), F32),
               pltpu.SemaphoreType.DMA((RING,)),
               pltpu.SemaphoreType.DMA((RING,)),
               pltpu.SemaphoreType.DMA((1,))]
    args = (stream.zs3,) * 4 + (stream.tab, stream.mult, stream.cache2, stream.kvn_rows)
    kw = dict(rows_kv=stream.rows_kv, seq0=seq0, n_chunks=stream.n_chunks,
              n_steps=n_seq * stream.n_chunks)
    return in_specs, out_specs, out_shapes, scratch, args, kw


def _stream_step_with_attention(k, stream_in, o_ref, out_any, scratch, *, rows_kv, seq0, n_chunks,
                                n_steps):
    q_ref, k_ref, v_ref, g_ref, tab_ref, mult_ref, cache_any, kvn_any = stream_in
    buf_ref, far_ref, sem_in, sem_out, sem_tail = scratch
    shift_sl = q_ref.shape[0] * rows_kv

    def attend(near_ref):
        _sample_heads(q_ref, k_ref, v_ref, g_ref, near_ref, far_ref, tab_ref, mult_ref, o_ref, rows_kv)

    @pl.when(k < n_steps)
    def _():
        _cache_stream_step(k, n_steps, seq0, n_chunks, cache_any, kvn_any, out_any, buf_ref, far_ref,
                           sem_in, sem_out, sem_tail, shift_sl, rows_kv, attend)


def _attn_sample_kernel(*refs, n_chunks, **stream_kw):
    k = pl.program_id(0) * n_chunks + pl.program_id(1)
    _stream_step_with_attention(k, refs[:8], refs[9], refs[10], refs[11:], n_chunks=n_chunks,
                                **stream_kw)


def _attn_sample(stream, seq0, n_seq, partial_cache):
    s_in, s_out, s_shapes, s_scratch, s_args, s_kw = _stream_operands(
        stream, seq0, n_seq, lambda i, c: i)
    return pl.pallas_call(
        functools.partial(_attn_sample_kernel, **s_kw),
        grid=(n_seq, stream.n_chunks),
        in_specs=s_in + [pl.BlockSpec(memory_space=pl.ANY)],
        out_specs=s_out,
        out_shape=s_shapes,
        scratch_shapes=s_scratch,
        input_output_aliases={len(s_in): 1},
        compiler_params=pltpu.CompilerParams(
            dimension_semantics=("arbitrary", "arbitrary"), vmem_limit_bytes=48 * MIB),
        name="attn_sample",
    )(*s_args, partial_cache)


def _prompt_bias_tables(rel_bias):
    nk = KEYS_PER_PATTERN
    qb = Q_BLOCK
    wrap = TABLE_LANES
    assert wrap >= 2 * qb + nk - 1
    m = np.arange(wrap)
    kdist = nk - np.where(m < qb + nk, m, m - wrap)
    valid = (kdist >= 0) & (kdist <= nk)
    tabs = []
    for _, dil in PATTERNS:
        bucket = _rel_bucket(jnp.asarray(np.clip(kdist, 0, nk) * dil, jnp.int32))
        vec = jnp.where(valid[:, None], rel_bias[bucket].astype(F32), NEG_INF).T
        tabs.append(jnp.broadcast_to(vec[:, None, :], (vec.shape[0], SUBLANES, wrap)))
    return jnp.stack(tabs)


def _pattern_count(dist, patterns):
    return sum(((dist % dil == 0) & (dist >= 0) & (dist <= window)).astype(np.int32)
               for window, dil in patterns)


def _sample_tables(rel_bias, t_new, past):
    def bias_at(dist):
        return rel_bias[_rel_bucket(jnp.asarray(dist, jnp.int32))].astype(F32)

    def masked(bias, count):
        return jnp.where(jnp.asarray(count > 0)[..., None], bias, NEG_INF)

    near_pats, far_pats = PATTERNS[:2], PATTERNS[2:]
    assert near_pats[-1][0] == NEAR_ROWS and far_pats[0][1] == FAR_STRIDE and t_new <= far_pats[0][1]
    desc = np.arange(NEAR_ROWS + t_new - 1, 0, -1)
    desc_cnt = _pattern_count(desc, near_pats)
    desc_tab = masked(bias_at(desc), desc_cnt)
    starts = [t_new - 1 - t for t in range(t_new)]
    near_tab = jnp.stack([desc_tab[s0:s0 + NEAR_ROWS] for s0 in starts])
    near_cnt = np.stack([desc_cnt[s0:s0 + NEAR_ROWS] for s0 in starts])
    groups = past // FAR_STRIDE
    far_dist = past - FAR_STRIDE * np.arange(groups)
    own = np.eye(t_new, dtype=bool)[:, None, :] & (_pattern_count(far_dist, far_pats) > 0)[None, :, None]
    far_tab = jnp.where(jnp.asarray(own)[..., None], bias_at(far_dist)[None, :, None, :], NEG_INF)
    far_tab = far_tab.reshape(t_new, groups * t_new, -1)
    far_cnt = np.ones((t_new, groups * t_new), np.int32)
    tj = np.arange(t_new)[:, None] - np.arange(HEAD_DIM)[None, :]
    new_cnt = np.where(np.arange(HEAD_DIM)[None, :] < t_new, _pattern_count(tj, PATTERNS), 0)
    new_tab = masked(bias_at(np.clip(tj, 0, None).reshape(-1)).reshape(t_new, HEAD_DIM, -1), new_cnt)
    tab = jnp.concatenate([near_tab, far_tab, new_tab], axis=1).transpose(2, 0, 1)
    cnt = np.concatenate([near_cnt, far_cnt, new_cnt], axis=1)
    pad = SUBLANES - t_new
    tab = jnp.pad(tab, ((0, 0), (0, pad), (0, 0)))
    mult = np.pad(np.maximum(cnt, 1), ((0, pad), (0, 0)), constant_values=1).astype(np.float32)
    return tab, jnp.asarray(mult)


def kernel(x_prompt, x_sample, cache_conv, cache_kv, rel_bias, norm_pre, w_in, conv_dw_w, conv_dw_b,
           conv_ln_g, conv_ln_b, conv_pw_w, conv_pw_b, w_out, norm_post):
    depth = w_in.shape[0]
    assert depth == 1
    bsz, seq, d_model = x_prompt.shape
    n_dec, t_new, _ = x_sample.shape
    n_heads = cache_kv.shape[4]
    d_attn = n_heads * HEAD_DIM
    d_conv = cache_conv.shape[-1]
    past = cache_kv.shape[2]
    assert past == MAX_WINDOW and seq >= MAX_WINDOW and t_new <= 4
    hist = CONV_WIDTH - 1

    w_in_bf = w_in[0].astype(BF16)
    w_out_bf = w_out[0].astype(BF16)
    pw_bf = conv_pw_w[0].astype(BF16)
    conv_args = (conv_dw_w[0], conv_dw_b[0], conv_ln_g[0], conv_ln_b[0], pw_bf, conv_pw_b[0])
    conv_col0 = 4 * d_attn

    xp2 = x_prompt.reshape(bsz * seq, d_model)
    xs2 = x_sample.reshape(n_dec * t_new, d_model)
    zs = _inproj(xs2, norm_pre[0], w_in_bf, tm=n_dec * t_new, tn=d_attn)
    zs3 = zs.reshape(n_dec, t_new, -1)

    def heads(col0):
        return zs3[:, :, col0:col0 + d_attn].reshape(n_dec, t_new, n_heads, HEAD_DIM)

    rows_kv = 2 * n_heads
    kvn_rows = jnp.stack([heads(d_attn), heads(2 * d_attn)], axis=3).reshape(
        n_dec, t_new * rows_kv, HEAD_DIM)
    cache2 = cache_kv[0].transpose(0, 1, 3, 2, 4).reshape(n_dec, past * rows_kv, HEAD_DIM)
    tab_s, mult_s = _sample_tables(rel_bias, t_new, past)
    assert past % NEAR_ROWS == 0 and t_new <= SUBLANES
    stream = _Stream(zs3, kvn_rows, cache2, tab_s, mult_s, d_attn, rows_kv, past // NEAR_ROWS)

    conv_row_blocks = 8
    n_in_conv = min(bsz * conv_row_blocks // stream.n_chunks, n_dec - 1)
    n_in_attn = min(bsz * n_heads // stream.n_chunks, n_dec - n_in_conv)
    n_in_proj = n_dec - n_in_attn - n_in_conv
    tm_p, tn_p = 1024, 512
    if 0 < n_in_proj * stream.n_chunks <= (bsz * seq // tm_p) * (w_in_bf.shape[1] // tn_p):
        zp, att_s0, part_cache = _inproj(xp2, norm_pre[0], w_in_bf, tm_p, tn_p, stream, n_in_proj)
    else:
        zp = _inproj(xp2, norm_pre[0], w_in_bf, tm_p, tn_p)
        n_in_proj = 0
        att_s0 = jnp.zeros((0, t_new, d_attn), F32)
        part_cache = jnp.zeros(cache2.shape, cache2.dtype)
    zp3 = zp.reshape(bsz, seq, -1)
    mix_att_p, att_s1, part_cache = _attn_prompt(zp3, _prompt_bias_tables(rel_bias), n_heads, stream,
                                                 n_in_proj, n_in_attn, part_cache)
    att_s0 = jnp.concatenate([att_s0, att_s1], axis=0)
    zero_prefix = jnp.zeros((bsz, hist, d_conv), F32)
    if n_in_conv > 0:
        mix_conv_p, new_conv_p, att_s2, part_cache = _conv_branch(
            zp3, zero_prefix, *conv_args, col0=conv_col0, row_blocks=conv_row_blocks,
            stream=stream, seq0=att_s0.shape[0], n_seq=n_in_conv, partial_cache=part_cache)
        att_s0 = jnp.concatenate([att_s0, att_s2], axis=0)
    else:
        mix_conv_p, new_conv_p = _conv_branch(zp3, zero_prefix, *conv_args, col0=conv_col0,
                                              row_blocks=conv_row_blocks)
    yp = _outproj(mix_att_p.reshape(bsz * seq, d_attn), mix_conv_p.reshape(bsz * seq, d_conv),
                  w_out_bf[:d_attn], w_out_bf[d_attn:], xp2, norm_post[0], tm=512)
    win = min(MAX_WINDOW, seq)
    kv_rows_p = _kv_rows(zp, d_attn, tm=512).reshape(bsz, seq, n_heads, 2, HEAD_DIM)
    new_kv_p = kv_rows_p[:, seq - win:].transpose(0, 1, 3, 2, 4)[None]

    n_hosted = att_s0.shape[0]
    att_s, new_rows = att_s0, part_cache
    if n_hosted < n_dec:
        att_s1, new_rows = _attn_sample(stream, n_hosted, n_dec - n_hosted, part_cache)
        att_s = jnp.concatenate([att_s0, att_s1], axis=0)
    new_kv_s = new_rows.reshape(n_dec, past, n_heads, 2, HEAD_DIM).transpose(0, 1, 3, 2, 4)[None]
    mix_att_s = att_s.reshape(n_dec * t_new, d_attn).astype(BF16)
    mix_conv_s, new_conv_s = _conv_branch(zs3, cache_conv[0], *conv_args, col0=conv_col0)
    ys = _outproj(mix_att_s, mix_conv_s.reshape(n_dec * t_new, d_conv),
                  w_out_bf[:d_attn], w_out_bf[d_attn:], xs2, norm_post[0], tm=n_dec * t_new)

    return (yp.reshape(bsz, seq, d_model), ys.reshape(n_dec, t_new, d_model),
            new_conv_p[None], new_kv_p, new_conv_s[None], new_kv_s)
```

```python
import functools
import math
from typing import NamedTuple

import jax
import jax.numpy as jnp
import numpy as np
from jax import lax
from jax.experimental import pallas as pl
from jax.experimental.pallas import tpu as pltpu

F32 = jnp.float32
BF16 = jnp.bfloat16

HEAD_DIM = 128
PATTERNS = ((128, 1), (512, 4), (2048, 16))
MAX_WINDOW = 2048
Q_BLOCK = 128
KEYS_PER_PATTERN = 128
TABLE_LANES = 384
CONV_WIDTH = 31
N_BUCKETS = 32
MAX_EXACT = 16
EPS = 1e-6
NEG_INF = -1e30
SUBLANES = 8
HEAD_PAD = 16

MIB = 1024 * 1024


def _rel_bucket(dist):
    d = jnp.maximum(dist, 1).astype(F32)
    log_b = MAX_EXACT + (jnp.log(d / MAX_EXACT) / math.log(MAX_WINDOW / MAX_EXACT)
                         * (N_BUCKETS - MAX_EXACT)).astype(jnp.int32)
    log_b = jnp.minimum(log_b, N_BUCKETS - 1)
    return jnp.where(dist < MAX_EXACT, dist, log_b)


def _round_up(x, m):
    return -(-x // m) * m


def _silu(x):
    return x * jax.nn.sigmoid(x)


def _inproj_kernel(x_ref, g_ref, w_ref, z_ref, h_ref):
    @pl.when(pl.program_id(1) == 0)
    def _():
        x = x_ref[...]
        ms = jnp.mean(x * x, axis=-1, keepdims=True)
        h_ref[...] = (x * lax.rsqrt(ms + EPS) * g_ref[...]).astype(BF16)

    z_ref[...] = jnp.dot(h_ref[...], w_ref[...], preferred_element_type=F32)


def _inproj_stream_kernel(x_ref, g_ref, w_ref, *rest, n_col_tiles, **stream_kw):
    stream_in, (z_ref, so_ref, out_any), scratch = rest[:8], rest[8:11], rest[11:]
    _inproj_kernel(x_ref, g_ref, w_ref, z_ref, scratch[0])
    k = pl.program_id(0) * n_col_tiles + pl.program_id(1)
    _stream_step_with_attention(k, stream_in, so_ref, out_any, scratch[1:], **stream_kw)


def _inproj(x2d, norm_g, w_bf, tm, tn, stream=None, n_seq=0):
    m, d = x2d.shape
    n = w_bf.shape[1]
    grid = (m // tm, n // tn)
    in_specs = [pl.BlockSpec((tm, d), lambda i, j: (i, 0)),
                pl.BlockSpec((1, d), lambda i, j: (0, 0)),
                pl.BlockSpec((d, tn), lambda i, j: (0, j))]
    z_spec = pl.BlockSpec((tm, tn), lambda i, j: (i, j))
    z_shape = jax.ShapeDtypeStruct((m, n), F32)
    h_scratch = pltpu.VMEM((tm, d), BF16)
    args = (x2d, norm_g.reshape(1, d), w_bf)
    if stream is None:
        return pl.pallas_call(
            _inproj_kernel, grid=grid, in_specs=in_specs, out_specs=z_spec, out_shape=z_shape,
            scratch_shapes=[h_scratch],
            compiler_params=pltpu.CompilerParams(
                dimension_semantics=("parallel", "arbitrary"), vmem_limit_bytes=48 * MIB),
            name="inproj",
        )(*args)
    assert n_seq * stream.n_chunks <= grid[0] * grid[1]
    s_in, s_out, s_shapes, s_scratch, s_args, s_kw = _stream_operands(
        stream, 0, n_seq, lambda i, j: (i * grid[1] + j) // stream.n_chunks)
    return pl.pallas_call(
        functools.partial(_inproj_stream_kernel, n_col_tiles=grid[1], **s_kw),
        grid=grid, in_specs=in_specs + s_in, out_specs=[z_spec] + s_out,
        out_shape=[z_shape] + s_shapes, scratch_shapes=[h_scratch] + s_scratch,
        compiler_params=pltpu.CompilerParams(
            dimension_semantics=("arbitrary", "arbitrary"), vmem_limit_bytes=60 * MIB),
        name="inproj",
    )(*args, *s_args)


def _kv_rows_kernel(k_ref, v_ref, o_ref):
    tm = k_ref.shape[0]
    hd = o_ref.shape[1]
    n_heads = k_ref.shape[1] // hd
    for h in range(n_heads):
        o_ref[pl.ds(2 * h, tm, stride=2 * n_heads), :] = k_ref[:, h * hd:(h + 1) * hd]
        o_ref[pl.ds(2 * h + 1, tm, stride=2 * n_heads), :] = v_ref[:, h * hd:(h + 1) * hd]


def _kv_rows(z, d_attn, tm):
    m = z.shape[0]
    rows_kv = 2 * d_attn // HEAD_DIM
    return pl.pallas_call(
        _kv_rows_kernel,
        grid=(m // tm,),
        in_specs=[pl.BlockSpec((tm, d_attn), lambda i: (i, 1)),
                  pl.BlockSpec((tm, d_attn), lambda i: (i, 2))],
        out_specs=pl.BlockSpec((tm * rows_kv, HEAD_DIM), lambda i: (i, 0)),
        out_shape=jax.ShapeDtypeStruct((m * rows_kv, HEAD_DIM), F32),
        compiler_params=pltpu.CompilerParams(
            dimension_semantics=("parallel",), vmem_limit_bytes=48 * MIB),
        name="kv_rows",
    )(z, z)


def _attn_block(qb, kw, vw, tab, scale):
    s = lax.dot_general(qb.astype(BF16), kw.astype(BF16), (((1,), (1,)), ((), ())),
                        preferred_element_type=F32)
    s = s * scale + tab
    m = jnp.max(s, axis=-1, keepdims=True)
    p = jnp.exp(s - m).astype(BF16)
    v_ones = jnp.concatenate([vw.astype(BF16), jnp.ones(vw.shape, BF16)], axis=1)
    acc_l = jnp.dot(p, v_ones, preferred_element_type=F32)
    d = vw.shape[1]
    return acc_l[:, :d], m, acc_l[:, d:]


def _attn_prompt_kernel(q_ref, k_ref, v_ref, g_ref, vec_ref, o_ref, acc_ref, m_ref, l_ref, tab_ref):
    seq = q_ref.shape[0]
    scale = HEAD_DIM ** -0.5
    qb_rows = Q_BLOCK
    nk = KEYS_PER_PATTERN

    for p in range(len(PATTERNS)):
        base = jnp.broadcast_to(vec_ref[p][0:1, :], (qb_rows, vec_ref.shape[-1]))
        tab_ref[p] = pltpu.roll(base, 0, 1, stride=1, stride_axis=0)[:, :qb_rows + nk]

    def rows(ref, start, size, stride):
        if stride == 1:
            return ref[pl.ds(start, size), :]
        return ref[pl.ds(start, size, stride=stride), :]

    def put(p, start, stride, acc, m, l):
        lanes = acc.shape[-1]
        if stride == 1:
            idx = pl.ds(start, qb_rows)
        else:
            idx = pl.ds(start, qb_rows, stride=stride)
        acc_ref[p, idx, :] = acc
        m_ref[p, idx, :] = jnp.broadcast_to(m, (qb_rows, lanes))
        l_ref[p, idx, :] = l

    def first_block(p, phase, dil):
        tab = tab_ref[p][:, nk:]
        qb = rows(q_ref, phase, qb_rows, dil)
        kw = rows(k_ref, phase, qb_rows, dil)
        vw = rows(v_ref, phase, qb_rows, dil)
        put(p, phase, dil, *_attn_block(qb, kw, vw, tab, scale))

    def later_block(p, phase, dil, n):
        tab = tab_ref[p]
        q0 = phase + dil * qb_rows * n
        k0 = q0 - dil * nk
        qb = rows(q_ref, q0, qb_rows, dil)
        kw = rows(k_ref, k0, qb_rows + nk, dil)
        vw = rows(v_ref, k0, qb_rows + nk, dil)
        put(p, q0, dil, *_attn_block(qb, kw, vw, tab, scale))

    for p, (window, dil) in enumerate(PATTERNS):
        n_blocks = seq // dil // qb_rows
        for phase in range(dil):
            first_block(p, phase, dil)
            for n in range(1, n_blocks):
                later_block(p, phase, dil, n)

    chunk = 256

    def combine(c, carry):
        sl = pl.ds(pl.multiple_of(c * chunk, chunk), chunk)
        m0, m1, m2 = m_ref[0, sl, :], m_ref[1, sl, :], m_ref[2, sl, :]
        mm = jnp.maximum(jnp.maximum(m0, m1), m2)
        e0, e1, e2 = jnp.exp(m0 - mm), jnp.exp(m1 - mm), jnp.exp(m2 - mm)
        num = e0 * acc_ref[0, sl, :] + e1 * acc_ref[1, sl, :] + e2 * acc_ref[2, sl, :]
        den = e0 * l_ref[0, sl, :] + e1 * l_ref[1, sl, :] + e2 * l_ref[2, sl, :]
        o_ref[sl, :] = (num / den * _silu(g_ref[sl, :])).astype(o_ref.dtype)
        return carry
    lax.fori_loop(0, seq // chunk, combine, 0)


def _attn_prompt_stream_kernel(q_ref, k_ref, v_ref, g_ref, tab_ref, *rest, n_heads, **stream_kw):
    stream_in, (o_ref, so_ref, out_any), scratch = rest[:8], rest[9:12], rest[12:]
    _attn_prompt_kernel(q_ref, k_ref, v_ref, g_ref, tab_ref, o_ref, *scratch[:4])
    k = pl.program_id(0) * n_heads + pl.program_id(1)
    _stream_step_with_attention(k, stream_in, so_ref, out_any, scratch[4:], **stream_kw)


def _attn_prompt(z3, tabs, n_heads, stream, seq0, n_seq, partial_cache):
    b, seq, _ = z3.shape
    hd = HEAD_DIM
    assert 0 < n_seq * stream.n_chunks <= b * n_heads

    def col(off):
        return pl.BlockSpec((None, seq, hd), lambda i, h: (i, 0, off + h))

    def seq_of(i, h):
        return (i * n_heads + h) // stream.n_chunks

    s_in, s_out, s_shapes, s_scratch, s_args, s_kw = _stream_operands(stream, seq0, n_seq, seq_of)
    in_specs = [col(0), col(n_heads), col(2 * n_heads), col(3 * n_heads),
                pl.BlockSpec((len(PATTERNS), None, SUBLANES, TABLE_LANES),
                             lambda i, h: (0, h, 0, 0))] + s_in + [pl.BlockSpec(memory_space=pl.ANY)]
    return pl.pallas_call(
        functools.partial(_attn_prompt_stream_kernel, n_heads=n_heads, **s_kw),
        grid=(b, n_heads),
        in_specs=in_specs,
        out_specs=[pl.BlockSpec((None, seq, hd), lambda i, h: (i, 0, h))] + s_out,
        out_shape=[jax.ShapeDtypeStruct((b, seq, n_heads * hd), BF16)] + s_shapes,
        scratch_shapes=([pltpu.VMEM((len(PATTERNS), seq, hd), F32)] * 3
                        + [pltpu.VMEM((len(PATTERNS), Q_BLOCK, Q_BLOCK + KEYS_PER_PATTERN), F32)]
                        + s_scratch),
        input_output_aliases={len(in_specs) - 1: 2},
        compiler_params=pltpu.CompilerParams(
            dimension_semantics=("arbitrary", "arbitrary"), vmem_limit_bytes=56 * MIB),
        name="attn_prompt",
    )(z3, z3, z3, z3, tabs, *s_args, partial_cache)


def _conv_kernel(ca_ref, cb_ref, gc_ref, pre_ref, dww_ref, dwb_ref, lng_ref, lnb_ref,
                 pww_ref, pwb_ref, o_ref, newc_ref, upad_ref, *, chunk):
    t_len = ca_ref.shape[0]
    rows_step = o_ref.shape[0]
    hist = CONV_WIDTH - 1
    rb = pl.program_id(1)

    @pl.when(rb == 0)
    def _():
        u = ca_ref[...] * jax.nn.sigmoid(cb_ref[...])
        upad_ref[0:hist, :] = pre_ref[...]
        upad_ref[hist:hist + t_len, :] = u
        n_pad = upad_ref.shape[0] - (hist + t_len)
        upad_ref[hist + t_len:, :] = jnp.zeros((n_pad, ca_ref.shape[1]), F32)
        newc_ref[...] = upad_ref[t_len:t_len + hist, :]

    win_rows = upad_ref.shape[0] - t_len + chunk

    def body(c, carry):
        l0 = c * chunk
        r0 = l0 if rows_step == t_len else rb * rows_step + l0
        if chunk % SUBLANES == 0:
            l0, r0 = pl.multiple_of(l0, SUBLANES), pl.multiple_of(r0, SUBLANES)
        win = upad_ref[pl.ds(r0, win_rows), :]
        y = jnp.zeros((chunk, ca_ref.shape[1]), F32) + dwb_ref[...]
        for s in range(SUBLANES):
            shifted = win if s == 0 else pltpu.roll(win, win_rows - s, 0)
            for a in range(-(-CONV_WIDTH // SUBLANES)):
                w = SUBLANES * a + s
                if w < CONV_WIDTH:
                    y = y + shifted[SUBLANES * a:SUBLANES * a + chunk] * dww_ref[w:w + 1, :]
        mu = jnp.mean(y, axis=-1, keepdims=True)
        var = jnp.mean(jnp.square(y - mu), axis=-1, keepdims=True)
        yn = (y - mu) * lax.rsqrt(var + EPS) * lng_ref[...] + lnb_ref[...]
        c_act = _silu(yn).astype(BF16)
        proj = jnp.dot(c_act, pww_ref[...], preferred_element_type=F32) + pwb_ref[...]
        o_ref[pl.ds(l0, chunk), :] = (proj * _silu(gc_ref[pl.ds(r0, chunk), :])).astype(o_ref.dtype)
        return carry
    if rows_step == chunk:
        body(0, 0)
    else:
        lax.fori_loop(0, rows_step // chunk, body, 0)


def _conv_stream_kernel(*refs, row_blocks, **stream_kw):
    chunk = stream_kw.pop("chunk")
    _conv_kernel(*refs[:10], refs[19], refs[20], refs[23], chunk=chunk)
    k = pl.program_id(0) * row_blocks + pl.program_id(1)
    _stream_step_with_attention(k, refs[10:18], refs[21], refs[22], refs[24:], **stream_kw)


def _conv_branch(z3, prefix, dw_w, dw_b, ln_g, ln_b, pw_w_bf, pw_b, col0, row_blocks=1,
                 stream=None, seq0=0, n_seq=0, partial_cache=None):
    n, t_len, _ = z3.shape
    c = prefix.shape[-1]
    hist = CONV_WIDTH - 1
    rows_step = t_len // row_blocks
    chunk = min(rows_step, 64)
    cblk = col0 // c

    def zc(j):
        return pl.BlockSpec((None, t_len, c), lambda i, r: (i, 0, cblk + j))

    def vec():
        return pl.BlockSpec((1, c), lambda i, r: (0, 0))

    in_specs = [zc(0), zc(1), zc(2),
                pl.BlockSpec((None, hist, c), lambda i, r: (i, 0, 0)),
                pl.BlockSpec((CONV_WIDTH, c), lambda i, r: (0, 0)),
                vec(), vec(), vec(),
                pl.BlockSpec((c, c), lambda i, r: (0, 0)),
                vec()]
    out_specs = [pl.BlockSpec((None, rows_step, c), lambda i, r: (i, r, 0)),
                 pl.BlockSpec((None, hist, c), lambda i, r: (i, 0, 0))]
    out_shape = [jax.ShapeDtypeStruct((n, t_len, c), BF16),
                 jax.ShapeDtypeStruct((n, hist, c), F32)]
    scratch = [pltpu.VMEM((t_len - chunk + _round_up(chunk + CONV_WIDTH + 1, SUBLANES), c), F32)]
    args = (z3, z3, z3, prefix, dw_w, dw_b.reshape(1, c), ln_g.reshape(1, c), ln_b.reshape(1, c),
            pw_w_bf, pw_b.reshape(1, c))
    if stream is None:
        return pl.pallas_call(
            functools.partial(_conv_kernel, chunk=chunk),
            grid=(n, row_blocks), in_specs=in_specs, out_specs=out_specs, out_shape=out_shape,
            scratch_shapes=scratch,
            compiler_params=pltpu.CompilerParams(
                dimension_semantics=("parallel", "arbitrary"), vmem_limit_bytes=48 * MIB),
            name="conv_branch",
        )(*args)
    assert 0 < n_seq * stream.n_chunks <= n * row_blocks
    s_in, s_out, s_shapes, s_scratch, s_args, s_kw = _stream_operands(
        stream, seq0, n_seq, lambda i, r: (i * row_blocks + r) // stream.n_chunks)
    in_specs = in_specs + s_in + [pl.BlockSpec(memory_space=pl.ANY)]
    return pl.pallas_call(
        functools.partial(_conv_stream_kernel, row_blocks=row_blocks, chunk=chunk, **s_kw),
        grid=(n, row_blocks), in_specs=in_specs, out_specs=out_specs + s_out,
        out_shape=out_shape + s_shapes, scratch_shapes=scratch + s_scratch,
        input_output_aliases={len(in_specs) - 1: 3},
        compiler_params=pltpu.CompilerParams(
            dimension_semantics=("arbitrary", "arbitrary"), vmem_limit_bytes=60 * MIB),
        name="conv_branch",
    )(*args, *s_args, partial_cache)


def _outproj_kernel(ma_ref, mc_ref, wa_ref, wc_ref, x_ref, g_ref, y_ref):
    y = jnp.dot(ma_ref[...], wa_ref[...], preferred_element_type=F32)
    y = y + jnp.dot(mc_ref[...], wc_ref[...], preferred_element_type=F32)
    ms = jnp.mean(y * y, axis=-1, keepdims=True)
    y_ref[...] = x_ref[...] + y * lax.rsqrt(ms + EPS) * g_ref[...]


def _outproj(mix_att, mix_conv, w_att_bf, w_conv_bf, x2d, norm_g, tm):
    m, d = x2d.shape
    da, dc = mix_att.shape[1], mix_conv.shape[1]
    return pl.pallas_call(
        _outproj_kernel,
        grid=(m // tm,),
        in_specs=[pl.BlockSpec((tm, da), lambda i: (i, 0)),
                  pl.BlockSpec((tm, dc), lambda i: (i, 0)),
                  pl.BlockSpec((da, d), lambda i: (0, 0)),
                  pl.BlockSpec((dc, d), lambda i: (0, 0)),
                  pl.BlockSpec((tm, d), lambda i: (i, 0)),
                  pl.BlockSpec((1, d), lambda i: (0, 0))],
        out_specs=pl.BlockSpec((tm, d), lambda i: (i, 0)),
        out_shape=jax.ShapeDtypeStruct((m, d), F32),
        compiler_params=pltpu.CompilerParams(
            dimension_semantics=("parallel",), vmem_limit_bytes=48 * MIB),
        name="outproj",
    )(mix_att, mix_conv, w_att_bf, w_conv_bf, x2d, norm_g.reshape(1, d))


NEAR_ROWS = 512
FAR_STRIDE = 16


RING = 3


def _sample_heads(q_ref, k_ref, v_ref, g_ref, near_ref, far_ref, tab_ref, mult_ref, o_ref, rows_kv):
    t_new = q_ref.shape[0]
    hd = near_ref.shape[1]
    scale = HEAD_DIM ** -0.5
    pad_q = jnp.zeros((SUBLANES - t_new, hd), F32)
    pad_kv = jnp.zeros((hd - t_new, hd), F32)
    nt = (((1,), (1,)), ((), ()))
    n_far = far_ref.shape[0] // rows_kv
    mult = mult_ref[...]
    n_heads = rows_kv // 2

    def head_rows(ref, h, n, parity):
        return ref[pl.ds(2 * h + parity, n, stride=rows_kv), :].astype(BF16)

    def pad_bf(ref, h, pad):
        return jnp.concatenate([ref[:, h * hd:(h + 1) * hd], pad], axis=0).astype(BF16)

    scores = []
    for h in range(n_heads):
        q8 = pad_bf(q_ref, h, pad_q)
        scores.append(jnp.concatenate(
            [lax.dot_general(q8, head_rows(near_ref, h, NEAR_ROWS, 0), nt, preferred_element_type=F32),
             lax.dot_general(q8, head_rows(far_ref, h, n_far, 0), nt, preferred_element_type=F32),
             lax.dot_general(q8, pad_bf(k_ref, h, pad_kv), nt, preferred_element_type=F32)], axis=1))
    probs = []
    for h, s in enumerate(scores):
        s = s * scale + tab_ref[h]
        m = jnp.max(s, axis=-1, keepdims=True)
        p = jnp.exp(s - m) * mult
        probs.append((p.astype(BF16), jnp.sum(p, axis=-1, keepdims=True)))
    for h, (pb, l) in enumerate(probs):
        acc = jnp.dot(pb[:, :NEAR_ROWS], head_rows(near_ref, h, NEAR_ROWS, 1),
                      preferred_element_type=F32)
        acc = acc + jnp.dot(pb[:, NEAR_ROWS:NEAR_ROWS + n_far], head_rows(far_ref, h, n_far, 1),
                            preferred_element_type=F32)
        acc = acc + jnp.dot(pb[:, NEAR_ROWS + n_far:], pad_bf(v_ref, h, pad_kv),
                            preferred_element_type=F32)
        cols = slice(h * hd, (h + 1) * hd)
        o_ref[:, cols] = (acc / l)[:t_new] * _silu(g_ref[:, cols])


def _cache_stream_step(k, n_steps, seq0, n_chunks, cache_any, kvn_any, out_any, buf_ref, far_ref,
                       sem_in, sem_out, sem_tail, shift_sl, rows_kv, attend):
    chunk_sl = buf_ref.shape[1]
    far_per_chunk = NEAR_ROWS // FAR_STRIDE
    seq_sl = n_chunks * chunk_sl

    def chunk_in(j):
        return pltpu.make_async_copy(
            cache_any.at[seq0 + j // n_chunks, pl.ds((j % n_chunks) * chunk_sl, chunk_sl)],
            buf_ref.at[j % RING], sem_in.at[j % RING])

    def first_out(j):
        return pltpu.make_async_copy(
            buf_ref.at[j % RING, pl.ds(shift_sl, chunk_sl - shift_sl)],
            out_any.at[seq0 + j // n_chunks, pl.ds(0, chunk_sl - shift_sl)], sem_out.at[j % RING])

    def later_out(j):
        return pltpu.make_async_copy(
            buf_ref.at[j % RING],
            out_any.at[seq0 + j // n_chunks, pl.ds((j % n_chunks) * chunk_sl - shift_sl, chunk_sl)],
            sem_out.at[j % RING])

    def tail(j):
        return pltpu.make_async_copy(
            kvn_any.at[seq0 + j // n_chunks],
            out_any.at[seq0 + j // n_chunks, pl.ds(seq_sl - shift_sl, shift_sl)], sem_tail.at[0])

    def on_chunk(j, first, later):
        pl.when(j % n_chunks == 0)(first)
        pl.when(j % n_chunks != 0)(later)

    @pl.when(k == 0)
    def _():
        for j in range(RING - 1):
            chunk_in(jnp.int32(j)).start()

    c = k % n_chunks
    slot = k % RING
    chunk_in(k).wait()
    on_chunk(k, lambda: first_out(k).start(), lambda: later_out(k).start())
    near_ref = buf_ref.at[slot]

    for grp in range(far_per_chunk):
        dst = pl.multiple_of((c * far_per_chunk + grp) * shift_sl, SUBLANES)
        src = grp * FAR_STRIDE * rows_kv
        far_ref[pl.ds(dst, shift_sl), :] = near_ref[src:src + shift_sl, :]

    @pl.when(c == n_chunks - 1)
    def _():
        tail(k).start()
        attend(near_ref)
        tail(k).wait()

    @pl.when(k >= 1)
    def _():
        on_chunk(k - 1, lambda: first_out(k - 1).wait(), lambda: later_out(k - 1).wait())

    @pl.when(k + RING - 1 < n_steps)
    def _():
        chunk_in(k + RING - 1).start()

    @pl.when(k == n_steps - 1)
    def _():
        on_chunk(k, lambda: first_out(k).wait(), lambda: later_out(k).wait())


class _Stream(NamedTuple):
    zs3: jax.Array
    kvn_rows: jax.Array
    cache2: jax.Array
    tab: jax.Array
    mult: jax.Array
    d_attn: int
    rows_kv: int
    n_chunks: int


def _stream_operands(stream, seq0, n_seq, seq_of):
    t_new = stream.zs3.shape[1]
    hd = stream.cache2.shape[2]
    chunk_sl = stream.cache2.shape[1] // stream.n_chunks
    n_far = stream.cache2.shape[1] // stream.rows_kv // FAR_STRIDE * t_new

    def local(*g):
        return jnp.minimum(seq_of(*g), n_seq - 1)

    def zcol(j):
        return pl.BlockSpec((None, t_new, stream.d_attn), lambda *g: (seq0 + local(*g), 0, j))

    in_specs = [zcol(0), zcol(1), zcol(2), zcol(3),
                pl.BlockSpec(stream.tab.shape, lambda *g: (0, 0, 0)),
                pl.BlockSpec(stream.mult.shape, lambda *g: (0, 0)),
                pl.BlockSpec(memory_space=pl.ANY),
                pl.BlockSpec(memory_space=pl.ANY)]
    out_specs = [pl.BlockSpec((None, t_new, stream.d_attn), lambda *g: (local(*g), 0, 0)),
                 pl.BlockSpec(memory_space=pl.ANY)]
    out_shapes = [jax.ShapeDtypeStruct((n_seq, t_new, stream.d_attn), F32),
                  jax.ShapeDtypeStruct(stream.cache2.shape, stream.cache2.dtype)]
    scratch = [pltpu.VMEM((RING, chunk_sl, hd), F32),
               pltpu.VMEM((n_far * stream.rows_kv, hd), F32),
               pltpu.SemaphoreType.DMA((RING,)),
               pltpu.SemaphoreType.DMA((RING,)),
               pltpu.SemaphoreType.DMA((1,))]
    args = (stream.zs3,) * 4 + (stream.tab, stream.mult, stream.cache2, stream.kvn_rows)
    kw = dict(rows_kv=stream.rows_kv, seq0=seq0, n_chunks=stream.n_chunks,
              n_steps=n_seq * stream.n_chunks)
    return in_specs, out_specs, out_shapes, scratch, args, kw


def _stream_step_with_attention(k, stream_in, o_ref, out_any, scratch, *, rows_kv, seq0, n_chunks,
                                n_steps):
    q_ref, k_ref, v_ref, g_ref, tab_ref, mult_ref, cache_any, kvn_any = stream_in
    buf_ref, far_ref, sem_in, sem_out, sem_tail = scratch
    shift_sl = q_ref.shape[0] * rows_kv

    def attend(near_ref):
        _sample_heads(q_ref, k_ref, v_ref, g_ref, near_ref, far_ref, tab_ref, mult_ref, o_ref, rows_kv)

    @pl.when(k < n_steps)
    def _():
        _cache_stream_step(k, n_steps, seq0, n_chunks, cache_any, kvn_any, out_any, buf_ref, far_ref,
                           sem_in, sem_out, sem_tail, shift_sl, rows_kv, attend)


def _attn_sample_kernel(*refs, n_chunks, **stream_kw):
    k = pl.program_id(0) * n_chunks + pl.program_id(1)
    _stream_step_with_attention(k, refs[:8], refs[9], refs[10], refs[11:], n_chunks=n_chunks,
                                **stream_kw)


def _attn_sample(stream, seq0, n_seq, partial_cache):
    s_in, s_out, s_shapes, s_scratch, s_args, s_kw = _stream_operands(
        stream, seq0, n_seq, lambda i, c: i)
    return pl.pallas_call(
        functools.partial(_attn_sample_kernel, **s_kw),
        grid=(n_seq, stream.n_chunks),
        in_specs=s_in + [pl.BlockSpec(memory_space=pl.ANY)],
        out_specs=s_out,
        out_shape=s_shapes,
        scratch_shapes=s_scratch,
        input_output_aliases={len(s_in): 1},
        compiler_params=pltpu.CompilerParams(
            dimension_semantics=("arbitrary", "arbitrary"), vmem_limit_bytes=48 * MIB),
        name="attn_sample",
    )(*s_args, partial_cache)


def _prompt_bias_tables(rel_bias):
    nk = KEYS_PER_PATTERN
    qb = Q_BLOCK
    wrap = TABLE_LANES
    assert wrap >= 2 * qb + nk - 1
    m = np.arange(wrap)
    kdist = nk - np.where(m < qb + nk, m, m - wrap)
    valid = (kdist >= 0) & (kdist <= nk)
    tabs = []
    for _, dil in PATTERNS:
        bucket = _rel_bucket(jnp.asarray(np.clip(kdist, 0, nk) * dil, jnp.int32))
        vec = jnp.where(valid[:, None], rel_bias[bucket].astype(F32), NEG_INF).T
        tabs.append(jnp.broadcast_to(vec[:, None, :], (vec.shape[0], SUBLANES, wrap)))
    return jnp.stack(tabs)


def _pattern_count(dist, patterns):
    return sum(((dist % dil == 0) & (dist >= 0) & (dist <= window)).astype(np.int32)
               for window, dil in patterns)


def _sample_tables(rel_bias, t_new, past):
    def bias_at(dist):
        return rel_bias[_rel_bucket(jnp.asarray(dist, jnp.int32))].astype(F32)

    def masked(bias, count):
        return jnp.where(jnp.asarray(count > 0)[..., None], bias, NEG_INF)

    near_pats, far_pats = PATTERNS[:2], PATTERNS[2:]
    assert near_pats[-1][0] == NEAR_ROWS and far_pats[0][1] == FAR_STRIDE and t_new <= far_pats[0][1]
    desc = np.arange(NEAR_ROWS + t_new - 1, 0, -1)
    desc_cnt = _pattern_count(desc, near_pats)
    desc_tab = masked(bias_at(desc), desc_cnt)
    starts = [t_new - 1 - t for t in range(t_new)]
    near_tab = jnp.stack([desc_tab[s0:s0 + NEAR_ROWS] for s0 in starts])
    near_cnt = np.stack([desc_cnt[s0:s0 + NEAR_ROWS] for s0 in starts])
    groups = past // FAR_STRIDE
    far_dist = past - FAR_STRIDE * np.arange(groups)
    own = np.eye(t_new, dtype=bool)[:, None, :] & (_pattern_count(far_dist, far_pats) > 0)[None, :, None]
    far_tab = jnp.where(jnp.asarray(own)[..., None], bias_at(far_dist)[None, :, None, :], NEG_INF)
    far_tab = far_tab.reshape(t_new, groups * t_new, -1)
    far_cnt = np.ones((t_new, groups * t_new), np.int32)
    tj = np.arange(t_new)[:, None] - np.arange(HEAD_DIM)[None, :]
    new_cnt = np.where(np.arange(HEAD_DIM)[None, :] < t_new, _pattern_count(tj, PATTERNS), 0)
    new_tab = masked(bias_at(np.clip(tj, 0, None).reshape(-1)).reshape(t_new, HEAD_DIM, -1), new_cnt)
    tab = jnp.concatenate([near_tab, far_tab, new_tab], axis=1).transpose(2, 0, 1)
    cnt = np.concatenate([near_cnt, far_cnt, new_cnt], axis=1)
    pad = SUBLANES - t_new
    tab = jnp.pad(tab, ((0, 0), (0, pad), (0, 0)))
    mult = np.pad(np.maximum(cnt, 1), ((0, pad), (0, 0)), constant_values=1).astype(np.float32)
    return tab, jnp.asarray(mult)


def kernel(x_prompt, x_sample, cache_conv, cache_kv, rel_bias, norm_pre, w_in, conv_dw_w, conv_dw_b,
           conv_ln_g, conv_ln_b, conv_pw_w, conv_pw_b, w_out, norm_post):
    depth = w_in.shape[0]
    assert depth == 1
    bsz, seq, d_model = x_prompt.shape
    n_dec, t_new, _ = x_sample.shape
    n_heads = cache_kv.shape[4]
    d_attn = n_heads * HEAD_DIM
    d_conv = cache_conv.shape[-1]
    past = cache_kv.shape[2]
    assert past == MAX_WINDOW and seq >= MAX_WINDOW and t_new <= 4
    hist = CONV_WIDTH - 1

    w_in_bf = w_in[0].astype(BF16)
    w_out_bf = w_out[0].astype(BF16)
    pw_bf = conv_pw_w[0].astype(BF16)
    conv_args = (conv_dw_w[0], conv_dw_b[0], conv_ln_g[0], conv_ln_b[0], pw_bf, conv_pw_b[0])
    conv_col0 = 4 * d_attn

    xp2 = x_prompt.reshape(bsz * seq, d_model)
    xs2 = x_sample.reshape(n_dec * t_new, d_model)
    zs = _inproj(xs2, norm_pre[0], w_in_bf, tm=n_dec * t_new, tn=d_attn)
    zs3 = zs.reshape(n_dec, t_new, -1)

    def heads(col0):
        return zs3[:, :, col0:col0 + d_attn].reshape(n_dec, t_new, n_heads, HEAD_DIM)

    rows_kv = 2 * n_heads
    kvn_rows = jnp.stack([heads(d_attn), heads(2 * d_attn)], axis=3).reshape(
        n_dec, t_new * rows_kv, HEAD_DIM)
    cache2 = cache_kv[0].transpose(0, 1, 3, 2, 4).reshape(n_dec, past * rows_kv, HEAD_DIM)
    tab_s, mult_s = _sample_tables(rel_bias, t_new, past)
    assert past % NEAR_ROWS == 0 and t_new <= SUBLANES
    stream = _Stream(zs3, kvn_rows, cache2, tab_s, mult_s, d_attn, rows_kv, past // NEAR_ROWS)

    conv_row_blocks = 8
    n_in_conv = min(bsz * conv_row_blocks // stream.n_chunks, n_dec - 1)
    n_in_attn = min(bsz * n_heads // stream.n_chunks, n_dec - n_in_conv)
    n_in_proj = n_dec - n_in_attn - n_in_conv
    tm_p, tn_p = 1024, 512
    if 0 < n_in_proj * stream.n_chunks <= (bsz * seq // tm_p) * (w_in_bf.shape[1] // tn_p):
        zp, att_s0, part_cache = _inproj(xp2, norm_pre[0], w_in_bf, tm_p, tn_p, stream, n_in_proj)
    else:
        zp = _inproj(xp2, norm_pre[0], w_in_bf, tm_p, tn_p)
        n_in_proj = 0
        att_s0 = jnp.zeros((0, t_new, d_attn), F32)
        part_cache = jnp.zeros(cache2.shape, cache2.dtype)
    zp3 = zp.reshape(bsz, seq, -1)
    mix_att_p, att_s1, part_cache = _attn_prompt(zp3, _prompt_bias_tables(rel_bias), n_heads, stream,
                                                 n_in_proj, n_in_attn, part_cache)
    att_s0 = jnp.concatenate([att_s0, att_s1], axis=0)
    zero_prefix = jnp.zeros((bsz, hist, d_conv), F32)
    if n_in_conv > 0:
        mix_conv_p, new_conv_p, att_s2, part_cache = _conv_branch(
            zp3, zero_prefix, *conv_args, col0=conv_col0, row_blocks=conv_row_blocks,
            stream=stream, seq0=att_s0.shape[0], n_seq=n_in_conv, partial_cache=part_cache)
        att_s0 = jnp.concatenate([att_s0, att_s2], axis=0)
    else:
        mix_conv_p, new_conv_p = _conv_branch(zp3, zero_prefix, *conv_args, col0=conv_col0,
                                              row_blocks=conv_row_blocks)
    yp = _outproj(mix_att_p.reshape(bsz * seq, d_attn), mix_conv_p.reshape(bsz * seq, d_conv),
                  w_out_bf[:d_attn], w_out_bf[d_attn:], xp2, norm_post[0], tm=512)
    win = min(MAX_WINDOW, seq)
    kv_rows_p = _kv_rows(zp, d_attn, tm=512).reshape(bsz, seq, n_heads, 2, HEAD_DIM)
    new_kv_p = kv_rows_p[:, seq - win:].transpose(0, 1, 3, 2, 4)[None]

    n_hosted = att_s0.shape[0]
    att_s, new_rows = att_s0, part_cache
    if n_hosted < n_dec:
        att_s1, new_rows = _attn_sample(stream, n_hosted, n_dec - n_hosted, part_cache)
        att_s = jnp.concatenate([att_s0, att_s1], axis=0)
    new_kv_s = new_rows.reshape(n_dec, past, n_heads, 2, HEAD_DIM).transpose(0, 1, 3, 2, 4)[None]
    mix_att_s = att_s.reshape(n_dec * t_new, d_attn).astype(BF16)
    mix_conv_s, new_conv_s = _conv_branch(zs3, cache_conv[0], *conv_args, col0=conv_col0)
    ys = _outproj(mix_att_s, mix_conv_s.reshape(n_dec * t_new, d_conv),
                  w_out_bf[:d_attn], w_out_bf[d_attn:], xs2, norm_post[0], tm=n_dec * t_new)

    return (yp.reshape(bsz, seq, d_model), ys.reshape(n_dec, t_new, d_model),
            new_conv_p[None], new_kv_p, new_conv_s[None], new_kv_s)
```

```python
import functools
import math
from typing import NamedTuple

import jax
import jax.numpy as jnp
import numpy as np
from jax import lax
from jax.experimental import pallas as pl
from jax.experimental.pallas import tpu as pltpu

F32 = jnp.float32
BF16 = jnp.bfloat16

HEAD_DIM = 128
PATTERNS = ((128, 1), (512, 4), (2048, 16))
MAX_WINDOW = 2048
Q_BLOCK = 128
KEYS_PER_PATTERN = 128
TABLE_LANES = 384
CONV_WIDTH = 31
N_BUCKETS = 32
MAX_EXACT = 16
EPS = 1e-6
NEG_INF = -1e30
SUBLANES = 8
HEAD_PAD = 16

MIB = 1024 * 1024


def _rel_bucket(dist):
    d = jnp.maximum(dist, 1).astype(F32)
    log_b = MAX_EXACT + (jnp.log(d / MAX_EXACT) / math.log(MAX_WINDOW / MAX_EXACT)
                         * (N_BUCKETS - MAX_EXACT)).astype(jnp.int32)
    log_b = jnp.minimum(log_b, N_BUCKETS - 1)
    return jnp.where(dist < MAX_EXACT, dist, log_b)


def _round_up(x, m):
    return -(-x // m) * m


def _silu(x):
    return x * jax.nn.sigmoid(x)


def _prenorm_kernel(x_ref, g_ref, h_ref):
    x = x_ref[...]
    ms = jnp.mean(x * x, axis=-1, keepdims=True)
    h_ref[...] = (x * lax.rsqrt(ms + EPS) * g_ref[...]).astype(h_ref.dtype)


def _prenorm(x2d, norm_g, tm):
    m, d = x2d.shape
    return pl.pallas_call(
        _prenorm_kernel,
        grid=(m // tm,),
        in_specs=[pl.BlockSpec((tm, d), lambda i: (i, 0)), pl.BlockSpec((1, d), lambda i: (0, 0))],
        out_specs=pl.BlockSpec((tm, d), lambda i: (i, 0)),
        out_shape=jax.ShapeDtypeStruct((m, d), BF16),
        compiler_params=pltpu.CompilerParams(
            dimension_semantics=("parallel",), vmem_limit_bytes=48 * MIB),
        name="prenorm",
    )(x2d, norm_g.reshape(1, d))


def _inproj_kernel(h_ref, w_ref, z_ref):
    z_ref[...] = jnp.dot(h_ref[...], w_ref[...], preferred_element_type=F32)


def _inproj_stream_kernel(h_ref, w_ref, *rest, n_col_tiles, **stream_kw):
    stream_in, (z_ref, so_ref, out_any), scratch = rest[:8], rest[8:11], rest[11:]
    _inproj_kernel(h_ref, w_ref, z_ref)
    k = pl.program_id(0) * n_col_tiles + pl.program_id(1)
    _stream_step_with_attention(k, stream_in, so_ref, out_any, scratch, **stream_kw)


def _inproj(h2d, w_bf, tm, tn, stream=None, n_seq=0):
    m, d = h2d.shape
    n = w_bf.shape[1]
    grid = (m // tm, n // tn)
    in_specs = [pl.BlockSpec((tm, d), lambda i, j: (i, 0)),
                pl.BlockSpec((d, tn), lambda i, j: (0, j))]
    z_spec = pl.BlockSpec((tm, tn), lambda i, j: (i, j))
    z_shape = jax.ShapeDtypeStruct((m, n), F32)
    if stream is None:
        return pl.pallas_call(
            _inproj_kernel, grid=grid, in_specs=in_specs, out_specs=z_spec, out_shape=z_shape,
            compiler_params=pltpu.CompilerParams(
                dimension_semantics=("parallel", "arbitrary"), vmem_limit_bytes=48 * MIB),
            name="inproj",
        )(h2d, w_bf)
    assert n_seq * stream.n_chunks <= grid[0] * grid[1]
    s_in, s_out, s_shapes, s_scratch, s_args, s_kw = _stream_operands(
        stream, 0, n_seq, lambda i, j: (i * grid[1] + j) // stream.n_chunks)
    return pl.pallas_call(
        functools.partial(_inproj_stream_kernel, n_col_tiles=grid[1], **s_kw),
        grid=grid, in_specs=in_specs + s_in, out_specs=[z_spec] + s_out,
        out_shape=[z_shape] + s_shapes, scratch_shapes=s_scratch,
        compiler_params=pltpu.CompilerParams(
            dimension_semantics=("arbitrary", "arbitrary"), vmem_limit_bytes=60 * MIB),
        name="inproj",
    )(h2d, w_bf, *s_args)


def _kv_rows_kernel(k_ref, v_ref, o_ref):
    tm = k_ref.shape[0]
    hd = o_ref.shape[1]
    n_heads = k_ref.shape[1] // hd
    for h in range(n_heads):
        o_ref[pl.ds(2 * h, tm, stride=2 * n_heads), :] = k_ref[:, h * hd:(h + 1) * hd]
        o_ref[pl.ds(2 * h + 1, tm, stride=2 * n_heads), :] = v_ref[:, h * hd:(h + 1) * hd]


def _kv_rows(z, d_attn, tm):
    m = z.shape[0]
    rows_kv = 2 * d_attn // HEAD_DIM
    return pl.pallas_call(
        _kv_rows_kernel,
        grid=(m // tm,),
        in_specs=[pl.BlockSpec((tm, d_attn), lambda i: (i, 1)),
                  pl.BlockSpec((tm, d_attn), lambda i: (i, 2))],
        out_specs=pl.BlockSpec((tm * rows_kv, HEAD_DIM), lambda i: (i, 0)),
        out_shape=jax.ShapeDtypeStruct((m * rows_kv, HEAD_DIM), F32),
        compiler_params=pltpu.CompilerParams(
            dimension_semantics=("parallel",), vmem_limit_bytes=48 * MIB),
        name="kv_rows",
    )(z, z)


def _attn_block(qb, kw, vw, tab, scale):
    s = lax.dot_general(qb.astype(BF16), kw.astype(BF16), (((1,), (1,)), ((), ())),
                        preferred_element_type=F32)
    s = s * scale + tab
    m = jnp.max(s, axis=-1, keepdims=True)
    p = jnp.exp(s - m).astype(BF16)
    v_ones = jnp.concatenate([vw.astype(BF16), jnp.ones(vw.shape, BF16)], axis=1)
    acc_l = jnp.dot(p, v_ones, preferred_element_type=F32)
    d = vw.shape[1]
    return acc_l[:, :d], m, acc_l[:, d:]


def _attn_prompt_kernel(q_ref, k_ref, v_ref, g_ref, vec_ref, o_ref, acc_ref, m_ref, l_ref, tab_ref):
    seq = q_ref.shape[0]
    scale = HEAD_DIM ** -0.5
    qb_rows = Q_BLOCK
    nk = KEYS_PER_PATTERN

    for p in range(len(PATTERNS)):
        base = jnp.broadcast_to(vec_ref[p][0:1, :], (qb_rows, vec_ref.shape[-1]))
        tab_ref[p] = pltpu.roll(base, 0, 1, stride=1, stride_axis=0)[:, :qb_rows + nk]

    def rows(ref, start, size, stride):
        if stride == 1:
            return ref[pl.ds(start, size), :]
        return ref[pl.ds(start, size, stride=stride), :]

    def put(p, start, stride, acc, m, l):
        lanes = acc.shape[-1]
        if stride == 1:
            idx = pl.ds(start, qb_rows)
        else:
            idx = pl.ds(start, qb_rows, stride=stride)
        acc_ref[p, idx, :] = acc
        m_ref[p, idx, :] = jnp.broadcast_to(m, (qb_rows, lanes))
        l_ref[p, idx, :] = l

    def first_block(p, phase, dil):
        tab = tab_ref[p][:, nk:]
        qb = rows(q_ref, phase, qb_rows, dil)
        kw = rows(k_ref, phase, qb_rows, dil)
        vw = rows(v_ref, phase, qb_rows, dil)
        put(p, phase, dil, *_attn_block(qb, kw, vw, tab, scale))

    def later_block(p, phase, dil, n):
        tab = tab_ref[p]
        q0 = phase + dil * qb_rows * n
        k0 = q0 - dil * nk
        qb = rows(q_ref, q0, qb_rows, dil)
        kw = rows(k_ref, k0, qb_rows + nk, dil)
        vw = rows(v_ref, k0, qb_rows + nk, dil)
        put(p, q0, dil, *_attn_block(qb, kw, vw, tab, scale))

    for p, (window, dil) in enumerate(PATTERNS):
        n_blocks = seq // dil // qb_rows
        for phase in range(dil):
            first_block(p, phase, dil)
            for n in range(1, n_blocks):
                later_block(p, phase, dil, n)

    chunk = 256

    def combine(c, carry):
        sl = pl.ds(pl.multiple_of(c * chunk, chunk), chunk)
        m0, m1, m2 = m_ref[0, sl, :], m_ref[1, sl, :], m_ref[2, sl, :]
        mm = jnp.maximum(jnp.maximum(m0, m1), m2)
        e0, e1, e2 = jnp.exp(m0 - mm), jnp.exp(m1 - mm), jnp.exp(m2 - mm)
        num = e0 * acc_ref[0, sl, :] + e1 * acc_ref[1, sl, :] + e2 * acc_ref[2, sl, :]
        den = e0 * l_ref[0, sl, :] + e1 * l_ref[1, sl, :] + e2 * l_ref[2, sl, :]
        o_ref[sl, :] = (num / den * _silu(g_ref[sl, :])).astype(o_ref.dtype)
        return carry
    lax.fori_loop(0, seq // chunk, combine, 0)


def _attn_prompt_stream_kernel(q_ref, k_ref, v_ref, g_ref, tab_ref, *rest, n_heads, **stream_kw):
    stream_in, (o_ref, so_ref, out_any), scratch = rest[:8], rest[9:12], rest[12:]
    _attn_prompt_kernel(q_ref, k_ref, v_ref, g_ref, tab_ref, o_ref, *scratch[:4])
    k = pl.program_id(0) * n_heads + pl.program_id(1)
    _stream_step_with_attention(k, stream_in, so_ref, out_any, scratch[4:], **stream_kw)


def _attn_prompt(z3, tabs, n_heads, stream, seq0, n_seq, partial_cache):
    b, seq, _ = z3.shape
    hd = HEAD_DIM
    assert 0 < n_seq * stream.n_chunks <= b * n_heads

    def col(off):
        return pl.BlockSpec((None, seq, hd), lambda i, h: (i, 0, off + h))

    def seq_of(i, h):
        return (i * n_heads + h) // stream.n_chunks

    s_in, s_out, s_shapes, s_scratch, s_args, s_kw = _stream_operands(stream, seq0, n_seq, seq_of)
    in_specs = [col(0), col(n_heads), col(2 * n_heads), col(3 * n_heads),
                pl.BlockSpec((len(PATTERNS), None, SUBLANES, TABLE_LANES),
                             lambda i, h: (0, h, 0, 0))] + s_in + [pl.BlockSpec(memory_space=pl.ANY)]
    return pl.pallas_call(
        functools.partial(_attn_prompt_stream_kernel, n_heads=n_heads, **s_kw),
        grid=(b, n_heads),
        in_specs=in_specs,
        out_specs=[pl.BlockSpec((None, seq, hd), lambda i, h: (i, 0, h))] + s_out,
        out_shape=[jax.ShapeDtypeStruct((b, seq, n_heads * hd), BF16)] + s_shapes,
        scratch_shapes=([pltpu.VMEM((len(PATTERNS), seq, hd), F32)] * 3
                        + [pltpu.VMEM((len(PATTERNS), Q_BLOCK, Q_BLOCK + KEYS_PER_PATTERN), F32)]
                        + s_scratch),
        input_output_aliases={len(in_specs) - 1: 2},
        compiler_params=pltpu.CompilerParams(
            dimension_semantics=("arbitrary", "arbitrary"), vmem_limit_bytes=56 * MIB),
        name="attn_prompt",
    )(z3, z3, z3, z3, tabs, *s_args, partial_cache)


def _conv_kernel(ca_ref, cb_ref, gc_ref, pre_ref, dww_ref, dwb_ref, lng_ref, lnb_ref,
                 pww_ref, pwb_ref, o_ref, newc_ref, upad_ref, *, chunk):
    t_len = ca_ref.shape[0]
    rows_step = o_ref.shape[0]
    hist = CONV_WIDTH - 1
    rb = pl.program_id(1)

    @pl.when(rb == 0)
    def _():
        u = ca_ref[...] * jax.nn.sigmoid(cb_ref[...])
        upad_ref[0:hist, :] = pre_ref[...]
        upad_ref[hist:hist + t_len, :] = u
        n_pad = upad_ref.shape[0] - (hist + t_len)
        upad_ref[hist + t_len:, :] = jnp.zeros((n_pad, ca_ref.shape[1]), F32)
        newc_ref[...] = upad_ref[t_len:t_len + hist, :]

    win_rows = upad_ref.shape[0] - t_len + chunk

    def body(c, carry):
        l0 = c * chunk
        r0 = l0 if rows_step == t_len else rb * rows_step + l0
        if chunk % SUBLANES == 0:
            l0, r0 = pl.multiple_of(l0, SUBLANES), pl.multiple_of(r0, SUBLANES)
        win = upad_ref[pl.ds(r0, win_rows), :]
        y = jnp.zeros((chunk, ca_ref.shape[1]), F32) + dwb_ref[...]
        for s in range(SUBLANES):
            shifted = win if s == 0 else pltpu.roll(win, win_rows - s, 0)
            for a in range(-(-CONV_WIDTH // SUBLANES)):
                w = SUBLANES * a + s
                if w < CONV_WIDTH:
                    y = y + shifted[SUBLANES * a:SUBLANES * a + chunk] * dww_ref[w:w + 1, :]
        mu = jnp.mean(y, axis=-1, keepdims=True)
        var = jnp.mean(jnp.square(y - mu), axis=-1, keepdims=True)
        yn = (y - mu) * lax.rsqrt(var + EPS) * lng_ref[...] + lnb_ref[...]
        c_act = _silu(yn).astype(BF16)
        proj = jnp.dot(c_act, pww_ref[...], preferred_element_type=F32) + pwb_ref[...]
        o_ref[pl.ds(l0, chunk), :] = (proj * _silu(gc_ref[pl.ds(r0, chunk), :])).astype(o_ref.dtype)
        return carry
    if rows_step == chunk:
        body(0, 0)
    else:
        lax.fori_loop(0, rows_step // chunk, body, 0)


def _conv_stream_kernel(*refs, row_blocks, **stream_kw):
    chunk = stream_kw.pop("chunk")
    _conv_kernel(*refs[:10], refs[19], refs[20], refs[23], chunk=chunk)
    k = pl.program_id(0) * row_blocks + pl.program_id(1)
    _stream_step_with_attention(k, refs[10:18], refs[21], refs[22], refs[24:], **stream_kw)


def _conv_branch(z3, prefix, dw_w, dw_b, ln_g, ln_b, pw_w_bf, pw_b, col0, row_blocks=1,
                 stream=None, seq0=0, n_seq=0, partial_cache=None):
    n, t_len, _ = z3.shape
    c = prefix.shape[-1]
    hist = CONV_WIDTH - 1
    rows_step = t_len // row_blocks
    chunk = min(rows_step, 64)
    cblk = col0 // c

    def zc(j):
        return pl.BlockSpec((None, t_len, c), lambda i, r: (i, 0, cblk + j))

    def vec():
        return pl.BlockSpec((1, c), lambda i, r: (0, 0))

    in_specs = [zc(0), zc(1), zc(2),
                pl.BlockSpec((None, hist, c), lambda i, r: (i, 0, 0)),
                pl.BlockSpec((CONV_WIDTH, c), lambda i, r: (0, 0)),
                vec(), vec(), vec(),
                pl.BlockSpec((c, c), lambda i, r: (0, 0)),
                vec()]
    out_specs = [pl.BlockSpec((None, rows_step, c), lambda i, r: (i, r, 0)),
                 pl.BlockSpec((None, hist, c), lambda i, r: (i, 0, 0))]
    out_shape = [jax.ShapeDtypeStruct((n, t_len, c), BF16),
                 jax.ShapeDtypeStruct((n, hist, c), F32)]
    scratch = [pltpu.VMEM((t_len - chunk + _round_up(chunk + CONV_WIDTH + 1, SUBLANES), c), F32)]
    args = (z3, z3, z3, prefix, dw_w, dw_b.reshape(1, c), ln_g.reshape(1, c), ln_b.reshape(1, c),
            pw_w_bf, pw_b.reshape(1, c))
    if stream is None:
        return pl.pallas_call(
            functools.partial(_conv_kernel, chunk=chunk),
            grid=(n, row_blocks), in_specs=in_specs, out_specs=out_specs, out_shape=out_shape,
            scratch_shapes=scratch,
            compiler_params=pltpu.CompilerParams(
                dimension_semantics=("parallel", "arbitrary"), vmem_limit_bytes=48 * MIB),
            name="conv_branch",
        )(*args)
    assert 0 < n_seq * stream.n_chunks <= n * row_blocks
    s_in, s_out, s_shapes, s_scratch, s_args, s_kw = _stream_operands(
        stream, seq0, n_seq, lambda i, r: (i * row_blocks + r) // stream.n_chunks)
    in_specs = in_specs + s_in + [pl.BlockSpec(memory_space=pl.ANY)]
    return pl.pallas_call(
        functools.partial(_conv_stream_kernel, row_blocks=row_blocks, chunk=chunk, **s_kw),
        grid=(n, row_blocks), in_specs=in_specs, out_specs=out_specs + s_out,
        out_shape=out_shape + s_shapes, scratch_shapes=scratch + s_scratch,
        input_output_aliases={len(in_specs) - 1: 3},
        compiler_params=pltpu.CompilerParams(
            dimension_semantics=("arbitrary", "arbitrary"), vmem_limit_bytes=60 * MIB),
        name="conv_branch",
    )(*args, *s_args, partial_cache)


def _outproj_kernel(ma_ref, mc_ref, wa_ref, wc_ref, x_ref, g_ref, y_ref):
    y = jnp.dot(ma_ref[...], wa_ref[...], preferred_element_type=F32)
    y = y + jnp.dot(mc_ref[...], wc_ref[...], preferred_element_type=F32)
    ms = jnp.mean(y * y, axis=-1, keepdims=True)
    y_ref[...] = x_ref[...] + y * lax.rsqrt(ms + EPS) * g_ref[...]


def _outproj(mix_att, mix_conv, w_att_bf, w_conv_bf, x2d, norm_g, tm):
    m, d = x2d.shape
    da, dc = mix_att.shape[1], mix_conv.shape[1]
    return pl.pallas_call(
        _outproj_kernel,
        grid=(m // tm,),
        in_specs=[pl.BlockSpec((tm, da), lambda i: (i, 0)),
                  pl.BlockSpec((tm, dc), lambda i: (i, 0)),
                  pl.BlockSpec((da, d), lambda i: (0, 0)),
                  pl.BlockSpec((dc, d), lambda i: (0, 0)),
                  pl.BlockSpec((tm, d), lambda i: (i, 0)),
                  pl.BlockSpec((1, d), lambda i: (0, 0))],
        out_specs=pl.BlockSpec((tm, d), lambda i: (i, 0)),
        out_shape=jax.ShapeDtypeStruct((m, d), F32),
        compiler_params=pltpu.CompilerParams(
            dimension_semantics=("parallel",), vmem_limit_bytes=48 * MIB),
        name="outproj",
    )(mix_att, mix_conv, w_att_bf, w_conv_bf, x2d, norm_g.reshape(1, d))


NEAR_ROWS = 512
FAR_STRIDE = 16


RING = 3


def _sample_heads(q_ref, k_ref, v_ref, g_ref, near_ref, far_ref, tab_ref, mult_ref, o_ref, rows_kv):
    t_new = q_ref.shape[0]
    hd = near_ref.shape[1]
    scale = HEAD_DIM ** -0.5
    pad_q = jnp.zeros((SUBLANES - t_new, hd), F32)
    pad_kv = jnp.zeros((hd - t_new, hd), F32)
    nt = (((1,), (1,)), ((), ()))
    n_far = far_ref.shape[0] // rows_kv
    mult = mult_ref[...]
    n_heads = rows_kv // 2

    def head_rows(ref, h, n, parity):
        return ref[pl.ds(2 * h + parity, n, stride=rows_kv), :].astype(BF16)

    def pad_bf(ref, h, pad):
        return jnp.concatenate([ref[:, h * hd:(h + 1) * hd], pad], axis=0).astype(BF16)

    scores = []
    for h in range(n_heads):
        q8 = pad_bf(q_ref, h, pad_q)
        scores.append(jnp.concatenate(
            [lax.dot_general(q8, head_rows(near_ref, h, NEAR_ROWS, 0), nt, preferred_element_type=F32),
             lax.dot_general(q8, head_rows(far_ref, h, n_far, 0), nt, preferred_element_type=F32),
             lax.dot_general(q8, pad_bf(k_ref, h, pad_kv), nt, preferred_element_type=F32)], axis=1))
    probs = []
    for h, s in enumerate(scores):
        s = s * scale + tab_ref[h]
        m = jnp.max(s, axis=-1, keepdims=True)
        p = jnp.exp(s - m) * mult
        probs.append((p.astype(BF16), jnp.sum(p, axis=-1, keepdims=True)))
    for h, (pb, l) in enumerate(probs):
        acc = jnp.dot(pb[:, :NEAR_ROWS], head_rows(near_ref, h, NEAR_ROWS, 1),
                      preferred_element_type=F32)
        acc = acc + jnp.dot(pb[:, NEAR_ROWS:NEAR_ROWS + n_far], head_rows(far_ref, h, n_far, 1),
                            preferred_element_type=F32)
        acc = acc + jnp.dot(pb[:, NEAR_ROWS + n_far:], pad_bf(v_ref, h, pad_kv),
                            preferred_element_type=F32)
        cols = slice(h * hd, (h + 1) * hd)
        o_ref[:, cols] = (acc / l)[:t_new] * _silu(g_ref[:, cols])


def _cache_stream_step(k, n_steps, seq0, n_chunks, cache_any, kvn_any, out_any, buf_ref, far_ref,
                       sem_in, sem_out, sem_tail, shift_sl, rows_kv, attend):
    chunk_sl = buf_ref.shape[1]
    far_per_chunk = NEAR_ROWS // FAR_STRIDE
    seq_sl = n_chunks * chunk_sl

    def chunk_in(j):
        return pltpu.make_async_copy(
            cache_any.at[seq0 + j // n_chunks, pl.ds((j % n_chunks) * chunk_sl, chunk_sl)],
            buf_ref.at[j % RING], sem_in.at[j % RING])

    def first_out(j):
        return pltpu.make_async_copy(
            buf_ref.at[j % RING, pl.ds(shift_sl, chunk_sl - shift_sl)],
            out_any.at[seq0 + j // n_chunks, pl.ds(0, chunk_sl - shift_sl)], sem_out.at[j % RING])

    def later_out(j):
        return pltpu.make_async_copy(
            buf_ref.at[j % RING],
            out_any.at[seq0 + j // n_chunks, pl.ds((j % n_chunks) * chunk_sl - shift_sl, chunk_sl)],
            sem_out.at[j % RING])

    def tail(j):
        return pltpu.make_async_copy(
            kvn_any.at[seq0 + j // n_chunks],
            out_any.at[seq0 + j // n_chunks, pl.ds(seq_sl - shift_sl, shift_sl)], sem_tail.at[0])

    def on_chunk(j, first, later):
        pl.when(j % n_chunks == 0)(first)
        pl.when(j % n_chunks != 0)(later)

    @pl.when(k == 0)
    def _():
        for j in range(RING - 1):
            chunk_in(jnp.int32(j)).start()

    c = k % n_chunks
    slot = k % RING
    chunk_in(k).wait()
    on_chunk(k, lambda: first_out(k).start(), lambda: later_out(k).start())
    near_ref = buf_ref.at[slot]

    for grp in range(far_per_chunk):
        dst = pl.multiple_of((c * far_per_chunk + grp) * shift_sl, SUBLANES)
        src = grp * FAR_STRIDE * rows_kv
        far_ref[pl.ds(dst, shift_sl), :] = near_ref[src:src + shift_sl, :]

    @pl.when(c == n_chunks - 1)
    def _():
        tail(k).start()
        attend(near_ref)
        tail(k).wait()

    @pl.when(k >= 1)
    def _():
        on_chunk(k - 1, lambda: first_out(k - 1).wait(), lambda: later_out(k - 1).wait())

    @pl.when(k + RING - 1 < n_steps)
    def _():
        chunk_in(k + RING - 1).start()

    @pl.when(k == n_steps - 1)
    def _():
        on_chunk(k, lambda: first_out(k).wait(), lambda: later_out(k).wait())


class _Stream(NamedTuple):
    zs3: jax.Array
    kvn_rows: jax.Array
    cache2: jax.Array
    tab: jax.Array
    mult: jax.Array
    d_attn: int
    rows_kv: int
    n_chunks: int


def _stream_operands(stream, seq0, n_seq, seq_of):
    t_new = stream.zs3.shape[1]
    hd = stream.cache2.shape[2]
    chunk_sl = stream.cache2.shape[1] // stream.n_chunks
    n_far = stream.cache2.shape[1] // stream.rows_kv // FAR_STRIDE * t_new

    def local(*g):
        return jnp.minimum(seq_of(*g), n_seq - 1)

    def zcol(j):
        return pl.BlockSpec((None, t_new, stream.d_attn), lambda *g: (seq0 + local(*g), 0, j))

    in_specs = [zcol(0), zcol(1), zcol(2), zcol(3),
                pl.BlockSpec(stream.tab.shape, lambda *g: (0, 0, 0)),
                pl.BlockSpec(stream.mult.shape, lambda *g: (0, 0)),
                pl.BlockSpec(memory_space=pl.ANY),
                pl.BlockSpec(memory_space=pl.ANY)]
    out_specs = [pl.BlockSpec((None, t_new, stream.d_attn), lambda *g: (local(*g), 0, 0)),
                 pl.BlockSpec(memory_space=pl.ANY)]
    out_shapes = [jax.ShapeDtypeStruct((n_seq, t_new, stream.d_attn), F32),
                  jax.ShapeDtypeStruct(stream.cache2.shape, stream.cache2.dtype)]
    scratch = [pltpu.VMEM((RING, chunk_sl, hd), F32),
               pltpu.VMEM((n_far * stream.rows_kv, hd), F32),
               pltpu.SemaphoreType.DMA((RING,)),
               pltpu.SemaphoreType.DMA((RING,)),
               pltpu.SemaphoreType.DMA((1,))]
    args = (stream.zs3,) * 4 + (stream.tab, stream.mult, stream.cache2, stream.kvn_rows)
    kw = dict(rows_kv=stream.rows_kv, seq0=seq0, n_chunks=stream.n_chunks,
              n_steps=n_seq * stream.n_chunks)
    return in_specs, out_specs, out_shapes, scratch, args, kw


def _stream_step_with_attention(k, stream_in, o_ref, out_any, scratch, *, rows_kv, seq0, n_chunks,
                                n_steps):
    q_ref, k_ref, v_ref, g_ref, tab_ref, mult_ref, cache_any, kvn_any = stream_in
    buf_ref, far_ref, sem_in, sem_out, sem_tail = scratch
    shift_sl = q_ref.shape[0] * rows_kv

    def attend(near_ref):
        _sample_heads(q_ref, k_ref, v_ref, g_ref, near_ref, far_ref, tab_ref, mult_ref, o_ref, rows_kv)

    @pl.when(k < n_steps)
    def _():
        _cache_stream_step(k, n_steps, seq0, n_chunks, cache_any, kvn_any, out_any, buf_ref, far_ref,
                           sem_in, sem_out, sem_tail, shift_sl, rows_kv, attend)


def _attn_sample_kernel(*refs, n_chunks, **stream_kw):
    k = pl.program_id(0) * n_chunks + pl.program_id(1)
    _stream_step_with_attention(k, refs[:8], refs[9], refs[10], refs[11:], n_chunks=n_chunks,
                                **stream_kw)


def _attn_sample(stream, seq0, n_seq, partial_cache):
    s_in, s_out, s_shapes, s_scratch, s_args, s_kw = _stream_operands(
        stream, seq0, n_seq, lambda i, c: i)
    return pl.pallas_call(
        functools.partial(_attn_sample_kernel, **s_kw),
        grid=(n_seq, stream.n_chunks),
        in_specs=s_in + [pl.BlockSpec(memory_space=pl.ANY)],
        out_specs=s_out,
        out_shape=s_shapes,
        scratch_shapes=s_scratch,
        input_output_aliases={len(s_in): 1},
        compiler_params=pltpu.CompilerParams(
            dimension_semantics=("arbitrary", "arbitrary"), vmem_limit_bytes=48 * MIB),
        name="attn_sample",
    )(*s_args, partial_cache)


def _prompt_bias_tables(rel_bias):
    nk = KEYS_PER_PATTERN
    qb = Q_BLOCK
    wrap = TABLE_LANES
    assert wrap >= 2 * qb + nk - 1
    m = np.arange(wrap)
    kdist = nk - np.where(m < qb + nk, m, m - wrap)
    valid = (kdist >= 0) & (kdist <= nk)
    tabs = []
    for _, dil in PATTERNS:
        bucket = _rel_bucket(jnp.asarray(np.clip(kdist, 0, nk) * dil, jnp.int32))
        vec = jnp.where(valid[:, None], rel_bias[bucket].astype(F32), NEG_INF).T
        tabs.append(jnp.broadcast_to(vec[:, None, :], (vec.shape[0], SUBLANES, wrap)))
    return jnp.stack(tabs)


def _pattern_count(dist, patterns):
    return sum(((dist % dil == 0) & (dist >= 0) & (dist <= window)).astype(np.int32)
               for window, dil in patterns)


def _sample_tables(rel_bias, t_new, past):
    def bias_at(dist):
        return rel_bias[_rel_bucket(jnp.asarray(dist, jnp.int32))].astype(F32)

    def masked(bias, count):
        return jnp.where(jnp.asarray(count > 0)[..., None], bias, NEG_INF)

    near_pats, far_pats = PATTERNS[:2], PATTERNS[2:]
    assert near_pats[-1][0] == NEAR_ROWS and far_pats[0][1] == FAR_STRIDE and t_new <= far_pats[0][1]
    desc = np.arange(NEAR_ROWS + t_new - 1, 0, -1)
    desc_cnt = _pattern_count(desc, near_pats)
    desc_tab = masked(bias_at(desc), desc_cnt)
    starts = [t_new - 1 - t for t in range(t_new)]
    near_tab = jnp.stack([desc_tab[s0:s0 + NEAR_ROWS] for s0 in starts])
    near_cnt = np.stack([desc_cnt[s0:s0 + NEAR_ROWS] for s0 in starts])
    groups = past // FAR_STRIDE
    far_dist = past - FAR_STRIDE * np.arange(groups)
    own = np.eye(t_new, dtype=bool)[:, None, :] & (_pattern_count(far_dist, far_pats) > 0)[None, :, None]
    far_tab = jnp.where(jnp.asarray(own)[..., None], bias_at(far_dist)[None, :, None, :], NEG_INF)
    far_tab = far_tab.reshape(t_new, groups * t_new, -1)
    far_cnt = np.ones((t_new, groups * t_new), np.int32)
    tj = np.arange(t_new)[:, None] - np.arange(HEAD_DIM)[None, :]
    new_cnt = np.where(np.arange(HEAD_DIM)[None, :] < t_new, _pattern_count(tj, PATTERNS), 0)
    new_tab = masked(bias_at(np.clip(tj, 0, None).reshape(-1)).reshape(t_new, HEAD_DIM, -1), new_cnt)
    tab = jnp.concatenate([near_tab, far_tab, new_tab], axis=1).transpose(2, 0, 1)
    cnt = np.concatenate([near_cnt, far_cnt, new_cnt], axis=1)
    pad = SUBLANES - t_new
    tab = jnp.pad(tab, ((0, 0), (0, pad), (0, 0)))
    mult = np.pad(np.maximum(cnt, 1), ((0, pad), (0, 0)), constant_values=1).astype(np.float32)
    return tab, jnp.asarray(mult)


def kernel(x_prompt, x_sample, cache_conv, cache_kv, rel_bias, norm_pre, w_in, conv_dw_w, conv_dw_b,
           conv_ln_g, conv_ln_b, conv_pw_w, conv_pw_b, w_out, norm_post):
    depth = w_in.shape[0]
    assert depth == 1
    bsz, seq, d_model = x_prompt.shape
    n_dec, t_new, _ = x_sample.shape
    n_heads = cache_kv.shape[4]
    d_attn = n_heads * HEAD_DIM
    d_conv = cache_conv.shape[-1]
    past = cache_kv.shape[2]
    assert past == MAX_WINDOW and seq >= MAX_WINDOW and t_new <= 4
    hist = CONV_WIDTH - 1

    w_in_bf = w_in[0].astype(BF16)
    w_out_bf = w_out[0].astype(BF16)
    pw_bf = conv_pw_w[0].astype(BF16)
    conv_args = (conv_dw_w[0], conv_dw_b[0], conv_ln_g[0], conv_ln_b[0], pw_bf, conv_pw_b[0])
    conv_col0 = 4 * d_attn

    xp2 = x_prompt.reshape(bsz * seq, d_model)
    xs2 = x_sample.reshape(n_dec * t_new, d_model)
    zs = _inproj(_prenorm(xs2, norm_pre[0], tm=n_dec * t_new), w_in_bf, tm=n_dec * t_new, tn=d_attn)
    zs3 = zs.reshape(n_dec, t_new, -1)

    def heads(col0):
        return zs3[:, :, col0:col0 + d_attn].reshape(n_dec, t_new, n_heads, HEAD_DIM)

    rows_kv = 2 * n_heads
    kvn_rows = jnp.stack([heads(d_attn), heads(2 * d_attn)], axis=3).reshape(
        n_dec, t_new * rows_kv, HEAD_DIM)
    cache2 = cache_kv[0].transpose(0, 1, 3, 2, 4).reshape(n_dec, past * rows_kv, HEAD_DIM)
    tab_s, mult_s = _sample_tables(rel_bias, t_new, past)
    assert past % NEAR_ROWS == 0 and t_new <= SUBLANES
    stream = _Stream(zs3, kvn_rows, cache2, tab_s, mult_s, d_attn, rows_kv, past // NEAR_ROWS)

    conv_row_blocks = 8
    n_in_conv = min(bsz * conv_row_blocks // stream.n_chunks, n_dec - 1)
    n_in_attn = min(bsz * n_heads // stream.n_chunks, n_dec - n_in_conv)
    n_in_proj = n_dec - n_in_attn - n_in_conv
    tm_p, tn_p = 2048, 512
    hp = _prenorm(xp2, norm_pre[0], tm=512)
    if 0 < n_in_proj * stream.n_chunks <= (bsz * seq // tm_p) * (w_in_bf.shape[1] // tn_p):
        zp, att_s0, part_cache = _inproj(hp, w_in_bf, tm_p, tn_p, stream, n_in_proj)
    else:
        zp = _inproj(hp, w_in_bf, tm_p, tn_p)
        n_in_proj = 0
        att_s0 = jnp.zeros((0, t_new, d_attn), F32)
        part_cache = jnp.zeros(cache2.shape, cache2.dtype)
    zp3 = zp.reshape(bsz, seq, -1)
    mix_att_p, att_s1, part_cache = _attn_prompt(zp3, _prompt_bias_tables(rel_bias), n_heads, stream,
                                                 n_in_proj, n_in_attn, part_cache)
    att_s0 = jnp.concatenate([att_s0, att_s1], axis=0)
    zero_prefix = jnp.zeros((bsz, hist, d_conv), F32)
    if n_in_conv > 0:
        mix_conv_p, new_conv_p, att_s2, part_cache = _conv_branch(
            zp3, zero_prefix, *conv_args, col0=conv_col0, row_blocks=conv_row_blocks,
            stream=stream, seq0=att_s0.shape[0], n_seq=n_in_conv, partial_cache=part_cache)
        att_s0 = jnp.concatenate([att_s0, att_s2], axis=0)
    else:
        mix_conv_p, new_conv_p = _conv_branch(zp3, zero_prefix, *conv_args, col0=conv_col0,
                                              row_blocks=conv_row_blocks)
    yp = _outproj(mix_att_p.reshape(bsz * seq, d_attn), mix_conv_p.reshape(bsz * seq, d_conv),
                  w_out_bf[:d_attn], w_out_bf[d_attn:], xp2, norm_post[0], tm=512)
    win = min(MAX_WINDOW, seq)
    kv_rows_p = _kv_rows(zp, d_attn, tm=512).reshape(bsz, seq, n_heads, 2, HEAD_DIM)
    new_kv_p = kv_rows_p[:, seq - win:].transpose(0, 1, 3, 2, 4)[None]

    n_hosted = att_s0.shape[0]
    att_s, new_rows = att_s0, part_cache
    if n_hosted < n_dec:
        att_s1, new_rows = _attn_sample(stream, n_hosted, n_dec - n_hosted, part_cache)
        att_s = jnp.concatenate([att_s0, att_s1], axis=0)
    new_kv_s = new_rows.reshape(n_dec, past, n_heads, 2, HEAD_DIM).transpose(0, 1, 3, 2, 4)[None]
    mix_att_s = att_s.reshape(n_dec * t_new, d_attn).astype(BF16)
    mix_conv_s, new_conv_s = _conv_branch(zs3, cache_conv[0], *conv_args, col0=conv_col0)
    ys = _outproj(mix_att_s, mix_conv_s.reshape(n_dec * t_new, d_conv),
                  w_out_bf[:d_attn], w_out_bf[d_attn:], xs2, norm_post[0], tm=n_dec * t_new)

    return (yp.reshape(bsz, seq, d_model), ys.reshape(n_dec, t_new, d_model),
            new_conv_p[None], new_kv_p, new_conv_s[None], new_kv_s)
```

```python
import functools
import math
from typing import NamedTuple

import jax
import jax.numpy as jnp
import numpy as np
from jax import lax
from jax.experimental import pallas as pl
from jax.experimental.pallas import tpu as pltpu

F32 = jnp.float32
BF16 = jnp.bfloat16

HEAD_DIM = 128
PATTERNS = ((128, 1), (512, 4), (2048, 16))
MAX_WINDOW = 2048
Q_BLOCK = 128
KEYS_PER_PATTERN = 128
TABLE_LANES = 384
CONV_WIDTH = 31
N_BUCKETS = 32
MAX_EXACT = 16
EPS = 1e-6
NEG_INF = -1e30
SUBLANES = 8
HEAD_PAD = 16

MIB = 1024 * 1024


def _rel_bucket(dist):
    d = jnp.maximum(dist, 1).astype(F32)
    log_b = MAX_EXACT + (jnp.log(d / MAX_EXACT) / math.log(MAX_WINDOW / MAX_EXACT)
                         * (N_BUCKETS - MAX_EXACT)).astype(jnp.int32)
    log_b = jnp.minimum(log_b, N_BUCKETS - 1)
    return jnp.where(dist < MAX_EXACT, dist, log_b)


def _round_up(x, m):
    return -(-x // m) * m


def _silu(x):
    return x * jax.nn.sigmoid(x)


def _prenorm_kernel(x_ref, g_ref, h_ref):
    x = x_ref[...]
    ms = jnp.mean(x * x, axis=-1, keepdims=True)
    h_ref[...] = (x * lax.rsqrt(ms + EPS) * g_ref[...]).astype(h_ref.dtype)


def _prenorm(x2d, norm_g, tm):
    m, d = x2d.shape
    return pl.pallas_call(
        _prenorm_kernel,
        grid=(m // tm,),
        in_specs=[pl.BlockSpec((tm, d), lambda i: (i, 0)), pl.BlockSpec((1, d), lambda i: (0, 0))],
        out_specs=pl.BlockSpec((tm, d), lambda i: (i, 0)),
        out_shape=jax.ShapeDtypeStruct((m, d), BF16),
        compiler_params=pltpu.CompilerParams(
            dimension_semantics=("parallel",), vmem_limit_bytes=48 * MIB),
        name="prenorm",
    )(x2d, norm_g.reshape(1, d))


def _inproj_kernel(h_ref, w_ref, z_ref):
    z_ref[...] = jnp.dot(h_ref[...], w_ref[...], preferred_element_type=F32)


def _inproj_stream_kernel(h_ref, w_ref, *rest, n_col_tiles, **stream_kw):
    stream_in, (z_ref, so_ref, out_any), scratch = rest[:8], rest[8:11], rest[11:]
    _inproj_kernel(h_ref, w_ref, z_ref)
    k = pl.program_id(0) * n_col_tiles + pl.program_id(1)
    _stream_step_with_attention(k, stream_in, so_ref, out_any, scratch, **stream_kw)


def _inproj(h2d, w_bf, tm, tn, stream=None, n_seq=0):
    m, d = h2d.shape
    n = w_bf.shape[1]
    grid = (m // tm, n // tn)
    in_specs = [pl.BlockSpec((tm, d), lambda i, j: (i, 0)),
                pl.BlockSpec((d, tn), lambda i, j: (0, j))]
    z_spec = pl.BlockSpec((tm, tn), lambda i, j: (i, j))
    z_shape = jax.ShapeDtypeStruct((m, n), F32)
    if stream is None:
        return pl.pallas_call(
            _inproj_kernel, grid=grid, in_specs=in_specs, out_specs=z_spec, out_shape=z_shape,
            compiler_params=pltpu.CompilerParams(
                dimension_semantics=("parallel", "arbitrary"), vmem_limit_bytes=48 * MIB),
            name="inproj",
        )(h2d, w_bf)
    assert n_seq * stream.n_chunks <= grid[0] * grid[1]
    s_in, s_out, s_shapes, s_scratch, s_args, s_kw = _stream_operands(
        stream, 0, n_seq, lambda i, j: (i * grid[1] + j) // stream.n_chunks)
    return pl.pallas_call(
        functools.partial(_inproj_stream_kernel, n_col_tiles=grid[1], **s_kw),
        grid=grid, in_specs=in_specs + s_in, out_specs=[z_spec] + s_out,
        out_shape=[z_shape] + s_shapes, scratch_shapes=s_scratch,
        compiler_params=pltpu.CompilerParams(
            dimension_semantics=("arbitrary", "arbitrary"), vmem_limit_bytes=60 * MIB),
        name="inproj",
    )(h2d, w_bf, *s_args)


def _kv_rows_kernel(k_ref, v_ref, o_ref):
    tm = k_ref.shape[0]
    hd = o_ref.shape[1]
    n_heads = k_ref.shape[1] // hd
    for h in range(n_heads):
        o_ref[pl.ds(2 * h, tm, stride=2 * n_heads), :] = k_ref[:, h * hd:(h + 1) * hd]
        o_ref[pl.ds(2 * h + 1, tm, stride=2 * n_heads), :] = v_ref[:, h * hd:(h + 1) * hd]


def _kv_rows(z, d_attn, tm):
    m = z.shape[0]
    rows_kv = 2 * d_attn // HEAD_DIM
    return pl.pallas_call(
        _kv_rows_kernel,
        grid=(m // tm,),
        in_specs=[pl.BlockSpec((tm, d_attn), lambda i: (i, 1)),
                  pl.BlockSpec((tm, d_attn), lambda i: (i, 2))],
        out_specs=pl.BlockSpec((tm * rows_kv, HEAD_DIM), lambda i: (i, 0)),
        out_shape=jax.ShapeDtypeStruct((m * rows_kv, HEAD_DIM), F32),
        compiler_params=pltpu.CompilerParams(
            dimension_semantics=("parallel",), vmem_limit_bytes=48 * MIB),
        name="kv_rows",
    )(z, z)


def _attn_block(qb, kw, vw, tab, scale):
    s = lax.dot_general(qb.astype(BF16), kw.astype(BF16), (((1,), (1,)), ((), ())),
                        preferred_element_type=F32)
    s = s * scale + tab
    m = jnp.max(s, axis=-1, keepdims=True)
    p = jnp.exp(s - m).astype(BF16)
    v_ones = jnp.concatenate([vw.astype(BF16), jnp.ones(vw.shape, BF16)], axis=1)
    acc_l = jnp.dot(p, v_ones, preferred_element_type=F32)
    d = vw.shape[1]
    return acc_l[:, :d], m, acc_l[:, d:]


def _attn_prompt_kernel(q_ref, k_ref, v_ref, g_ref, vec_ref, o_ref, acc_ref, m_ref, l_ref, tab_ref):
    seq = q_ref.shape[0]
    scale = HEAD_DIM ** -0.5
    qb_rows = Q_BLOCK
    nk = KEYS_PER_PATTERN

    for p in range(len(PATTERNS)):
        base = jnp.broadcast_to(vec_ref[p][0:1, :], (qb_rows, vec_ref.shape[-1]))
        tab_ref[p] = pltpu.roll(base, 0, 1, stride=1, stride_axis=0)[:, :qb_rows + nk]

    def rows(ref, start, size, stride):
        if stride == 1:
            return ref[pl.ds(start, size), :]
        return ref[pl.ds(start, size, stride=stride), :]

    def put(p, start, stride, acc, m, l):
        lanes = acc.shape[-1]
        if stride == 1:
            idx = pl.ds(start, qb_rows)
        else:
            idx = pl.ds(start, qb_rows, stride=stride)
        acc_ref[p, idx, :] = acc
        m_ref[p, idx, :] = jnp.broadcast_to(m, (qb_rows, lanes))
        l_ref[p, idx, :] = l

    def first_block(p, phase, dil):
        tab = tab_ref[p][:, nk:]
        qb = rows(q_ref, phase, qb_rows, dil)
        kw = rows(k_ref, phase, qb_rows, dil)
        vw = rows(v_ref, phase, qb_rows, dil)
        put(p, phase, dil, *_attn_block(qb, kw, vw, tab, scale))

    def later_block(p, phase, dil, n):
        tab = tab_ref[p]
        q0 = phase + dil * qb_rows * n
        k0 = q0 - dil * nk
        qb = rows(q_ref, q0, qb_rows, dil)
        kw = rows(k_ref, k0, qb_rows + nk, dil)
        vw = rows(v_ref, k0, qb_rows + nk, dil)
        put(p, q0, dil, *_attn_block(qb, kw, vw, tab, scale))

    for p, (window, dil) in enumerate(PATTERNS):
        n_blocks = seq // dil // qb_rows
        for phase in range(dil):
            first_block(p, phase, dil)
            for n in range(1, n_blocks):
                later_block(p, phase, dil, n)

    chunk = 256

    def combine(c, carry):
        sl = pl.ds(pl.multiple_of(c * chunk, chunk), chunk)
        m0, m1, m2 = m_ref[0, sl, :], m_ref[1, sl, :], m_ref[2, sl, :]
        mm = jnp.maximum(jnp.maximum(m0, m1), m2)
        e0, e1, e2 = jnp.exp(m0 - mm), jnp.exp(m1 - mm), jnp.exp(m2 - mm)
        num = e0 * acc_ref[0, sl, :] + e1 * acc_ref[1, sl, :] + e2 * acc_ref[2, sl, :]
        den = e0 * l_ref[0, sl, :] + e1 * l_ref[1, sl, :] + e2 * l_ref[2, sl, :]
        o_ref[sl, :] = (num / den * _silu(g_ref[sl, :])).astype(o_ref.dtype)
        return carry
    lax.fori_loop(0, seq // chunk, combine, 0)


def _attn_prompt_stream_kernel(q_ref, k_ref, v_ref, g_ref, tab_ref, *rest, n_heads, **stream_kw):
    stream_in, (o_ref, so_ref, out_any), scratch = rest[:8], rest[9:12], rest[12:]
    _attn_prompt_kernel(q_ref, k_ref, v_ref, g_ref, tab_ref, o_ref, *scratch[:4])
    k = pl.program_id(0) * n_heads + pl.program_id(1)
    _stream_step_with_attention(k, stream_in, so_ref, out_any, scratch[4:], **stream_kw)


def _attn_prompt(z3, tabs, n_heads, stream, seq0, n_seq, partial_cache):
    b, seq, _ = z3.shape
    hd = HEAD_DIM
    assert 0 < n_seq * stream.n_chunks <= b * n_heads

    def col(off):
        return pl.BlockSpec((None, seq, hd), lambda i, h: (i, 0, off + h))

    def seq_of(i, h):
        return (i * n_heads + h) // stream.n_chunks

    s_in, s_out, s_shapes, s_scratch, s_args, s_kw = _stream_operands(stream, seq0, n_seq, seq_of)
    in_specs = [col(0), col(n_heads), col(2 * n_heads), col(3 * n_heads),
                pl.BlockSpec((len(PATTERNS), None, SUBLANES, TABLE_LANES),
                             lambda i, h: (0, h, 0, 0))] + s_in + [pl.BlockSpec(memory_space=pl.ANY)]
    return pl.pallas_call(
        functools.partial(_attn_prompt_stream_kernel, n_heads=n_heads, **s_kw),
        grid=(b, n_heads),
        in_specs=in_specs,
        out_specs=[pl.BlockSpec((None, seq, hd), lambda i, h: (i, 0, h))] + s_out,
        out_shape=[jax.ShapeDtypeStruct((b, seq, n_heads * hd), BF16)] + s_shapes,
        scratch_shapes=([pltpu.VMEM((len(PATTERNS), seq, hd), F32)] * 3
                        + [pltpu.VMEM((len(PATTERNS), Q_BLOCK, Q_BLOCK + KEYS_PER_PATTERN), F32)]
                        + s_scratch),
        input_output_aliases={len(in_specs) - 1: 2},
        compiler_params=pltpu.CompilerParams(
            dimension_semantics=("arbitrary", "arbitrary"), vmem_limit_bytes=56 * MIB),
        name="attn_prompt",
    )(z3, z3, z3, z3, tabs, *s_args, partial_cache)


def _conv_kernel(ca_ref, cb_ref, gc_ref, pre_ref, dww_ref, dwb_ref, lng_ref, lnb_ref,
                 pww_ref, pwb_ref, o_ref, newc_ref, upad_ref, *, chunk):
    def one(s, carry):
        _conv_one_seq(ca_ref.at[s], cb_ref.at[s], gc_ref.at[s], pre_ref.at[s], dww_ref, dwb_ref,
                      lng_ref, lnb_ref, pww_ref, pwb_ref, o_ref.at[s], newc_ref.at[s], upad_ref,
                      chunk=chunk)
        return carry
    if ca_ref.shape[0] == 1:
        one(0, 0)
    else:
        lax.fori_loop(0, ca_ref.shape[0], one, 0)


def _conv_one_seq(ca_ref, cb_ref, gc_ref, pre_ref, dww_ref, dwb_ref, lng_ref, lnb_ref,
                  pww_ref, pwb_ref, o_ref, newc_ref, upad_ref, *, chunk):
    t_len = ca_ref.shape[0]
    rows_step = o_ref.shape[0]
    hist = CONV_WIDTH - 1
    rb = pl.program_id(1)

    @pl.when(rb == 0)
    def _():
        u = ca_ref[...] * jax.nn.sigmoid(cb_ref[...])
        upad_ref[0:hist, :] = pre_ref[...]
        upad_ref[hist:hist + t_len, :] = u
        n_pad = upad_ref.shape[0] - (hist + t_len)
        upad_ref[hist + t_len:, :] = jnp.zeros((n_pad, ca_ref.shape[1]), F32)
        newc_ref[...] = upad_ref[t_len:t_len + hist, :]

    win_rows = upad_ref.shape[0] - t_len + chunk

    def body(c, carry):
        l0 = c * chunk
        r0 = l0 if rows_step == t_len else rb * rows_step + l0
        if chunk % SUBLANES == 0:
            l0, r0 = pl.multiple_of(l0, SUBLANES), pl.multiple_of(r0, SUBLANES)
        win = upad_ref[pl.ds(r0, win_rows), :]
        y = jnp.zeros((chunk, ca_ref.shape[1]), F32) + dwb_ref[...]
        for s in range(SUBLANES):
            shifted = win if s == 0 else pltpu.roll(win, win_rows - s, 0)
            for a in range(-(-CONV_WIDTH // SUBLANES)):
                w = SUBLANES * a + s
                if w < CONV_WIDTH:
                    y = y + shifted[SUBLANES * a:SUBLANES * a + chunk] * dww_ref[w:w + 1, :]
        mu = jnp.mean(y, axis=-1, keepdims=True)
        var = jnp.mean(jnp.square(y - mu), axis=-1, keepdims=True)
        yn = (y - mu) * lax.rsqrt(var + EPS) * lng_ref[...] + lnb_ref[...]
        c_act = _silu(yn).astype(BF16)
        proj = jnp.dot(c_act, pww_ref[...], preferred_element_type=F32) + pwb_ref[...]
        o_ref[pl.ds(l0, chunk), :] = (proj * _silu(gc_ref[pl.ds(r0, chunk), :])).astype(o_ref.dtype)
        return carry
    if rows_step == chunk:
        body(0, 0)
    else:
        lax.fori_loop(0, rows_step // chunk, body, 0)


def _conv_stream_kernel(*refs, row_blocks, **stream_kw):
    chunk = stream_kw.pop("chunk")
    _conv_kernel(*refs[:10], refs[19], refs[20], refs[23], chunk=chunk)
    k = pl.program_id(0) * row_blocks + pl.program_id(1)
    _stream_step_with_attention(k, refs[10:18], refs[21], refs[22], refs[24:], **stream_kw)


def _conv_branch(z3, prefix, dw_w, dw_b, ln_g, ln_b, pw_w_bf, pw_b, col0, row_blocks=1,
                 stream=None, seq0=0, n_seq=0, partial_cache=None):
    n_all, t_len, _ = z3.shape
    c = prefix.shape[-1]
    hist = CONV_WIDTH - 1
    rows_step = t_len // row_blocks
    chunk = min(rows_step, 64)
    cblk = col0 // c
    group = n_all if (stream is None and t_len * c * 4 <= 64 * 1024) else 1
    n = n_all // group

    def zc(j):
        return pl.BlockSpec((group, t_len, c), lambda i, r: (i, 0, cblk + j))

    def vec():
        return pl.BlockSpec((1, c), lambda i, r: (0, 0))

    in_specs = [zc(0), zc(1), zc(2),
                pl.BlockSpec((group, hist, c), lambda i, r: (i, 0, 0)),
                pl.BlockSpec((CONV_WIDTH, c), lambda i, r: (0, 0)),
                vec(), vec(), vec(),
                pl.BlockSpec((c, c), lambda i, r: (0, 0)),
                vec()]
    out_specs = [pl.BlockSpec((group, rows_step, c), lambda i, r: (i, r, 0)),
                 pl.BlockSpec((group, hist, c), lambda i, r: (i, 0, 0))]
    out_shape = [jax.ShapeDtypeStruct((n_all, t_len, c), BF16),
                 jax.ShapeDtypeStruct((n_all, hist, c), F32)]
    scratch = [pltpu.VMEM((t_len - chunk + _round_up(chunk + CONV_WIDTH + 1, SUBLANES), c), F32)]
    args = (z3, z3, z3, prefix, dw_w, dw_b.reshape(1, c), ln_g.reshape(1, c), ln_b.reshape(1, c),
            pw_w_bf, pw_b.reshape(1, c))
    if stream is None:
        return pl.pallas_call(
            functools.partial(_conv_kernel, chunk=chunk),
            grid=(n, row_blocks), in_specs=in_specs, out_specs=out_specs, out_shape=out_shape,
            scratch_shapes=scratch,
            compiler_params=pltpu.CompilerParams(
                dimension_semantics=("parallel", "arbitrary"), vmem_limit_bytes=48 * MIB),
            name="conv_branch",
        )(*args)
    assert 0 < n_seq * stream.n_chunks <= n * row_blocks
    s_in, s_out, s_shapes, s_scratch, s_args, s_kw = _stream_operands(
        stream, seq0, n_seq, lambda i, r: (i * row_blocks + r) // stream.n_chunks)
    in_specs = in_specs + s_in + [pl.BlockSpec(memory_space=pl.ANY)]
    return pl.pallas_call(
        functools.partial(_conv_stream_kernel, row_blocks=row_blocks, chunk=chunk, **s_kw),
        grid=(n, row_blocks), in_specs=in_specs, out_specs=out_specs + s_out,
        out_shape=out_shape + s_shapes, scratch_shapes=scratch + s_scratch,
        input_output_aliases={len(in_specs) - 1: 3},
        compiler_params=pltpu.CompilerParams(
            dimension_semantics=("arbitrary", "arbitrary"), vmem_limit_bytes=60 * MIB),
        name="conv_branch",
    )(*args, *s_args, partial_cache)


def _outproj_kernel(ma_ref, mc_ref, wa_ref, wc_ref, x_ref, g_ref, y_ref):
    y = jnp.dot(ma_ref[...], wa_ref[...], preferred_element_type=F32)
    y = y + jnp.dot(mc_ref[...], wc_ref[...], preferred_element_type=F32)
    ms = jnp.mean(y * y, axis=-1, keepdims=True)
    y_ref[...] = x_ref[...] + y * lax.rsqrt(ms + EPS) * g_ref[...]


def _outproj(mix_att, mix_conv, w_att_bf, w_conv_bf, x2d, norm_g, tm):
    m, d = x2d.shape
    da, dc = mix_att.shape[1], mix_conv.shape[1]
    return pl.pallas_call(
        _outproj_kernel,
        grid=(m // tm,),
        in_specs=[pl.BlockSpec((tm, da), lambda i: (i, 0)),
                  pl.BlockSpec((tm, dc), lambda i: (i, 0)),
                  pl.BlockSpec((da, d), lambda i: (0, 0)),
                  pl.BlockSpec((dc, d), lambda i: (0, 0)),
                  pl.BlockSpec((tm, d), lambda i: (i, 0)),
                  pl.BlockSpec((1, d), lambda i: (0, 0))],
        out_specs=pl.BlockSpec((tm, d), lambda i: (i, 0)),
        out_shape=jax.ShapeDtypeStruct((m, d), F32),
        compiler_params=pltpu.CompilerParams(
            dimension_semantics=("parallel",), vmem_limit_bytes=48 * MIB),
        name="outproj",
    )(mix_att, mix_conv, w_att_bf, w_conv_bf, x2d, norm_g.reshape(1, d))


NEAR_ROWS = 512
FAR_STRIDE = 16


RING = 3


def _sample_heads(q_ref, k_ref, v_ref, g_ref, near_ref, far_ref, tab_ref, mult_ref, o_ref, rows_kv):
    t_new = q_ref.shape[0]
    hd = near_ref.shape[1]
    scale = HEAD_DIM ** -0.5
    pad_q = jnp.zeros((SUBLANES - t_new, hd), F32)
    pad_kv = jnp.zeros((hd - t_new, hd), F32)
    nt = (((1,), (1,)), ((), ()))
    n_far = far_ref.shape[0] // rows_kv
    mult = mult_ref[...]
    n_heads = rows_kv // 2

    def head_rows(ref, h, n, parity):
        return ref[pl.ds(2 * h + parity, n, stride=rows_kv), :].astype(BF16)

    def pad_bf(ref, h, pad):
        return jnp.concatenate([ref[:, h * hd:(h + 1) * hd], pad], axis=0).astype(BF16)

    scores = []
    for h in range(n_heads):
        q8 = pad_bf(q_ref, h, pad_q)
        scores.append(jnp.concatenate(
            [lax.dot_general(q8, head_rows(near_ref, h, NEAR_ROWS, 0), nt, preferred_element_type=F32),
             lax.dot_general(q8, head_rows(far_ref, h, n_far, 0), nt, preferred_element_type=F32),
             lax.dot_general(q8, pad_bf(k_ref, h, pad_kv), nt, preferred_element_type=F32)], axis=1))
    probs = []
    for h, s in enumerate(scores):
        s = s * scale + tab_ref[h]
        m = jnp.max(s, axis=-1, keepdims=True)
        p = jnp.exp(s - m) * mult
        probs.append((p.astype(BF16), jnp.sum(p, axis=-1, keepdims=True)))
    for h, (pb, l) in enumerate(probs):
        acc = jnp.dot(pb[:, :NEAR_ROWS], head_rows(near_ref, h, NEAR_ROWS, 1),
                      preferred_element_type=F32)
        acc = acc + jnp.dot(pb[:, NEAR_ROWS:NEAR_ROWS + n_far], head_rows(far_ref, h, n_far, 1),
                            preferred_element_type=F32)
        acc = acc + jnp.dot(pb[:, NEAR_ROWS + n_far:], pad_bf(v_ref, h, pad_kv),
                            preferred_element_type=F32)
        cols = slice(h * hd, (h + 1) * hd)
        o_ref[:, cols] = (acc / l)[:t_new] * _silu(g_ref[:, cols])


def _cache_stream_step(k, n_steps, seq0, n_chunks, cache_any, kvn_any, out_any, buf_ref, far_ref,
                       sem_in, sem_out, sem_tail, shift_sl, rows_kv, attend):
    chunk_sl = buf_ref.shape[1]
    far_per_chunk = NEAR_ROWS // FAR_STRIDE
    seq_sl = n_chunks * chunk_sl

    def chunk_in(j):
        return pltpu.make_async_copy(
            cache_any.at[seq0 + j // n_chunks, pl.ds((j % n_chunks) * chunk_sl, chunk_sl)],
            buf_ref.at[j % RING], sem_in.at[j % RING])

    def first_out(j):
        return pltpu.make_async_copy(
            buf_ref.at[j % RING, pl.ds(shift_sl, chunk_sl - shift_sl)],
            out_any.at[seq0 + j // n_chunks, pl.ds(0, chunk_sl - shift_sl)], sem_out.at[j % RING])

    def later_out(j):
        return pltpu.make_async_copy(
            buf_ref.at[j % RING],
            out_any.at[seq0 + j // n_chunks, pl.ds((j % n_chunks) * chunk_sl - shift_sl, chunk_sl)],
            sem_out.at[j % RING])

    def tail(j):
        return pltpu.make_async_copy(
            kvn_any.at[seq0 + j // n_chunks],
            out_any.at[seq0 + j // n_chunks, pl.ds(seq_sl - shift_sl, shift_sl)], sem_tail.at[0])

    def on_chunk(j, first, later):
        pl.when(j % n_chunks == 0)(first)
        pl.when(j % n_chunks != 0)(later)

    @pl.when(k == 0)
    def _():
        for j in range(RING - 1):
            chunk_in(jnp.int32(j)).start()

    c = k % n_chunks
    slot = k % RING
    chunk_in(k).wait()
    on_chunk(k, lambda: first_out(k).start(), lambda: later_out(k).start())
    near_ref = buf_ref.at[slot]

    for grp in range(far_per_chunk):
        dst = pl.multiple_of((c * far_per_chunk + grp) * shift_sl, SUBLANES)
        src = grp * FAR_STRIDE * rows_kv
        far_ref[pl.ds(dst, shift_sl), :] = near_ref[src:src + shift_sl, :]

    @pl.when(c == n_chunks - 1)
    def _():
        tail(k).start()
        attend(near_ref)
        tail(k).wait()

    @pl.when(k >= 1)
    def _():
        on_chunk(k - 1, lambda: first_out(k - 1).wait(), lambda: later_out(k - 1).wait())

    @pl.when(k + RING - 1 < n_steps)
    def _():
        chunk_in(k + RING - 1).start()

    @pl.when(k == n_steps - 1)
    def _():
        on_chunk(k, lambda: first_out(k).wait(), lambda: later_out(k).wait())


class _Stream(NamedTuple):
    zs3: jax.Array
    kvn_rows: jax.Array
    cache2: jax.Array
    tab: jax.Array
    mult: jax.Array
    d_attn: int
    rows_kv: int
    n_chunks: int


def _stream_operands(stream, seq0, n_seq, seq_of):
    t_new = stream.zs3.shape[1]
    hd = stream.cache2.shape[2]
    chunk_sl = stream.cache2.shape[1] // stream.n_chunks
    n_far = stream.cache2.shape[1] // stream.rows_kv // FAR_STRIDE * t_new

    def local(*g):
        return jnp.minimum(seq_of(*g), n_seq - 1)

    def zcol(j):
        return pl.BlockSpec((None, t_new, stream.d_attn), lambda *g: (seq0 + local(*g), 0, j))

    in_specs = [zcol(0), zcol(1), zcol(2), zcol(3),
                pl.BlockSpec(stream.tab.shape, lambda *g: (0, 0, 0)),
                pl.BlockSpec(stream.mult.shape, lambda *g: (0, 0)),
                pl.BlockSpec(memory_space=pl.ANY),
                pl.BlockSpec(memory_space=pl.ANY)]
    out_specs = [pl.BlockSpec((None, t_new, stream.d_attn), lambda *g: (local(*g), 0, 0)),
                 pl.BlockSpec(memory_space=pl.ANY)]
    out_shapes = [jax.ShapeDtypeStruct((n_seq, t_new, stream.d_attn), F32),
                  jax.ShapeDtypeStruct(stream.cache2.shape, stream.cache2.dtype)]
    scratch = [pltpu.VMEM((RING, chunk_sl, hd), F32),
               pltpu.VMEM((n_far * stream.rows_kv, hd), F32),
               pltpu.SemaphoreType.DMA((RING,)),
               pltpu.SemaphoreType.DMA((RING,)),
               pltpu.SemaphoreType.DMA((1,))]
    args = (stream.zs3,) * 4 + (stream.tab, stream.mult, stream.cache2, stream.kvn_rows)
    kw = dict(rows_kv=stream.rows_kv, seq0=seq0, n_chunks=stream.n_chunks,
              n_steps=n_seq * stream.n_chunks)
    return in_specs, out_specs, out_shapes, scratch, args, kw


def _stream_step_with_attention(k, stream_in, o_ref, out_any, scratch, *, rows_kv, seq0, n_chunks,
                                n_steps):
    q_ref, k_ref, v_ref, g_ref, tab_ref, mult_ref, cache_any, kvn_any = stream_in
    buf_ref, far_ref, sem_in, sem_out, sem_tail = scratch
    shift_sl = q_ref.shape[0] * rows_kv

    def attend(near_ref):
        _sample_heads(q_ref, k_ref, v_ref, g_ref, near_ref, far_ref, tab_ref, mult_ref, o_ref, rows_kv)

    @pl.when(k < n_steps)
    def _():
        _cache_stream_step(k, n_steps, seq0, n_chunks, cache_any, kvn_any, out_any, buf_ref, far_ref,
                           sem_in, sem_out, sem_tail, shift_sl, rows_kv, attend)


def _attn_sample_kernel(*refs, n_chunks, **stream_kw):
    k = pl.program_id(0) * n_chunks + pl.program_id(1)
    _stream_step_with_attention(k, refs[:8], refs[9], refs[10], refs[11:], n_chunks=n_chunks,
                                **stream_kw)


def _attn_sample(stream, seq0, n_seq, partial_cache):
    s_in, s_out, s_shapes, s_scratch, s_args, s_kw = _stream_operands(
        stream, seq0, n_seq, lambda i, c: i)
    return pl.pallas_call(
        functools.partial(_attn_sample_kernel, **s_kw),
        grid=(n_seq, stream.n_chunks),
        in_specs=s_in + [pl.BlockSpec(memory_space=pl.ANY)],
        out_specs=s_out,
        out_shape=s_shapes,
        scratch_shapes=s_scratch,
        input_output_aliases={len(s_in): 1},
        compiler_params=pltpu.CompilerParams(
            dimension_semantics=("arbitrary", "arbitrary"), vmem_limit_bytes=48 * MIB),
        name="attn_sample",
    )(*s_args, partial_cache)


def _prompt_bias_tables(rel_bias):
    nk = KEYS_PER_PATTERN
    qb = Q_BLOCK
    wrap = TABLE_LANES
    assert wrap >= 2 * qb + nk - 1
    m = np.arange(wrap)
    kdist = nk - np.where(m < qb + nk, m, m - wrap)
    valid = (kdist >= 0) & (kdist <= nk)
    tabs = []
    for _, dil in PATTERNS:
        bucket = _rel_bucket(jnp.asarray(np.clip(kdist, 0, nk) * dil, jnp.int32))
        vec = jnp.where(valid[:, None], rel_bias[bucket].astype(F32), NEG_INF).T
        tabs.append(jnp.broadcast_to(vec[:, None, :], (vec.shape[0], SUBLANES, wrap)))
    return jnp.stack(tabs)


def _pattern_count(dist, patterns):
    return sum(((dist % dil == 0) & (dist >= 0) & (dist <= window)).astype(np.int32)
               for window, dil in patterns)


def _sample_tables(rel_bias, t_new, past):
    def bias_at(dist):
        return rel_bias[_rel_bucket(jnp.asarray(dist, jnp.int32))].astype(F32)

    def masked(bias, count):
        return jnp.where(jnp.asarray(count > 0)[..., None], bias, NEG_INF)

    near_pats, far_pats = PATTERNS[:2], PATTERNS[2:]
    assert near_pats[-1][0] == NEAR_ROWS and far_pats[0][1] == FAR_STRIDE and t_new <= far_pats[0][1]
    desc = np.arange(NEAR_ROWS + t_new - 1, 0, -1)
    desc_cnt = _pattern_count(desc, near_pats)
    desc_tab = masked(bias_at(desc), desc_cnt)
    starts = [t_new - 1 - t for t in range(t_new)]
    near_tab = jnp.stack([desc_tab[s0:s0 + NEAR_ROWS] for s0 in starts])
    near_cnt = np.stack([desc_cnt[s0:s0 + NEAR_ROWS] for s0 in starts])
    groups = past // FAR_STRIDE
    far_dist = past - FAR_STRIDE * np.arange(groups)
    own = np.eye(t_new, dtype=bool)[:, None, :] & (_pattern_count(far_dist, far_pats) > 0)[None, :, None]
    far_tab = jnp.where(jnp.asarray(own)[..., None], bias_at(far_dist)[None, :, None, :], NEG_INF)
    far_tab = far_tab.reshape(t_new, groups * t_new, -1)
    far_cnt = np.ones((t_new, groups * t_new), np.int32)
    tj = np.arange(t_new)[:, None] - np.arange(HEAD_DIM)[None, :]
    new_cnt = np.where(np.arange(HEAD_DIM)[None, :] < t_new, _pattern_count(tj, PATTERNS), 0)
    new_tab = masked(bias_at(np.clip(tj, 0, None).reshape(-1)).reshape(t_new, HEAD_DIM, -1), new_cnt)
    tab = jnp.concatenate([near_tab, far_tab, new_tab], axis=1).transpose(2, 0, 1)
    cnt = np.concatenate([near_cnt, far_cnt, new_cnt], axis=1)
    pad = SUBLANES - t_new
    tab = jnp.pad(tab, ((0, 0), (0, pad), (0, 0)))
    mult = np.pad(np.maximum(cnt, 1), ((0, pad), (0, 0)), constant_values=1).astype(np.float32)
    return tab, jnp.asarray(mult)


def kernel(x_prompt, x_sample, cache_conv, cache_kv, rel_bias, norm_pre, w_in, conv_dw_w, conv_dw_b,
           conv_ln_g, conv_ln_b, conv_pw_w, conv_pw_b, w_out, norm_post):
    depth = w_in.shape[0]
    assert depth == 1
    bsz, seq, d_model = x_prompt.shape
    n_dec, t_new, _ = x_sample.shape
    n_heads = cache_kv.shape[4]
    d_attn = n_heads * HEAD_DIM
    d_conv = cache_conv.shape[-1]
    past = cache_kv.shape[2]
    assert past == MAX_WINDOW and seq >= MAX_WINDOW and t_new <= 4
    hist = CONV_WIDTH - 1

    w_in_bf = w_in[0].astype(BF16)
    w_out_bf = w_out[0].astype(BF16)
    pw_bf = conv_pw_w[0].astype(BF16)
    conv_args = (conv_dw_w[0], conv_dw_b[0], conv_ln_g[0], conv_ln_b[0], pw_bf, conv_pw_b[0])
    conv_col0 = 4 * d_attn

    xp2 = x_prompt.reshape(bsz * seq, d_model)
    xs2 = x_sample.reshape(n_dec * t_new, d_model)
    zs = _inproj(_prenorm(xs2, norm_pre[0], tm=n_dec * t_new), w_in_bf, tm=n_dec * t_new, tn=d_attn)
    zs3 = zs.reshape(n_dec, t_new, -1)

    def heads(col0):
        return zs3[:, :, col0:col0 + d_attn].reshape(n_dec, t_new, n_heads, HEAD_DIM)

    rows_kv = 2 * n_heads
    kvn_rows = jnp.stack([heads(d_attn), heads(2 * d_attn)], axis=3).reshape(
        n_dec, t_new * rows_kv, HEAD_DIM)
    cache2 = cache_kv[0].transpose(0, 1, 3, 2, 4).reshape(n_dec, past * rows_kv, HEAD_DIM)
    tab_s, mult_s = _sample_tables(rel_bias, t_new, past)
    assert past % NEAR_ROWS == 0 and t_new <= SUBLANES
    stream = _Stream(zs3, kvn_rows, cache2, tab_s, mult_s, d_attn, rows_kv, past // NEAR_ROWS)

    conv_row_blocks = 8
    n_in_conv = min(bsz * conv_row_blocks // stream.n_chunks, n_dec - 1)
    n_in_attn = min(bsz * n_heads // stream.n_chunks, n_dec - n_in_conv)
    n_in_proj = n_dec - n_in_attn - n_in_conv
    tm_p, tn_p = 2048, 512
    hp = _prenorm(xp2, norm_pre[0], tm=512)
    if 0 < n_in_proj * stream.n_chunks <= (bsz * seq // tm_p) * (w_in_bf.shape[1] // tn_p):
        zp, att_s0, part_cache = _inproj(hp, w_in_bf, tm_p, tn_p, stream, n_in_proj)
    else:
        zp = _inproj(hp, w_in_bf, tm_p, tn_p)
        n_in_proj = 0
        att_s0 = jnp.zeros((0, t_new, d_attn), F32)
        part_cache = jnp.zeros(cache2.shape, cache2.dtype)
    zp3 = zp.reshape(bsz, seq, -1)
    mix_att_p, att_s1, part_cache = _attn_prompt(zp3, _prompt_bias_tables(rel_bias), n_heads, stream,
                                                 n_in_proj, n_in_attn, part_cache)
    att_s0 = jnp.concatenate([att_s0, att_s1], axis=0)
    zero_prefix = jnp.zeros((bsz, hist, d_conv), F32)
    if n_in_conv > 0:
        mix_conv_p, new_conv_p, att_s2, part_cache = _conv_branch(
            zp3, zero_prefix, *conv_args, col0=conv_col0, row_blocks=conv_row_blocks,
            stream=stream, seq0=att_s0.shape[0], n_seq=n_in_conv, partial_cache=part_cache)
        att_s0 = jnp.concatenate([att_s0, att_s2], axis=0)
    else:
        mix_conv_p, new_conv_p = _conv_branch(zp3, zero_prefix, *conv_args, col0=conv_col0,
                                              row_blocks=conv_row_blocks)
    yp = _outproj(mix_att_p.reshape(bsz * seq, d_attn), mix_conv_p.reshape(bsz * seq, d_conv),
                  w_out_bf[:d_attn], w_out_bf[d_attn:], xp2, norm_post[0], tm=512)
    win = min(MAX_WINDOW, seq)
    kv_rows_p = _kv_rows(zp, d_attn, tm=512).reshape(bsz, seq, n_heads, 2, HEAD_DIM)
    new_kv_p = kv_rows_p[:, seq - win:].transpose(0, 1, 3, 2, 4)[None]

    n_hosted = att_s0.shape[0]
    att_s, new_rows = att_s0, part_cache
    if n_hosted < n_dec:
        att_s1, new_rows = _attn_sample(stream, n_hosted, n_dec - n_hosted, part_cache)
        att_s = jnp.concatenate([att_s0, att_s1], axis=0)
    new_kv_s = new_rows.reshape(n_dec, past, n_heads, 2, HEAD_DIM).transpose(0, 1, 3, 2, 4)[None]
    mix_att_s = att_s.reshape(n_dec * t_new, d_attn).astype(BF16)
    mix_conv_s, new_conv_s = _conv_branch(zs3, cache_conv[0], *conv_args, col0=conv_col0)
    ys = _outproj(mix_att_s, mix_conv_s.reshape(n_dec * t_new, d_conv),
                  w_out_bf[:d_attn], w_out_bf[d_attn:], xs2, norm_post[0], tm=n_dec * t_new)

    return (yp.reshape(bsz, seq, d_model), ys.reshape(n_dec, t_new, d_model),
            new_conv_p[None], new_kv_p, new_conv_s[None], new_kv_s)
```

```python
import functools
import math
from typing import NamedTuple

import jax
import jax.numpy as jnp
import numpy as np
from jax import lax
from jax.experimental import pallas as pl
from jax.experimental.pallas import tpu as pltpu

F32 = jnp.float32
BF16 = jnp.bfloat16

HEAD_DIM = 128
PATTERNS = ((128, 1), (512, 4), (2048, 16))
MAX_WINDOW = 2048
Q_BLOCK = 128
KEYS_PER_PATTERN = 128
TABLE_LANES = 384
CONV_WIDTH = 31
N_BUCKETS = 32
MAX_EXACT = 16
EPS = 1e-6
NEG_INF = -1e30
SUBLANES = 8
HEAD_PAD = 16

MIB = 1024 * 1024
VMEM_BYTES_V7X = 64 * MIB
VMEM_LIMIT = 3 * VMEM_BYTES_V7X // 4
VMEM_LIMIT_CARRIER = VMEM_BYTES_V7X - 4 * MIB


def _rel_bucket(dist):
    d = jnp.maximum(dist, 1).astype(F32)
    log_b = MAX_EXACT + (jnp.log(d / MAX_EXACT) / math.log(MAX_WINDOW / MAX_EXACT)
                         * (N_BUCKETS - MAX_EXACT)).astype(jnp.int32)
    log_b = jnp.minimum(log_b, N_BUCKETS - 1)
    return jnp.where(dist < MAX_EXACT, dist, log_b)


def _round_up(x, m):
    return -(-x // m) * m


def _silu(x):
    return x * jax.nn.sigmoid(x)


def _prenorm_kernel(x_ref, g_ref, h_ref):
    x = x_ref[...]
    ms = jnp.mean(x * x, axis=-1, keepdims=True)
    h_ref[...] = (x * lax.rsqrt(ms + EPS) * g_ref[...]).astype(h_ref.dtype)


def _prenorm(x2d, norm_g, tm):
    m, d = x2d.shape
    return pl.pallas_call(
        _prenorm_kernel,
        grid=(m // tm,),
        in_specs=[pl.BlockSpec((tm, d), lambda i: (i, 0)), pl.BlockSpec((1, d), lambda i: (0, 0))],
        out_specs=pl.BlockSpec((tm, d), lambda i: (i, 0)),
        out_shape=jax.ShapeDtypeStruct((m, d), BF16),
        compiler_params=pltpu.CompilerParams(
            dimension_semantics=("parallel",), vmem_limit_bytes=VMEM_LIMIT),
        name="prenorm",
    )(x2d, norm_g.reshape(1, d))


def _inproj_kernel(h_ref, w_ref, z_ref):
    z_ref[...] = jnp.dot(h_ref[...], w_ref[...], preferred_element_type=F32)


def _inproj_stream_kernel(h_ref, w_ref, *rest, n_col_tiles, **stream_kw):
    stream_in, (z_ref, so_ref, out_any), scratch = rest[:8], rest[8:11], rest[11:]
    _inproj_kernel(h_ref, w_ref, z_ref)
    k = pl.program_id(0) * n_col_tiles + pl.program_id(1)
    _stream_step_with_attention(k, stream_in, so_ref, out_any, scratch, **stream_kw)


def _inproj(h2d, w_bf, tm, tn, stream=None, n_seq=0):
    m, d = h2d.shape
    n = w_bf.shape[1]
    grid = (m // tm, n // tn)
    in_specs = [pl.BlockSpec((tm, d), lambda i, j: (i, 0)),
                pl.BlockSpec((d, tn), lambda i, j: (0, j))]
    z_spec = pl.BlockSpec((tm, tn), lambda i, j: (i, j))
    z_shape = jax.ShapeDtypeStruct((m, n), F32)
    if stream is None:
        return pl.pallas_call(
            _inproj_kernel, grid=grid, in_specs=in_specs, out_specs=z_spec, out_shape=z_shape,
            compiler_params=pltpu.CompilerParams(
                dimension_semantics=("parallel", "arbitrary"), vmem_limit_bytes=VMEM_LIMIT),
            name="inproj",
        )(h2d, w_bf)
    assert n_seq * stream.n_chunks <= grid[0] * grid[1]
    s_in, s_out, s_shapes, s_scratch, s_args, s_kw = _stream_operands(
        stream, 0, n_seq, lambda i, j: (i * grid[1] + j) // stream.n_chunks)
    return pl.pallas_call(
        functools.partial(_inproj_stream_kernel, n_col_tiles=grid[1], **s_kw),
        grid=grid, in_specs=in_specs + s_in, out_specs=[z_spec] + s_out,
        out_shape=[z_shape] + s_shapes, scratch_shapes=s_scratch,
        compiler_params=pltpu.CompilerParams(
            dimension_semantics=("arbitrary", "arbitrary"), vmem_limit_bytes=VMEM_LIMIT_CARRIER),
        name="inproj",
    )(h2d, w_bf, *s_args)


def _kv_rows_kernel(k_ref, v_ref, o_ref):
    tm = k_ref.shape[0]
    hd = o_ref.shape[1]
    n_heads = k_ref.shape[1] // hd
    for h in range(n_heads):
        o_ref[pl.ds(2 * h, tm, stride=2 * n_heads), :] = k_ref[:, h * hd:(h + 1) * hd]
        o_ref[pl.ds(2 * h + 1, tm, stride=2 * n_heads), :] = v_ref[:, h * hd:(h + 1) * hd]


def _kv_rows(z, d_attn, tm):
    m = z.shape[0]
    rows_kv = 2 * d_attn // HEAD_DIM
    return pl.pallas_call(
        _kv_rows_kernel,
        grid=(m // tm,),
        in_specs=[pl.BlockSpec((tm, d_attn), lambda i: (i, 1)),
                  pl.BlockSpec((tm, d_attn), lambda i: (i, 2))],
        out_specs=pl.BlockSpec((tm * rows_kv, HEAD_DIM), lambda i: (i, 0)),
        out_shape=jax.ShapeDtypeStruct((m * rows_kv, HEAD_DIM), F32),
        compiler_params=pltpu.CompilerParams(
            dimension_semantics=("parallel",), vmem_limit_bytes=VMEM_LIMIT),
        name="kv_rows",
    )(z, z)


def _attn_block(qb, kw, vw, tab, scale):
    s = lax.dot_general(qb.astype(BF16), kw.astype(BF16), (((1,), (1,)), ((), ())),
                        preferred_element_type=F32)
    s = s * scale + tab
    m = jnp.max(s, axis=-1, keepdims=True)
    p = jnp.exp(s - m).astype(BF16)
    v_ones = jnp.concatenate([vw.astype(BF16), jnp.ones(vw.shape, BF16)], axis=1)
    acc_l = jnp.dot(p, v_ones, preferred_element_type=F32)
    d = vw.shape[1]
    return acc_l[:, :d], m, acc_l[:, d:]


def _attn_prompt_kernel(q_ref, k_ref, v_ref, g_ref, vec_ref, o_ref, acc_ref, m_ref, l_ref, ph_ref,
                        tab_ref):
    seq = q_ref.shape[0]
    scale = HEAD_DIM ** -0.5
    qb_rows = Q_BLOCK
    nk = KEYS_PER_PATTERN

    for p in range(len(PATTERNS)):
        base = jnp.broadcast_to(vec_ref[p][0:1, :], (qb_rows, vec_ref.shape[-1]))
        tab_ref[p] = pltpu.roll(base, 0, 1, stride=1, stride_axis=0)[:, :qb_rows + nk]

    sub = PATTERNS[1][1]
    assert all(dil == 1 or dil % sub == 0 for _, dil in PATTERNS)
    sub_len = seq // sub
    srcs = (q_ref, k_ref, v_ref)
    for a, ref in enumerate(srcs):
        for s in range(sub):
            ph_ref[a, s * sub_len:(s + 1) * sub_len, :] = ref[pl.ds(s, sub_len, stride=sub), :]

    def rows(a, phase, dil, start, size):
        if dil == 1:
            return srcs[a][pl.ds(start, size), :]
        step = dil // sub
        base = (phase % sub) * sub_len + phase // sub + step * start
        if step == 1:
            return ph_ref[a, pl.ds(base, size), :]
        return ph_ref[a, pl.ds(base, size, stride=step), :]

    def put(p, start, stride, acc, m, l):
        lanes = acc.shape[-1]
        if stride == 1:
            idx = pl.ds(start, qb_rows)
        else:
            idx = pl.ds(start, qb_rows, stride=stride)
        acc_ref[p, idx, :] = acc
        m_ref[p, idx, :] = jnp.broadcast_to(m, (qb_rows, lanes))
        l_ref[p, idx, :] = l

    def first_block(p, phase, dil):
        tab = tab_ref[p][:, nk:]
        qb = rows(0, phase, dil, 0, qb_rows)
        kw = rows(1, phase, dil, 0, qb_rows)
        vw = rows(2, phase, dil, 0, qb_rows)
        put(p, phase, dil, *_attn_block(qb, kw, vw, tab, scale))

    def later_block(p, phase, dil, n):
        tab = tab_ref[p]
        qb = rows(0, phase, dil, qb_rows * n, qb_rows)
        kw = rows(1, phase, dil, qb_rows * n - nk, qb_rows + nk)
        vw = rows(2, phase, dil, qb_rows * n - nk, qb_rows + nk)
        put(p, phase + dil * qb_rows * n, dil, *_attn_block(qb, kw, vw, tab, scale))

    for p, (window, dil) in enumerate(PATTERNS):
        n_blocks = seq // dil // qb_rows
        for phase in range(dil):
            first_block(p, phase, dil)
            for n in range(1, n_blocks):
                later_block(p, phase, dil, n)

    chunk = 256

    def combine(c, carry):
        sl = pl.ds(pl.multiple_of(c * chunk, chunk), chunk)
        m0, m1, m2 = m_ref[0, sl, :], m_ref[1, sl, :], m_ref[2, sl, :]
        mm = jnp.maximum(jnp.maximum(m0, m1), m2)
        e0, e1, e2 = jnp.exp(m0 - mm), jnp.exp(m1 - mm), jnp.exp(m2 - mm)
        num = e0 * acc_ref[0, sl, :] + e1 * acc_ref[1, sl, :] + e2 * acc_ref[2, sl, :]
        den = e0 * l_ref[0, sl, :] + e1 * l_ref[1, sl, :] + e2 * l_ref[2, sl, :]
        o_ref[sl, :] = (num / den * _silu(g_ref[sl, :])).astype(o_ref.dtype)
        return carry
    lax.fori_loop(0, seq // chunk, combine, 0)


def _attn_prompt_stream_kernel(q_ref, k_ref, v_ref, g_ref, tab_ref, *rest, n_heads, **stream_kw):
    stream_in, (o_ref, so_ref, out_any), scratch = rest[:8], rest[9:12], rest[12:]
    _attn_prompt_kernel(q_ref, k_ref, v_ref, g_ref, tab_ref, o_ref, *scratch[:5])
    k = pl.program_id(0) * n_heads + pl.program_id(1)
    _stream_step_with_attention(k, stream_in, so_ref, out_any, scratch[5:], **stream_kw)


def _attn_prompt(z3, tabs, n_heads, stream, seq0, n_seq, partial_cache):
    b, seq, _ = z3.shape
    hd = HEAD_DIM
    assert 0 < n_seq * stream.n_chunks <= b * n_heads

    def col(off):
        return pl.BlockSpec((None, seq, hd), lambda i, h: (i, 0, off + h))

    def seq_of(i, h):
        return (i * n_heads + h) // stream.n_chunks

    s_in, s_out, s_shapes, s_scratch, s_args, s_kw = _stream_operands(stream, seq0, n_seq, seq_of)
    in_specs = [col(0), col(n_heads), col(2 * n_heads), col(3 * n_heads),
                pl.BlockSpec((len(PATTERNS), None, SUBLANES, TABLE_LANES),
                             lambda i, h: (0, h, 0, 0))] + s_in + [pl.BlockSpec(memory_space=pl.ANY)]
    return pl.pallas_call(
        functools.partial(_attn_prompt_stream_kernel, n_heads=n_heads, **s_kw),
        grid=(b, n_heads),
        in_specs=in_specs,
        out_specs=[pl.BlockSpec((None, seq, hd), lambda i, h: (i, 0, h))] + s_out,
        out_shape=[jax.ShapeDtypeStruct((b, seq, n_heads * hd), BF16)] + s_shapes,
        scratch_shapes=([pltpu.VMEM((len(PATTERNS), seq, hd), F32)] * 4
                        + [pltpu.VMEM((len(PATTERNS), Q_BLOCK, Q_BLOCK + KEYS_PER_PATTERN), F32)]
                        + s_scratch),
        input_output_aliases={len(in_specs) - 1: 2},
        compiler_params=pltpu.CompilerParams(
            dimension_semantics=("arbitrary", "arbitrary"), vmem_limit_bytes=VMEM_LIMIT_CARRIER),
        name="attn_prompt",
    )(z3, z3, z3, z3, tabs, *s_args, partial_cache)


def _conv_kernel(ca_ref, cb_ref, gc_ref, pre_ref, dww_ref, dwb_ref, lng_ref, lnb_ref,
                 pww_ref, pwb_ref, o_ref, newc_ref, upad_ref, *, chunk):
    def one(s, carry):
        _conv_one_seq(ca_ref.at[s], cb_ref.at[s], gc_ref.at[s], pre_ref.at[s], dww_ref, dwb_ref,
                      lng_ref, lnb_ref, pww_ref, pwb_ref, o_ref.at[s], newc_ref.at[s], upad_ref,
                      chunk=chunk)
        return carry
    if ca_ref.shape[0] == 1:
        one(0, 0)
    else:
        lax.fori_loop(0, ca_ref.shape[0], one, 0)


def _conv_one_seq(ca_ref, cb_ref, gc_ref, pre_ref, dww_ref, dwb_ref, lng_ref, lnb_ref,
                  pww_ref, pwb_ref, o_ref, newc_ref, upad_ref, *, chunk):
    t_len = ca_ref.shape[0]
    rows_step = o_ref.shape[0]
    hist = CONV_WIDTH - 1
    rb = pl.program_id(1)

    @pl.when(rb == 0)
    def _():
        u = ca_ref[...] * jax.nn.sigmoid(cb_ref[...])
        upad_ref[0:hist, :] = pre_ref[...]
        upad_ref[hist:hist + t_len, :] = u
        n_pad = upad_ref.shape[0] - (hist + t_len)
        upad_ref[hist + t_len:, :] = jnp.zeros((n_pad, ca_ref.shape[1]), F32)
        newc_ref[...] = upad_ref[t_len:t_len + hist, :]

    win_rows = upad_ref.shape[0] - t_len + chunk

    def body(c, carry):
        l0 = c * chunk
        r0 = l0 if rows_step == t_len else rb * rows_step + l0
        if chunk % SUBLANES == 0:
            l0, r0 = pl.multiple_of(l0, SUBLANES), pl.multiple_of(r0, SUBLANES)
        win = upad_ref[pl.ds(r0, win_rows), :]
        y = jnp.zeros((chunk, ca_ref.shape[1]), F32) + dwb_ref[...]
        for s in range(SUBLANES):
            shifted = win if s == 0 else pltpu.roll(win, win_rows - s, 0)
            for a in range(-(-CONV_WIDTH // SUBLANES)):
                w = SUBLANES * a + s
                if w < CONV_WIDTH:
                    y = y + shifted[SUBLANES * a:SUBLANES * a + chunk] * dww_ref[w:w + 1, :]
        mu = jnp.mean(y, axis=-1, keepdims=True)
        var = jnp.mean(jnp.square(y - mu), axis=-1, keepdims=True)
        yn = (y - mu) * lax.rsqrt(var + EPS) * lng_ref[...] + lnb_ref[...]
        c_act = _silu(yn).astype(BF16)
        proj = jnp.dot(c_act, pww_ref[...], preferred_element_type=F32) + pwb_ref[...]
        o_ref[pl.ds(l0, chunk), :] = (proj * _silu(gc_ref[pl.ds(r0, chunk), :])).astype(o_ref.dtype)
        return carry
    if rows_step == chunk:
        body(0, 0)
    else:
        lax.fori_loop(0, rows_step // chunk, body, 0)


def _conv_stream_kernel(*refs, row_blocks, **stream_kw):
    chunk = stream_kw.pop("chunk")
    _conv_kernel(*refs[:10], refs[19], refs[20], refs[23], chunk=chunk)
    k = pl.program_id(0) * row_blocks + pl.program_id(1)
    _stream_step_with_attention(k, refs[10:18], refs[21], refs[22], refs[24:], **stream_kw)


def _conv_branch(z3, prefix, dw_w, dw_b, ln_g, ln_b, pw_w_bf, pw_b, col0, row_blocks=1,
                 stream=None, seq0=0, n_seq=0, partial_cache=None):
    n_all, t_len, _ = z3.shape
    c = prefix.shape[-1]
    hist = CONV_WIDTH - 1
    rows_step = t_len // row_blocks
    chunk = min(rows_step, 64)
    cblk = col0 // c
    group = n_all if (stream is None and t_len * c * 4 <= 64 * 1024) else 1
    n = n_all // group

    def zc(j):
        return pl.BlockSpec((group, t_len, c), lambda i, r: (i, 0, cblk + j))

    def vec():
        return pl.BlockSpec((1, c), lambda i, r: (0, 0))

    in_specs = [zc(0), zc(1), zc(2),
                pl.BlockSpec((group, hist, c), lambda i, r: (i, 0, 0)),
                pl.BlockSpec((CONV_WIDTH, c), lambda i, r: (0, 0)),
                vec(), vec(), vec(),
                pl.BlockSpec((c, c), lambda i, r: (0, 0)),
                vec()]
    out_specs = [pl.BlockSpec((group, rows_step, c), lambda i, r: (i, r, 0)),
                 pl.BlockSpec((group, hist, c), lambda i, r: (i, 0, 0))]
    out_shape = [jax.ShapeDtypeStruct((n_all, t_len, c), BF16),
                 jax.ShapeDtypeStruct((n_all, hist, c), F32)]
    scratch = [pltpu.VMEM((t_len - chunk + _round_up(chunk + CONV_WIDTH + 1, SUBLANES), c), F32)]
    args = (z3, z3, z3, prefix, dw_w, dw_b.reshape(1, c), ln_g.reshape(1, c), ln_b.reshape(1, c),
            pw_w_bf, pw_b.reshape(1, c))
    if stream is None:
        return pl.pallas_call(
            functools.partial(_conv_kernel, chunk=chunk),
            grid=(n, row_blocks), in_specs=in_specs, out_specs=out_specs, out_shape=out_shape,
            scratch_shapes=scratch,
            compiler_params=pltpu.CompilerParams(
                dimension_semantics=("parallel", "arbitrary"), vmem_limit_bytes=VMEM_LIMIT),
            name="conv_branch",
        )(*args)
    assert 0 < n_seq * stream.n_chunks <= n * row_blocks
    s_in, s_out, s_shapes, s_scratch, s_args, s_kw = _stream_operands(
        stream, seq0, n_seq, lambda i, r: (i * row_blocks + r) // stream.n_chunks)
    in_specs = in_specs + s_in + [pl.BlockSpec(memory_space=pl.ANY)]
    return pl.pallas_call(
        functools.partial(_conv_stream_kernel, row_blocks=row_blocks, chunk=chunk, **s_kw),
        grid=(n, row_blocks), in_specs=in_specs, out_specs=out_specs + s_out,
        out_shape=out_shape + s_shapes, scratch_shapes=scratch + s_scratch,
        input_output_aliases={len(in_specs) - 1: 3},
        compiler_params=pltpu.CompilerParams(
            dimension_semantics=("arbitrary", "arbitrary"), vmem_limit_bytes=VMEM_LIMIT_CARRIER),
        name="conv_branch",
    )(*args, *s_args, partial_cache)


def _outproj_kernel(ma_ref, mc_ref, wa_ref, wc_ref, x_ref, g_ref, y_ref):
    y = jnp.dot(ma_ref[...], wa_ref[...], preferred_element_type=F32)
    y = y + jnp.dot(mc_ref[...], wc_ref[...], preferred_element_type=F32)
    ms = jnp.mean(y * y, axis=-1, keepdims=True)
    y_ref[...] = x_ref[...] + y * lax.rsqrt(ms + EPS) * g_ref[...]


def _outproj(mix_att, mix_conv, w_att_bf, w_conv_bf, x2d, norm_g, tm):
    m, d = x2d.shape
    da, dc = mix_att.shape[1], mix_conv.shape[1]
    return pl.pallas_call(
        _outproj_kernel,
        grid=(m // tm,),
        in_specs=[pl.BlockSpec((tm, da), lambda i: (i, 0)),
                  pl.BlockSpec((tm, dc), lambda i: (i, 0)),
                  pl.BlockSpec((da, d), lambda i: (0, 0)),
                  pl.BlockSpec((dc, d), lambda i: (0, 0)),
                  pl.BlockSpec((tm, d), lambda i: (i, 0)),
                  pl.BlockSpec((1, d), lambda i: (0, 0))],
        out_specs=pl.BlockSpec((tm, d), lambda i: (i, 0)),
        out_shape=jax.ShapeDtypeStruct((m, d), F32),
        compiler_params=pltpu.CompilerParams(
            dimension_semantics=("parallel",), vmem_limit_bytes=VMEM_LIMIT),
        name="outproj",
    )(mix_att, mix_conv, w_att_bf, w_conv_bf, x2d, norm_g.reshape(1, d))


NEAR_ROWS = 512
FAR_STRIDE = 16


RING = 3


def _sample_heads(q_ref, k_ref, v_ref, g_ref, near_ref, far_ref, tab_ref, mult_ref, o_ref, rows_kv):
    t_new = q_ref.shape[0]
    hd = near_ref.shape[1]
    scale = HEAD_DIM ** -0.5
    pad_q = jnp.zeros((SUBLANES - t_new, hd), F32)
    pad_kv = jnp.zeros((hd - t_new, hd), F32)
    nt = (((1,), (1,)), ((), ()))
    n_far = far_ref.shape[0] // rows_kv
    mult = mult_ref[...]
    n_heads = rows_kv // 2

    def head_rows(ref, h, n, parity):
        return ref[pl.ds(2 * h + parity, n, stride=rows_kv), :].astype(BF16)

    def pad_bf(ref, h, pad):
        return jnp.concatenate([ref[:, h * hd:(h + 1) * hd], pad], axis=0).astype(BF16)

    scores = []
    for h in range(n_heads):
        q8 = pad_bf(q_ref, h, pad_q)
        scores.append(jnp.concatenate(
            [lax.dot_general(q8, head_rows(near_ref, h, NEAR_ROWS, 0), nt, preferred_element_type=F32),
             lax.dot_general(q8, head_rows(far_ref, h, n_far, 0), nt, preferred_element_type=F32),
             lax.dot_general(q8, pad_bf(k_ref, h, pad_kv), nt, preferred_element_type=F32)], axis=1))
    probs = []
    for h, s in enumerate(scores):
        s = s * scale + tab_ref[h]
        m = jnp.max(s, axis=-1, keepdims=True)
        p = jnp.exp(s - m) * mult
        probs.append((p.astype(BF16), jnp.sum(p, axis=-1, keepdims=True)))
    for h, (pb, l) in enumerate(probs):
        acc = jnp.dot(pb[:, :NEAR_ROWS], head_rows(near_ref, h, NEAR_ROWS, 1),
                      preferred_element_type=F32)
        acc = acc + jnp.dot(pb[:, NEAR_ROWS:NEAR_ROWS + n_far], head_rows(far_ref, h, n_far, 1),
                            preferred_element_type=F32)
        acc = acc + jnp.dot(pb[:, NEAR_ROWS + n_far:], pad_bf(v_ref, h, pad_kv),
                            preferred_element_type=F32)
        cols = slice(h * hd, (h + 1) * hd)
        o_ref[:, cols] = (acc / l)[:t_new] * _silu(g_ref[:, cols])


def _cache_stream_step(k, n_steps, seq0, n_chunks, cache_any, kvn_any, out_any, buf_ref, far_ref,
                       sem_in, sem_out, sem_tail, shift_sl, rows_kv, attend):
    chunk_sl = buf_ref.shape[1]
    far_per_chunk = NEAR_ROWS // FAR_STRIDE
    seq_sl = n_chunks * chunk_sl

    def chunk_in(j):
        return pltpu.make_async_copy(
            cache_any.at[seq0 + j // n_chunks, pl.ds((j % n_chunks) * chunk_sl, chunk_sl)],
            buf_ref.at[j % RING], sem_in.at[j % RING])

    def first_out(j):
        return pltpu.make_async_copy(
            buf_ref.at[j % RING, pl.ds(shift_sl, chunk_sl - shift_sl)],
            out_any.at[seq0 + j // n_chunks, pl.ds(0, chunk_sl - shift_sl)], sem_out.at[j % RING])

    def later_out(j):
        return pltpu.make_async_copy(
            buf_ref.at[j % RING],
            out_any.at[seq0 + j // n_chunks, pl.ds((j % n_chunks) * chunk_sl - shift_sl, chunk_sl)],
            sem_out.at[j % RING])

    def tail(j):
        return pltpu.make_async_copy(
            kvn_any.at[seq0 + j // n_chunks],
            out_any.at[seq0 + j // n_chunks, pl.ds(seq_sl - shift_sl, shift_sl)], sem_tail.at[0])

    def on_chunk(j, first, later):
        pl.when(j % n_chunks == 0)(first)
        pl.when(j % n_chunks != 0)(later)

    @pl.when(k == 0)
    def _():
        for j in range(RING - 1):
            chunk_in(jnp.int32(j)).start()

    c = k % n_chunks
    slot = k % RING
    chunk_in(k).wait()
    on_chunk(k, lambda: first_out(k).start(), lambda: later_out(k).start())
    near_ref = buf_ref.at[slot]

    for grp in range(far_per_chunk):
        dst = pl.multiple_of((c * far_per_chunk + grp) * shift_sl, SUBLANES)
        src = grp * FAR_STRIDE * rows_kv
        far_ref[pl.ds(dst, shift_sl), :] = near_ref[src:src + shift_sl, :]

    @pl.when(c == n_chunks - 1)
    def _():
        tail(k).start()
        attend(near_ref)
        tail(k).wait()

    @pl.when(k >= 1)
    def _():
        on_chunk(k - 1, lambda: first_out(k - 1).wait(), lambda: later_out(k - 1).wait())

    @pl.when(k + RING - 1 < n_steps)
    def _():
        chunk_in(k + RING - 1).start()

    @pl.when(k == n_steps - 1)
    def _():
        on_chunk(k, lambda: first_out(k).wait(), lambda: later_out(k).wait())


class _Stream(NamedTuple):
    zs3: jax.Array
    kvn_rows: jax.Array
    cache2: jax.Array
    tab: jax.Array
    mult: jax.Array
    d_attn: int
    rows_kv: int
    n_chunks: int


def _stream_operands(stream, seq0, n_seq, seq_of):
    t_new = stream.zs3.shape[1]
    hd = stream.cache2.shape[2]
    chunk_sl = stream.cache2.shape[1] // stream.n_chunks
    n_far = stream.cache2.shape[1] // stream.rows_kv // FAR_STRIDE * t_new

    def local(*g):
        return jnp.minimum(seq_of(*g), n_seq - 1)

    def zcol(j):
        return pl.BlockSpec((None, t_new, stream.d_attn), lambda *g: (seq0 + local(*g), 0, j))

    in_specs = [zcol(0), zcol(1), zcol(2), zcol(3),
                pl.BlockSpec(stream.tab.shape, lambda *g: (0, 0, 0)),
                pl.BlockSpec(stream.mult.shape, lambda *g: (0, 0)),
                pl.BlockSpec(memory_space=pl.ANY),
                pl.BlockSpec(memory_space=pl.ANY)]
    out_specs = [pl.BlockSpec((None, t_new, stream.d_attn), lambda *g: (local(*g), 0, 0)),
                 pl.BlockSpec(memory_space=pl.ANY)]
    out_shapes = [jax.ShapeDtypeStruct((n_seq, t_new, stream.d_attn), F32),
                  jax.ShapeDtypeStruct(stream.cache2.shape, stream.cache2.dtype)]
    scratch = [pltpu.VMEM((RING, chunk_sl, hd), F32),
               pltpu.VMEM((n_far * stream.rows_kv, hd), F32),
               pltpu.SemaphoreType.DMA((RING,)),
               pltpu.SemaphoreType.DMA((RING,)),
               pltpu.SemaphoreType.DMA((1,))]
    args = (stream.zs3,) * 4 + (stream.tab, stream.mult, stream.cache2, stream.kvn_rows)
    kw = dict(rows_kv=stream.rows_kv, seq0=seq0, n_chunks=stream.n_chunks,
              n_steps=n_seq * stream.n_chunks)
    return in_specs, out_specs, out_shapes, scratch, args, kw


def _stream_step_with_attention(k, stream_in, o_ref, out_any, scratch, *, rows_kv, seq0, n_chunks,
                                n_steps):
    q_ref, k_ref, v_ref, g_ref, tab_ref, mult_ref, cache_any, kvn_any = stream_in
    buf_ref, far_ref, sem_in, sem_out, sem_tail = scratch
    shift_sl = q_ref.shape[0] * rows_kv

    def attend(near_ref):
        _sample_heads(q_ref, k_ref, v_ref, g_ref, near_ref, far_ref, tab_ref, mult_ref, o_ref, rows_kv)

    @pl.when(k < n_steps)
    def _():
        _cache_stream_step(k, n_steps, seq0, n_chunks, cache_any, kvn_any, out_any, buf_ref, far_ref,
                           sem_in, sem_out, sem_tail, shift_sl, rows_kv, attend)


def _attn_sample_kernel(*refs, n_chunks, **stream_kw):
    k = pl.program_id(0) * n_chunks + pl.program_id(1)
    _stream_step_with_attention(k, refs[:8], refs[9], refs[10], refs[11:], n_chunks=n_chunks,
                                **stream_kw)


def _attn_sample(stream, seq0, n_seq, partial_cache):
    s_in, s_out, s_shapes, s_scratch, s_args, s_kw = _stream_operands(
        stream, seq0, n_seq, lambda i, c: i)
    return pl.pallas_call(
        functools.partial(_attn_sample_kernel, **s_kw),
        grid=(n_seq, stream.n_chunks),
        in_specs=s_in + [pl.BlockSpec(memory_space=pl.ANY)],
        out_specs=s_out,
        out_shape=s_shapes,
        scratch_shapes=s_scratch,
        input_output_aliases={len(s_in): 1},
        compiler_params=pltpu.CompilerParams(
            dimension_semantics=("arbitrary", "arbitrary"), vmem_limit_bytes=VMEM_LIMIT),
        name="attn_sample",
    )(*s_args, partial_cache)


def _prompt_bias_tables(rel_bias):
    nk = KEYS_PER_PATTERN
    qb = Q_BLOCK
    wrap = TABLE_LANES
    assert wrap >= 2 * qb + nk - 1
    m = np.arange(wrap)
    kdist = nk - np.where(m < qb + nk, m, m - wrap)
    valid = (kdist >= 0) & (kdist <= nk)
    tabs = []
    for _, dil in PATTERNS:
        bucket = _rel_bucket(jnp.asarray(np.clip(kdist, 0, nk) * dil, jnp.int32))
        vec = jnp.where(valid[:, None], rel_bias[bucket].astype(F32), NEG_INF).T
        tabs.append(jnp.broadcast_to(vec[:, None, :], (vec.shape[0], SUBLANES, wrap)))
    return jnp.stack(tabs)


def _pattern_count(dist, patterns):
    return sum(((dist % dil == 0) & (dist >= 0) & (dist <= window)).astype(np.int32)
               for window, dil in patterns)


def _sample_tables(rel_bias, t_new, past):
    def bias_at(dist):
        return rel_bias[_rel_bucket(jnp.asarray(dist, jnp.int32))].astype(F32)

    def masked(bias, count):
        return jnp.where(jnp.asarray(count > 0)[..., None], bias, NEG_INF)

    near_pats, far_pats = PATTERNS[:2], PATTERNS[2:]
    assert near_pats[-1][0] == NEAR_ROWS and far_pats[0][1] == FAR_STRIDE and t_new <= far_pats[0][1]
    desc = np.arange(NEAR_ROWS + t_new - 1, 0, -1)
    desc_cnt = _pattern_count(desc, near_pats)
    desc_tab = masked(bias_at(desc), desc_cnt)
    starts = [t_new - 1 - t for t in range(t_new)]
    near_tab = jnp.stack([desc_tab[s0:s0 + NEAR_ROWS] for s0 in starts])
    near_cnt = np.stack([desc_cnt[s0:s0 + NEAR_ROWS] for s0 in starts])
    groups = past // FAR_STRIDE
    far_dist = past - FAR_STRIDE * np.arange(groups)
    own = np.eye(t_new, dtype=bool)[:, None, :] & (_pattern_count(far_dist, far_pats) > 0)[None, :, None]
    far_tab = jnp.where(jnp.asarray(own)[..., None], bias_at(far_dist)[None, :, None, :], NEG_INF)
    far_tab = far_tab.reshape(t_new, groups * t_new, -1)
    far_cnt = np.ones((t_new, groups * t_new), np.int32)
    tj = np.arange(t_new)[:, None] - np.arange(HEAD_DIM)[None, :]
    new_cnt = np.where(np.arange(HEAD_DIM)[None, :] < t_new, _pattern_count(tj, PATTERNS), 0)
    new_tab = masked(bias_at(np.clip(tj, 0, None).reshape(-1)).reshape(t_new, HEAD_DIM, -1), new_cnt)
    tab = jnp.concatenate([near_tab, far_tab, new_tab], axis=1).transpose(2, 0, 1)
    cnt = np.concatenate([near_cnt, far_cnt, new_cnt], axis=1)
    pad = SUBLANES - t_new
    tab = jnp.pad(tab, ((0, 0), (0, pad), (0, 0)))
    mult = np.pad(np.maximum(cnt, 1), ((0, pad), (0, 0)), constant_values=1).astype(np.float32)
    return tab, jnp.asarray(mult)


def kernel(x_prompt, x_sample, cache_conv, cache_kv, rel_bias, norm_pre, w_in, conv_dw_w, conv_dw_b,
           conv_ln_g, conv_ln_b, conv_pw_w, conv_pw_b, w_out, norm_post):
    depth = w_in.shape[0]
    assert depth == 1
    bsz, seq, d_model = x_prompt.shape
    n_dec, t_new, _ = x_sample.shape
    n_heads = cache_kv.shape[4]
    d_attn = n_heads * HEAD_DIM
    d_conv = cache_conv.shape[-1]
    past = cache_kv.shape[2]
    assert past == MAX_WINDOW and seq >= MAX_WINDOW and t_new <= 4
    hist = CONV_WIDTH - 1

    w_in_bf = w_in[0].astype(BF16)
    w_out_bf = w_out[0].astype(BF16)
    pw_bf = conv_pw_w[0].astype(BF16)
    conv_args = (conv_dw_w[0], conv_dw_b[0], conv_ln_g[0], conv_ln_b[0], pw_bf, conv_pw_b[0])
    conv_col0 = 4 * d_attn

    xp2 = x_prompt.reshape(bsz * seq, d_model)
    xs2 = x_sample.reshape(n_dec * t_new, d_model)
    zs = _inproj(_prenorm(xs2, norm_pre[0], tm=n_dec * t_new), w_in_bf, tm=n_dec * t_new, tn=d_attn)
    zs3 = zs.reshape(n_dec, t_new, -1)

    def heads(col0):
        return zs3[:, :, col0:col0 + d_attn].reshape(n_dec, t_new, n_heads, HEAD_DIM)

    rows_kv = 2 * n_heads
    kvn_rows = jnp.stack([heads(d_attn), heads(2 * d_attn)], axis=3).reshape(
        n_dec, t_new * rows_kv, HEAD_DIM)
    cache2 = cache_kv[0].transpose(0, 1, 3, 2, 4).reshape(n_dec, past * rows_kv, HEAD_DIM)
    tab_s, mult_s = _sample_tables(rel_bias, t_new, past)
    assert past % NEAR_ROWS == 0 and t_new <= SUBLANES
    stream = _Stream(zs3, kvn_rows, cache2, tab_s, mult_s, d_attn, rows_kv, past // NEAR_ROWS)

    conv_row_blocks = 8
    n_in_conv = min(bsz * conv_row_blocks // stream.n_chunks, n_dec - 1)
    n_in_attn = min(bsz * n_heads // stream.n_chunks, n_dec - n_in_conv)
    n_in_proj = n_dec - n_in_attn - n_in_conv
    tm_p, tn_p = 2048, 512
    hp = _prenorm(xp2, norm_pre[0], tm=512)
    if 0 < n_in_proj * stream.n_chunks <= (bsz * seq // tm_p) * (w_in_bf.shape[1] // tn_p):
        zp, att_s0, part_cache = _inproj(hp, w_in_bf, tm_p, tn_p, stream, n_in_proj)
    else:
        zp = _inproj(hp, w_in_bf, tm_p, tn_p)
        n_in_proj = 0
        att_s0 = jnp.zeros((0, t_new, d_attn), F32)
        part_cache = jnp.zeros(cache2.shape, cache2.dtype)
    zp3 = zp.reshape(bsz, seq, -1)
    mix_att_p, att_s1, part_cache = _attn_prompt(zp3, _prompt_bias_tables(rel_bias), n_heads, stream,
                                                 n_in_proj, n_in_attn, part_cache)
    att_s0 = jnp.concatenate([att_s0, att_s1], axis=0)
    zero_prefix = jnp.zeros((bsz, hist, d_conv), F32)
    if n_in_conv > 0:
        mix_conv_p, new_conv_p, att_s2, part_cache = _conv_branch(
            zp3, zero_prefix, *conv_args, col0=conv_col0, row_blocks=conv_row_blocks,
            stream=stream, seq0=att_s0.shape[0], n_seq=n_in_conv, partial_cache=part_cache)
        att_s0 = jnp.concatenate([att_s0, att_s2], axis=0)
    else:
        mix_conv_p, new_conv_p = _conv_branch(zp3, zero_prefix, *conv_args, col0=conv_col0,
                                              row_blocks=conv_row_blocks)
    yp = _outproj(mix_att_p.reshape(bsz * seq, d_attn), mix_conv_p.reshape(bsz * seq, d_conv),
                  w_out_bf[:d_attn], w_out_bf[d_attn:], xp2, norm_post[0], tm=512)
    win = min(MAX_WINDOW, seq)
    kv_rows_p = _kv_rows(zp, d_attn, tm=512).reshape(bsz, seq, n_heads, 2, HEAD_DIM)
    new_kv_p = kv_rows_p[:, seq - win:].transpose(0, 1, 3, 2, 4)[None]

    n_hosted = att_s0.shape[0]
    att_s, new_rows = att_s0, part_cache
    if n_hosted < n_dec:
        att_s1, new_rows = _attn_sample(stream, n_hosted, n_dec - n_hosted, part_cache)
        att_s = jnp.concatenate([att_s0, att_s1], axis=0)
    new_kv_s = new_rows.reshape(n_dec, past, n_heads, 2, HEAD_DIM).transpose(0, 1, 3, 2, 4)[None]
    mix_att_s = att_s.reshape(n_dec * t_new, d_attn).astype(BF16)
    mix_conv_s, new_conv_s = _conv_branch(zs3, cache_conv[0], *conv_args, col0=conv_col0)
    ys = _outproj(mix_att_s, mix_conv_s.reshape(n_dec * t_new, d_conv),
                  w_out_bf[:d_attn], w_out_bf[d_attn:], xs2, norm_post[0], tm=n_dec * t_new)

    return (yp.reshape(bsz, seq, d_model), ys.reshape(n_dec, t_new, d_model),
            new_conv_p[None], new_kv_p, new_conv_s[None], new_kv_s)
```

```python
import functools
import math
from typing import NamedTuple

import jax
import jax.numpy as jnp
import numpy as np
from jax import lax
from jax.experimental import pallas as pl
from jax.experimental.pallas import tpu as pltpu

F32 = jnp.float32
BF16 = jnp.bfloat16

HEAD_DIM = 128
PATTERNS = ((128, 1), (512, 4), (2048, 16))
MAX_WINDOW = 2048
Q_BLOCK = 128
KEYS_PER_PATTERN = 128
TABLE_LANES = 384
CONV_WIDTH = 31
N_BUCKETS = 32
MAX_EXACT = 16
EPS = 1e-6
NEG_INF = -1e30
SUBLANES = 8

MIB = 1024 * 1024
VMEM_BYTES_V7X = 64 * MIB
VMEM_LIMIT = 3 * VMEM_BYTES_V7X // 4
VMEM_LIMIT_CARRIER = VMEM_BYTES_V7X - 4 * MIB


def _rel_bucket(dist):
    d = jnp.maximum(dist, 1).astype(F32)
    log_b = MAX_EXACT + (jnp.log(d / MAX_EXACT) / math.log(MAX_WINDOW / MAX_EXACT)
                         * (N_BUCKETS - MAX_EXACT)).astype(jnp.int32)
    log_b = jnp.minimum(log_b, N_BUCKETS - 1)
    return jnp.where(dist < MAX_EXACT, dist, log_b)


def _round_up(x, m):
    return -(-x // m) * m


def _silu(x):
    return x * jax.nn.sigmoid(x)


def _prenorm_kernel(x_ref, g_ref, h_ref):
    x = x_ref[...]
    ms = jnp.mean(x * x, axis=-1, keepdims=True)
    h_ref[...] = (x * lax.rsqrt(ms + EPS) * g_ref[...]).astype(h_ref.dtype)


def _prenorm(x2d, norm_g, tm):
    m, d = x2d.shape
    return pl.pallas_call(
        _prenorm_kernel,
        grid=(m // tm,),
        in_specs=[pl.BlockSpec((tm, d), lambda i: (i, 0)), pl.BlockSpec((1, d), lambda i: (0, 0))],
        out_specs=pl.BlockSpec((tm, d), lambda i: (i, 0)),
        out_shape=jax.ShapeDtypeStruct((m, d), BF16),
        compiler_params=pltpu.CompilerParams(
            dimension_semantics=("parallel",), vmem_limit_bytes=VMEM_LIMIT),
        name="prenorm",
    )(x2d, norm_g.reshape(1, d))


def _inproj_kernel(h_ref, w_ref, z_ref):
    z_ref[...] = jnp.dot(h_ref[...], w_ref[...], preferred_element_type=F32)


def _inproj_stream_kernel(h_ref, w_ref, *rest, n_col_tiles, **stream_kw):
    stream_in, (z_ref, so_ref, out_any), scratch = rest[:8], rest[8:11], rest[11:]
    _inproj_kernel(h_ref, w_ref, z_ref)
    k = pl.program_id(0) * n_col_tiles + pl.program_id(1)
    _stream_step_with_attention(k, stream_in, so_ref, out_any, scratch, **stream_kw)


def _inproj(h2d, w_bf, tm, tn, stream=None, n_seq=0):
    m, d = h2d.shape
    n = w_bf.shape[1]
    grid = (m // tm, n // tn)
    in_specs = [pl.BlockSpec((tm, d), lambda i, j: (i, 0)),
                pl.BlockSpec((d, tn), lambda i, j: (0, j))]
    z_spec = pl.BlockSpec((tm, tn), lambda i, j: (i, j))
    z_shape = jax.ShapeDtypeStruct((m, n), F32)
    if stream is None:
        return pl.pallas_call(
            _inproj_kernel, grid=grid, in_specs=in_specs, out_specs=z_spec, out_shape=z_shape,
            compiler_params=pltpu.CompilerParams(
                dimension_semantics=("parallel", "arbitrary"), vmem_limit_bytes=VMEM_LIMIT),
            name="inproj",
        )(h2d, w_bf)
    assert n_seq * stream.n_chunks <= grid[0] * grid[1]
    s_in, s_out, s_shapes, s_scratch, s_args, s_kw = _stream_operands(
        stream, 0, n_seq, lambda i, j: (i * grid[1] + j) // stream.n_chunks)
    return pl.pallas_call(
        functools.partial(_inproj_stream_kernel, n_col_tiles=grid[1], **s_kw),
        grid=grid, in_specs=in_specs + s_in, out_specs=[z_spec] + s_out,
        out_shape=[z_shape] + s_shapes, scratch_shapes=s_scratch,
        compiler_params=pltpu.CompilerParams(
            dimension_semantics=("arbitrary", "arbitrary"), vmem_limit_bytes=VMEM_LIMIT_CARRIER),
        name="inproj",
    )(h2d, w_bf, *s_args)


def _kv_rows_kernel(k_ref, v_ref, o_ref):
    tm = k_ref.shape[0]
    hd = o_ref.shape[1]
    n_heads = k_ref.shape[1] // hd
    for h in range(n_heads):
        o_ref[pl.ds(2 * h, tm, stride=2 * n_heads), :] = k_ref[:, h * hd:(h + 1) * hd]
        o_ref[pl.ds(2 * h + 1, tm, stride=2 * n_heads), :] = v_ref[:, h * hd:(h + 1) * hd]


def _kv_rows(z, d_attn, tm):
    m = z.shape[0]
    rows_kv = 2 * d_attn // HEAD_DIM
    return pl.pallas_call(
        _kv_rows_kernel,
        grid=(m // tm,),
        in_specs=[pl.BlockSpec((tm, d_attn), lambda i: (i, 1)),
                  pl.BlockSpec((tm, d_attn), lambda i: (i, 2))],
        out_specs=pl.BlockSpec((tm * rows_kv, HEAD_DIM), lambda i: (i, 0)),
        out_shape=jax.ShapeDtypeStruct((m * rows_kv, HEAD_DIM), F32),
        compiler_params=pltpu.CompilerParams(
            dimension_semantics=("parallel",), vmem_limit_bytes=VMEM_LIMIT),
        name="kv_rows",
    )(z, z)


def _attn_block(qb, kw, vw, tab, scale):
    s = lax.dot_general(qb.astype(BF16), kw.astype(BF16), (((1,), (1,)), ((), ())),
                        preferred_element_type=F32)
    s = s * scale + tab
    m = jnp.max(s, axis=-1, keepdims=True)
    p = jnp.exp(s - m).astype(BF16)
    v_ones = jnp.concatenate([vw.astype(BF16), jnp.ones(vw.shape, BF16)], axis=1)
    acc_l = jnp.dot(p, v_ones, preferred_element_type=F32)
    d = vw.shape[1]
    return acc_l[:, :d], m, acc_l[:, d:]


def _attn_prompt_kernel(q_ref, k_ref, v_ref, g_ref, vec_ref, o_ref, acc_ref, m_ref, l_ref, ph_ref,
                        tab_ref):
    seq = q_ref.shape[0]
    scale = HEAD_DIM ** -0.5
    qb_rows = Q_BLOCK
    nk = KEYS_PER_PATTERN

    for p in range(len(PATTERNS)):
        base = jnp.broadcast_to(vec_ref[p][0:1, :], (qb_rows, vec_ref.shape[-1]))
        tab_ref[p] = pltpu.roll(base, 0, 1, stride=1, stride_axis=0)[:, :qb_rows + nk]

    sub = PATTERNS[1][1]
    assert all(dil == 1 or dil % sub == 0 for _, dil in PATTERNS)
    sub_len = seq // sub
    srcs = (q_ref, k_ref, v_ref)
    for a, ref in enumerate(srcs):
        for s in range(sub):
            ph_ref[a, s * sub_len:(s + 1) * sub_len, :] = ref[pl.ds(s, sub_len, stride=sub), :]

    def rows(a, phase, dil, start, size):
        if dil == 1:
            return srcs[a][pl.ds(start, size), :]
        step = dil // sub
        base = (phase % sub) * sub_len + phase // sub + step * start
        if step == 1:
            return ph_ref[a, pl.ds(base, size), :]
        return ph_ref[a, pl.ds(base, size, stride=step), :]

    def put(p, start, stride, acc, m, l):
        lanes = acc.shape[-1]
        if stride == 1:
            idx = pl.ds(start, qb_rows)
        else:
            idx = pl.ds(start, qb_rows, stride=stride)
        acc_ref[p, idx, :] = acc
        m_ref[p, idx, :] = jnp.broadcast_to(m, (qb_rows, lanes))
        l_ref[p, idx, :] = l

    def first_block(p, phase, dil):
        tab = tab_ref[p][:, nk:]
        qb = rows(0, phase, dil, 0, qb_rows)
        kw = rows(1, phase, dil, 0, qb_rows)
        vw = rows(2, phase, dil, 0, qb_rows)
        put(p, phase, dil, *_attn_block(qb, kw, vw, tab, scale))

    def later_block(p, phase, dil, n):
        tab = tab_ref[p]
        qb = rows(0, phase, dil, qb_rows * n, qb_rows)
        kw = rows(1, phase, dil, qb_rows * n - nk, qb_rows + nk)
        vw = rows(2, phase, dil, qb_rows * n - nk, qb_rows + nk)
        put(p, phase + dil * qb_rows * n, dil, *_attn_block(qb, kw, vw, tab, scale))

    for p, (window, dil) in enumerate(PATTERNS):
        n_blocks = seq // dil // qb_rows
        for phase in range(dil):
            first_block(p, phase, dil)
            for n in range(1, n_blocks):
                later_block(p, phase, dil, n)

    chunk = 256

    def combine(c, carry):
        sl = pl.ds(pl.multiple_of(c * chunk, chunk), chunk)
        m0, m1, m2 = m_ref[0, sl, :], m_ref[1, sl, :], m_ref[2, sl, :]
        mm = jnp.maximum(jnp.maximum(m0, m1), m2)
        e0, e1, e2 = jnp.exp(m0 - mm), jnp.exp(m1 - mm), jnp.exp(m2 - mm)
        num = e0 * acc_ref[0, sl, :] + e1 * acc_ref[1, sl, :] + e2 * acc_ref[2, sl, :]
        den = e0 * l_ref[0, sl, :] + e1 * l_ref[1, sl, :] + e2 * l_ref[2, sl, :]
        o_ref[sl, :] = (num / den * _silu(g_ref[sl, :])).astype(o_ref.dtype)
        return carry
    lax.fori_loop(0, seq // chunk, combine, 0)


def _attn_prompt_stream_kernel(q_ref, k_ref, v_ref, g_ref, tab_ref, *rest, n_heads, **stream_kw):
    stream_in, (o_ref, so_ref, out_any), scratch = rest[:8], rest[9:12], rest[12:]
    _attn_prompt_kernel(q_ref, k_ref, v_ref, g_ref, tab_ref, o_ref, *scratch[:5])
    k = pl.program_id(0) * n_heads + pl.program_id(1)
    _stream_step_with_attention(k, stream_in, so_ref, out_any, scratch[5:], **stream_kw)


def _attn_prompt(z3, tabs, n_heads, stream, seq0, n_seq, partial_cache):
    b, seq, _ = z3.shape
    hd = HEAD_DIM
    assert 0 < n_seq * stream.n_chunks <= b * n_heads

    def col(off):
        return pl.BlockSpec((None, seq, hd), lambda i, h: (i, 0, off + h))

    def seq_of(i, h):
        return (i * n_heads + h) // stream.n_chunks

    s_in, s_out, s_shapes, s_scratch, s_args, s_kw = _stream_operands(stream, seq0, n_seq, seq_of)
    in_specs = [col(0), col(n_heads), col(2 * n_heads), col(3 * n_heads),
                pl.BlockSpec((len(PATTERNS), None, SUBLANES, TABLE_LANES),
                             lambda i, h: (0, h, 0, 0))] + s_in + [pl.BlockSpec(memory_space=pl.ANY)]
    return pl.pallas_call(
        functools.partial(_attn_prompt_stream_kernel, n_heads=n_heads, **s_kw),
        grid=(b, n_heads),
        in_specs=in_specs,
        out_specs=[pl.BlockSpec((None, seq, hd), lambda i, h: (i, 0, h))] + s_out,
        out_shape=[jax.ShapeDtypeStruct((b, seq, n_heads * hd), BF16)] + s_shapes,
        scratch_shapes=([pltpu.VMEM((len(PATTERNS), seq, hd), F32)] * 4
                        + [pltpu.VMEM((len(PATTERNS), Q_BLOCK, Q_BLOCK + KEYS_PER_PATTERN), F32)]
                        + s_scratch),
        input_output_aliases={len(in_specs) - 1: 2},
        compiler_params=pltpu.CompilerParams(
            dimension_semantics=("arbitrary", "arbitrary"), vmem_limit_bytes=VMEM_LIMIT_CARRIER),
        name="attn_prompt",
    )(z3, z3, z3, z3, tabs, *s_args, partial_cache)


def _conv_kernel(ca_ref, cb_ref, gc_ref, pre_ref, dww_ref, dwb_ref, lng_ref, lnb_ref,
                 pww_ref, pwb_ref, o_ref, newc_ref, upad_ref, *, chunk):
    def one(s, carry):
        _conv_one_seq(ca_ref.at[s], cb_ref.at[s], gc_ref.at[s], pre_ref.at[s], dww_ref, dwb_ref,
                      lng_ref, lnb_ref, pww_ref, pwb_ref, o_ref.at[s], newc_ref.at[s], upad_ref,
                      chunk=chunk)
        return carry
    if ca_ref.shape[0] == 1:
        one(0, 0)
    else:
        lax.fori_loop(0, ca_ref.shape[0], one, 0)


def _conv_one_seq(ca_ref, cb_ref, gc_ref, pre_ref, dww_ref, dwb_ref, lng_ref, lnb_ref,
                  pww_ref, pwb_ref, o_ref, newc_ref, upad_ref, *, chunk):
    t_len = ca_ref.shape[0]
    rows_step = o_ref.shape[0]
    hist = CONV_WIDTH - 1
    rb = pl.program_id(1)

    @pl.when(rb == 0)
    def _():
        u = ca_ref[...] * jax.nn.sigmoid(cb_ref[...])
        upad_ref[0:hist, :] = pre_ref[...]
        upad_ref[hist:hist + t_len, :] = u
        n_pad = upad_ref.shape[0] - (hist + t_len)
        upad_ref[hist + t_len:, :] = jnp.zeros((n_pad, ca_ref.shape[1]), F32)
        newc_ref[...] = upad_ref[t_len:t_len + hist, :]

    win_rows = upad_ref.shape[0] - t_len + chunk

    def body(c, carry):
        l0 = c * chunk
        r0 = l0 if rows_step == t_len else rb * rows_step + l0
        if chunk % SUBLANES == 0:
            l0, r0 = pl.multiple_of(l0, SUBLANES), pl.multiple_of(r0, SUBLANES)
        win = upad_ref[pl.ds(r0, win_rows), :]
        y = jnp.zeros((chunk, ca_ref.shape[1]), F32) + dwb_ref[...]
        for s in range(SUBLANES):
            shifted = win if s == 0 else pltpu.roll(win, win_rows - s, 0)
            for a in range(-(-CONV_WIDTH // SUBLANES)):
                w = SUBLANES * a + s
                if w < CONV_WIDTH:
                    y = y + shifted[SUBLANES * a:SUBLANES * a + chunk] * dww_ref[w:w + 1, :]
        mu = jnp.mean(y, axis=-1, keepdims=True)
        var = jnp.mean(jnp.square(y - mu), axis=-1, keepdims=True)
        yn = (y - mu) * lax.rsqrt(var + EPS) * lng_ref[...] + lnb_ref[...]
        c_act = _silu(yn).astype(BF16)
        proj = jnp.dot(c_act, pww_ref[...], preferred_element_type=F32) + pwb_ref[...]
        o_ref[pl.ds(l0, chunk), :] = (proj * _silu(gc_ref[pl.ds(r0, chunk), :])).astype(o_ref.dtype)
        return carry
    if rows_step == chunk:
        body(0, 0)
    else:
        lax.fori_loop(0, rows_step // chunk, body, 0)


def _conv_stream_kernel(*refs, row_blocks, **stream_kw):
    chunk = stream_kw.pop("chunk")
    _conv_kernel(*refs[:10], refs[19], refs[20], refs[23], chunk=chunk)
    k = pl.program_id(0) * row_blocks + pl.program_id(1)
    _stream_step_with_attention(k, refs[10:18], refs[21], refs[22], refs[24:], **stream_kw)


def _conv_branch(z3, prefix, dw_w, dw_b, ln_g, ln_b, pw_w_bf, pw_b, col0, row_blocks=1,
                 stream=None, seq0=0, n_seq=0, partial_cache=None):
    n_all, t_len, _ = z3.shape
    c = prefix.shape[-1]
    hist = CONV_WIDTH - 1
    rows_step = t_len // row_blocks
    chunk = min(rows_step, 64)
    cblk = col0 // c
    group = n_all if (stream is None and t_len * c * 4 <= 64 * 1024) else 1
    n = n_all // group

    def zc(j):
        return pl.BlockSpec((group, t_len, c), lambda i, r: (i, 0, cblk + j))

    def vec():
        return pl.BlockSpec((1, c), lambda i, r: (0, 0))

    in_specs = [zc(0), zc(1), zc(2),
                pl.BlockSpec((group, hist, c), lambda i, r: (i, 0, 0)),
                pl.BlockSpec((CONV_WIDTH, c), lambda i, r: (0, 0)),
                vec(), vec(), vec(),
                pl.BlockSpec((c, c), lambda i, r: (0, 0)),
                vec()]
    out_specs = [pl.BlockSpec((group, rows_step, c), lambda i, r: (i, r, 0)),
                 pl.BlockSpec((group, hist, c), lambda i, r: (i, 0, 0))]
    out_shape = [jax.ShapeDtypeStruct((n_all, t_len, c), BF16),
                 jax.ShapeDtypeStruct((n_all, hist, c), F32)]
    scratch = [pltpu.VMEM((t_len - chunk + _round_up(chunk + CONV_WIDTH + 1, SUBLANES), c), F32)]
    args = (z3, z3, z3, prefix, dw_w, dw_b.reshape(1, c), ln_g.reshape(1, c), ln_b.reshape(1, c),
            pw_w_bf, pw_b.reshape(1, c))
    if stream is None:
        return pl.pallas_call(
            functools.partial(_conv_kernel, chunk=chunk),
            grid=(n, row_blocks), in_specs=in_specs, out_specs=out_specs, out_shape=out_shape,
            scratch_shapes=scratch,
            compiler_params=pltpu.CompilerParams(
                dimension_semantics=("parallel", "arbitrary"), vmem_limit_bytes=VMEM_LIMIT),
            name="conv_branch",
        )(*args)
    assert 0 < n_seq * stream.n_chunks <= n * row_blocks
    s_in, s_out, s_shapes, s_scratch, s_args, s_kw = _stream_operands(
        stream, seq0, n_seq, lambda i, r: (i * row_blocks + r) // stream.n_chunks)
    in_specs = in_specs + s_in + [pl.BlockSpec(memory_space=pl.ANY)]
    return pl.pallas_call(
        functools.partial(_conv_stream_kernel, row_blocks=row_blocks, chunk=chunk, **s_kw),
        grid=(n, row_blocks), in_specs=in_specs, out_specs=out_specs + s_out,
        out_shape=out_shape + s_shapes, scratch_shapes=scratch + s_scratch,
        input_output_aliases={len(in_specs) - 1: 3},
        compiler_params=pltpu.CompilerParams(
            dimension_semantics=("arbitrary", "arbitrary"), vmem_limit_bytes=VMEM_LIMIT_CARRIER),
        name="conv_branch",
    )(*args, *s_args, partial_cache)


def _outproj_kernel(ma_ref, mc_ref, wa_ref, wc_ref, x_ref, g_ref, y_ref):
    y = jnp.dot(ma_ref[...], wa_ref[...], preferred_element_type=F32)
    y = y + jnp.dot(mc_ref[...], wc_ref[...], preferred_element_type=F32)
    ms = jnp.mean(y * y, axis=-1, keepdims=True)
    y_ref[...] = x_ref[...] + y * lax.rsqrt(ms + EPS) * g_ref[...]


def _outproj(mix_att, mix_conv, w_bf, x2d, norm_g, tm):
    m, d = x2d.shape
    da, dc = mix_att.shape[1], mix_conv.shape[1]
    assert w_bf.shape[0] == da + dc and da % dc == 0
    return pl.pallas_call(
        _outproj_kernel,
        grid=(m // tm,),
        in_specs=[pl.BlockSpec((tm, da), lambda i: (i, 0)),
                  pl.BlockSpec((tm, dc), lambda i: (i, 0)),
                  pl.BlockSpec((da, d), lambda i: (0, 0)),
                  pl.BlockSpec((dc, d), lambda i: (da // dc, 0)),
                  pl.BlockSpec((tm, d), lambda i: (i, 0)),
                  pl.BlockSpec((1, d), lambda i: (0, 0))],
        out_specs=pl.BlockSpec((tm, d), lambda i: (i, 0)),
        out_shape=jax.ShapeDtypeStruct((m, d), F32),
        compiler_params=pltpu.CompilerParams(
            dimension_semantics=("parallel",), vmem_limit_bytes=VMEM_LIMIT),
        name="outproj",
    )(mix_att, mix_conv, w_bf, w_bf, x2d, norm_g.reshape(1, d))


NEAR_ROWS = 512
FAR_STRIDE = 16


RING = 3


def _sample_heads(q_ref, k_ref, v_ref, g_ref, near_ref, far_ref, tab_ref, mult_ref, o_ref, rows_kv):
    t_new = q_ref.shape[0]
    hd = near_ref.shape[1]
    scale = HEAD_DIM ** -0.5
    pad_q = jnp.zeros((SUBLANES - t_new, hd), F32)
    pad_kv = jnp.zeros((hd - t_new, hd), F32)
    nt = (((1,), (1,)), ((), ()))
    n_far = far_ref.shape[0] // rows_kv
    mult = mult_ref[...]
    n_heads = rows_kv // 2

    def head_rows(ref, h, n, parity):
        return ref[pl.ds(2 * h + parity, n, stride=rows_kv), :].astype(BF16)

    def pad_bf(ref, h, pad):
        return jnp.concatenate([ref[:, h * hd:(h + 1) * hd], pad], axis=0).astype(BF16)

    scores = []
    for h in range(n_heads):
        q8 = pad_bf(q_ref, h, pad_q)
        scores.append(jnp.concatenate(
            [lax.dot_general(q8, head_rows(near_ref, h, NEAR_ROWS, 0), nt, preferred_element_type=F32),
             lax.dot_general(q8, head_rows(far_ref, h, n_far, 0), nt, preferred_element_type=F32),
             lax.dot_general(q8, pad_bf(k_ref, h, pad_kv), nt, preferred_element_type=F32)], axis=1))
    probs = []
    for h, s in enumerate(scores):
        s = s * scale + tab_ref[h]
        m = jnp.max(s, axis=-1, keepdims=True)
        p = jnp.exp(s - m) * mult
        probs.append((p.astype(BF16), jnp.sum(p, axis=-1, keepdims=True)))
    for h, (pb, l) in enumerate(probs):
        acc = jnp.dot(pb[:, :NEAR_ROWS], head_rows(near_ref, h, NEAR_ROWS, 1),
                      preferred_element_type=F32)
        acc = acc + jnp.dot(pb[:, NEAR_ROWS:NEAR_ROWS + n_far], head_rows(far_ref, h, n_far, 1),
                            preferred_element_type=F32)
        acc = acc + jnp.dot(pb[:, NEAR_ROWS + n_far:], pad_bf(v_ref, h, pad_kv),
                            preferred_element_type=F32)
        cols = slice(h * hd, (h + 1) * hd)
        o_ref[:, cols] = (acc / l)[:t_new] * _silu(g_ref[:, cols])


def _cache_stream_step(k, n_steps, seq0, n_chunks, cache_any, kvn_any, out_any, buf_ref, far_ref,
                       sem_in, sem_out, sem_tail, shift_sl, rows_kv, attend):
    chunk_sl = buf_ref.shape[1]
    far_per_chunk = NEAR_ROWS // FAR_STRIDE
    seq_sl = n_chunks * chunk_sl

    def chunk_in(j):
        return pltpu.make_async_copy(
            cache_any.at[seq0 + j // n_chunks, pl.ds((j % n_chunks) * chunk_sl, chunk_sl)],
            buf_ref.at[j % RING], sem_in.at[j % RING])

    def first_out(j):
        return pltpu.make_async_copy(
            buf_ref.at[j % RING, pl.ds(shift_sl, chunk_sl - shift_sl)],
            out_any.at[seq0 + j // n_chunks, pl.ds(0, chunk_sl - shift_sl)], sem_out.at[j % RING])

    def later_out(j):
        return pltpu.make_async_copy(
            buf_ref.at[j % RING],
            out_any.at[seq0 + j // n_chunks, pl.ds((j % n_chunks) * chunk_sl - shift_sl, chunk_sl)],
            sem_out.at[j % RING])

    def tail(j):
        return pltpu.make_async_copy(
            kvn_any.at[seq0 + j // n_chunks],
            out_any.at[seq0 + j // n_chunks, pl.ds(seq_sl - shift_sl, shift_sl)], sem_tail.at[0])

    def on_chunk(j, first, later):
        pl.when(j % n_chunks == 0)(first)
        pl.when(j % n_chunks != 0)(later)

    @pl.when(k == 0)
    def _():
        for j in range(RING - 1):
            chunk_in(jnp.int32(j)).start()

    c = k % n_chunks
    slot = k % RING
    chunk_in(k).wait()
    on_chunk(k, lambda: first_out(k).start(), lambda: later_out(k).start())
    near_ref = buf_ref.at[slot]

    for grp in range(far_per_chunk):
        dst = pl.multiple_of((c * far_per_chunk + grp) * shift_sl, SUBLANES)
        src = grp * FAR_STRIDE * rows_kv
        far_ref[pl.ds(dst, shift_sl), :] = near_ref[src:src + shift_sl, :]

    @pl.when(c == n_chunks - 1)
    def _():
        tail(k).start()
        attend(near_ref)
        tail(k).wait()

    @pl.when(k >= 1)
    def _():
        on_chunk(k - 1, lambda: first_out(k - 1).wait(), lambda: later_out(k - 1).wait())

    @pl.when(k + RING - 1 < n_steps)
    def _():
        chunk_in(k + RING - 1).start()

    @pl.when(k == n_steps - 1)
    def _():
        on_chunk(k, lambda: first_out(k).wait(), lambda: later_out(k).wait())


class _Stream(NamedTuple):
    zs3: jax.Array
    kvn_rows: jax.Array
    cache2: jax.Array
    tab: jax.Array
    mult: jax.Array
    d_attn: int
    rows_kv: int
    n_chunks: int


def _stream_operands(stream, seq0, n_seq, seq_of):
    t_new = stream.zs3.shape[1]
    hd = stream.cache2.shape[2]
    chunk_sl = stream.cache2.shape[1] // stream.n_chunks
    n_far = stream.cache2.shape[1] // stream.rows_kv // FAR_STRIDE * t_new

    def local(*g):
        return jnp.minimum(seq_of(*g), n_seq - 1)

    def zcol(j):
        return pl.BlockSpec((None, t_new, stream.d_attn), lambda *g: (seq0 + local(*g), 0, j))

    in_specs = [zcol(0), zcol(1), zcol(2), zcol(3),
                pl.BlockSpec(stream.tab.shape, lambda *g: (0, 0, 0)),
                pl.BlockSpec(stream.mult.shape, lambda *g: (0, 0)),
                pl.BlockSpec(memory_space=pl.ANY),
                pl.BlockSpec(memory_space=pl.ANY)]
    out_specs = [pl.BlockSpec((None, t_new, stream.d_attn), lambda *g: (local(*g), 0, 0)),
                 pl.BlockSpec(memory_space=pl.ANY)]
    out_shapes = [jax.ShapeDtypeStruct((n_seq, t_new, stream.d_attn), F32),
                  jax.ShapeDtypeStruct(stream.cache2.shape, stream.cache2.dtype)]
    scratch = [pltpu.VMEM((RING, chunk_sl, hd), F32),
               pltpu.VMEM((n_far * stream.rows_kv, hd), F32),
               pltpu.SemaphoreType.DMA((RING,)),
               pltpu.SemaphoreType.DMA((RING,)),
               pltpu.SemaphoreType.DMA((1,))]
    args = (stream.zs3,) * 4 + (stream.tab, stream.mult, stream.cache2, stream.kvn_rows)
    kw = dict(rows_kv=stream.rows_kv, seq0=seq0, n_chunks=stream.n_chunks,
              n_steps=n_seq * stream.n_chunks)
    return in_specs, out_specs, out_shapes, scratch, args, kw


def _stream_step_with_attention(k, stream_in, o_ref, out_any, scratch, *, rows_kv, seq0, n_chunks,
                                n_steps):
    q_ref, k_ref, v_ref, g_ref, tab_ref, mult_ref, cache_any, kvn_any = stream_in
    buf_ref, far_ref, sem_in, sem_out, sem_tail = scratch
    shift_sl = q_ref.shape[0] * rows_kv

    def attend(near_ref):
        _sample_heads(q_ref, k_ref, v_ref, g_ref, near_ref, far_ref, tab_ref, mult_ref, o_ref, rows_kv)

    @pl.when(k < n_steps)
    def _():
        _cache_stream_step(k, n_steps, seq0, n_chunks, cache_any, kvn_any, out_any, buf_ref, far_ref,
                           sem_in, sem_out, sem_tail, shift_sl, rows_kv, attend)


def _attn_sample_kernel(*refs, n_chunks, **stream_kw):
    k = pl.program_id(0) * n_chunks + pl.program_id(1)
    _stream_step_with_attention(k, refs[:8], refs[9], refs[10], refs[11:], n_chunks=n_chunks,
                                **stream_kw)


def _attn_sample(stream, seq0, n_seq, partial_cache):
    s_in, s_out, s_shapes, s_scratch, s_args, s_kw = _stream_operands(
        stream, seq0, n_seq, lambda i, c: i)
    return pl.pallas_call(
        functools.partial(_attn_sample_kernel, **s_kw),
        grid=(n_seq, stream.n_chunks),
        in_specs=s_in + [pl.BlockSpec(memory_space=pl.ANY)],
        out_specs=s_out,
        out_shape=s_shapes,
        scratch_shapes=s_scratch,
        input_output_aliases={len(s_in): 1},
        compiler_params=pltpu.CompilerParams(
            dimension_semantics=("arbitrary", "arbitrary"), vmem_limit_bytes=VMEM_LIMIT),
        name="attn_sample",
    )(*s_args, partial_cache)


def _prompt_bias_tables(rel_bias):
    nk = KEYS_PER_PATTERN
    qb = Q_BLOCK
    wrap = TABLE_LANES
    assert wrap >= 2 * qb + nk - 1
    m = np.arange(wrap)
    kdist = nk - np.where(m < qb + nk, m, m - wrap)
    valid = (kdist >= 0) & (kdist <= nk)
    tabs = []
    for _, dil in PATTERNS:
        bucket = _rel_bucket(jnp.asarray(np.clip(kdist, 0, nk) * dil, jnp.int32))
        vec = jnp.where(valid[:, None], rel_bias[bucket].astype(F32), NEG_INF).T
        tabs.append(jnp.broadcast_to(vec[:, None, :], (vec.shape[0], SUBLANES, wrap)))
    return jnp.stack(tabs)


def _pattern_count(dist, patterns):
    return sum(((dist % dil == 0) & (dist >= 0) & (dist <= window)).astype(np.int32)
               for window, dil in patterns)


def _sample_tables(rel_bias, t_new, past):
    def bias_at(dist):
        return rel_bias[_rel_bucket(jnp.asarray(dist, jnp.int32))].astype(F32)

    def masked(bias, count):
        return jnp.where(jnp.asarray(count > 0)[..., None], bias, NEG_INF)

    near_pats, far_pats = PATTERNS[:2], PATTERNS[2:]
    assert near_pats[-1][0] == NEAR_ROWS and far_pats[0][1] == FAR_STRIDE and t_new <= far_pats[0][1]
    desc = np.arange(NEAR_ROWS + t_new - 1, 0, -1)
    desc_cnt = _pattern_count(desc, near_pats)
    desc_tab = masked(bias_at(desc), desc_cnt)
    starts = [t_new - 1 - t for t in range(t_new)]
    near_tab = jnp.stack([desc_tab[s0:s0 + NEAR_ROWS] for s0 in starts])
    near_cnt = np.stack([desc_cnt[s0:s0 + NEAR_ROWS] for s0 in starts])
    groups = past // FAR_STRIDE
    far_dist = past - FAR_STRIDE * np.arange(groups)
    own = np.eye(t_new, dtype=bool)[:, None, :] & (_pattern_count(far_dist, far_pats) > 0)[None, :, None]
    far_tab = jnp.where(jnp.asarray(own)[..., None], bias_at(far_dist)[None, :, None, :], NEG_INF)
    far_tab = far_tab.reshape(t_new, groups * t_new, -1)
    far_cnt = np.ones((t_new, groups * t_new), np.int32)
    tj = np.arange(t_new)[:, None] - np.arange(HEAD_DIM)[None, :]
    new_cnt = np.where(np.arange(HEAD_DIM)[None, :] < t_new, _pattern_count(tj, PATTERNS), 0)
    new_tab = masked(bias_at(np.clip(tj, 0, None).reshape(-1)).reshape(t_new, HEAD_DIM, -1), new_cnt)
    tab = jnp.concatenate([near_tab, far_tab, new_tab], axis=1).transpose(2, 0, 1)
    cnt = np.concatenate([near_cnt, far_cnt, new_cnt], axis=1)
    pad = SUBLANES - t_new
    tab = jnp.pad(tab, ((0, 0), (0, pad), (0, 0)))
    mult = np.pad(np.maximum(cnt, 1), ((0, pad), (0, 0)), constant_values=1).astype(np.float32)
    return tab, jnp.asarray(mult)


def kernel(x_prompt, x_sample, cache_conv, cache_kv, rel_bias, norm_pre, w_in, conv_dw_w, conv_dw_b,
           conv_ln_g, conv_ln_b, conv_pw_w, conv_pw_b, w_out, norm_post):
    depth = w_in.shape[0]
    assert depth == 1
    bsz, seq, d_model = x_prompt.shape
    n_dec, t_new, _ = x_sample.shape
    n_heads = cache_kv.shape[4]
    d_attn = n_heads * HEAD_DIM
    d_conv = cache_conv.shape[-1]
    past = cache_kv.shape[2]
    assert past == MAX_WINDOW and seq >= MAX_WINDOW and t_new <= 4
    hist = CONV_WIDTH - 1

    w_in_bf = w_in[0].astype(BF16)
    w_out_bf = w_out[0].astype(BF16)
    pw_bf = conv_pw_w[0].astype(BF16)
    conv_args = (conv_dw_w[0], conv_dw_b[0], conv_ln_g[0], conv_ln_b[0], pw_bf, conv_pw_b[0])
    conv_col0 = 4 * d_attn

    xp2 = x_prompt.reshape(bsz * seq, d_model)
    xs2 = x_sample.reshape(n_dec * t_new, d_model)
    zs = _inproj(_prenorm(xs2, norm_pre[0], tm=n_dec * t_new), w_in_bf, tm=n_dec * t_new, tn=d_attn)
    zs3 = zs.reshape(n_dec, t_new, -1)

    def heads(col0):
        return zs3[:, :, col0:col0 + d_attn].reshape(n_dec, t_new, n_heads, HEAD_DIM)

    rows_kv = 2 * n_heads
    kvn_rows = jnp.stack([heads(d_attn), heads(2 * d_attn)], axis=3).reshape(
        n_dec, t_new * rows_kv, HEAD_DIM)
    cache2 = cache_kv[0].transpose(0, 1, 3, 2, 4).reshape(n_dec, past * rows_kv, HEAD_DIM)
    tab_s, mult_s = _sample_tables(rel_bias, t_new, past)
    assert past % NEAR_ROWS == 0 and t_new <= SUBLANES
    stream = _Stream(zs3, kvn_rows, cache2, tab_s, mult_s, d_attn, rows_kv, past // NEAR_ROWS)

    conv_row_blocks = 8
    n_in_conv = min(bsz * conv_row_blocks // stream.n_chunks, n_dec - 1)
    n_in_attn = min(bsz * n_heads // stream.n_chunks, n_dec - n_in_conv)
    n_in_proj = n_dec - n_in_attn - n_in_conv
    tm_p, tn_p = 2048, 512
    hp = _prenorm(xp2, norm_pre[0], tm=512)
    if 0 < n_in_proj * stream.n_chunks <= (bsz * seq // tm_p) * (w_in_bf.shape[1] // tn_p):
        zp, att_s0, part_cache = _inproj(hp, w_in_bf, tm_p, tn_p, stream, n_in_proj)
    else:
        zp = _inproj(hp, w_in_bf, tm_p, tn_p)
        n_in_proj = 0
        att_s0 = jnp.zeros((0, t_new, d_attn), F32)
        part_cache = jnp.zeros(cache2.shape, cache2.dtype)
    zp3 = zp.reshape(bsz, seq, -1)
    mix_att_p, att_s1, part_cache = _attn_prompt(zp3, _prompt_bias_tables(rel_bias), n_heads, stream,
                                                 n_in_proj, n_in_attn, part_cache)
    att_s0 = jnp.concatenate([att_s0, att_s1], axis=0)
    zero_prefix = jnp.zeros((bsz, hist, d_conv), F32)
    if n_in_conv > 0:
        mix_conv_p, new_conv_p, att_s2, part_cache = _conv_branch(
            zp3, zero_prefix, *conv_args, col0=conv_col0, row_blocks=conv_row_blocks,
            stream=stream, seq0=att_s0.shape[0], n_seq=n_in_conv, partial_cache=part_cache)
        att_s0 = jnp.concatenate([att_s0, att_s2], axis=0)
    else:
        mix_conv_p, new_conv_p = _conv_branch(zp3, zero_prefix, *conv_args, col0=conv_col0,
                                              row_blocks=conv_row_blocks)
    yp = _outproj(mix_att_p.reshape(bsz * seq, d_attn), mix_conv_p.reshape(bsz * seq, d_conv),
                  w_out_bf, xp2, norm_post[0], tm=512)
    win = min(MAX_WINDOW, seq)
    kv_rows_p = _kv_rows(zp, d_attn, tm=512).reshape(bsz, seq, n_heads, 2, HEAD_DIM)
    new_kv_p = kv_rows_p[:, seq - win:].transpose(0, 1, 3, 2, 4)[None]

    n_hosted = att_s0.shape[0]
    att_s, new_rows = att_s0, part_cache
    if n_hosted < n_dec:
        att_s1, new_rows = _attn_sample(stream, n_hosted, n_dec - n_hosted, part_cache)
        att_s = jnp.concatenate([att_s0, att_s1], axis=0)
    new_kv_s = new_rows.reshape(n_dec, past, n_heads, 2, HEAD_DIM).transpose(0, 1, 3, 2, 4)[None]
    mix_att_s = att_s.reshape(n_dec * t_new, d_attn).astype(BF16)
    mix_conv_s, new_conv_s = _conv_branch(zs3, cache_conv[0], *conv_args, col0=conv_col0)
    ys = _outproj(mix_att_s, mix_conv_s.reshape(n_dec * t_new, d_conv),
                  w_out_bf, xs2, norm_post[0], tm=n_dec * t_new)

    return (yp.reshape(bsz, seq, d_model), ys.reshape(n_dec, t_new, d_model),
            new_conv_p[None], new_kv_p, new_conv_s[None], new_kv_s)
```

```python
import functools
import math
from typing import NamedTuple

import jax
import jax.numpy as jnp
import numpy as np
from jax import lax
from jax.experimental import pallas as pl
from jax.experimental.pallas import tpu as pltpu

F32 = jnp.float32
BF16 = jnp.bfloat16

HEAD_DIM = 128
PATTERNS = ((128, 1), (512, 4), (2048, 16))
MAX_WINDOW = 2048
Q_BLOCK = 128
KEYS_PER_PATTERN = 128
TABLE_LANES = 384
CONV_WIDTH = 31
N_BUCKETS = 32
MAX_EXACT = 16
EPS = 1e-6
NEG_INF = -1e30
SUBLANES = 8

MIB = 1024 * 1024
VMEM_BYTES_V7X = 64 * MIB
VMEM_LIMIT = 3 * VMEM_BYTES_V7X // 4
VMEM_LIMIT_CARRIER = VMEM_BYTES_V7X - 4 * MIB


def _rel_bucket(dist):
    d = jnp.maximum(dist, 1).astype(F32)
    log_b = MAX_EXACT + (jnp.log(d / MAX_EXACT) / math.log(MAX_WINDOW / MAX_EXACT)
                         * (N_BUCKETS - MAX_EXACT)).astype(jnp.int32)
    log_b = jnp.minimum(log_b, N_BUCKETS - 1)
    return jnp.where(dist < MAX_EXACT, dist, log_b)


def _round_up(x, m):
    return -(-x // m) * m


def _silu(x):
    return x * jax.nn.sigmoid(x)


def _prenorm_kernel(x_ref, g_ref, h_ref):
    x = x_ref[...]
    ms = jnp.mean(x * x, axis=-1, keepdims=True)
    h_ref[...] = (x * lax.rsqrt(ms + EPS) * g_ref[...]).astype(h_ref.dtype)


def _prenorm(x2d, norm_g, tm):
    m, d = x2d.shape
    return pl.pallas_call(
        _prenorm_kernel,
        grid=(m // tm,),
        in_specs=[pl.BlockSpec((tm, d), lambda i: (i, 0)), pl.BlockSpec((1, d), lambda i: (0, 0))],
        out_specs=pl.BlockSpec((tm, d), lambda i: (i, 0)),
        out_shape=jax.ShapeDtypeStruct((m, d), BF16),
        compiler_params=pltpu.CompilerParams(
            dimension_semantics=("parallel",), vmem_limit_bytes=VMEM_LIMIT),
        name="prenorm",
    )(x2d, norm_g.reshape(1, d))


def _inproj_kernel(h_ref, w_ref, z_ref):
    z_ref[...] = jnp.dot(h_ref[...], w_ref[...], preferred_element_type=F32)


def _inproj_stream_kernel(h_ref, w_ref, *rest, n_col_tiles, **stream_kw):
    stream_in, (z_ref, so_ref, out_any), scratch = rest[:8], rest[8:11], rest[11:]
    _inproj_kernel(h_ref, w_ref, z_ref)
    k = pl.program_id(0) * n_col_tiles + pl.program_id(1)
    _stream_step_with_attention(k, stream_in, so_ref, out_any, scratch, **stream_kw)


def _inproj(h2d, w_bf, tm, tn, stream=None, n_seq=0):
    m, d = h2d.shape
    n = w_bf.shape[1]
    grid = (m // tm, n // tn)
    in_specs = [pl.BlockSpec((tm, d), lambda i, j: (i, 0)),
                pl.BlockSpec((d, tn), lambda i, j: (0, j))]
    z_spec = pl.BlockSpec((tm, tn), lambda i, j: (i, j))
    z_shape = jax.ShapeDtypeStruct((m, n), F32)
    if stream is None:
        return pl.pallas_call(
            _inproj_kernel, grid=grid, in_specs=in_specs, out_specs=z_spec, out_shape=z_shape,
            compiler_params=pltpu.CompilerParams(
                dimension_semantics=("parallel", "arbitrary"), vmem_limit_bytes=VMEM_LIMIT),
            name="inproj",
        )(h2d, w_bf)
    assert n_seq * stream.n_chunks <= grid[0] * grid[1]
    s_in, s_out, s_shapes, s_scratch, s_args, s_kw = _stream_operands(
        stream, 0, n_seq, lambda i, j: (i * grid[1] + j) // stream.n_chunks)
    return pl.pallas_call(
        functools.partial(_inproj_stream_kernel, n_col_tiles=grid[1], **s_kw),
        grid=grid, in_specs=in_specs + s_in, out_specs=[z_spec] + s_out,
        out_shape=[z_shape] + s_shapes, scratch_shapes=s_scratch,
        compiler_params=pltpu.CompilerParams(
            dimension_semantics=("arbitrary", "arbitrary"), vmem_limit_bytes=VMEM_LIMIT_CARRIER),
        name="inproj",
    )(h2d, w_bf, *s_args)


def _kv_rows_kernel(k_ref, v_ref, o_ref):
    tm = k_ref.shape[0]
    hd = o_ref.shape[1]
    n_heads = k_ref.shape[1] // hd
    for h in range(n_heads):
        o_ref[pl.ds(2 * h, tm, stride=2 * n_heads), :] = k_ref[:, h * hd:(h + 1) * hd]
        o_ref[pl.ds(2 * h + 1, tm, stride=2 * n_heads), :] = v_ref[:, h * hd:(h + 1) * hd]


def _kv_rows(z, d_attn, tm):
    m = z.shape[0]
    rows_kv = 2 * d_attn // HEAD_DIM
    return pl.pallas_call(
        _kv_rows_kernel,
        grid=(m // tm,),
        in_specs=[pl.BlockSpec((tm, d_attn), lambda i: (i, 1)),
                  pl.BlockSpec((tm, d_attn), lambda i: (i, 2))],
        out_specs=pl.BlockSpec((tm * rows_kv, HEAD_DIM), lambda i: (i, 0)),
        out_shape=jax.ShapeDtypeStruct((m * rows_kv, HEAD_DIM), F32),
        compiler_params=pltpu.CompilerParams(
            dimension_semantics=("parallel",), vmem_limit_bytes=VMEM_LIMIT),
        name="kv_rows",
    )(z, z)


def _attn_block(qb, kw, vw, tab, scale):
    s = lax.dot_general(qb.astype(BF16), kw.astype(BF16), (((1,), (1,)), ((), ())),
                        preferred_element_type=F32)
    s = s * scale + tab
    m = jnp.max(s, axis=-1, keepdims=True)
    p = jnp.exp(s - m).astype(BF16)
    v_ones = jnp.concatenate([vw.astype(BF16), jnp.ones(vw.shape, BF16)], axis=1)
    acc_l = jnp.dot(p, v_ones, preferred_element_type=F32)
    d = vw.shape[1]
    return acc_l[:, :d], m, acc_l[:, d:]


def _attn_prompt_kernel(q_ref, k_ref, v_ref, g_ref, vec_ref, o_ref, acc_ref, m_ref, l_ref, ph_ref,
                        tab_ref):
    seq = q_ref.shape[0]
    scale = HEAD_DIM ** -0.5
    qb_rows = Q_BLOCK
    nk = KEYS_PER_PATTERN

    for p in range(len(PATTERNS)):
        base = jnp.broadcast_to(vec_ref[p][0:1, :], (qb_rows, vec_ref.shape[-1]))
        tab_ref[p] = pltpu.roll(base, 0, 1, stride=1, stride_axis=0)[:, :qb_rows + nk]

    sub = PATTERNS[1][1]
    assert all(dil == 1 or dil % sub == 0 for _, dil in PATTERNS)
    sub_len = seq // sub
    srcs = (q_ref, k_ref, v_ref)
    for a, ref in enumerate(srcs):
        for s in range(sub):
            ph_ref[a, s * sub_len:(s + 1) * sub_len, :] = ref[pl.ds(s, sub_len, stride=sub), :]

    def rows(a, phase, dil, start, size):
        if dil == 1:
            return srcs[a][pl.ds(start, size), :]
        step = dil // sub
        base = (phase % sub) * sub_len + phase // sub + step * start
        if step == 1:
            return ph_ref[a, pl.ds(base, size), :]
        return ph_ref[a, pl.ds(base, size, stride=step), :]

    def put(p, start, stride, acc, m, l):
        lanes = acc.shape[-1]
        if stride == 1:
            idx = pl.ds(start, qb_rows)
        else:
            idx = pl.ds(start, qb_rows, stride=stride)
        acc_ref[p, idx, :] = acc
        m_ref[p, idx, :] = jnp.broadcast_to(m, (qb_rows, lanes))
        l_ref[p, idx, :] = l

    def first_block(p, phase, dil):
        tab = tab_ref[p][:, nk:]
        qb = rows(0, phase, dil, 0, qb_rows)
        kw = rows(1, phase, dil, 0, qb_rows)
        vw = rows(2, phase, dil, 0, qb_rows)
        put(p, phase, dil, *_attn_block(qb, kw, vw, tab, scale))

    def later_block(p, phase, dil, n):
        tab = tab_ref[p]
        qb = rows(0, phase, dil, qb_rows * n, qb_rows)
        kw = rows(1, phase, dil, qb_rows * n - nk, qb_rows + nk)
        vw = rows(2, phase, dil, qb_rows * n - nk, qb_rows + nk)
        put(p, phase + dil * qb_rows * n, dil, *_attn_block(qb, kw, vw, tab, scale))

    for p, (window, dil) in enumerate(PATTERNS):
        n_blocks = seq // dil // qb_rows
        for phase in range(dil):
            first_block(p, phase, dil)
            for n in range(1, n_blocks):
                later_block(p, phase, dil, n)

    chunk = 256

    def combine(c, carry):
        sl = pl.ds(pl.multiple_of(c * chunk, chunk), chunk)
        m0, m1, m2 = m_ref[0, sl, :], m_ref[1, sl, :], m_ref[2, sl, :]
        mm = jnp.maximum(jnp.maximum(m0, m1), m2)
        e0, e1, e2 = jnp.exp(m0 - mm), jnp.exp(m1 - mm), jnp.exp(m2 - mm)
        num = e0 * acc_ref[0, sl, :] + e1 * acc_ref[1, sl, :] + e2 * acc_ref[2, sl, :]
        den = e0 * l_ref[0, sl, :] + e1 * l_ref[1, sl, :] + e2 * l_ref[2, sl, :]
        o_ref[sl, :] = (num / den * _silu(g_ref[sl, :])).astype(o_ref.dtype)
        return carry
    lax.fori_loop(0, seq // chunk, combine, 0)


def _attn_prompt_stream_kernel(q_ref, k_ref, v_ref, g_ref, tab_ref, *rest, n_heads, **stream_kw):
    stream_in, (o_ref, so_ref, out_any), scratch = rest[:8], rest[9:12], rest[12:]
    _attn_prompt_kernel(q_ref, k_ref, v_ref, g_ref, tab_ref, o_ref, *scratch[:5])
    k = pl.program_id(0) * n_heads + pl.program_id(1)
    _stream_step_with_attention(k, stream_in, so_ref, out_any, scratch[5:], **stream_kw)


def _attn_prompt(z3, tabs, n_heads, stream, seq0, n_seq, partial_cache):
    b, seq, _ = z3.shape
    hd = HEAD_DIM
    assert 0 < n_seq * stream.n_chunks <= b * n_heads

    def col(off):
        return pl.BlockSpec((None, seq, hd), lambda i, h: (i, 0, off + h))

    def seq_of(i, h):
        return (i * n_heads + h) // stream.n_chunks

    s_in, s_out, s_shapes, s_scratch, s_args, s_kw = _stream_operands(stream, seq0, n_seq, seq_of)
    in_specs = [col(0), col(n_heads), col(2 * n_heads), col(3 * n_heads),
                pl.BlockSpec((len(PATTERNS), None, SUBLANES, TABLE_LANES),
                             lambda i, h: (0, h, 0, 0))] + s_in + [pl.BlockSpec(memory_space=pl.ANY)]
    return pl.pallas_call(
        functools.partial(_attn_prompt_stream_kernel, n_heads=n_heads, **s_kw),
        grid=(b, n_heads),
        in_specs=in_specs,
        out_specs=[pl.BlockSpec((None, seq, hd), lambda i, h: (i, 0, h))] + s_out,
        out_shape=[jax.ShapeDtypeStruct((b, seq, n_heads * hd), BF16)] + s_shapes,
        scratch_shapes=([pltpu.VMEM((len(PATTERNS), seq, hd), F32)] * 4
                        + [pltpu.VMEM((len(PATTERNS), Q_BLOCK, Q_BLOCK + KEYS_PER_PATTERN), F32)]
                        + s_scratch),
        input_output_aliases={len(in_specs) - 1: 2},
        compiler_params=pltpu.CompilerParams(
            dimension_semantics=("arbitrary", "arbitrary"), vmem_limit_bytes=VMEM_LIMIT_CARRIER),
        name="attn_prompt",
    )(z3, z3, z3, z3, tabs, *s_args, partial_cache)


def _conv_kernel(ca_ref, cb_ref, gc_ref, pre_ref, dww_ref, dwb_ref, lng_ref, lnb_ref,
                 pww_ref, pwb_ref, o_ref, newc_ref, upad_ref, *, chunk):
    def one(s, carry):
        _conv_one_seq(ca_ref.at[s], cb_ref.at[s], gc_ref.at[s], pre_ref.at[s], dww_ref, dwb_ref,
                      lng_ref, lnb_ref, pww_ref, pwb_ref, o_ref.at[s], newc_ref.at[s], upad_ref,
                      chunk=chunk)
        return carry
    if ca_ref.shape[0] == 1:
        one(0, 0)
    else:
        lax.fori_loop(0, ca_ref.shape[0], one, 0)


def _conv_one_seq(ca_ref, cb_ref, gc_ref, pre_ref, dww_ref, dwb_ref, lng_ref, lnb_ref,
                  pww_ref, pwb_ref, o_ref, newc_ref, upad_ref, *, chunk):
    t_len = ca_ref.shape[0]
    rows_step = o_ref.shape[0]
    hist = CONV_WIDTH - 1
    rb = pl.program_id(1)

    @pl.when(rb == 0)
    def _():
        u = ca_ref[...] * jax.nn.sigmoid(cb_ref[...])
        upad_ref[0:hist, :] = pre_ref[...]
        upad_ref[hist:hist + t_len, :] = u
        n_pad = upad_ref.shape[0] - (hist + t_len)
        upad_ref[hist + t_len:, :] = jnp.zeros((n_pad, ca_ref.shape[1]), F32)
        newc_ref[...] = upad_ref[t_len:t_len + hist, :]

    win_rows = upad_ref.shape[0] - t_len + chunk

    def body(c, carry):
        l0 = c * chunk
        r0 = l0 if rows_step == t_len else rb * rows_step + l0
        if chunk % SUBLANES == 0:
            l0, r0 = pl.multiple_of(l0, SUBLANES), pl.multiple_of(r0, SUBLANES)
        win = upad_ref[pl.ds(r0, win_rows), :]
        y = jnp.zeros((chunk, ca_ref.shape[1]), F32) + dwb_ref[...]
        for s in range(SUBLANES):
            shifted = win if s == 0 else pltpu.roll(win, win_rows - s, 0)
            for a in range(-(-CONV_WIDTH // SUBLANES)):
                w = SUBLANES * a + s
                if w < CONV_WIDTH:
                    y = y + shifted[SUBLANES * a:SUBLANES * a + chunk] * dww_ref[w:w + 1, :]
        mu = jnp.mean(y, axis=-1, keepdims=True)
        var = jnp.mean(jnp.square(y - mu), axis=-1, keepdims=True)
        yn = (y - mu) * lax.rsqrt(var + EPS) * lng_ref[...] + lnb_ref[...]
        c_act = _silu(yn).astype(BF16)
        proj = jnp.dot(c_act, pww_ref[...], preferred_element_type=F32) + pwb_ref[...]
        o_ref[pl.ds(l0, chunk), :] = (proj * _silu(gc_ref[pl.ds(r0, chunk), :])).astype(o_ref.dtype)
        return carry
    if rows_step == chunk:
        body(0, 0)
    else:
        lax.fori_loop(0, rows_step // chunk, body, 0)


def _conv_stream_kernel(*refs, row_blocks, **stream_kw):
    chunk = stream_kw.pop("chunk")
    _conv_kernel(*refs[:10], refs[19], refs[20], refs[23], chunk=chunk)
    k = pl.program_id(0) * row_blocks + pl.program_id(1)
    _stream_step_with_attention(k, refs[10:18], refs[21], refs[22], refs[24:], **stream_kw)


def _conv_branch(z3, prefix, dw_w, dw_b, ln_g, ln_b, pw_w_bf, pw_b, col0, row_blocks=1,
                 stream=None, seq0=0, n_seq=0, partial_cache=None):
    n_all, t_len, _ = z3.shape
    c = prefix.shape[-1]
    hist = CONV_WIDTH - 1
    rows_step = t_len // row_blocks
    chunk = min(rows_step, 64)
    cblk = col0 // c
    group = n_all if (stream is None and t_len * c * 4 <= 64 * 1024) else 1
    n = n_all // group

    def zc(j):
        return pl.BlockSpec((group, t_len, c), lambda i, r: (i, 0, cblk + j))

    def vec():
        return pl.BlockSpec((1, c), lambda i, r: (0, 0))

    in_specs = [zc(0), zc(1), zc(2),
                pl.BlockSpec((group, hist, c), lambda i, r: (i, 0, 0)),
                pl.BlockSpec((CONV_WIDTH, c), lambda i, r: (0, 0)),
                vec(), vec(), vec(),
                pl.BlockSpec((c, c), lambda i, r: (0, 0)),
                vec()]
    out_specs = [pl.BlockSpec((group, rows_step, c), lambda i, r: (i, r, 0)),
                 pl.BlockSpec((group, hist, c), lambda i, r: (i, 0, 0))]
    out_shape = [jax.ShapeDtypeStruct((n_all, t_len, c), BF16),
                 jax.ShapeDtypeStruct((n_all, hist, c), F32)]
    scratch = [pltpu.VMEM((t_len - chunk + _round_up(chunk + CONV_WIDTH + 1, SUBLANES), c), F32)]
    args = (z3, z3, z3, prefix, dw_w, dw_b.reshape(1, c), ln_g.reshape(1, c), ln_b.reshape(1, c),
            pw_w_bf, pw_b.reshape(1, c))
    if stream is None:
        return pl.pallas_call(
            functools.partial(_conv_kernel, chunk=chunk),
            grid=(n, row_blocks), in_specs=in_specs, out_specs=out_specs, out_shape=out_shape,
            scratch_shapes=scratch,
            compiler_params=pltpu.CompilerParams(
                dimension_semantics=("parallel", "arbitrary"), vmem_limit_bytes=VMEM_LIMIT),
            name="conv_branch",
        )(*args)
    assert 0 < n_seq * stream.n_chunks <= n * row_blocks
    s_in, s_out, s_shapes, s_scratch, s_args, s_kw = _stream_operands(
        stream, seq0, n_seq, lambda i, r: (i * row_blocks + r) // stream.n_chunks)
    in_specs = in_specs + s_in + [pl.BlockSpec(memory_space=pl.ANY)]
    return pl.pallas_call(
        functools.partial(_conv_stream_kernel, row_blocks=row_blocks, chunk=chunk, **s_kw),
        grid=(n, row_blocks), in_specs=in_specs, out_specs=out_specs + s_out,
        out_shape=out_shape + s_shapes, scratch_shapes=scratch + s_scratch,
        input_output_aliases={len(in_specs) - 1: 3},
        compiler_params=pltpu.CompilerParams(
            dimension_semantics=("arbitrary", "arbitrary"), vmem_limit_bytes=VMEM_LIMIT_CARRIER),
        name="conv_branch",
    )(*args, *s_args, partial_cache)


def _outproj_kernel(ma_ref, mc_ref, wa_ref, wc_ref, x_ref, g_ref, y_ref):
    y = jnp.dot(ma_ref[...], wa_ref[...], preferred_element_type=F32)
    y = y + jnp.dot(mc_ref[...], wc_ref[...], preferred_element_type=F32)
    ms = jnp.mean(y * y, axis=-1, keepdims=True)
    y_ref[...] = x_ref[...] + y * lax.rsqrt(ms + EPS) * g_ref[...]


def _outproj(mix_att, mix_conv, w_bf, x2d, norm_g, tm):
    m, d = x2d.shape
    da, dc = mix_att.shape[1], mix_conv.shape[1]
    assert w_bf.shape[0] == da + dc and da % dc == 0
    return pl.pallas_call(
        _outproj_kernel,
        grid=(m // tm,),
        in_specs=[pl.BlockSpec((tm, da), lambda i: (i, 0)),
                  pl.BlockSpec((tm, dc), lambda i: (i, 0)),
                  pl.BlockSpec((da, d), lambda i: (0, 0)),
                  pl.BlockSpec((dc, d), lambda i: (da // dc, 0)),
                  pl.BlockSpec((tm, d), lambda i: (i, 0)),
                  pl.BlockSpec((1, d), lambda i: (0, 0))],
        out_specs=pl.BlockSpec((tm, d), lambda i: (i, 0)),
        out_shape=jax.ShapeDtypeStruct((m, d), F32),
        compiler_params=pltpu.CompilerParams(
            dimension_semantics=("parallel",), vmem_limit_bytes=VMEM_LIMIT),
        name="outproj",
    )(mix_att, mix_conv, w_bf, w_bf, x2d, norm_g.reshape(1, d))


NEAR_ROWS = 512
FAR_STRIDE = 16


RING = 3


def _sample_heads(q_ref, k_ref, v_ref, g_ref, near_ref, far_ref, tab_ref, mult_ref, o_ref, rows_kv):
    t_new = q_ref.shape[0]
    hd = near_ref.shape[1]
    scale = HEAD_DIM ** -0.5
    pad_q = jnp.zeros((SUBLANES - t_new, hd), F32)
    pad_kv = jnp.zeros((hd - t_new, hd), F32)
    nt = (((1,), (1,)), ((), ()))
    n_far = far_ref.shape[1]
    mult = mult_ref[...]
    n_heads = rows_kv // 2

    def head_rows(ref, h, n, parity):
        return ref[pl.ds(2 * h + parity, n, stride=rows_kv), :].astype(BF16)

    def pad_bf(ref, h, pad):
        return jnp.concatenate([ref[:, h * hd:(h + 1) * hd], pad], axis=0).astype(BF16)

    scores = []
    for h in range(n_heads):
        q8 = pad_bf(q_ref, h, pad_q)
        scores.append(jnp.concatenate(
            [lax.dot_general(q8, head_rows(near_ref, h, NEAR_ROWS, 0), nt, preferred_element_type=F32),
             lax.dot_general(q8, far_ref[2 * h], nt, preferred_element_type=F32),
             lax.dot_general(q8, pad_bf(k_ref, h, pad_kv), nt, preferred_element_type=F32)], axis=1))
    probs = []
    for h, s in enumerate(scores):
        s = s * scale + tab_ref[h]
        m = jnp.max(s, axis=-1, keepdims=True)
        p = jnp.exp(s - m) * mult
        probs.append((p.astype(BF16), jnp.sum(p, axis=-1, keepdims=True)))
    for h, (pb, l) in enumerate(probs):
        acc = jnp.dot(pb[:, :NEAR_ROWS], head_rows(near_ref, h, NEAR_ROWS, 1),
                      preferred_element_type=F32)
        acc = acc + jnp.dot(pb[:, NEAR_ROWS:NEAR_ROWS + n_far], far_ref[2 * h + 1],
                            preferred_element_type=F32)
        acc = acc + jnp.dot(pb[:, NEAR_ROWS + n_far:], pad_bf(v_ref, h, pad_kv),
                            preferred_element_type=F32)
        cols = slice(h * hd, (h + 1) * hd)
        o_ref[:, cols] = (acc / l)[:t_new] * _silu(g_ref[:, cols])


def _cache_stream_step(k, n_steps, seq0, n_chunks, cache_any, kvn_any, out_any, buf_ref, far_ref,
                       stash_ref, sem_in, sem_out, sem_tail, shift_sl, rows_kv, attend):
    chunk_sl = buf_ref.shape[1]
    far_per_chunk = NEAR_ROWS // FAR_STRIDE
    seq_sl = n_chunks * chunk_sl

    def chunk_in(j):
        return pltpu.make_async_copy(
            cache_any.at[seq0 + j // n_chunks, pl.ds((j % n_chunks) * chunk_sl, chunk_sl)],
            buf_ref.at[j % RING], sem_in.at[j % RING])

    def first_out(j):
        return pltpu.make_async_copy(
            buf_ref.at[j % RING, pl.ds(shift_sl, chunk_sl - shift_sl)],
            out_any.at[seq0 + j // n_chunks, pl.ds(0, chunk_sl - shift_sl)], sem_out.at[j % RING])

    def later_out(j):
        return pltpu.make_async_copy(
            buf_ref.at[j % RING],
            out_any.at[seq0 + j // n_chunks, pl.ds((j % n_chunks) * chunk_sl - shift_sl, chunk_sl)],
            sem_out.at[j % RING])

    def tail(j):
        return pltpu.make_async_copy(
            kvn_any.at[seq0 + j // n_chunks],
            out_any.at[seq0 + j // n_chunks, pl.ds(seq_sl - shift_sl, shift_sl)], sem_tail.at[0])

    def on_chunk(j, first, later):
        pl.when(j % n_chunks == 0)(first)
        pl.when(j % n_chunks != 0)(later)

    @pl.when(k == 0)
    def _():
        for j in range(RING - 1):
            chunk_in(jnp.int32(j)).start()

    c = k % n_chunks
    slot = k % RING
    chunk_in(k).wait()
    on_chunk(k, lambda: first_out(k).start(), lambda: later_out(k).start())
    near_ref = buf_ref.at[slot]

    for grp in range(far_per_chunk):
        src = grp * FAR_STRIDE * rows_kv
        stash_ref[grp * shift_sl:(grp + 1) * shift_sl, :] = near_ref[src:src + shift_sl, :]
    keys_chunk = stash_ref.shape[0] // rows_kv
    key0 = pl.multiple_of(c * keys_chunk, keys_chunk)
    for j in range(rows_kv):
        far_ref[j, pl.ds(key0, keys_chunk), :] = (
            stash_ref[pl.ds(j, keys_chunk, stride=rows_kv), :].astype(far_ref.dtype))

    @pl.when(c == n_chunks - 1)
    def _():
        tail(k).start()
        attend(near_ref)
        tail(k).wait()

    @pl.when(k >= 1)
    def _():
        on_chunk(k - 1, lambda: first_out(k - 1).wait(), lambda: later_out(k - 1).wait())

    @pl.when(k + RING - 1 < n_steps)
    def _():
        chunk_in(k + RING - 1).start()

    @pl.when(k == n_steps - 1)
    def _():
        on_chunk(k, lambda: first_out(k).wait(), lambda: later_out(k).wait())


class _Stream(NamedTuple):
    zs3: jax.Array
    kvn_rows: jax.Array
    cache2: jax.Array
    tab: jax.Array
    mult: jax.Array
    d_attn: int
    rows_kv: int
    n_chunks: int


def _stream_operands(stream, seq0, n_seq, seq_of):
    t_new = stream.zs3.shape[1]
    hd = stream.cache2.shape[2]
    chunk_sl = stream.cache2.shape[1] // stream.n_chunks
    n_far = stream.cache2.shape[1] // stream.rows_kv // FAR_STRIDE * t_new

    def local(*g):
        return jnp.minimum(seq_of(*g), n_seq - 1)

    def zcol(j):
        return pl.BlockSpec((None, t_new, stream.d_attn), lambda *g: (seq0 + local(*g), 0, j))

    in_specs = [zcol(0), zcol(1), zcol(2), zcol(3),
                pl.BlockSpec(stream.tab.shape, lambda *g: (0, 0, 0)),
                pl.BlockSpec(stream.mult.shape, lambda *g: (0, 0)),
                pl.BlockSpec(memory_space=pl.ANY),
                pl.BlockSpec(memory_space=pl.ANY)]
    out_specs = [pl.BlockSpec((None, t_new, stream.d_attn), lambda *g: (local(*g), 0, 0)),
                 pl.BlockSpec(memory_space=pl.ANY)]
    out_shapes = [jax.ShapeDtypeStruct((n_seq, t_new, stream.d_attn), F32),
                  jax.ShapeDtypeStruct(stream.cache2.shape, stream.cache2.dtype)]
    scratch = [pltpu.VMEM((RING, chunk_sl, hd), F32),
               pltpu.VMEM((stream.rows_kv, n_far, hd), BF16),
               pltpu.VMEM((NEAR_ROWS // FAR_STRIDE * t_new * stream.rows_kv, hd), F32),
               pltpu.SemaphoreType.DMA((RING,)),
               pltpu.SemaphoreType.DMA((RING,)),
               pltpu.SemaphoreType.DMA((1,))]
    args = (stream.zs3,) * 4 + (stream.tab, stream.mult, stream.cache2, stream.kvn_rows)
    kw = dict(rows_kv=stream.rows_kv, seq0=seq0, n_chunks=stream.n_chunks,
              n_steps=n_seq * stream.n_chunks)
    return in_specs, out_specs, out_shapes, scratch, args, kw


def _stream_step_with_attention(k, stream_in, o_ref, out_any, scratch, *, rows_kv, seq0, n_chunks,
                                n_steps):
    q_ref, k_ref, v_ref, g_ref, tab_ref, mult_ref, cache_any, kvn_any = stream_in
    buf_ref, far_ref, stash_ref, sem_in, sem_out, sem_tail = scratch
    shift_sl = q_ref.shape[0] * rows_kv

    def attend(near_ref):
        _sample_heads(q_ref, k_ref, v_ref, g_ref, near_ref, far_ref, tab_ref, mult_ref, o_ref, rows_kv)

    @pl.when(k < n_steps)
    def _():
        _cache_stream_step(k, n_steps, seq0, n_chunks, cache_any, kvn_any, out_any, buf_ref, far_ref,
                           stash_ref, sem_in, sem_out, sem_tail, shift_sl, rows_kv, attend)


def _attn_sample_kernel(*refs, n_chunks, **stream_kw):
    k = pl.program_id(0) * n_chunks + pl.program_id(1)
    _stream_step_with_attention(k, refs[:8], refs[9], refs[10], refs[11:], n_chunks=n_chunks,
                                **stream_kw)


def _attn_sample(stream, seq0, n_seq, partial_cache):
    s_in, s_out, s_shapes, s_scratch, s_args, s_kw = _stream_operands(
        stream, seq0, n_seq, lambda i, c: i)
    return pl.pallas_call(
        functools.partial(_attn_sample_kernel, **s_kw),
        grid=(n_seq, stream.n_chunks),
        in_specs=s_in + [pl.BlockSpec(memory_space=pl.ANY)],
        out_specs=s_out,
        out_shape=s_shapes,
        scratch_shapes=s_scratch,
        input_output_aliases={len(s_in): 1},
        compiler_params=pltpu.CompilerParams(
            dimension_semantics=("arbitrary", "arbitrary"), vmem_limit_bytes=VMEM_LIMIT),
        name="attn_sample",
    )(*s_args, partial_cache)


def _prompt_bias_tables(rel_bias):
    nk = KEYS_PER_PATTERN
    qb = Q_BLOCK
    wrap = TABLE_LANES
    assert wrap >= 2 * qb + nk - 1
    m = np.arange(wrap)
    kdist = nk - np.where(m < qb + nk, m, m - wrap)
    valid = (kdist >= 0) & (kdist <= nk)
    tabs = []
    for _, dil in PATTERNS:
        bucket = _rel_bucket(jnp.asarray(np.clip(kdist, 0, nk) * dil, jnp.int32))
        vec = jnp.where(valid[:, None], rel_bias[bucket].astype(F32), NEG_INF).T
        tabs.append(jnp.broadcast_to(vec[:, None, :], (vec.shape[0], SUBLANES, wrap)))
    return jnp.stack(tabs)


def _pattern_count(dist, patterns):
    return sum(((dist % dil == 0) & (dist >= 0) & (dist <= window)).astype(np.int32)
               for window, dil in patterns)


def _sample_tables(rel_bias, t_new, past):
    def bias_at(dist):
        return rel_bias[_rel_bucket(jnp.asarray(dist, jnp.int32))].astype(F32)

    def masked(bias, count):
        return jnp.where(jnp.asarray(count > 0)[..., None], bias, NEG_INF)

    near_pats, far_pats = PATTERNS[:2], PATTERNS[2:]
    assert near_pats[-1][0] == NEAR_ROWS and far_pats[0][1] == FAR_STRIDE and t_new <= far_pats[0][1]
    desc = np.arange(NEAR_ROWS + t_new - 1, 0, -1)
    desc_cnt = _pattern_count(desc, near_pats)
    desc_tab = masked(bias_at(desc), desc_cnt)
    starts = [t_new - 1 - t for t in range(t_new)]
    near_tab = jnp.stack([desc_tab[s0:s0 + NEAR_ROWS] for s0 in starts])
    near_cnt = np.stack([desc_cnt[s0:s0 + NEAR_ROWS] for s0 in starts])
    groups = past // FAR_STRIDE
    far_dist = past - FAR_STRIDE * np.arange(groups)
    own = np.eye(t_new, dtype=bool)[:, None, :] & (_pattern_count(far_dist, far_pats) > 0)[None, :, None]
    far_tab = jnp.where(jnp.asarray(own)[..., None], bias_at(far_dist)[None, :, None, :], NEG_INF)
    far_tab = far_tab.reshape(t_new, groups * t_new, -1)
    far_cnt = np.ones((t_new, groups * t_new), np.int32)
    tj = np.arange(t_new)[:, None] - np.arange(HEAD_DIM)[None, :]
    new_cnt = np.where(np.arange(HEAD_DIM)[None, :] < t_new, _pattern_count(tj, PATTERNS), 0)
    new_tab = masked(bias_at(np.clip(tj, 0, None).reshape(-1)).reshape(t_new, HEAD_DIM, -1), new_cnt)
    tab = jnp.concatenate([near_tab, far_tab, new_tab], axis=1).transpose(2, 0, 1)
    cnt = np.concatenate([near_cnt, far_cnt, new_cnt], axis=1)
    pad = SUBLANES - t_new
    tab = jnp.pad(tab, ((0, 0), (0, pad), (0, 0)))
    mult = np.pad(np.maximum(cnt, 1), ((0, pad), (0, 0)), constant_values=1).astype(np.float32)
    return tab, jnp.asarray(mult)


def kernel(x_prompt, x_sample, cache_conv, cache_kv, rel_bias, norm_pre, w_in, conv_dw_w, conv_dw_b,
           conv_ln_g, conv_ln_b, conv_pw_w, conv_pw_b, w_out, norm_post):
    depth = w_in.shape[0]
    assert depth == 1
    bsz, seq, d_model = x_prompt.shape
    n_dec, t_new, _ = x_sample.shape
    n_heads = cache_kv.shape[4]
    d_attn = n_heads * HEAD_DIM
    d_conv = cache_conv.shape[-1]
    past = cache_kv.shape[2]
    assert past == MAX_WINDOW and seq >= MAX_WINDOW and t_new <= 4
    hist = CONV_WIDTH - 1

    w_in_bf = w_in[0].astype(BF16)
    w_out_bf = w_out[0].astype(BF16)
    pw_bf = conv_pw_w[0].astype(BF16)
    conv_args = (conv_dw_w[0], conv_dw_b[0], conv_ln_g[0], conv_ln_b[0], pw_bf, conv_pw_b[0])
    conv_col0 = 4 * d_attn

    xp2 = x_prompt.reshape(bsz * seq, d_model)
    xs2 = x_sample.reshape(n_dec * t_new, d_model)
    zs = _inproj(_prenorm(xs2, norm_pre[0], tm=n_dec * t_new), w_in_bf, tm=n_dec * t_new, tn=d_attn)
    zs3 = zs.reshape(n_dec, t_new, -1)

    def heads(col0):
        return zs3[:, :, col0:col0 + d_attn].reshape(n_dec, t_new, n_heads, HEAD_DIM)

    rows_kv = 2 * n_heads
    kvn_rows = jnp.stack([heads(d_attn), heads(2 * d_attn)], axis=3).reshape(
        n_dec, t_new * rows_kv, HEAD_DIM)
    cache2 = cache_kv[0].transpose(0, 1, 3, 2, 4).reshape(n_dec, past * rows_kv, HEAD_DIM)
    tab_s, mult_s = _sample_tables(rel_bias, t_new, past)
    assert past % NEAR_ROWS == 0 and t_new <= SUBLANES
    stream = _Stream(zs3, kvn_rows, cache2, tab_s, mult_s, d_attn, rows_kv, past // NEAR_ROWS)

    conv_row_blocks = 8
    n_in_conv = min(bsz * conv_row_blocks // stream.n_chunks, n_dec - 1)
    n_in_attn = min(bsz * n_heads // stream.n_chunks, n_dec - n_in_conv)
    n_in_proj = n_dec - n_in_attn - n_in_conv
    tm_p, tn_p = 2048, 512
    hp = _prenorm(xp2, norm_pre[0], tm=512)
    if 0 < n_in_proj * stream.n_chunks <= (bsz * seq // tm_p) * (w_in_bf.shape[1] // tn_p):
        zp, att_s0, part_cache = _inproj(hp, w_in_bf, tm_p, tn_p, stream, n_in_proj)
    else:
        zp = _inproj(hp, w_in_bf, tm_p, tn_p)
        n_in_proj = 0
        att_s0 = jnp.zeros((0, t_new, d_attn), F32)
        part_cache = jnp.zeros(cache2.shape, cache2.dtype)
    zp3 = zp.reshape(bsz, seq, -1)
    mix_att_p, att_s1, part_cache = _attn_prompt(zp3, _prompt_bias_tables(rel_bias), n_heads, stream,
                                                 n_in_proj, n_in_attn, part_cache)
    att_s0 = jnp.concatenate([att_s0, att_s1], axis=0)
    zero_prefix = jnp.zeros((bsz, hist, d_conv), F32)
    if n_in_conv > 0:
        mix_conv_p, new_conv_p, att_s2, part_cache = _conv_branch(
            zp3, zero_prefix, *conv_args, col0=conv_col0, row_blocks=conv_row_blocks,
            stream=stream, seq0=att_s0.shape[0], n_seq=n_in_conv, partial_cache=part_cache)
        att_s0 = jnp.concatenate([att_s0, att_s2], axis=0)
    else:
        mix_conv_p, new_conv_p = _conv_branch(zp3, zero_prefix, *conv_args, col0=conv_col0,
                                              row_blocks=conv_row_blocks)
    yp = _outproj(mix_att_p.reshape(bsz * seq, d_attn), mix_conv_p.reshape(bsz * seq, d_conv),
                  w_out_bf, xp2, norm_post[0], tm=512)
    win = min(MAX_WINDOW, seq)
    kv_rows_p = _kv_rows(zp, d_attn, tm=512).reshape(bsz, seq, n_heads, 2, HEAD_DIM)
    new_kv_p = kv_rows_p[:, seq - win:].transpose(0, 1, 3, 2, 4)[None]

    n_hosted = att_s0.shape[0]
    att_s, new_rows = att_s0, part_cache
    if n_hosted < n_dec:
        att_s1, new_rows = _attn_sample(stream, n_hosted, n_dec - n_hosted, part_cache)
        att_s = jnp.concatenate([att_s0, att_s1], axis=0)
    new_kv_s = new_rows.reshape(n_dec, past, n_heads, 2, HEAD_DIM).transpose(0, 1, 3, 2, 4)[None]
    mix_att_s = att_s.reshape(n_dec * t_new, d_attn).astype(BF16)
    mix_conv_s, new_conv_s = _conv_branch(zs3, cache_conv[0], *conv_args, col0=conv_col0)
    ys = _outproj(mix_att_s, mix_conv_s.reshape(n_dec * t_new, d_conv),
                  w_out_bf, xs2, norm_post[0], tm=n_dec * t_new)

    return (yp.reshape(bsz, seq, d_model), ys.reshape(n_dec, t_new, d_model),
            new_conv_p[None], new_kv_p, new_conv_s[None], new_kv_s)
```

```python
import functools
import math
from typing import NamedTuple

import jax
import jax.numpy as jnp
import numpy as np
from jax import lax
from jax.experimental import pallas as pl
from jax.experimental.pallas import tpu as pltpu

F32 = jnp.float32
BF16 = jnp.bfloat16

HEAD_DIM = 128
PATTERNS = ((128, 1), (512, 4), (2048, 16))
MAX_WINDOW = 2048
Q_BLOCK = 128
KEYS_PER_PATTERN = 128
TABLE_LANES = 384
CONV_WIDTH = 31
N_BUCKETS = 32
MAX_EXACT = 16
EPS = 1e-6
NEG_INF = -1e30
SUBLANES = 8

MIB = 1024 * 1024
VMEM_BYTES_V7X = 64 * MIB
VMEM_LIMIT = 3 * VMEM_BYTES_V7X // 4
VMEM_LIMIT_CARRIER = VMEM_BYTES_V7X - 4 * MIB


def _rel_bucket(dist):
    d = jnp.maximum(dist, 1).astype(F32)
    log_b = MAX_EXACT + (jnp.log(d / MAX_EXACT) / math.log(MAX_WINDOW / MAX_EXACT)
                         * (N_BUCKETS - MAX_EXACT)).astype(jnp.int32)
    log_b = jnp.minimum(log_b, N_BUCKETS - 1)
    return jnp.where(dist < MAX_EXACT, dist, log_b)


def _round_up(x, m):
    return -(-x // m) * m


def _silu(x):
    return x * jax.nn.sigmoid(x)


def _prenorm_kernel(x_ref, g_ref, h_ref):
    x = x_ref[...]
    ms = jnp.mean(x * x, axis=-1, keepdims=True)
    h_ref[...] = (x * lax.rsqrt(ms + EPS) * g_ref[...]).astype(h_ref.dtype)


def _prenorm(x2d, norm_g, tm):
    m, d = x2d.shape
    return pl.pallas_call(
        _prenorm_kernel,
        grid=(m // tm,),
        in_specs=[pl.BlockSpec((tm, d), lambda i: (i, 0)), pl.BlockSpec((1, d), lambda i: (0, 0))],
        out_specs=pl.BlockSpec((tm, d), lambda i: (i, 0)),
        out_shape=jax.ShapeDtypeStruct((m, d), BF16),
        compiler_params=pltpu.CompilerParams(
            dimension_semantics=("parallel",), vmem_limit_bytes=VMEM_LIMIT),
        name="prenorm",
    )(x2d, norm_g.reshape(1, d))


def _inproj_kernel(h_ref, w_ref, z_ref):
    z_ref[...] = jnp.dot(h_ref[...], w_ref[...], preferred_element_type=F32)


def _inproj_stream_kernel(h_ref, w_ref, *rest, n_col_tiles, **stream_kw):
    stream_in, (z_ref, so_ref, out_any), scratch = rest[:8], rest[8:11], rest[11:]
    _inproj_kernel(h_ref, w_ref, z_ref)
    k = pl.program_id(0) * n_col_tiles + pl.program_id(1)
    _stream_step_with_attention(k, stream_in, so_ref, out_any, scratch, **stream_kw)


def _inproj(h2d, w_bf, tm, tn, stream=None, n_seq=0):
    m, d = h2d.shape
    n = w_bf.shape[1]
    grid = (m // tm, n // tn)
    in_specs = [pl.BlockSpec((tm, d), lambda i, j: (i, 0)),
                pl.BlockSpec((d, tn), lambda i, j: (0, j))]
    z_spec = pl.BlockSpec((tm, tn), lambda i, j: (i, j))
    z_shape = jax.ShapeDtypeStruct((m, n), F32)
    if stream is None:
        return pl.pallas_call(
            _inproj_kernel, grid=grid, in_specs=in_specs, out_specs=z_spec, out_shape=z_shape,
            compiler_params=pltpu.CompilerParams(
                dimension_semantics=("parallel", "arbitrary"), vmem_limit_bytes=VMEM_LIMIT),
            name="inproj",
        )(h2d, w_bf)
    assert n_seq * stream.n_chunks <= grid[0] * grid[1]
    s_in, s_out, s_shapes, s_scratch, s_args, s_kw = _stream_operands(
        stream, 0, n_seq, lambda i, j: (i * grid[1] + j) // stream.n_chunks)
    return pl.pallas_call(
        functools.partial(_inproj_stream_kernel, n_col_tiles=grid[1], **s_kw),
        grid=grid, in_specs=in_specs + s_in, out_specs=[z_spec] + s_out,
        out_shape=[z_shape] + s_shapes, scratch_shapes=s_scratch,
        compiler_params=pltpu.CompilerParams(
            dimension_semantics=("arbitrary", "arbitrary"), vmem_limit_bytes=VMEM_LIMIT_CARRIER),
        name="inproj",
    )(h2d, w_bf, *s_args)


def _kv_rows_kernel(k_ref, v_ref, o_ref):
    tm = k_ref.shape[0]
    hd = o_ref.shape[1]
    n_heads = k_ref.shape[1] // hd
    for h in range(n_heads):
        o_ref[pl.ds(2 * h, tm, stride=2 * n_heads), :] = k_ref[:, h * hd:(h + 1) * hd]
        o_ref[pl.ds(2 * h + 1, tm, stride=2 * n_heads), :] = v_ref[:, h * hd:(h + 1) * hd]


def _attn_block(qb, kw, vw, tab, scale):
    s = lax.dot_general(qb.astype(BF16), kw.astype(BF16), (((1,), (1,)), ((), ())),
                        preferred_element_type=F32)
    s = s * scale + tab
    m = jnp.max(s, axis=-1, keepdims=True)
    p = jnp.exp(s - m).astype(BF16)
    v_ones = jnp.concatenate([vw.astype(BF16), jnp.ones(vw.shape, BF16)], axis=1)
    acc_l = jnp.dot(p, v_ones, preferred_element_type=F32)
    d = vw.shape[1]
    return acc_l[:, :d], m, acc_l[:, d:]


def _attn_prompt_kernel(q_ref, k_ref, v_ref, g_ref, vec_ref, o_ref, acc_ref, m_ref, l_ref, ph_ref,
                        tab_ref):
    seq = q_ref.shape[0]
    scale = HEAD_DIM ** -0.5
    qb_rows = Q_BLOCK
    nk = KEYS_PER_PATTERN

    for p in range(len(PATTERNS)):
        base = jnp.broadcast_to(vec_ref[p][0:1, :], (qb_rows, vec_ref.shape[-1]))
        tab_ref[p] = pltpu.roll(base, 0, 1, stride=1, stride_axis=0)[:, :qb_rows + nk]

    sub = PATTERNS[1][1]
    assert all(dil == 1 or dil % sub == 0 for _, dil in PATTERNS)
    sub_len = seq // sub
    srcs = (q_ref, k_ref, v_ref)
    for a, ref in enumerate(srcs):
        for s in range(sub):
            ph_ref[a, s * sub_len:(s + 1) * sub_len, :] = ref[pl.ds(s, sub_len, stride=sub), :]

    def rows(a, phase, dil, start, size):
        if dil == 1:
            return srcs[a][pl.ds(start, size), :]
        step = dil // sub
        base = (phase % sub) * sub_len + phase // sub + step * start
        if step == 1:
            return ph_ref[a, pl.ds(base, size), :]
        return ph_ref[a, pl.ds(base, size, stride=step), :]

    def put(p, start, stride, acc, m, l):
        lanes = acc.shape[-1]
        if stride == 1:
            idx = pl.ds(start, qb_rows)
        else:
            idx = pl.ds(start, qb_rows, stride=stride)
        acc_ref[p, idx, :] = acc
        m_ref[p, idx, :] = jnp.broadcast_to(m, (qb_rows, lanes))
        l_ref[p, idx, :] = l

    def first_block(p, phase, dil):
        tab = tab_ref[p][:, nk:]
        qb = rows(0, phase, dil, 0, qb_rows)
        kw = rows(1, phase, dil, 0, qb_rows)
        vw = rows(2, phase, dil, 0, qb_rows)
        put(p, phase, dil, *_attn_block(qb, kw, vw, tab, scale))

    def later_block(p, phase, dil, n):
        tab = tab_ref[p]
        qb = rows(0, phase, dil, qb_rows * n, qb_rows)
        kw = rows(1, phase, dil, qb_rows * n - nk, qb_rows + nk)
        vw = rows(2, phase, dil, qb_rows * n - nk, qb_rows + nk)
        put(p, phase + dil * qb_rows * n, dil, *_attn_block(qb, kw, vw, tab, scale))

    for p, (window, dil) in enumerate(PATTERNS):
        n_blocks = seq // dil // qb_rows
        for phase in range(dil):
            first_block(p, phase, dil)
            for n in range(1, n_blocks):
                later_block(p, phase, dil, n)

    chunk = 256

    def combine(c, carry):
        sl = pl.ds(pl.multiple_of(c * chunk, chunk), chunk)
        m0, m1, m2 = m_ref[0, sl, :], m_ref[1, sl, :], m_ref[2, sl, :]
        mm = jnp.maximum(jnp.maximum(m0, m1), m2)
        e0, e1, e2 = jnp.exp(m0 - mm), jnp.exp(m1 - mm), jnp.exp(m2 - mm)
        num = e0 * acc_ref[0, sl, :] + e1 * acc_ref[1, sl, :] + e2 * acc_ref[2, sl, :]
        den = e0 * l_ref[0, sl, :] + e1 * l_ref[1, sl, :] + e2 * l_ref[2, sl, :]
        o_ref[sl, :] = (num / den * _silu(g_ref[sl, :])).astype(o_ref.dtype)
        return carry
    lax.fori_loop(0, seq // chunk, combine, 0)


def _attn_prompt_stream_kernel(q_ref, k_ref, v_ref, g_ref, tab_ref, *rest, n_heads, **stream_kw):
    stream_in, (o_ref, so_ref, out_any), scratch = rest[:8], rest[9:12], rest[12:]
    _attn_prompt_kernel(q_ref, k_ref, v_ref, g_ref, tab_ref, o_ref, *scratch[:5])
    k = pl.program_id(0) * n_heads + pl.program_id(1)
    _stream_step_with_attention(k, stream_in, so_ref, out_any, scratch[5:], **stream_kw)


def _attn_prompt(z3, tabs, n_heads, stream, seq0, n_seq, partial_cache):
    b, seq, _ = z3.shape
    hd = HEAD_DIM
    assert 0 < n_seq * stream.n_chunks <= b * n_heads

    def col(off):
        return pl.BlockSpec((None, seq, hd), lambda i, h: (i, 0, off + h))

    def seq_of(i, h):
        return (i * n_heads + h) // stream.n_chunks

    s_in, s_out, s_shapes, s_scratch, s_args, s_kw = _stream_operands(stream, seq0, n_seq, seq_of)
    in_specs = [col(0), col(n_heads), col(2 * n_heads), col(3 * n_heads),
                pl.BlockSpec((len(PATTERNS), None, SUBLANES, TABLE_LANES),
                             lambda i, h: (0, h, 0, 0))] + s_in + [pl.BlockSpec(memory_space=pl.ANY)]
    return pl.pallas_call(
        functools.partial(_attn_prompt_stream_kernel, n_heads=n_heads, **s_kw),
        grid=(b, n_heads),
        in_specs=in_specs,
        out_specs=[pl.BlockSpec((None, seq, hd), lambda i, h: (i, 0, h))] + s_out,
        out_shape=[jax.ShapeDtypeStruct((b, seq, n_heads * hd), BF16)] + s_shapes,
        scratch_shapes=([pltpu.VMEM((len(PATTERNS), seq, hd), F32)] * 4
                        + [pltpu.VMEM((len(PATTERNS), Q_BLOCK, Q_BLOCK + KEYS_PER_PATTERN), F32)]
                        + s_scratch),
        input_output_aliases={len(in_specs) - 1: 2},
        compiler_params=pltpu.CompilerParams(
            dimension_semantics=("arbitrary", "arbitrary"), vmem_limit_bytes=VMEM_LIMIT_CARRIER),
        name="attn_prompt",
    )(z3, z3, z3, z3, tabs, *s_args, partial_cache)


def _conv_kernel(ca_ref, cb_ref, gc_ref, pre_ref, dww_ref, dwb_ref, lng_ref, lnb_ref,
                 pww_ref, pwb_ref, o_ref, newc_ref, upad_ref, *, chunk):
    def one(s, carry):
        _conv_one_seq(ca_ref.at[s], cb_ref.at[s], gc_ref.at[s], pre_ref.at[s], dww_ref, dwb_ref,
                      lng_ref, lnb_ref, pww_ref, pwb_ref, o_ref.at[s], newc_ref.at[s], upad_ref,
                      chunk=chunk)
        return carry
    if ca_ref.shape[0] == 1:
        one(0, 0)
    else:
        lax.fori_loop(0, ca_ref.shape[0], one, 0)


def _conv_one_seq(ca_ref, cb_ref, gc_ref, pre_ref, dww_ref, dwb_ref, lng_ref, lnb_ref,
                  pww_ref, pwb_ref, o_ref, newc_ref, upad_ref, *, chunk):
    t_len = ca_ref.shape[0]
    rows_step = o_ref.shape[0]
    hist = CONV_WIDTH - 1
    rb = pl.program_id(1)

    @pl.when(rb == 0)
    def _():
        u = ca_ref[...] * jax.nn.sigmoid(cb_ref[...])
        upad_ref[0:hist, :] = pre_ref[...]
        upad_ref[hist:hist + t_len, :] = u
        n_pad = upad_ref.shape[0] - (hist + t_len)
        upad_ref[hist + t_len:, :] = jnp.zeros((n_pad, ca_ref.shape[1]), F32)
        newc_ref[...] = upad_ref[t_len:t_len + hist, :]

    win_rows = upad_ref.shape[0] - t_len + chunk

    def body(c, carry):
        l0 = c * chunk
        r0 = l0 if rows_step == t_len else rb * rows_step + l0
        if chunk % SUBLANES == 0:
            l0, r0 = pl.multiple_of(l0, SUBLANES), pl.multiple_of(r0, SUBLANES)
        win = upad_ref[pl.ds(r0, win_rows), :]
        y = jnp.zeros((chunk, ca_ref.shape[1]), F32) + dwb_ref[...]
        for s in range(SUBLANES):
            shifted = win if s == 0 else pltpu.roll(win, win_rows - s, 0)
            for a in range(-(-CONV_WIDTH // SUBLANES)):
                w = SUBLANES * a + s
                if w < CONV_WIDTH:
                    y = y + shifted[SUBLANES * a:SUBLANES * a + chunk] * dww_ref[w:w + 1, :]
        mu = jnp.mean(y, axis=-1, keepdims=True)
        var = jnp.mean(jnp.square(y - mu), axis=-1, keepdims=True)
        yn = (y - mu) * lax.rsqrt(var + EPS) * lng_ref[...] + lnb_ref[...]
        c_act = _silu(yn).astype(BF16)
        proj = jnp.dot(c_act, pww_ref[...], preferred_element_type=F32) + pwb_ref[...]
        o_ref[pl.ds(l0, chunk), :] = (proj * _silu(gc_ref[pl.ds(r0, chunk), :])).astype(o_ref.dtype)
        return carry
    if rows_step == chunk:
        body(0, 0)
    else:
        lax.fori_loop(0, rows_step // chunk, body, 0)


def _conv_stream_kernel(*refs, row_blocks, **stream_kw):
    chunk = stream_kw.pop("chunk")
    _conv_kernel(*refs[:10], refs[19], refs[20], refs[23], chunk=chunk)
    k = pl.program_id(0) * row_blocks + pl.program_id(1)
    _stream_step_with_attention(k, refs[10:18], refs[21], refs[22], refs[24:], **stream_kw)


def _conv_branch(z3, prefix, dw_w, dw_b, ln_g, ln_b, pw_w_bf, pw_b, col0, row_blocks=1,
                 stream=None, seq0=0, n_seq=0, partial_cache=None):
    n_all, t_len, _ = z3.shape
    c = prefix.shape[-1]
    hist = CONV_WIDTH - 1
    rows_step = t_len // row_blocks
    chunk = min(rows_step, 64)
    cblk = col0 // c
    group = n_all if (stream is None and t_len * c * 4 <= 64 * 1024) else 1
    n = n_all // group

    def zc(j):
        return pl.BlockSpec((group, t_len, c), lambda i, r: (i, 0, cblk + j))

    def vec():
        return pl.BlockSpec((1, c), lambda i, r: (0, 0))

    in_specs = [zc(0), zc(1), zc(2),
                pl.BlockSpec((group, hist, c), lambda i, r: (i, 0, 0)),
                pl.BlockSpec((CONV_WIDTH, c), lambda i, r: (0, 0)),
                vec(), vec(), vec(),
                pl.BlockSpec((c, c), lambda i, r: (0, 0)),
                vec()]
    out_specs = [pl.BlockSpec((group, rows_step, c), lambda i, r: (i, r, 0)),
                 pl.BlockSpec((group, hist, c), lambda i, r: (i, 0, 0))]
    out_shape = [jax.ShapeDtypeStruct((n_all, t_len, c), BF16),
                 jax.ShapeDtypeStruct((n_all, hist, c), F32)]
    scratch = [pltpu.VMEM((t_len - chunk + _round_up(chunk + CONV_WIDTH + 1, SUBLANES), c), F32)]
    args = (z3, z3, z3, prefix, dw_w, dw_b.reshape(1, c), ln_g.reshape(1, c), ln_b.reshape(1, c),
            pw_w_bf, pw_b.reshape(1, c))
    if stream is None:
        return pl.pallas_call(
            functools.partial(_conv_kernel, chunk=chunk),
            grid=(n, row_blocks), in_specs=in_specs, out_specs=out_specs, out_shape=out_shape,
            scratch_shapes=scratch,
            compiler_params=pltpu.CompilerParams(
                dimension_semantics=("parallel", "arbitrary"), vmem_limit_bytes=VMEM_LIMIT),
            name="conv_branch",
        )(*args)
    assert 0 < n_seq * stream.n_chunks <= n * row_blocks
    s_in, s_out, s_shapes, s_scratch, s_args, s_kw = _stream_operands(
        stream, seq0, n_seq, lambda i, r: (i * row_blocks + r) // stream.n_chunks)
    in_specs = in_specs + s_in + [pl.BlockSpec(memory_space=pl.ANY)]
    return pl.pallas_call(
        functools.partial(_conv_stream_kernel, row_blocks=row_blocks, chunk=chunk, **s_kw),
        grid=(n, row_blocks), in_specs=in_specs, out_specs=out_specs + s_out,
        out_shape=out_shape + s_shapes, scratch_shapes=scratch + s_scratch,
        input_output_aliases={len(in_specs) - 1: 3},
        compiler_params=pltpu.CompilerParams(
            dimension_semantics=("arbitrary", "arbitrary"), vmem_limit_bytes=VMEM_LIMIT_CARRIER),
        name="conv_branch",
    )(*args, *s_args, partial_cache)


def _outproj_kernel(ma_ref, mc_ref, wa_ref, wc_ref, x_ref, g_ref, *rest):
    y_ref = rest[-1] if len(rest) == 1 else rest[2]
    y = jnp.dot(ma_ref[...], wa_ref[...], preferred_element_type=F32)
    y = y + jnp.dot(mc_ref[...], wc_ref[...], preferred_element_type=F32)
    ms = jnp.mean(y * y, axis=-1, keepdims=True)
    y_ref[...] = x_ref[...] + y * lax.rsqrt(ms + EPS) * g_ref[...]
    if len(rest) > 1:
        _kv_rows_kernel(rest[0], rest[1], rest[3])


def _outproj(mix_att, mix_conv, w_bf, x2d, norm_g, tm, z=None):
    m, d = x2d.shape
    da, dc = mix_att.shape[1], mix_conv.shape[1]
    assert w_bf.shape[0] == da + dc and da % dc == 0
    once = pl.Buffered(1)
    in_specs = [pl.BlockSpec((tm, da), lambda i: (i, 0)),
                pl.BlockSpec((tm, dc), lambda i: (i, 0)),
                pl.BlockSpec((da, d), lambda i: (0, 0), pipeline_mode=once),
                pl.BlockSpec((dc, d), lambda i: (da // dc, 0), pipeline_mode=once),
                pl.BlockSpec((tm, d), lambda i: (i, 0)),
                pl.BlockSpec((1, d), lambda i: (0, 0))]
    out_specs = [pl.BlockSpec((tm, d), lambda i: (i, 0))]
    out_shape = [jax.ShapeDtypeStruct((m, d), F32)]
    args = (mix_att, mix_conv, w_bf, w_bf, x2d, norm_g.reshape(1, d))
    if z is not None:
        rows_kv = 2 * da // HEAD_DIM
        in_specs += [pl.BlockSpec((tm, da), lambda i: (i, 1)), pl.BlockSpec((tm, da), lambda i: (i, 2))]
        out_specs.append(pl.BlockSpec((tm * rows_kv, HEAD_DIM), lambda i: (i, 0)))
        out_shape.append(jax.ShapeDtypeStruct((m * rows_kv, HEAD_DIM), F32))
        args += (z, z)
    res = pl.pallas_call(
        _outproj_kernel,
        grid=(m // tm,),
        in_specs=in_specs, out_specs=out_specs, out_shape=out_shape,
        compiler_params=pltpu.CompilerParams(
            dimension_semantics=("parallel",), vmem_limit_bytes=VMEM_LIMIT_CARRIER),
        name="outproj",
    )(*args)
    return res if z is not None else res[0]


NEAR_ROWS = 512
FAR_STRIDE = 16


RING = 3


def _sample_heads(q_ref, k_ref, v_ref, g_ref, near_ref, far_ref, tab_ref, mult_ref, o_ref, rows_kv):
    t_new = q_ref.shape[0]
    hd = near_ref.shape[1]
    scale = HEAD_DIM ** -0.5
    pad_q = jnp.zeros((SUBLANES - t_new, hd), F32)
    pad_kv = jnp.zeros((hd - t_new, hd), F32)
    nt = (((1,), (1,)), ((), ()))
    n_far = far_ref.shape[1]
    mult = mult_ref[...]
    n_heads = rows_kv // 2

    def head_rows(ref, h, n, parity):
        return ref[pl.ds(2 * h + parity, n, stride=rows_kv), :].astype(BF16)

    def pad_bf(ref, h, pad):
        return jnp.concatenate([ref[:, h * hd:(h + 1) * hd], pad], axis=0).astype(BF16)

    scores = []
    for h in range(n_heads):
        q8 = pad_bf(q_ref, h, pad_q)
        scores.append(jnp.concatenate(
            [lax.dot_general(q8, head_rows(near_ref, h, NEAR_ROWS, 0), nt, preferred_element_type=F32),
             lax.dot_general(q8, far_ref[2 * h], nt, preferred_element_type=F32),
             lax.dot_general(q8, pad_bf(k_ref, h, pad_kv), nt, preferred_element_type=F32)], axis=1))
    probs = []
    for h, s in enumerate(scores):
        s = s * scale + tab_ref[h]
        m = jnp.max(s, axis=-1, keepdims=True)
        p = jnp.exp(s - m) * mult
        probs.append((p.astype(BF16), jnp.sum(p, axis=-1, keepdims=True)))
    for h, (pb, l) in enumerate(probs):
        acc = jnp.dot(pb[:, :NEAR_ROWS], head_rows(near_ref, h, NEAR_ROWS, 1),
                      preferred_element_type=F32)
        acc = acc + jnp.dot(pb[:, NEAR_ROWS:NEAR_ROWS + n_far], far_ref[2 * h + 1],
                            preferred_element_type=F32)
        acc = acc + jnp.dot(pb[:, NEAR_ROWS + n_far:], pad_bf(v_ref, h, pad_kv),
                            preferred_element_type=F32)
        cols = slice(h * hd, (h + 1) * hd)
        o_ref[:, cols] = (acc / l)[:t_new] * _silu(g_ref[:, cols])


def _cache_stream_step(k, n_steps, seq0, n_chunks, cache_any, kvn_any, out_any, buf_ref, far_ref,
                       stash_ref, sem_in, sem_out, sem_tail, shift_sl, rows_kv, attend):
    chunk_sl = buf_ref.shape[1]
    far_per_chunk = NEAR_ROWS // FAR_STRIDE
    seq_sl = n_chunks * chunk_sl

    def chunk_in(j):
        return pltpu.make_async_copy(
            cache_any.at[seq0 + j // n_chunks, pl.ds((j % n_chunks) * chunk_sl, chunk_sl)],
            buf_ref.at[j % RING], sem_in.at[j % RING])

    def first_out(j):
        return pltpu.make_async_copy(
            buf_ref.at[j % RING, pl.ds(shift_sl, chunk_sl - shift_sl)],
            out_any.at[seq0 + j // n_chunks, pl.ds(0, chunk_sl - shift_sl)], sem_out.at[j % RING])

    def later_out(j):
        return pltpu.make_async_copy(
            buf_ref.at[j % RING],
            out_any.at[seq0 + j // n_chunks, pl.ds((j % n_chunks) * chunk_sl - shift_sl, chunk_sl)],
            sem_out.at[j % RING])

    def tail(j):
        return pltpu.make_async_copy(
            kvn_any.at[seq0 + j // n_chunks],
            out_any.at[seq0 + j // n_chunks, pl.ds(seq_sl - shift_sl, shift_sl)], sem_tail.at[0])

    def on_chunk(j, first, later):
        pl.when(j % n_chunks == 0)(first)
        pl.when(j % n_chunks != 0)(later)

    @pl.when(k == 0)
    def _():
        for j in range(RING - 1):
            chunk_in(jnp.int32(j)).start()

    c = k % n_chunks
    slot = k % RING
    chunk_in(k).wait()
    on_chunk(k, lambda: first_out(k).start(), lambda: later_out(k).start())
    near_ref = buf_ref.at[slot]

    for grp in range(far_per_chunk):
        src = grp * FAR_STRIDE * rows_kv
        stash_ref[grp * shift_sl:(grp + 1) * shift_sl, :] = near_ref[src:src + shift_sl, :]
    keys_chunk = stash_ref.shape[0] // rows_kv
    key0 = pl.multiple_of(c * keys_chunk, keys_chunk)
    for j in range(rows_kv):
        far_ref[j, pl.ds(key0, keys_chunk), :] = (
            stash_ref[pl.ds(j, keys_chunk, stride=rows_kv), :].astype(far_ref.dtype))

    @pl.when(c == n_chunks - 1)
    def _():
        tail(k).start()
        attend(near_ref)
        tail(k).wait()

    @pl.when(k >= 1)
    def _():
        on_chunk(k - 1, lambda: first_out(k - 1).wait(), lambda: later_out(k - 1).wait())

    @pl.when(k + RING - 1 < n_steps)
    def _():
        chunk_in(k + RING - 1).start()

    @pl.when(k == n_steps - 1)
    def _():
        on_chunk(k, lambda: first_out(k).wait(), lambda: later_out(k).wait())


class _Stream(NamedTuple):
    zs3: jax.Array
    kvn_rows: jax.Array
    cache2: jax.Array
    tab: jax.Array
    mult: jax.Array
    d_attn: int
    rows_kv: int
    n_chunks: int


def _stream_operands(stream, seq0, n_seq, seq_of):
    t_new = stream.zs3.shape[1]
    hd = stream.cache2.shape[2]
    chunk_sl = stream.cache2.shape[1] // stream.n_chunks
    n_far = stream.cache2.shape[1] // stream.rows_kv // FAR_STRIDE * t_new

    def local(*g):
        return jnp.minimum(seq_of(*g), n_seq - 1)

    def zcol(j):
        return pl.BlockSpec((None, t_new, stream.d_attn), lambda *g: (seq0 + local(*g), 0, j))

    in_specs = [zcol(0), zcol(1), zcol(2), zcol(3),
                pl.BlockSpec(stream.tab.shape, lambda *g: (0, 0, 0)),
                pl.BlockSpec(stream.mult.shape, lambda *g: (0, 0)),
                pl.BlockSpec(memory_space=pl.ANY),
                pl.BlockSpec(memory_space=pl.ANY)]
    out_specs = [pl.BlockSpec((None, t_new, stream.d_attn), lambda *g: (local(*g), 0, 0)),
                 pl.BlockSpec(memory_space=pl.ANY)]
    out_shapes = [jax.ShapeDtypeStruct((n_seq, t_new, stream.d_attn), F32),
                  jax.ShapeDtypeStruct(stream.cache2.shape, stream.cache2.dtype)]
    scratch = [pltpu.VMEM((RING, chunk_sl, hd), F32),
               pltpu.VMEM((stream.rows_kv, n_far, hd), BF16),
               pltpu.VMEM((NEAR_ROWS // FAR_STRIDE * t_new * stream.rows_kv, hd), F32),
               pltpu.SemaphoreType.DMA((RING,)),
               pltpu.SemaphoreType.DMA((RING,)),
               pltpu.SemaphoreType.DMA((1,))]
    args = (stream.zs3,) * 4 + (stream.tab, stream.mult, stream.cache2, stream.kvn_rows)
    kw = dict(rows_kv=stream.rows_kv, seq0=seq0, n_chunks=stream.n_chunks,
              n_steps=n_seq * stream.n_chunks)
    return in_specs, out_specs, out_shapes, scratch, args, kw


def _stream_step_with_attention(k, stream_in, o_ref, out_any, scratch, *, rows_kv, seq0, n_chunks,
                                n_steps):
    q_ref, k_ref, v_ref, g_ref, tab_ref, mult_ref, cache_any, kvn_any = stream_in
    buf_ref, far_ref, stash_ref, sem_in, sem_out, sem_tail = scratch
    shift_sl = q_ref.shape[0] * rows_kv

    def attend(near_ref):
        _sample_heads(q_ref, k_ref, v_ref, g_ref, near_ref, far_ref, tab_ref, mult_ref, o_ref, rows_kv)

    @pl.when(k < n_steps)
    def _():
        _cache_stream_step(k, n_steps, seq0, n_chunks, cache_any, kvn_any, out_any, buf_ref, far_ref,
                           stash_ref, sem_in, sem_out, sem_tail, shift_sl, rows_kv, attend)


def _attn_sample_kernel(*refs, n_chunks, **stream_kw):
    k = pl.program_id(0) * n_chunks + pl.program_id(1)
    _stream_step_with_attention(k, refs[:8], refs[9], refs[10], refs[11:], n_chunks=n_chunks,
                                **stream_kw)


def _attn_sample(stream, seq0, n_seq, partial_cache):
    s_in, s_out, s_shapes, s_scratch, s_args, s_kw = _stream_operands(
        stream, seq0, n_seq, lambda i, c: i)
    return pl.pallas_call(
        functools.partial(_attn_sample_kernel, **s_kw),
        grid=(n_seq, stream.n_chunks),
        in_specs=s_in + [pl.BlockSpec(memory_space=pl.ANY)],
        out_specs=s_out,
        out_shape=s_shapes,
        scratch_shapes=s_scratch,
        input_output_aliases={len(s_in): 1},
        compiler_params=pltpu.CompilerParams(
            dimension_semantics=("arbitrary", "arbitrary"), vmem_limit_bytes=VMEM_LIMIT),
        name="attn_sample",
    )(*s_args, partial_cache)


def _prompt_bias_tables(rel_bias):
    nk = KEYS_PER_PATTERN
    qb = Q_BLOCK
    wrap = TABLE_LANES
    assert wrap >= 2 * qb + nk - 1
    m = np.arange(wrap)
    kdist = nk - np.where(m < qb + nk, m, m - wrap)
    valid = (kdist >= 0) & (kdist <= nk)
    tabs = []
    for _, dil in PATTERNS:
        bucket = _rel_bucket(jnp.asarray(np.clip(kdist, 0, nk) * dil, jnp.int32))
        vec = jnp.where(valid[:, None], rel_bias[bucket].astype(F32), NEG_INF).T
        tabs.append(jnp.broadcast_to(vec[:, None, :], (vec.shape[0], SUBLANES, wrap)))
    return jnp.stack(tabs)


def _pattern_count(dist, patterns):
    return sum(((dist % dil == 0) & (dist >= 0) & (dist <= window)).astype(np.int32)
               for window, dil in patterns)


def _sample_tables(rel_bias, t_new, past):
    def bias_at(dist):
        return rel_bias[_rel_bucket(jnp.asarray(dist, jnp.int32))].astype(F32)

    def masked(bias, count):
        return jnp.where(jnp.asarray(count > 0)[..., None], bias, NEG_INF)

    near_pats, far_pats = PATTERNS[:2], PATTERNS[2:]
    assert near_pats[-1][0] == NEAR_ROWS and far_pats[0][1] == FAR_STRIDE and t_new <= far_pats[0][1]
    desc = np.arange(NEAR_ROWS + t_new - 1, 0, -1)
    desc_cnt = _pattern_count(desc, near_pats)
    desc_tab = masked(bias_at(desc), desc_cnt)
    starts = [t_new - 1 - t for t in range(t_new)]
    near_tab = jnp.stack([desc_tab[s0:s0 + NEAR_ROWS] for s0 in starts])
    near_cnt = np.stack([desc_cnt[s0:s0 + NEAR_ROWS] for s0 in starts])
    groups = past // FAR_STRIDE
    far_dist = past - FAR_STRIDE * np.arange(groups)
    own = np.eye(t_new, dtype=bool)[:, None, :] & (_pattern_count(far_dist, far_pats) > 0)[None, :, None]
    far_tab = jnp.where(jnp.asarray(own)[..., None], bias_at(far_dist)[None, :, None, :], NEG_INF)
    far_tab = far_tab.reshape(t_new, groups * t_new, -1)
    far_cnt = np.ones((t_new, groups * t_new), np.int32)
    tj = np.arange(t_new)[:, None] - np.arange(HEAD_DIM)[None, :]
    new_cnt = np.where(np.arange(HEAD_DIM)[None, :] < t_new, _pattern_count(tj, PATTERNS), 0)
    new_tab = masked(bias_at(np.clip(tj, 0, None).reshape(-1)).reshape(t_new, HEAD_DIM, -1), new_cnt)
    tab = jnp.concatenate([near_tab, far_tab, new_tab], axis=1).transpose(2, 0, 1)
    cnt = np.concatenate([near_cnt, far_cnt, new_cnt], axis=1)
    pad = SUBLANES - t_new
    tab = jnp.pad(tab, ((0, 0), (0, pad), (0, 0)))
    mult = np.pad(np.maximum(cnt, 1), ((0, pad), (0, 0)), constant_values=1).astype(np.float32)
    return tab, jnp.asarray(mult)


def kernel(x_prompt, x_sample, cache_conv, cache_kv, rel_bias, norm_pre, w_in, conv_dw_w, conv_dw_b,
           conv_ln_g, conv_ln_b, conv_pw_w, conv_pw_b, w_out, norm_post):
    depth = w_in.shape[0]
    assert depth == 1
    bsz, seq, d_model = x_prompt.shape
    n_dec, t_new, _ = x_sample.shape
    n_heads = cache_kv.shape[4]
    d_attn = n_heads * HEAD_DIM
    d_conv = cache_conv.shape[-1]
    past = cache_kv.shape[2]
    assert past == MAX_WINDOW and seq >= MAX_WINDOW and t_new <= 4
    hist = CONV_WIDTH - 1

    w_in_bf = w_in[0].astype(BF16)
    w_out_bf = w_out[0].astype(BF16)
    pw_bf = conv_pw_w[0].astype(BF16)
    conv_args = (conv_dw_w[0], conv_dw_b[0], conv_ln_g[0], conv_ln_b[0], pw_bf, conv_pw_b[0])
    conv_col0 = 4 * d_attn

    xp2 = x_prompt.reshape(bsz * seq, d_model)
    xs2 = x_sample.reshape(n_dec * t_new, d_model)
    zs = _inproj(_prenorm(xs2, norm_pre[0], tm=n_dec * t_new), w_in_bf, tm=n_dec * t_new, tn=d_attn)
    zs3 = zs.reshape(n_dec, t_new, -1)

    def heads(col0):
        return zs3[:, :, col0:col0 + d_attn].reshape(n_dec, t_new, n_heads, HEAD_DIM)

    rows_kv = 2 * n_heads
    kvn_rows = jnp.stack([heads(d_attn), heads(2 * d_attn)], axis=3).reshape(
        n_dec, t_new * rows_kv, HEAD_DIM)
    cache2 = cache_kv[0].transpose(0, 1, 3, 2, 4).reshape(n_dec, past * rows_kv, HEAD_DIM)
    tab_s, mult_s = _sample_tables(rel_bias, t_new, past)
    assert past % NEAR_ROWS == 0 and t_new <= SUBLANES
    stream = _Stream(zs3, kvn_rows, cache2, tab_s, mult_s, d_attn, rows_kv, past // NEAR_ROWS)

    conv_row_blocks = 8
    n_in_conv = min(bsz * conv_row_blocks // stream.n_chunks, n_dec - 1)
    n_in_attn = min(bsz * n_heads // stream.n_chunks, n_dec - n_in_conv)
    n_in_proj = n_dec - n_in_attn - n_in_conv
    tm_p, tn_p = 2048, 512
    hp = _prenorm(xp2, norm_pre[0], tm=512)
    if 0 < n_in_proj * stream.n_chunks <= (bsz * seq // tm_p) * (w_in_bf.shape[1] // tn_p):
        zp, att_s0, part_cache = _inproj(hp, w_in_bf, tm_p, tn_p, stream, n_in_proj)
    else:
        zp = _inproj(hp, w_in_bf, tm_p, tn_p)
        n_in_proj = 0
        att_s0 = jnp.zeros((0, t_new, d_attn), F32)
        part_cache = jnp.zeros(cache2.shape, cache2.dtype)
    zp3 = zp.reshape(bsz, seq, -1)
    mix_att_p, att_s1, part_cache = _attn_prompt(zp3, _prompt_bias_tables(rel_bias), n_heads, stream,
                                                 n_in_proj, n_in_attn, part_cache)
    att_s0 = jnp.concatenate([att_s0, att_s1], axis=0)
    zero_prefix = jnp.zeros((bsz, hist, d_conv), F32)
    if n_in_conv > 0:
        mix_conv_p, new_conv_p, att_s2, part_cache = _conv_branch(
            zp3, zero_prefix, *conv_args, col0=conv_col0, row_blocks=conv_row_blocks,
            stream=stream, seq0=att_s0.shape[0], n_seq=n_in_conv, partial_cache=part_cache)
        att_s0 = jnp.concatenate([att_s0, att_s2], axis=0)
    else:
        mix_conv_p, new_conv_p = _conv_branch(zp3, zero_prefix, *conv_args, col0=conv_col0,
                                              row_blocks=conv_row_blocks)
    yp, kv_rows_p = _outproj(mix_att_p.reshape(bsz * seq, d_attn),
                             mix_conv_p.reshape(bsz * seq, d_conv),
                             w_out_bf, xp2, norm_post[0], tm=512, z=zp)
    win = min(MAX_WINDOW, seq)
    kv_rows_p = kv_rows_p.reshape(bsz, seq, n_heads, 2, HEAD_DIM)
    new_kv_p = kv_rows_p[:, seq - win:].transpose(0, 1, 3, 2, 4)[None]

    n_hosted = att_s0.shape[0]
    att_s, new_rows = att_s0, part_cache
    if n_hosted < n_dec:
        att_s1, new_rows = _attn_sample(stream, n_hosted, n_dec - n_hosted, part_cache)
        att_s = jnp.concatenate([att_s0, att_s1], axis=0)
    new_kv_s = new_rows.reshape(n_dec, past, n_heads, 2, HEAD_DIM).transpose(0, 1, 3, 2, 4)[None]
    mix_att_s = att_s.reshape(n_dec * t_new, d_attn).astype(BF16)
    mix_conv_s, new_conv_s = _conv_branch(zs3, cache_conv[0], *conv_args, col0=conv_col0)
    ys = _outproj(mix_att_s, mix_conv_s.reshape(n_dec * t_new, d_conv),
                  w_out_bf, xs2, norm_post[0], tm=n_dec * t_new)

    return (yp.reshape(bsz, seq, d_model), ys.reshape(n_dec, t_new, d_model),
            new_conv_p[None], new_kv_p, new_conv_s[None], new_kv_s)
```

```python
import functools
import math
from typing import NamedTuple

import jax
import jax.numpy as jnp
import numpy as np
from jax import lax
from jax.experimental import pallas as pl
from jax.experimental.pallas import tpu as pltpu

F32 = jnp.float32
BF16 = jnp.bfloat16

HEAD_DIM = 128
PATTERNS = ((128, 1), (512, 4), (2048, 16))
MAX_WINDOW = 2048
Q_BLOCK = 128
KEYS_PER_PATTERN = 128
TABLE_LANES = 384
CONV_WIDTH = 31
N_BUCKETS = 32
MAX_EXACT = 16
EPS = 1e-6
NEG_INF = -1e30
SUBLANES = 8

MIB = 1024 * 1024
VMEM_BYTES_V7X = 64 * MIB
VMEM_LIMIT = 3 * VMEM_BYTES_V7X // 4
VMEM_LIMIT_CARRIER = VMEM_BYTES_V7X - 4 * MIB


def _rel_bucket(dist):
    d = jnp.maximum(dist, 1).astype(F32)
    log_b = MAX_EXACT + (jnp.log(d / MAX_EXACT) / math.log(MAX_WINDOW / MAX_EXACT)
                         * (N_BUCKETS - MAX_EXACT)).astype(jnp.int32)
    log_b = jnp.minimum(log_b, N_BUCKETS - 1)
    return jnp.where(dist < MAX_EXACT, dist, log_b)


def _round_up(x, m):
    return -(-x // m) * m


def _silu(x):
    return x * jax.nn.sigmoid(x)


def _prenorm_kernel(x_ref, g_ref, h_ref):
    x = x_ref[...]
    ms = jnp.mean(x * x, axis=-1, keepdims=True)
    h_ref[...] = (x * lax.rsqrt(ms + EPS) * g_ref[...]).astype(h_ref.dtype)


def _prenorm(x2d, norm_g, tm):
    m, d = x2d.shape
    return pl.pallas_call(
        _prenorm_kernel,
        grid=(m // tm,),
        in_specs=[pl.BlockSpec((tm, d), lambda i: (i, 0)), pl.BlockSpec((1, d), lambda i: (0, 0))],
        out_specs=pl.BlockSpec((tm, d), lambda i: (i, 0)),
        out_shape=jax.ShapeDtypeStruct((m, d), BF16),
        compiler_params=pltpu.CompilerParams(
            dimension_semantics=("parallel",), vmem_limit_bytes=VMEM_LIMIT),
        name="prenorm",
    )(x2d, norm_g.reshape(1, d))


def _inproj_kernel(h_ref, w_ref, z_ref):
    z_ref[...] = jnp.dot(h_ref[...], w_ref[...], preferred_element_type=F32)


def _inproj_cast_kernel(h_ref, w_ref, z_ref, wbf_ref):
    wbf_ref[...] = w_ref[...].astype(wbf_ref.dtype)
    z_ref[...] = jnp.dot(h_ref[...], wbf_ref[...], preferred_element_type=F32)


def _inproj_cast(h2d, w_f32, tn):
    m, d = h2d.shape
    n = w_f32.shape[1]
    return pl.pallas_call(
        _inproj_cast_kernel,
        grid=(n // tn,),
        in_specs=[pl.BlockSpec((m, d), lambda j: (0, 0)), pl.BlockSpec((d, tn), lambda j: (0, j))],
        out_specs=[pl.BlockSpec((m, tn), lambda j: (0, j)), pl.BlockSpec((d, tn), lambda j: (0, j))],
        out_shape=[jax.ShapeDtypeStruct((m, n), F32), jax.ShapeDtypeStruct((d, n), BF16)],
        compiler_params=pltpu.CompilerParams(
            dimension_semantics=("parallel",), vmem_limit_bytes=VMEM_LIMIT),
        name="inproj_cast",
    )(h2d, w_f32)


def _inproj_stream_kernel(h_ref, w_ref, *rest, n_col_tiles, **stream_kw):
    stream_in, (z_ref, so_ref, out_any), scratch = rest[:8], rest[8:11], rest[11:]
    _inproj_kernel(h_ref, w_ref, z_ref)
    k = pl.program_id(0) * n_col_tiles + pl.program_id(1)
    _stream_step_with_attention(k, stream_in, so_ref, out_any, scratch, **stream_kw)


def _inproj(h2d, w_bf, tm, tn, stream=None, n_seq=0):
    m, d = h2d.shape
    n = w_bf.shape[1]
    grid = (m // tm, n // tn)
    in_specs = [pl.BlockSpec((tm, d), lambda i, j: (i, 0)),
                pl.BlockSpec((d, tn), lambda i, j: (0, j))]
    z_spec = pl.BlockSpec((tm, tn), lambda i, j: (i, j))
    z_shape = jax.ShapeDtypeStruct((m, n), F32)
    if stream is None:
        return pl.pallas_call(
            _inproj_kernel, grid=grid, in_specs=in_specs, out_specs=z_spec, out_shape=z_shape,
            compiler_params=pltpu.CompilerParams(
                dimension_semantics=("parallel", "arbitrary"), vmem_limit_bytes=VMEM_LIMIT),
            name="inproj",
        )(h2d, w_bf)
    assert n_seq * stream.n_chunks <= grid[0] * grid[1]
    s_in, s_out, s_shapes, s_scratch, s_args, s_kw = _stream_operands(
        stream, 0, n_seq, lambda i, j: (i * grid[1] + j) // stream.n_chunks)
    return pl.pallas_call(
        functools.partial(_inproj_stream_kernel, n_col_tiles=grid[1], **s_kw),
        grid=grid, in_specs=in_specs + s_in, out_specs=[z_spec] + s_out,
        out_shape=[z_shape] + s_shapes, scratch_shapes=s_scratch,
        compiler_params=pltpu.CompilerParams(
            dimension_semantics=("arbitrary", "arbitrary"), vmem_limit_bytes=VMEM_LIMIT_CARRIER),
        name="inproj",
    )(h2d, w_bf, *s_args)


def _kv_rows_kernel(k_ref, v_ref, o_ref):
    tm = k_ref.shape[0]
    hd = o_ref.shape[1]
    n_heads = k_ref.shape[1] // hd
    for h in range(n_heads):
        o_ref[pl.ds(2 * h, tm, stride=2 * n_heads), :] = k_ref[:, h * hd:(h + 1) * hd]
        o_ref[pl.ds(2 * h + 1, tm, stride=2 * n_heads), :] = v_ref[:, h * hd:(h + 1) * hd]


def _attn_block(qb, kw, vw, tab, scale):
    s = lax.dot_general(qb.astype(BF16), kw.astype(BF16), (((1,), (1,)), ((), ())),
                        preferred_element_type=F32)
    s = s * scale + tab
    m = jnp.max(s, axis=-1, keepdims=True)
    p = jnp.exp(s - m).astype(BF16)
    v_ones = jnp.concatenate([vw.astype(BF16), jnp.ones(vw.shape, BF16)], axis=1)
    acc_l = jnp.dot(p, v_ones, preferred_element_type=F32)
    d = vw.shape[1]
    return acc_l[:, :d], m, acc_l[:, d:]


def _attn_prompt_kernel(q_ref, k_ref, v_ref, g_ref, vec_ref, o_ref, acc_ref, m_ref, l_ref, ph_ref,
                        tab_ref):
    seq = q_ref.shape[0]
    scale = HEAD_DIM ** -0.5
    qb_rows = Q_BLOCK
    nk = KEYS_PER_PATTERN

    for p in range(len(PATTERNS)):
        base = jnp.broadcast_to(vec_ref[p][0:1, :], (qb_rows, vec_ref.shape[-1]))
        tab_ref[p] = pltpu.roll(base, 0, 1, stride=1, stride_axis=0)[:, :qb_rows + nk]

    sub = PATTERNS[1][1]
    assert all(dil == 1 or dil % sub == 0 for _, dil in PATTERNS)
    sub_len = seq // sub
    srcs = (q_ref, k_ref, v_ref)
    for a, ref in enumerate(srcs):
        for s in range(sub):
            ph_ref[a, s * sub_len:(s + 1) * sub_len, :] = ref[pl.ds(s, sub_len, stride=sub), :]

    def rows(a, phase, dil, start, size):
        if dil == 1:
            return srcs[a][pl.ds(start, size), :]
        step = dil // sub
        base = (phase % sub) * sub_len + phase // sub + step * start
        if step == 1:
            return ph_ref[a, pl.ds(base, size), :]
        return ph_ref[a, pl.ds(base, size, stride=step), :]

    def put(p, start, stride, acc, m, l):
        lanes = acc.shape[-1]
        if stride == 1:
            idx = pl.ds(start, qb_rows)
        else:
            idx = pl.ds(start, qb_rows, stride=stride)
        acc_ref[p, idx, :] = acc
        m_ref[p, idx, :] = jnp.broadcast_to(m, (qb_rows, lanes))
        l_ref[p, idx, :] = l

    def first_block(p, phase, dil):
        tab = tab_ref[p][:, nk:]
        qb = rows(0, phase, dil, 0, qb_rows)
        kw = rows(1, phase, dil, 0, qb_rows)
        vw = rows(2, phase, dil, 0, qb_rows)
        put(p, phase, dil, *_attn_block(qb, kw, vw, tab, scale))

    def later_block(p, phase, dil, n):
        tab = tab_ref[p]
        qb = rows(0, phase, dil, qb_rows * n, qb_rows)
        kw = rows(1, phase, dil, qb_rows * n - nk, qb_rows + nk)
        vw = rows(2, phase, dil, qb_rows * n - nk, qb_rows + nk)
        put(p, phase + dil * qb_rows * n, dil, *_attn_block(qb, kw, vw, tab, scale))

    for p, (window, dil) in enumerate(PATTERNS):
        n_blocks = seq // dil // qb_rows
        for phase in range(dil):
            first_block(p, phase, dil)
            for n in range(1, n_blocks):
                later_block(p, phase, dil, n)

    chunk = 256

    def combine(c, carry):
        sl = pl.ds(pl.multiple_of(c * chunk, chunk), chunk)
        m0, m1, m2 = m_ref[0, sl, :], m_ref[1, sl, :], m_ref[2, sl, :]
        mm = jnp.maximum(jnp.maximum(m0, m1), m2)
        e0, e1, e2 = jnp.exp(m0 - mm), jnp.exp(m1 - mm), jnp.exp(m2 - mm)
        num = e0 * acc_ref[0, sl, :] + e1 * acc_ref[1, sl, :] + e2 * acc_ref[2, sl, :]
        den = e0 * l_ref[0, sl, :] + e1 * l_ref[1, sl, :] + e2 * l_ref[2, sl, :]
        o_ref[sl, :] = (num / den * _silu(g_ref[sl, :])).astype(o_ref.dtype)
        return carry
    lax.fori_loop(0, seq // chunk, combine, 0)


def _attn_prompt_stream_kernel(q_ref, k_ref, v_ref, g_ref, tab_ref, *rest, n_heads, **stream_kw):
    stream_in, (o_ref, so_ref, out_any), scratch = rest[:8], rest[9:12], rest[12:]
    _attn_prompt_kernel(q_ref, k_ref, v_ref, g_ref, tab_ref, o_ref, *scratch[:5])
    k = pl.program_id(0) * n_heads + pl.program_id(1)
    _stream_step_with_attention(k, stream_in, so_ref, out_any, scratch[5:], **stream_kw)


def _attn_prompt(z3, tabs, n_heads, stream, seq0, n_seq, partial_cache):
    b, seq, _ = z3.shape
    hd = HEAD_DIM
    assert 0 < n_seq * stream.n_chunks <= b * n_heads

    def col(off):
        return pl.BlockSpec((None, seq, hd), lambda i, h: (i, 0, off + h))

    def seq_of(i, h):
        return (i * n_heads + h) // stream.n_chunks

    s_in, s_out, s_shapes, s_scratch, s_args, s_kw = _stream_operands(stream, seq0, n_seq, seq_of)
    in_specs = [col(0), col(n_heads), col(2 * n_heads), col(3 * n_heads),
                pl.BlockSpec((len(PATTERNS), None, SUBLANES, TABLE_LANES),
                             lambda i, h: (0, h, 0, 0))] + s_in + [pl.BlockSpec(memory_space=pl.ANY)]
    return pl.pallas_call(
        functools.partial(_attn_prompt_stream_kernel, n_heads=n_heads, **s_kw),
        grid=(b, n_heads),
        in_specs=in_specs,
        out_specs=[pl.BlockSpec((None, seq, hd), lambda i, h: (i, 0, h))] + s_out,
        out_shape=[jax.ShapeDtypeStruct((b, seq, n_heads * hd), BF16)] + s_shapes,
        scratch_shapes=([pltpu.VMEM((len(PATTERNS), seq, hd), F32)] * 4
                        + [pltpu.VMEM((len(PATTERNS), Q_BLOCK, Q_BLOCK + KEYS_PER_PATTERN), F32)]
                        + s_scratch),
        input_output_aliases={len(in_specs) - 1: 2},
        compiler_params=pltpu.CompilerParams(
            dimension_semantics=("arbitrary", "arbitrary"), vmem_limit_bytes=VMEM_LIMIT_CARRIER),
        name="attn_prompt",
    )(z3, z3, z3, z3, tabs, *s_args, partial_cache)


def _conv_kernel(ca_ref, cb_ref, gc_ref, pre_ref, dww_ref, dwb_ref, lng_ref, lnb_ref,
                 pww_ref, pwb_ref, o_ref, newc_ref, upad_ref, *, chunk):
    def one(s, carry):
        _conv_one_seq(ca_ref.at[s], cb_ref.at[s], gc_ref.at[s], pre_ref.at[s], dww_ref, dwb_ref,
                      lng_ref, lnb_ref, pww_ref, pwb_ref, o_ref.at[s], newc_ref.at[s], upad_ref,
                      chunk=chunk)
        return carry
    if ca_ref.shape[0] == 1:
        one(0, 0)
    else:
        lax.fori_loop(0, ca_ref.shape[0], one, 0)


def _conv_one_seq(ca_ref, cb_ref, gc_ref, pre_ref, dww_ref, dwb_ref, lng_ref, lnb_ref,
                  pww_ref, pwb_ref, o_ref, newc_ref, upad_ref, *, chunk):
    t_len = ca_ref.shape[0]
    rows_step = o_ref.shape[0]
    hist = CONV_WIDTH - 1
    rb = pl.program_id(1)

    @pl.when(rb == 0)
    def _():
        u = ca_ref[...] * jax.nn.sigmoid(cb_ref[...])
        upad_ref[0:hist, :] = pre_ref[...]
        upad_ref[hist:hist + t_len, :] = u
        n_pad = upad_ref.shape[0] - (hist + t_len)
        upad_ref[hist + t_len:, :] = jnp.zeros((n_pad, ca_ref.shape[1]), F32)
        newc_ref[...] = upad_ref[t_len:t_len + hist, :]

    win_rows = upad_ref.shape[0] - t_len + chunk

    def body(c, carry):
        l0 = c * chunk
        r0 = l0 if rows_step == t_len else rb * rows_step + l0
        if chunk % SUBLANES == 0:
            l0, r0 = pl.multiple_of(l0, SUBLANES), pl.multiple_of(r0, SUBLANES)
        win = upad_ref[pl.ds(r0, win_rows), :]
        y = jnp.zeros((chunk, ca_ref.shape[1]), F32) + dwb_ref[...]
        for s in range(SUBLANES):
            shifted = win if s == 0 else pltpu.roll(win, win_rows - s, 0)
            for a in range(-(-CONV_WIDTH // SUBLANES)):
                w = SUBLANES * a + s
                if w < CONV_WIDTH:
                    y = y + shifted[SUBLANES * a:SUBLANES * a + chunk] * dww_ref[w:w + 1, :]
        mu = jnp.mean(y, axis=-1, keepdims=True)
        var = jnp.mean(jnp.square(y - mu), axis=-1, keepdims=True)
        yn = (y - mu) * lax.rsqrt(var + EPS) * lng_ref[...] + lnb_ref[...]
        c_act = _silu(yn).astype(BF16)
        proj = jnp.dot(c_act, pww_ref[...], preferred_element_type=F32) + pwb_ref[...]
        o_ref[pl.ds(l0, chunk), :] = (proj * _silu(gc_ref[pl.ds(r0, chunk), :])).astype(o_ref.dtype)
        return carry
    if rows_step == chunk:
        body(0, 0)
    else:
        lax.fori_loop(0, rows_step // chunk, body, 0)


def _conv_stream_kernel(*refs, row_blocks, **stream_kw):
    chunk = stream_kw.pop("chunk")
    _conv_kernel(*refs[:10], refs[19], refs[20], refs[23], chunk=chunk)
    k = pl.program_id(0) * row_blocks + pl.program_id(1)
    _stream_step_with_attention(k, refs[10:18], refs[21], refs[22], refs[24:], **stream_kw)


def _conv_branch(z3, prefix, dw_w, dw_b, ln_g, ln_b, pw_w_bf, pw_b, col0, row_blocks=1,
                 stream=None, seq0=0, n_seq=0, partial_cache=None):
    n_all, t_len, _ = z3.shape
    c = prefix.shape[-1]
    hist = CONV_WIDTH - 1
    rows_step = t_len // row_blocks
    chunk = min(rows_step, 64)
    cblk = col0 // c
    group = n_all if (stream is None and t_len * c * 4 <= 64 * 1024) else 1
    n = n_all // group

    def zc(j):
        return pl.BlockSpec((group, t_len, c), lambda i, r: (i, 0, cblk + j))

    def vec():
        return pl.BlockSpec((1, c), lambda i, r: (0, 0))

    in_specs = [zc(0), zc(1), zc(2),
                pl.BlockSpec((group, hist, c), lambda i, r: (i, 0, 0)),
                pl.BlockSpec((CONV_WIDTH, c), lambda i, r: (0, 0)),
                vec(), vec(), vec(),
                pl.BlockSpec((c, c), lambda i, r: (0, 0)),
                vec()]
    out_specs = [pl.BlockSpec((group, rows_step, c), lambda i, r: (i, r, 0)),
                 pl.BlockSpec((group, hist, c), lambda i, r: (i, 0, 0))]
    out_shape = [jax.ShapeDtypeStruct((n_all, t_len, c), BF16),
                 jax.ShapeDtypeStruct((n_all, hist, c), F32)]
    scratch = [pltpu.VMEM((t_len - chunk + _round_up(chunk + CONV_WIDTH + 1, SUBLANES), c), F32)]
    args = (z3, z3, z3, prefix, dw_w, dw_b.reshape(1, c), ln_g.reshape(1, c), ln_b.reshape(1, c),
            pw_w_bf, pw_b.reshape(1, c))
    if stream is None:
        return pl.pallas_call(
            functools.partial(_conv_kernel, chunk=chunk),
            grid=(n, row_blocks), in_specs=in_specs, out_specs=out_specs, out_shape=out_shape,
            scratch_shapes=scratch,
            compiler_params=pltpu.CompilerParams(
                dimension_semantics=("parallel", "arbitrary"), vmem_limit_bytes=VMEM_LIMIT),
            name="conv_branch",
        )(*args)
    assert 0 < n_seq * stream.n_chunks <= n * row_blocks
    s_in, s_out, s_shapes, s_scratch, s_args, s_kw = _stream_operands(
        stream, seq0, n_seq, lambda i, r: (i * row_blocks + r) // stream.n_chunks)
    in_specs = in_specs + s_in + [pl.BlockSpec(memory_space=pl.ANY)]
    return pl.pallas_call(
        functools.partial(_conv_stream_kernel, row_blocks=row_blocks, chunk=chunk, **s_kw),
        grid=(n, row_blocks), in_specs=in_specs, out_specs=out_specs + s_out,
        out_shape=out_shape + s_shapes, scratch_shapes=scratch + s_scratch,
        input_output_aliases={len(in_specs) - 1: 3},
        compiler_params=pltpu.CompilerParams(
            dimension_semantics=("arbitrary", "arbitrary"), vmem_limit_bytes=VMEM_LIMIT_CARRIER),
        name="conv_branch",
    )(*args, *s_args, partial_cache)


def _outproj_kernel(ma_ref, mc_ref, wa_ref, wc_ref, x_ref, g_ref, *rest):
    y_ref = rest[-1] if len(rest) == 1 else rest[2]
    y = jnp.dot(ma_ref[...], wa_ref[...], preferred_element_type=F32)
    y = y + jnp.dot(mc_ref[...], wc_ref[...], preferred_element_type=F32)
    ms = jnp.mean(y * y, axis=-1, keepdims=True)
    y_ref[...] = x_ref[...] + y * lax.rsqrt(ms + EPS) * g_ref[...]
    if len(rest) > 1:
        _kv_rows_kernel(rest[0], rest[1], rest[3])


def _outproj(mix_att, mix_conv, w_bf, x2d, norm_g, tm, z=None):
    m, d = x2d.shape
    da, dc = mix_att.shape[1], mix_conv.shape[1]
    assert w_bf.shape[0] == da + dc and da % dc == 0
    once = pl.Buffered(1)
    in_specs = [pl.BlockSpec((tm, da), lambda i: (i, 0)),
                pl.BlockSpec((tm, dc), lambda i: (i, 0)),
                pl.BlockSpec((da, d), lambda i: (0, 0), pipeline_mode=once),
                pl.BlockSpec((dc, d), lambda i: (da // dc, 0), pipeline_mode=once),
                pl.BlockSpec((tm, d), lambda i: (i, 0)),
                pl.BlockSpec((1, d), lambda i: (0, 0))]
    out_specs = [pl.BlockSpec((tm, d), lambda i: (i, 0))]
    out_shape = [jax.ShapeDtypeStruct((m, d), F32)]
    args = (mix_att, mix_conv, w_bf, w_bf, x2d, norm_g.reshape(1, d))
    if z is not None:
        rows_kv = 2 * da // HEAD_DIM
        in_specs += [pl.BlockSpec((tm, da), lambda i: (i, 1)), pl.BlockSpec((tm, da), lambda i: (i, 2))]
        out_specs.append(pl.BlockSpec((tm * rows_kv, HEAD_DIM), lambda i: (i, 0)))
        out_shape.append(jax.ShapeDtypeStruct((m * rows_kv, HEAD_DIM), F32))
        args += (z, z)
    res = pl.pallas_call(
        _outproj_kernel,
        grid=(m // tm,),
        in_specs=in_specs, out_specs=out_specs, out_shape=out_shape,
        compiler_params=pltpu.CompilerParams(
            dimension_semantics=("parallel",), vmem_limit_bytes=VMEM_LIMIT_CARRIER),
        name="outproj",
    )(*args)
    return res if z is not None else res[0]


NEAR_ROWS = 512
FAR_STRIDE = 16


RING = 3


def _sample_heads(q_ref, k_ref, v_ref, g_ref, near_ref, far_ref, tab_ref, mult_ref, o_ref, rows_kv):
    t_new = q_ref.shape[0]
    hd = near_ref.shape[1]
    scale = HEAD_DIM ** -0.5
    pad_q = jnp.zeros((SUBLANES - t_new, hd), F32)
    pad_kv = jnp.zeros((hd - t_new, hd), F32)
    nt = (((1,), (1,)), ((), ()))
    n_far = far_ref.shape[1]
    mult = mult_ref[...]
    n_heads = rows_kv // 2

    def head_rows(ref, h, n, parity):
        return ref[pl.ds(2 * h + parity, n, stride=rows_kv), :].astype(BF16)

    def pad_bf(ref, h, pad):
        return jnp.concatenate([ref[:, h * hd:(h + 1) * hd], pad], axis=0).astype(BF16)

    scores = []
    for h in range(n_heads):
        q8 = pad_bf(q_ref, h, pad_q)
        scores.append(jnp.concatenate(
            [lax.dot_general(q8, head_rows(near_ref, h, NEAR_ROWS, 0), nt, preferred_element_type=F32),
             lax.dot_general(q8, far_ref[2 * h], nt, preferred_element_type=F32),
             lax.dot_general(q8, pad_bf(k_ref, h, pad_kv), nt, preferred_element_type=F32)], axis=1))
    probs = []
    for h, s in enumerate(scores):
        s = s * scale + tab_ref[h]
        m = jnp.max(s, axis=-1, keepdims=True)
        p = jnp.exp(s - m) * mult
        probs.append((p.astype(BF16), jnp.sum(p, axis=-1, keepdims=True)))
    for h, (pb, l) in enumerate(probs):
        acc = jnp.dot(pb[:, :NEAR_ROWS], head_rows(near_ref, h, NEAR_ROWS, 1),
                      preferred_element_type=F32)
        acc = acc + jnp.dot(pb[:, NEAR_ROWS:NEAR_ROWS + n_far], far_ref[2 * h + 1],
                            preferred_element_type=F32)
        acc = acc + jnp.dot(pb[:, NEAR_ROWS + n_far:], pad_bf(v_ref, h, pad_kv),
                            preferred_element_type=F32)
        cols = slice(h * hd, (h + 1) * hd)
        o_ref[:, cols] = (acc / l)[:t_new] * _silu(g_ref[:, cols])


def _cache_stream_step(k, n_steps, seq0, n_chunks, cache_any, kvn_any, out_any, buf_ref, far_ref,
                       stash_ref, sem_in, sem_out, sem_tail, shift_sl, rows_kv, attend):
    chunk_sl = buf_ref.shape[1]
    far_per_chunk = NEAR_ROWS // FAR_STRIDE
    seq_sl = n_chunks * chunk_sl

    def chunk_in(j):
        return pltpu.make_async_copy(
            cache_any.at[seq0 + j // n_chunks, pl.ds((j % n_chunks) * chunk_sl, chunk_sl)],
            buf_ref.at[j % RING], sem_in.at[j % RING])

    def first_out(j):
        return pltpu.make_async_copy(
            buf_ref.at[j % RING, pl.ds(shift_sl, chunk_sl - shift_sl)],
            out_any.at[seq0 + j // n_chunks, pl.ds(0, chunk_sl - shift_sl)], sem_out.at[j % RING])

    def later_out(j):
        return pltpu.make_async_copy(
            buf_ref.at[j % RING],
            out_any.at[seq0 + j // n_chunks, pl.ds((j % n_chunks) * chunk_sl - shift_sl, chunk_sl)],
            sem_out.at[j % RING])

    def tail(j):
        return pltpu.make_async_copy(
            kvn_any.at[seq0 + j // n_chunks],
            out_any.at[seq0 + j // n_chunks, pl.ds(seq_sl - shift_sl, shift_sl)], sem_tail.at[0])

    def on_chunk(j, first, later):
        pl.when(j % n_chunks == 0)(first)
        pl.when(j % n_chunks != 0)(later)

    @pl.when(k == 0)
    def _():
        for j in range(RING - 1):
            chunk_in(jnp.int32(j)).start()

    c = k % n_chunks
    slot = k % RING
    chunk_in(k).wait()
    on_chunk(k, lambda: first_out(k).start(), lambda: later_out(k).start())
    near_ref = buf_ref.at[slot]

    for grp in range(far_per_chunk):
        src = grp * FAR_STRIDE * rows_kv
        stash_ref[grp * shift_sl:(grp + 1) * shift_sl, :] = near_ref[src:src + shift_sl, :]
    keys_chunk = stash_ref.shape[0] // rows_kv
    key0 = pl.multiple_of(c * keys_chunk, keys_chunk)
    for j in range(rows_kv):
        far_ref[j, pl.ds(key0, keys_chunk), :] = (
            stash_ref[pl.ds(j, keys_chunk, stride=rows_kv), :].astype(far_ref.dtype))

    @pl.when(c == n_chunks - 1)
    def _():
        tail(k).start()
        attend(near_ref)
        tail(k).wait()

    @pl.when(k >= 1)
    def _():
        on_chunk(k - 1, lambda: first_out(k - 1).wait(), lambda: later_out(k - 1).wait())

    @pl.when(k + RING - 1 < n_steps)
    def _():
        chunk_in(k + RING - 1).start()

    @pl.when(k == n_steps - 1)
    def _():
        on_chunk(k, lambda: first_out(k).wait(), lambda: later_out(k).wait())


class _Stream(NamedTuple):
    zs3: jax.Array
    kvn_rows: jax.Array
    cache2: jax.Array
    tab: jax.Array
    mult: jax.Array
    d_attn: int
    rows_kv: int
    n_chunks: int


def _stream_operands(stream, seq0, n_seq, seq_of):
    t_new = stream.zs3.shape[1]
    hd = stream.cache2.shape[2]
    chunk_sl = stream.cache2.shape[1] // stream.n_chunks
    n_far = stream.cache2.shape[1] // stream.rows_kv // FAR_STRIDE * t_new

    def local(*g):
        return jnp.minimum(seq_of(*g), n_seq - 1)

    def zcol(j):
        return pl.BlockSpec((None, t_new, stream.d_attn), lambda *g: (seq0 + local(*g), 0, j))

    in_specs = [zcol(0), zcol(1), zcol(2), zcol(3),
                pl.BlockSpec(stream.tab.shape, lambda *g: (0, 0, 0)),
                pl.BlockSpec(stream.mult.shape, lambda *g: (0, 0)),
                pl.BlockSpec(memory_space=pl.ANY),
                pl.BlockSpec(memory_space=pl.ANY)]
    out_specs = [pl.BlockSpec((None, t_new, stream.d_attn), lambda *g: (local(*g), 0, 0)),
                 pl.BlockSpec(memory_space=pl.ANY)]
    out_shapes = [jax.ShapeDtypeStruct((n_seq, t_new, stream.d_attn), F32),
                  jax.ShapeDtypeStruct(stream.cache2.shape, stream.cache2.dtype)]
    scratch = [pltpu.VMEM((RING, chunk_sl, hd), F32),
               pltpu.VMEM((stream.rows_kv, n_far, hd), BF16),
               pltpu.VMEM((NEAR_ROWS // FAR_STRIDE * t_new * stream.rows_kv, hd), F32),
               pltpu.SemaphoreType.DMA((RING,)),
               pltpu.SemaphoreType.DMA((RING,)),
               pltpu.SemaphoreType.DMA((1,))]
    args = (stream.zs3,) * 4 + (stream.tab, stream.mult, stream.cache2, stream.kvn_rows)
    kw = dict(rows_kv=stream.rows_kv, seq0=seq0, n_chunks=stream.n_chunks,
              n_steps=n_seq * stream.n_chunks)
    return in_specs, out_specs, out_shapes, scratch, args, kw


def _stream_step_with_attention(k, stream_in, o_ref, out_any, scratch, *, rows_kv, seq0, n_chunks,
                                n_steps):
    q_ref, k_ref, v_ref, g_ref, tab_ref, mult_ref, cache_any, kvn_any = stream_in
    buf_ref, far_ref, stash_ref, sem_in, sem_out, sem_tail = scratch
    shift_sl = q_ref.shape[0] * rows_kv

    def attend(near_ref):
        _sample_heads(q_ref, k_ref, v_ref, g_ref, near_ref, far_ref, tab_ref, mult_ref, o_ref, rows_kv)

    @pl.when(k < n_steps)
    def _():
        _cache_stream_step(k, n_steps, seq0, n_chunks, cache_any, kvn_any, out_any, buf_ref, far_ref,
                           stash_ref, sem_in, sem_out, sem_tail, shift_sl, rows_kv, attend)


def _attn_sample_kernel(*refs, n_chunks, **stream_kw):
    k = pl.program_id(0) * n_chunks + pl.program_id(1)
    _stream_step_with_attention(k, refs[:8], refs[9], refs[10], refs[11:], n_chunks=n_chunks,
                                **stream_kw)


def _attn_sample(stream, seq0, n_seq, partial_cache):
    s_in, s_out, s_shapes, s_scratch, s_args, s_kw = _stream_operands(
        stream, seq0, n_seq, lambda i, c: i)
    return pl.pallas_call(
        functools.partial(_attn_sample_kernel, **s_kw),
        grid=(n_seq, stream.n_chunks),
        in_specs=s_in + [pl.BlockSpec(memory_space=pl.ANY)],
        out_specs=s_out,
        out_shape=s_shapes,
        scratch_shapes=s_scratch,
        input_output_aliases={len(s_in): 1},
        compiler_params=pltpu.CompilerParams(
            dimension_semantics=("arbitrary", "arbitrary"), vmem_limit_bytes=VMEM_LIMIT),
        name="attn_sample",
    )(*s_args, partial_cache)


def _prompt_bias_tables(rel_bias):
    nk = KEYS_PER_PATTERN
    qb = Q_BLOCK
    wrap = TABLE_LANES
    assert wrap >= 2 * qb + nk - 1
    m = np.arange(wrap)
    kdist = nk - np.where(m < qb + nk, m, m - wrap)
    valid = (kdist >= 0) & (kdist <= nk)
    tabs = []
    for _, dil in PATTERNS:
        bucket = _rel_bucket(jnp.asarray(np.clip(kdist, 0, nk) * dil, jnp.int32))
        vec = jnp.where(valid[:, None], rel_bias[bucket].astype(F32), NEG_INF).T
        tabs.append(jnp.broadcast_to(vec[:, None, :], (vec.shape[0], SUBLANES, wrap)))
    return jnp.stack(tabs)


def _pattern_count(dist, patterns):
    return sum(((dist % dil == 0) & (dist >= 0) & (dist <= window)).astype(np.int32)
               for window, dil in patterns)


def _sample_tables(rel_bias, t_new, past):
    def bias_at(dist):
        return rel_bias[_rel_bucket(jnp.asarray(dist, jnp.int32))].astype(F32)

    def masked(bias, count):
        return jnp.where(jnp.asarray(count > 0)[..., None], bias, NEG_INF)

    near_pats, far_pats = PATTERNS[:2], PATTERNS[2:]
    assert near_pats[-1][0] == NEAR_ROWS and far_pats[0][1] == FAR_STRIDE and t_new <= far_pats[0][1]
    desc = np.arange(NEAR_ROWS + t_new - 1, 0, -1)
    desc_cnt = _pattern_count(desc, near_pats)
    desc_tab = masked(bias_at(desc), desc_cnt)
    starts = [t_new - 1 - t for t in range(t_new)]
    near_tab = jnp.stack([desc_tab[s0:s0 + NEAR_ROWS] for s0 in starts])
    near_cnt = np.stack([desc_cnt[s0:s0 + NEAR_ROWS] for s0 in starts])
    groups = past // FAR_STRIDE
    far_dist = past - FAR_STRIDE * np.arange(groups)
    own = np.eye(t_new, dtype=bool)[:, None, :] & (_pattern_count(far_dist, far_pats) > 0)[None, :, None]
    far_tab = jnp.where(jnp.asarray(own)[..., None], bias_at(far_dist)[None, :, None, :], NEG_INF)
    far_tab = far_tab.reshape(t_new, groups * t_new, -1)
    far_cnt = np.ones((t_new, groups * t_new), np.int32)
    tj = np.arange(t_new)[:, None] - np.arange(HEAD_DIM)[None, :]
    new_cnt = np.where(np.arange(HEAD_DIM)[None, :] < t_new, _pattern_count(tj, PATTERNS), 0)
    new_tab = masked(bias_at(np.clip(tj, 0, None).reshape(-1)).reshape(t_new, HEAD_DIM, -1), new_cnt)
    tab = jnp.concatenate([near_tab, far_tab, new_tab], axis=1).transpose(2, 0, 1)
    cnt = np.concatenate([near_cnt, far_cnt, new_cnt], axis=1)
    pad = SUBLANES - t_new
    tab = jnp.pad(tab, ((0, 0), (0, pad), (0, 0)))
    mult = np.pad(np.maximum(cnt, 1), ((0, pad), (0, 0)), constant_values=1).astype(np.float32)
    return tab, jnp.asarray(mult)


def kernel(x_prompt, x_sample, cache_conv, cache_kv, rel_bias, norm_pre, w_in, conv_dw_w, conv_dw_b,
           conv_ln_g, conv_ln_b, conv_pw_w, conv_pw_b, w_out, norm_post):
    depth = w_in.shape[0]
    assert depth == 1
    bsz, seq, d_model = x_prompt.shape
    n_dec, t_new, _ = x_sample.shape
    n_heads = cache_kv.shape[4]
    d_attn = n_heads * HEAD_DIM
    d_conv = cache_conv.shape[-1]
    past = cache_kv.shape[2]
    assert past == MAX_WINDOW and seq >= MAX_WINDOW and t_new <= 4
    hist = CONV_WIDTH - 1

    w_out_bf = w_out[0].astype(BF16)
    pw_bf = conv_pw_w[0].astype(BF16)
    conv_args = (conv_dw_w[0], conv_dw_b[0], conv_ln_g[0], conv_ln_b[0], pw_bf, conv_pw_b[0])
    conv_col0 = 4 * d_attn

    xp2 = x_prompt.reshape(bsz * seq, d_model)
    xs2 = x_sample.reshape(n_dec * t_new, d_model)
    zs, w_in_bf = _inproj_cast(_prenorm(xs2, norm_pre[0], tm=n_dec * t_new), w_in[0], tn=d_attn)
    zs3 = zs.reshape(n_dec, t_new, -1)

    def heads(col0):
        return zs3[:, :, col0:col0 + d_attn].reshape(n_dec, t_new, n_heads, HEAD_DIM)

    rows_kv = 2 * n_heads
    kvn_rows = jnp.stack([heads(d_attn), heads(2 * d_attn)], axis=3).reshape(
        n_dec, t_new * rows_kv, HEAD_DIM)
    cache2 = cache_kv[0].transpose(0, 1, 3, 2, 4).reshape(n_dec, past * rows_kv, HEAD_DIM)
    tab_s, mult_s = _sample_tables(rel_bias, t_new, past)
    assert past % NEAR_ROWS == 0 and t_new <= SUBLANES
    stream = _Stream(zs3, kvn_rows, cache2, tab_s, mult_s, d_attn, rows_kv, past // NEAR_ROWS)

    conv_row_blocks = 8
    n_in_conv = min(bsz * conv_row_blocks // stream.n_chunks, n_dec - 1)
    n_in_attn = min(bsz * n_heads // stream.n_chunks, n_dec - n_in_conv)
    n_in_proj = n_dec - n_in_attn - n_in_conv
    tm_p, tn_p = 2048, 512
    hp = _prenorm(xp2, norm_pre[0], tm=512)
    if 0 < n_in_proj * stream.n_chunks <= (bsz * seq // tm_p) * (w_in_bf.shape[1] // tn_p):
        zp, att_s0, part_cache = _inproj(hp, w_in_bf, tm_p, tn_p, stream, n_in_proj)
    else:
        zp = _inproj(hp, w_in_bf, tm_p, tn_p)
        n_in_proj = 0
        att_s0 = jnp.zeros((0, t_new, d_attn), F32)
        part_cache = jnp.zeros(cache2.shape, cache2.dtype)
    zp3 = zp.reshape(bsz, seq, -1)
    mix_att_p, att_s1, part_cache = _attn_prompt(zp3, _prompt_bias_tables(rel_bias), n_heads, stream,
                                                 n_in_proj, n_in_attn, part_cache)
    att_s0 = jnp.concatenate([att_s0, att_s1], axis=0)
    zero_prefix = jnp.zeros((bsz, hist, d_conv), F32)
    if n_in_conv > 0:
        mix_conv_p, new_conv_p, att_s2, part_cache = _conv_branch(
            zp3, zero_prefix, *conv_args, col0=conv_col0, row_blocks=conv_row_blocks,
            stream=stream, seq0=att_s0.shape[0], n_seq=n_in_conv, partial_cache=part_cache)
        att_s0 = jnp.concatenate([att_s0, att_s2], axis=0)
    else:
        mix_conv_p, new_conv_p = _conv_branch(zp3, zero_prefix, *conv_args, col0=conv_col0,
                                              row_blocks=conv_row_blocks)
    yp, kv_rows_p = _outproj(mix_att_p.reshape(bsz * seq, d_attn),
                             mix_conv_p.reshape(bsz * seq, d_conv),
                             w_out_bf, xp2, norm_post[0], tm=512, z=zp)
    win = min(MAX_WINDOW, seq)
    kv_rows_p = kv_rows_p.reshape(bsz, seq, n_heads, 2, HEAD_DIM)
    new_kv_p = kv_rows_p[:, seq - win:].transpose(0, 1, 3, 2, 4)[None]

    n_hosted = att_s0.shape[0]
    att_s, new_rows = att_s0, part_cache
    if n_hosted < n_dec:
        att_s1, new_rows = _attn_sample(stream, n_hosted, n_dec - n_hosted, part_cache)
        att_s = jnp.concatenate([att_s0, att_s1], axis=0)
    new_kv_s = new_rows.reshape(n_dec, past, n_heads, 2, HEAD_DIM).transpose(0, 1, 3, 2, 4)[None]
    mix_att_s = att_s.reshape(n_dec * t_new, d_attn).astype(BF16)
    mix_conv_s, new_conv_s = _conv_branch(zs3, cache_conv[0], *conv_args, col0=conv_col0)
    ys = _outproj(mix_att_s, mix_conv_s.reshape(n_dec * t_new, d_conv),
                  w_out_bf, xs2, norm_post[0], tm=n_dec * t_new)

    return (yp.reshape(bsz, seq, d_model), ys.reshape(n_dec, t_new, d_model),
            new_conv_p[None], new_kv_p, new_conv_s[None], new_kv_s)
```

```python
import functools
import math
from typing import NamedTuple

import jax
import jax.numpy as jnp
import numpy as np
from jax import lax
from jax.experimental import pallas as pl
from jax.experimental.pallas import tpu as pltpu

F32 = jnp.float32
BF16 = jnp.bfloat16

HEAD_DIM = 128
PATTERNS = ((128, 1), (512, 4), (2048, 16))
MAX_WINDOW = 2048
Q_BLOCK = 128
KEYS_PER_PATTERN = 128
TABLE_LANES = 384
CONV_WIDTH = 31
N_BUCKETS = 32
MAX_EXACT = 16
EPS = 1e-6
NEG_INF = -1e30
SUBLANES = 8

MIB = 1024 * 1024
VMEM_BYTES_V7X = 64 * MIB
VMEM_LIMIT = 3 * VMEM_BYTES_V7X // 4
VMEM_LIMIT_CARRIER = VMEM_BYTES_V7X - 4 * MIB


def _rel_bucket(dist):
    d = jnp.maximum(dist, 1).astype(F32)
    log_b = MAX_EXACT + (jnp.log(d / MAX_EXACT) / math.log(MAX_WINDOW / MAX_EXACT)
                         * (N_BUCKETS - MAX_EXACT)).astype(jnp.int32)
    log_b = jnp.minimum(log_b, N_BUCKETS - 1)
    return jnp.where(dist < MAX_EXACT, dist, log_b)


def _round_up(x, m):
    return -(-x // m) * m


def _silu(x):
    return x * jax.nn.sigmoid(x)


def _prenorm_kernel(x_ref, g_ref, h_ref):
    x = x_ref[...]
    ms = jnp.mean(x * x, axis=-1, keepdims=True)
    h_ref[...] = (x * lax.rsqrt(ms + EPS) * g_ref[...]).astype(h_ref.dtype)


def _prenorm(x2d, norm_g, tm):
    m, d = x2d.shape
    return pl.pallas_call(
        _prenorm_kernel,
        grid=(m // tm,),
        in_specs=[pl.BlockSpec((tm, d), lambda i: (i, 0)), pl.BlockSpec((1, d), lambda i: (0, 0))],
        out_specs=pl.BlockSpec((tm, d), lambda i: (i, 0)),
        out_shape=jax.ShapeDtypeStruct((m, d), BF16),
        compiler_params=pltpu.CompilerParams(
            dimension_semantics=("parallel",), vmem_limit_bytes=VMEM_LIMIT),
        name="prenorm",
    )(x2d, norm_g.reshape(1, d))


def _inproj_kernel(h_ref, w_ref, z_ref):
    z_ref[...] = jnp.dot(h_ref[...], w_ref[...], preferred_element_type=F32)


def _inproj_cast_kernel(h_ref, w_ref, z_ref, wbf_ref):
    wbf_ref[...] = w_ref[...].astype(wbf_ref.dtype)
    z_ref[...] = jnp.dot(h_ref[...], wbf_ref[...], preferred_element_type=F32)


def _inproj_cast(h2d, w_f32, tn):
    m, d = h2d.shape
    n = w_f32.shape[1]
    return pl.pallas_call(
        _inproj_cast_kernel,
        grid=(n // tn,),
        in_specs=[pl.BlockSpec((m, d), lambda j: (0, 0)), pl.BlockSpec((d, tn), lambda j: (0, j))],
        out_specs=[pl.BlockSpec((m, tn), lambda j: (0, j)), pl.BlockSpec((d, tn), lambda j: (0, j))],
        out_shape=[jax.ShapeDtypeStruct((m, n), F32), jax.ShapeDtypeStruct((d, n), BF16)],
        compiler_params=pltpu.CompilerParams(
            dimension_semantics=("parallel",), vmem_limit_bytes=VMEM_LIMIT),
        name="inproj_cast",
    )(h2d, w_f32)


def _inproj_stream_kernel(h_ref, w_ref, *rest, n_col_tiles, **stream_kw):
    stream_in, (z_ref, so_ref, out_any), scratch = rest[:8], rest[8:11], rest[11:]
    _inproj_kernel(h_ref, w_ref, z_ref)
    k = pl.program_id(0) * n_col_tiles + pl.program_id(1)
    _stream_step_with_attention(k, stream_in, so_ref, out_any, scratch, **stream_kw)


def _inproj(h2d, w_bf, tm, tn, stream=None, n_seq=0):
    m, d = h2d.shape
    n = w_bf.shape[1]
    grid = (m // tm, n // tn)
    in_specs = [pl.BlockSpec((tm, d), lambda i, j: (i, 0)),
                pl.BlockSpec((d, tn), lambda i, j: (0, j))]
    z_spec = pl.BlockSpec((tm, tn), lambda i, j: (i, j))
    z_shape = jax.ShapeDtypeStruct((m, n), F32)
    if stream is None:
        return pl.pallas_call(
            _inproj_kernel, grid=grid, in_specs=in_specs, out_specs=z_spec, out_shape=z_shape,
            compiler_params=pltpu.CompilerParams(
                dimension_semantics=("parallel", "arbitrary"), vmem_limit_bytes=VMEM_LIMIT),
            name="inproj",
        )(h2d, w_bf)
    assert n_seq * stream.n_chunks <= grid[0] * grid[1]
    s_in, s_out, s_shapes, s_scratch, s_args, s_kw = _stream_operands(
        stream, 0, n_seq, lambda i, j: (i * grid[1] + j) // stream.n_chunks)
    return pl.pallas_call(
        functools.partial(_inproj_stream_kernel, n_col_tiles=grid[1], **s_kw),
        grid=grid, in_specs=in_specs + s_in, out_specs=[z_spec] + s_out,
        out_shape=[z_shape] + s_shapes, scratch_shapes=s_scratch,
        compiler_params=pltpu.CompilerParams(
            dimension_semantics=("arbitrary", "arbitrary"), vmem_limit_bytes=VMEM_LIMIT_CARRIER),
        name="inproj",
    )(h2d, w_bf, *s_args)


def _kv_rows_kernel(k_ref, v_ref, o_ref):
    tm = k_ref.shape[0]
    hd = o_ref.shape[1]
    n_heads = k_ref.shape[1] // hd
    for h in range(n_heads):
        o_ref[pl.ds(2 * h, tm, stride=2 * n_heads), :] = k_ref[:, h * hd:(h + 1) * hd]
        o_ref[pl.ds(2 * h + 1, tm, stride=2 * n_heads), :] = v_ref[:, h * hd:(h + 1) * hd]


def _attn_block(qb, kw, vw, tab, scale):
    s = lax.dot_general(qb.astype(BF16), kw.astype(BF16), (((1,), (1,)), ((), ())),
                        preferred_element_type=F32)
    s = s * scale + tab
    m = jnp.max(s, axis=-1, keepdims=True)
    p = jnp.exp(s - m).astype(BF16)
    v_ones = jnp.concatenate([vw.astype(BF16), jnp.ones(vw.shape, BF16)], axis=1)
    acc_l = jnp.dot(p, v_ones, preferred_element_type=F32)
    d = vw.shape[1]
    return acc_l[:, :d], m, acc_l[:, d:]


def _attn_prompt_kernel(q_ref, k_ref, v_ref, g_ref, vec_ref, o_ref, acc_ref, m_ref, l_ref, ph_ref,
                        tab_ref):
    seq = q_ref.shape[0]
    scale = HEAD_DIM ** -0.5
    qb_rows = Q_BLOCK
    nk = KEYS_PER_PATTERN

    for p in range(len(PATTERNS)):
        base = jnp.broadcast_to(vec_ref[p][0:1, :], (qb_rows, vec_ref.shape[-1]))
        tab_ref[p] = pltpu.roll(base, 0, 1, stride=1, stride_axis=0)[:, :qb_rows + nk]

    sub = PATTERNS[1][1]
    assert all(dil == 1 or dil % sub == 0 for _, dil in PATTERNS)
    sub_len = seq // sub
    srcs = (q_ref, k_ref, v_ref)
    for a, ref in enumerate(srcs):
        for s in range(sub):
            ph_ref[a, s * sub_len:(s + 1) * sub_len, :] = ref[pl.ds(s, sub_len, stride=sub), :]

    def rows(a, phase, dil, start, size):
        if dil == 1:
            return srcs[a][pl.ds(start, size), :]
        step = dil // sub
        base = (phase % sub) * sub_len + phase // sub + step * start
        if step == 1:
            return ph_ref[a, pl.ds(base, size), :]
        return ph_ref[a, pl.ds(base, size, stride=step), :]

    def put(p, start, stride, acc, m, l):
        lanes = acc.shape[-1]
        if stride == 1:
            idx = pl.ds(start, qb_rows)
        else:
            idx = pl.ds(start, qb_rows, stride=stride)
        acc_ref[p, idx, :] = acc
        m_ref[p, idx, :] = jnp.broadcast_to(m, (qb_rows, lanes))
        l_ref[p, idx, :] = l

    def first_block(p, phase, dil):
        tab = tab_ref[p][:, nk:]
        qb = rows(0, phase, dil, 0, qb_rows)
        kw = rows(1, phase, dil, 0, qb_rows)
        vw = rows(2, phase, dil, 0, qb_rows)
        put(p, phase, dil, *_attn_block(qb, kw, vw, tab, scale))

    def later_block(p, phase, dil, n):
        tab = tab_ref[p]
        qb = rows(0, phase, dil, qb_rows * n, qb_rows)
        kw = rows(1, phase, dil, qb_rows * n - nk, qb_rows + nk)
        vw = rows(2, phase, dil, qb_rows * n - nk, qb_rows + nk)
        put(p, phase + dil * qb_rows * n, dil, *_attn_block(qb, kw, vw, tab, scale))

    for p, (window, dil) in enumerate(PATTERNS):
        n_blocks = seq // dil // qb_rows
        for phase in range(dil):
            first_block(p, phase, dil)
            for n in range(1, n_blocks):
                later_block(p, phase, dil, n)

    chunk = 256

    def combine(c, carry):
        sl = pl.ds(pl.multiple_of(c * chunk, chunk), chunk)
        m0, m1, m2 = m_ref[0, sl, :], m_ref[1, sl, :], m_ref[2, sl, :]
        mm = jnp.maximum(jnp.maximum(m0, m1), m2)
        e0, e1, e2 = jnp.exp(m0 - mm), jnp.exp(m1 - mm), jnp.exp(m2 - mm)
        num = e0 * acc_ref[0, sl, :] + e1 * acc_ref[1, sl, :] + e2 * acc_ref[2, sl, :]
        den = e0 * l_ref[0, sl, :] + e1 * l_ref[1, sl, :] + e2 * l_ref[2, sl, :]
        o_ref[sl, :] = (num / den * _silu(g_ref[sl, :])).astype(o_ref.dtype)
        return carry
    lax.fori_loop(0, seq // chunk, combine, 0)


def _attn_prompt_stream_kernel(q_ref, k_ref, v_ref, g_ref, tab_ref, *rest, n_heads, **stream_kw):
    stream_in, (o_ref, so_ref, out_any), scratch = rest[:8], rest[9:12], rest[12:]
    _attn_prompt_kernel(q_ref, k_ref, v_ref, g_ref, tab_ref, o_ref, *scratch[:5])
    k = pl.program_id(0) * n_heads + pl.program_id(1)
    _stream_step_with_attention(k, stream_in, so_ref, out_any, scratch[5:], **stream_kw)


def _attn_prompt(z3, tabs, n_heads, stream, seq0, n_seq, partial_cache):
    b, seq, _ = z3.shape
    hd = HEAD_DIM
    assert 0 < n_seq * stream.n_chunks <= b * n_heads

    def col(off):
        return pl.BlockSpec((None, seq, hd), lambda i, h: (i, 0, off + h))

    def seq_of(i, h):
        return (i * n_heads + h) // stream.n_chunks

    s_in, s_out, s_shapes, s_scratch, s_args, s_kw = _stream_operands(stream, seq0, n_seq, seq_of)
    in_specs = [col(0), col(n_heads), col(2 * n_heads), col(3 * n_heads),
                pl.BlockSpec((len(PATTERNS), None, SUBLANES, TABLE_LANES),
                             lambda i, h: (0, h, 0, 0))] + s_in + [pl.BlockSpec(memory_space=pl.ANY)]
    return pl.pallas_call(
        functools.partial(_attn_prompt_stream_kernel, n_heads=n_heads, **s_kw),
        grid=(b, n_heads),
        in_specs=in_specs,
        out_specs=[pl.BlockSpec((None, seq, hd), lambda i, h: (i, 0, h))] + s_out,
        out_shape=[jax.ShapeDtypeStruct((b, seq, n_heads * hd), BF16)] + s_shapes,
        scratch_shapes=([pltpu.VMEM((len(PATTERNS), seq, hd), F32)] * 4
                        + [pltpu.VMEM((len(PATTERNS), Q_BLOCK, Q_BLOCK + KEYS_PER_PATTERN), F32)]
                        + s_scratch),
        input_output_aliases={len(in_specs) - 1: 2},
        compiler_params=pltpu.CompilerParams(
            dimension_semantics=("arbitrary", "arbitrary"), vmem_limit_bytes=VMEM_LIMIT_CARRIER),
        name="attn_prompt",
    )(z3, z3, z3, z3, tabs, *s_args, partial_cache)


def _conv_kernel(ca_ref, cb_ref, gc_ref, pre_ref, dww_ref, dwb_ref, lng_ref, lnb_ref,
                 pww_ref, pwb_ref, o_ref, newc_ref, upad_ref, *, chunk):
    def one(s, carry):
        _conv_one_seq(ca_ref.at[s], cb_ref.at[s], gc_ref.at[s], pre_ref.at[s], dww_ref, dwb_ref,
                      lng_ref, lnb_ref, pww_ref, pwb_ref, o_ref.at[s], newc_ref.at[s], upad_ref,
                      chunk=chunk)
        return carry
    if ca_ref.shape[0] == 1:
        one(0, 0)
    else:
        lax.fori_loop(0, ca_ref.shape[0], one, 0)


def _conv_one_seq(ca_ref, cb_ref, gc_ref, pre_ref, dww_ref, dwb_ref, lng_ref, lnb_ref,
                  pww_ref, pwb_ref, o_ref, newc_ref, upad_ref, *, chunk):
    t_len = ca_ref.shape[0]
    rows_step = o_ref.shape[0]
    hist = CONV_WIDTH - 1
    rb = pl.program_id(1)

    @pl.when(rb == 0)
    def _():
        u = ca_ref[...] * jax.nn.sigmoid(cb_ref[...])
        upad_ref[0:hist, :] = pre_ref[...]
        upad_ref[hist:hist + t_len, :] = u
        n_pad = upad_ref.shape[0] - (hist + t_len)
        upad_ref[hist + t_len:, :] = jnp.zeros((n_pad, ca_ref.shape[1]), F32)
        newc_ref[...] = upad_ref[t_len:t_len + hist, :]

    win_rows = upad_ref.shape[0] - t_len + chunk

    def body(c, carry):
        l0 = c * chunk
        r0 = l0 if rows_step == t_len else rb * rows_step + l0
        if chunk % SUBLANES == 0:
            l0, r0 = pl.multiple_of(l0, SUBLANES), pl.multiple_of(r0, SUBLANES)
        win = upad_ref[pl.ds(r0, win_rows), :]
        y = jnp.zeros((chunk, ca_ref.shape[1]), F32) + dwb_ref[...]
        for s in range(SUBLANES):
            shifted = win if s == 0 else pltpu.roll(win, win_rows - s, 0)
            for a in range(-(-CONV_WIDTH // SUBLANES)):
                w = SUBLANES * a + s
                if w < CONV_WIDTH:
                    y = y + shifted[SUBLANES * a:SUBLANES * a + chunk] * dww_ref[w:w + 1, :]
        mu = jnp.mean(y, axis=-1, keepdims=True)
        var = jnp.mean(jnp.square(y - mu), axis=-1, keepdims=True)
        yn = (y - mu) * lax.rsqrt(var + EPS) * lng_ref[...] + lnb_ref[...]
        c_act = _silu(yn).astype(BF16)
        proj = jnp.dot(c_act, pww_ref[...], preferred_element_type=F32) + pwb_ref[...]
        o_ref[pl.ds(l0, chunk), :] = (proj * _silu(gc_ref[pl.ds(r0, chunk), :])).astype(o_ref.dtype)
        return carry
    if rows_step == chunk:
        body(0, 0)
    else:
        lax.fori_loop(0, rows_step // chunk, body, 0)


def _conv_stream_kernel(*refs, row_blocks, **stream_kw):
    chunk = stream_kw.pop("chunk")
    _conv_kernel(*refs[:10], refs[19], refs[20], refs[23], chunk=chunk)
    k = pl.program_id(0) * row_blocks + pl.program_id(1)
    _stream_step_with_attention(k, refs[10:18], refs[21], refs[22], refs[24:], **stream_kw)


def _conv_branch(z3, prefix, dw_w, dw_b, ln_g, ln_b, pw_w_bf, pw_b, col0, row_blocks=1,
                 stream=None, seq0=0, n_seq=0, partial_cache=None):
    n_all, t_len, _ = z3.shape
    c = prefix.shape[-1]
    hist = CONV_WIDTH - 1
    rows_step = t_len // row_blocks
    chunk = min(rows_step, 64)
    cblk = col0 // c
    group = n_all if (stream is None and t_len * c * 4 <= 64 * 1024) else 1
    n = n_all // group

    def zc(j):
        return pl.BlockSpec((group, t_len, c), lambda i, r: (i, 0, cblk + j))

    def vec():
        return pl.BlockSpec((1, c), lambda i, r: (0, 0))

    in_specs = [zc(0), zc(1), zc(2),
                pl.BlockSpec((group, hist, c), lambda i, r: (i, 0, 0)),
                pl.BlockSpec((CONV_WIDTH, c), lambda i, r: (0, 0)),
                vec(), vec(), vec(),
                pl.BlockSpec((c, c), lambda i, r: (0, 0)),
                vec()]
    out_specs = [pl.BlockSpec((group, rows_step, c), lambda i, r: (i, r, 0)),
                 pl.BlockSpec((group, hist, c), lambda i, r: (i, 0, 0))]
    out_shape = [jax.ShapeDtypeStruct((n_all, t_len, c), BF16),
                 jax.ShapeDtypeStruct((n_all, hist, c), F32)]
    scratch = [pltpu.VMEM((t_len - chunk + _round_up(chunk + CONV_WIDTH + 1, SUBLANES), c), F32)]
    args = (z3, z3, z3, prefix, dw_w, dw_b.reshape(1, c), ln_g.reshape(1, c), ln_b.reshape(1, c),
            pw_w_bf, pw_b.reshape(1, c))
    if stream is None:
        return pl.pallas_call(
            functools.partial(_conv_kernel, chunk=chunk),
            grid=(n, row_blocks), in_specs=in_specs, out_specs=out_specs, out_shape=out_shape,
            scratch_shapes=scratch,
            compiler_params=pltpu.CompilerParams(
                dimension_semantics=("parallel", "arbitrary"), vmem_limit_bytes=VMEM_LIMIT),
            name="conv_branch",
        )(*args)
    assert 0 < n_seq * stream.n_chunks <= n * row_blocks
    s_in, s_out, s_shapes, s_scratch, s_args, s_kw = _stream_operands(
        stream, seq0, n_seq, lambda i, r: (i * row_blocks + r) // stream.n_chunks)
    in_specs = in_specs + s_in + [pl.BlockSpec(memory_space=pl.ANY)]
    return pl.pallas_call(
        functools.partial(_conv_stream_kernel, row_blocks=row_blocks, chunk=chunk, **s_kw),
        grid=(n, row_blocks), in_specs=in_specs, out_specs=out_specs + s_out,
        out_shape=out_shape + s_shapes, scratch_shapes=scratch + s_scratch,
        input_output_aliases={len(in_specs) - 1: 3},
        compiler_params=pltpu.CompilerParams(
            dimension_semantics=("arbitrary", "arbitrary"), vmem_limit_bytes=VMEM_LIMIT_CARRIER),
        name="conv_branch",
    )(*args, *s_args, partial_cache)


def _outproj_kernel(ma_ref, mc_ref, wa_ref, wc_ref, x_ref, g_ref, *rest):
    y_ref = rest[-1] if len(rest) == 1 else rest[2]
    y = jnp.dot(ma_ref[...], wa_ref[...], preferred_element_type=F32)
    y = y + jnp.dot(mc_ref[...], wc_ref[...], preferred_element_type=F32)
    ms = jnp.mean(y * y, axis=-1, keepdims=True)
    y_ref[...] = x_ref[...] + y * lax.rsqrt(ms + EPS) * g_ref[...]
    if len(rest) > 1:
        _kv_rows_kernel(rest[0], rest[1], rest[3])


def _outproj(mix_att, mix_conv, w_bf, x2d, norm_g, tm, z=None):
    m, d = x2d.shape
    da, dc = mix_att.shape[1], mix_conv.shape[1]
    assert w_bf.shape[0] == da + dc and da % dc == 0
    once = pl.Buffered(1)
    in_specs = [pl.BlockSpec((tm, da), lambda i: (i, 0)),
                pl.BlockSpec((tm, dc), lambda i: (i, 0)),
                pl.BlockSpec((da, d), lambda i: (0, 0), pipeline_mode=once),
                pl.BlockSpec((dc, d), lambda i: (da // dc, 0), pipeline_mode=once),
                pl.BlockSpec((tm, d), lambda i: (i, 0)),
                pl.BlockSpec((1, d), lambda i: (0, 0))]
    out_specs = [pl.BlockSpec((tm, d), lambda i: (i, 0))]
    out_shape = [jax.ShapeDtypeStruct((m, d), F32)]
    args = (mix_att, mix_conv, w_bf, w_bf, x2d, norm_g.reshape(1, d))
    if z is not None:
        rows_kv = 2 * da // HEAD_DIM
        in_specs += [pl.BlockSpec((tm, da), lambda i: (i, 1)), pl.BlockSpec((tm, da), lambda i: (i, 2))]
        out_specs.append(pl.BlockSpec((tm * rows_kv, HEAD_DIM), lambda i: (i, 0)))
        out_shape.append(jax.ShapeDtypeStruct((m * rows_kv, HEAD_DIM), F32))
        args += (z, z)
    res = pl.pallas_call(
        _outproj_kernel,
        grid=(m // tm,),
        in_specs=in_specs, out_specs=out_specs, out_shape=out_shape,
        compiler_params=pltpu.CompilerParams(
            dimension_semantics=("parallel",), vmem_limit_bytes=VMEM_LIMIT_CARRIER),
        name="outproj",
    )(*args)
    return res if z is not None else res[0]


NEAR_ROWS = 512
FAR_STRIDE = 16


RING = 3
STREAM_DMA_PRIORITY = 1


def _sample_heads(q_ref, k_ref, v_ref, g_ref, near_ref, far_ref, tab_ref, mult_ref, o_ref, rows_kv):
    t_new = q_ref.shape[0]
    hd = near_ref.shape[1]
    scale = HEAD_DIM ** -0.5
    pad_q = jnp.zeros((SUBLANES - t_new, hd), F32)
    pad_kv = jnp.zeros((hd - t_new, hd), F32)
    nt = (((1,), (1,)), ((), ()))
    n_far = far_ref.shape[1]
    mult = mult_ref[...]
    n_heads = rows_kv // 2

    def head_rows(ref, h, n, parity):
        return ref[pl.ds(2 * h + parity, n, stride=rows_kv), :].astype(BF16)

    def pad_bf(ref, h, pad):
        return jnp.concatenate([ref[:, h * hd:(h + 1) * hd], pad], axis=0).astype(BF16)

    scores = []
    for h in range(n_heads):
        q8 = pad_bf(q_ref, h, pad_q)
        scores.append(jnp.concatenate(
            [lax.dot_general(q8, head_rows(near_ref, h, NEAR_ROWS, 0), nt, preferred_element_type=F32),
             lax.dot_general(q8, far_ref[2 * h], nt, preferred_element_type=F32),
             lax.dot_general(q8, pad_bf(k_ref, h, pad_kv), nt, preferred_element_type=F32)], axis=1))
    probs = []
    for h, s in enumerate(scores):
        s = s * scale + tab_ref[h]
        m = jnp.max(s, axis=-1, keepdims=True)
        p = jnp.exp(s - m) * mult
        probs.append((p.astype(BF16), jnp.sum(p, axis=-1, keepdims=True)))
    for h, (pb, l) in enumerate(probs):
        acc = jnp.dot(pb[:, :NEAR_ROWS], head_rows(near_ref, h, NEAR_ROWS, 1),
                      preferred_element_type=F32)
        acc = acc + jnp.dot(pb[:, NEAR_ROWS:NEAR_ROWS + n_far], far_ref[2 * h + 1],
                            preferred_element_type=F32)
        acc = acc + jnp.dot(pb[:, NEAR_ROWS + n_far:], pad_bf(v_ref, h, pad_kv),
                            preferred_element_type=F32)
        cols = slice(h * hd, (h + 1) * hd)
        o_ref[:, cols] = (acc / l)[:t_new] * _silu(g_ref[:, cols])


def _cache_stream_step(k, n_steps, seq0, n_chunks, cache_any, kvn_any, out_any, buf_ref, far_ref,
                       stash_ref, sem_in, sem_out, sem_tail, shift_sl, rows_kv, attend):
    chunk_sl = buf_ref.shape[1]
    far_per_chunk = NEAR_ROWS // FAR_STRIDE
    seq_sl = n_chunks * chunk_sl

    def chunk_in(j):
        return pltpu.make_async_copy(
            cache_any.at[seq0 + j // n_chunks, pl.ds((j % n_chunks) * chunk_sl, chunk_sl)],
            buf_ref.at[j % RING], sem_in.at[j % RING])

    def first_out(j):
        return pltpu.make_async_copy(
            buf_ref.at[j % RING, pl.ds(shift_sl, chunk_sl - shift_sl)],
            out_any.at[seq0 + j // n_chunks, pl.ds(0, chunk_sl - shift_sl)], sem_out.at[j % RING])

    def later_out(j):
        return pltpu.make_async_copy(
            buf_ref.at[j % RING],
            out_any.at[seq0 + j // n_chunks, pl.ds((j % n_chunks) * chunk_sl - shift_sl, chunk_sl)],
            sem_out.at[j % RING])

    def tail(j):
        return pltpu.make_async_copy(
            kvn_any.at[seq0 + j // n_chunks],
            out_any.at[seq0 + j // n_chunks, pl.ds(seq_sl - shift_sl, shift_sl)], sem_tail.at[0])

    def on_chunk(j, first, later):
        pl.when(j % n_chunks == 0)(first)
        pl.when(j % n_chunks != 0)(later)

    @pl.when(k == 0)
    def _():
        for j in range(RING - 1):
            chunk_in(jnp.int32(j)).start(priority=STREAM_DMA_PRIORITY)

    c = k % n_chunks
    slot = k % RING
    chunk_in(k).wait()
    on_chunk(k, lambda: first_out(k).start(priority=STREAM_DMA_PRIORITY),
             lambda: later_out(k).start(priority=STREAM_DMA_PRIORITY))
    near_ref = buf_ref.at[slot]

    for grp in range(far_per_chunk):
        src = grp * FAR_STRIDE * rows_kv
        stash_ref[grp * shift_sl:(grp + 1) * shift_sl, :] = near_ref[src:src + shift_sl, :]
    keys_chunk = stash_ref.shape[0] // rows_kv
    key0 = pl.multiple_of(c * keys_chunk, keys_chunk)
    for j in range(rows_kv):
        far_ref[j, pl.ds(key0, keys_chunk), :] = (
            stash_ref[pl.ds(j, keys_chunk, stride=rows_kv), :].astype(far_ref.dtype))

    @pl.when(c == n_chunks - 1)
    def _():
        tail(k).start()
        attend(near_ref)
        tail(k).wait()

    @pl.when(k >= 1)
    def _():
        on_chunk(k - 1, lambda: first_out(k - 1).wait(), lambda: later_out(k - 1).wait())

    @pl.when(k + RING - 1 < n_steps)
    def _():
        chunk_in(k + RING - 1).start(priority=STREAM_DMA_PRIORITY)

    @pl.when(k == n_steps - 1)
    def _():
        on_chunk(k, lambda: first_out(k).wait(), lambda: later_out(k).wait())


class _Stream(NamedTuple):
    zs3: jax.Array
    kvn_rows: jax.Array
    cache2: jax.Array
    tab: jax.Array
    mult: jax.Array
    d_attn: int
    rows_kv: int
    n_chunks: int


def _stream_operands(stream, seq0, n_seq, seq_of):
    t_new = stream.zs3.shape[1]
    hd = stream.cache2.shape[2]
    chunk_sl = stream.cache2.shape[1] // stream.n_chunks
    n_far = stream.cache2.shape[1] // stream.rows_kv // FAR_STRIDE * t_new

    def local(*g):
        return jnp.minimum(seq_of(*g), n_seq - 1)

    def zcol(j):
        return pl.BlockSpec((None, t_new, stream.d_attn), lambda *g: (seq0 + local(*g), 0, j))

    in_specs = [zcol(0), zcol(1), zcol(2), zcol(3),
                pl.BlockSpec(stream.tab.shape, lambda *g: (0, 0, 0)),
                pl.BlockSpec(stream.mult.shape, lambda *g: (0, 0)),
                pl.BlockSpec(memory_space=pl.ANY),
                pl.BlockSpec(memory_space=pl.ANY)]
    out_specs = [pl.BlockSpec((None, t_new, stream.d_attn), lambda *g: (local(*g), 0, 0)),
                 pl.BlockSpec(memory_space=pl.ANY)]
    out_shapes = [jax.ShapeDtypeStruct((n_seq, t_new, stream.d_attn), F32),
                  jax.ShapeDtypeStruct(stream.cache2.shape, stream.cache2.dtype)]
    scratch = [pltpu.VMEM((RING, chunk_sl, hd), F32),
               pltpu.VMEM((stream.rows_kv, n_far, hd), BF16),
               pltpu.VMEM((NEAR_ROWS // FAR_STRIDE * t_new * stream.rows_kv, hd), F32),
               pltpu.SemaphoreType.DMA((RING,)),
               pltpu.SemaphoreType.DMA((RING,)),
               pltpu.SemaphoreType.DMA((1,))]
    args = (stream.zs3,) * 4 + (stream.tab, stream.mult, stream.cache2, stream.kvn_rows)
    kw = dict(rows_kv=stream.rows_kv, seq0=seq0, n_chunks=stream.n_chunks,
              n_steps=n_seq * stream.n_chunks)
    return in_specs, out_specs, out_shapes, scratch, args, kw


def _stream_step_with_attention(k, stream_in, o_ref, out_any, scratch, *, rows_kv, seq0, n_chunks,
                                n_steps):
    q_ref, k_ref, v_ref, g_ref, tab_ref, mult_ref, cache_any, kvn_any = stream_in
    buf_ref, far_ref, stash_ref, sem_in, sem_out, sem_tail = scratch
    shift_sl = q_ref.shape[0] * rows_kv

    def attend(near_ref):
        _sample_heads(q_ref, k_ref, v_ref, g_ref, near_ref, far_ref, tab_ref, mult_ref, o_ref, rows_kv)

    @pl.when(k < n_steps)
    def _():
        _cache_stream_step(k, n_steps, seq0, n_chunks, cache_any, kvn_any, out_any, buf_ref, far_ref,
                           stash_ref, sem_in, sem_out, sem_tail, shift_sl, rows_kv, attend)


def _attn_sample_kernel(*refs, n_chunks, **stream_kw):
    k = pl.program_id(0) * n_chunks + pl.program_id(1)
    _stream_step_with_attention(k, refs[:8], refs[9], refs[10], refs[11:], n_chunks=n_chunks,
                                **stream_kw)


def _attn_sample(stream, seq0, n_seq, partial_cache):
    s_in, s_out, s_shapes, s_scratch, s_args, s_kw = _stream_operands(
        stream, seq0, n_seq, lambda i, c: i)
    return pl.pallas_call(
        functools.partial(_attn_sample_kernel, **s_kw),
        grid=(n_seq, stream.n_chunks),
        in_specs=s_in + [pl.BlockSpec(memory_space=pl.ANY)],
        out_specs=s_out,
        out_shape=s_shapes,
        scratch_shapes=s_scratch,
        input_output_aliases={len(s_in): 1},
        compiler_params=pltpu.CompilerParams(
            dimension_semantics=("arbitrary", "arbitrary"), vmem_limit_bytes=VMEM_LIMIT),
        name="attn_sample",
    )(*s_args, partial_cache)


def _prompt_bias_tables(rel_bias):
    nk = KEYS_PER_PATTERN
    qb = Q_BLOCK
    wrap = TABLE_LANES
    assert wrap >= 2 * qb + nk - 1
    m = np.arange(wrap)
    kdist = nk - np.where(m < qb + nk, m, m - wrap)
    valid = (kdist >= 0) & (kdist <= nk)
    tabs = []
    for _, dil in PATTERNS:
        bucket = _rel_bucket(jnp.asarray(np.clip(kdist, 0, nk) * dil, jnp.int32))
        vec = jnp.where(valid[:, None], rel_bias[bucket].astype(F32), NEG_INF).T
        tabs.append(jnp.broadcast_to(vec[:, None, :], (vec.shape[0], SUBLANES, wrap)))
    return jnp.stack(tabs)


def _pattern_count(dist, patterns):
    return sum(((dist % dil == 0) & (dist >= 0) & (dist <= window)).astype(np.int32)
               for window, dil in patterns)


def _sample_tables(rel_bias, t_new, past):
    def bias_at(dist):
        return rel_bias[_rel_bucket(jnp.asarray(dist, jnp.int32))].astype(F32)

    def masked(bias, count):
        return jnp.where(jnp.asarray(count > 0)[..., None], bias, NEG_INF)

    near_pats, far_pats = PATTERNS[:2], PATTERNS[2:]
    assert near_pats[-1][0] == NEAR_ROWS and far_pats[0][1] == FAR_STRIDE and t_new <= far_pats[0][1]
    desc = np.arange(NEAR_ROWS + t_new - 1, 0, -1)
    desc_cnt = _pattern_count(desc, near_pats)
    desc_tab = masked(bias_at(desc), desc_cnt)
    starts = [t_new - 1 - t for t in range(t_new)]
    near_tab = jnp.stack([desc_tab[s0:s0 + NEAR_ROWS] for s0 in starts])
    near_cnt = np.stack([desc_cnt[s0:s0 + NEAR_ROWS] for s0 in starts])
    groups = past // FAR_STRIDE
    far_dist = past - FAR_STRIDE * np.arange(groups)
    own = np.eye(t_new, dtype=bool)[:, None, :] & (_pattern_count(far_dist, far_pats) > 0)[None, :, None]
    far_tab = jnp.where(jnp.asarray(own)[..., None], bias_at(far_dist)[None, :, None, :], NEG_INF)
    far_tab = far_tab.reshape(t_new, groups * t_new, -1)
    far_cnt = np.ones((t_new, groups * t_new), np.int32)
    tj = np.arange(t_new)[:, None] - np.arange(HEAD_DIM)[None, :]
    new_cnt = np.where(np.arange(HEAD_DIM)[None, :] < t_new, _pattern_count(tj, PATTERNS), 0)
    new_tab = masked(bias_at(np.clip(tj, 0, None).reshape(-1)).reshape(t_new, HEAD_DIM, -1), new_cnt)
    tab = jnp.concatenate([near_tab, far_tab, new_tab], axis=1).transpose(2, 0, 1)
    cnt = np.concatenate([near_cnt, far_cnt, new_cnt], axis=1)
    pad = SUBLANES - t_new
    tab = jnp.pad(tab, ((0, 0), (0, pad), (0, 0)))
    mult = np.pad(np.maximum(cnt, 1), ((0, pad), (0, 0)), constant_values=1).astype(np.float32)
    return tab, jnp.asarray(mult)


def kernel(x_prompt, x_sample, cache_conv, cache_kv, rel_bias, norm_pre, w_in, conv_dw_w, conv_dw_b,
           conv_ln_g, conv_ln_b, conv_pw_w, conv_pw_b, w_out, norm_post):
    depth = w_in.shape[0]
    assert depth == 1
    bsz, seq, d_model = x_prompt.shape
    n_dec, t_new, _ = x_sample.shape
    n_heads = cache_kv.shape[4]
    d_attn = n_heads * HEAD_DIM
    d_conv = cache_conv.shape[-1]
    past = cache_kv.shape[2]
    assert past == MAX_WINDOW and seq >= MAX_WINDOW and t_new <= 4
    hist = CONV_WIDTH - 1

    w_out_bf = w_out[0].astype(BF16)
    pw_bf = conv_pw_w[0].astype(BF16)
    conv_args = (conv_dw_w[0], conv_dw_b[0], conv_ln_g[0], conv_ln_b[0], pw_bf, conv_pw_b[0])
    conv_col0 = 4 * d_attn

    xp2 = x_prompt.reshape(bsz * seq, d_model)
    xs2 = x_sample.reshape(n_dec * t_new, d_model)
    zs, w_in_bf = _inproj_cast(_prenorm(xs2, norm_pre[0], tm=n_dec * t_new), w_in[0], tn=d_attn)
    zs3 = zs.reshape(n_dec, t_new, -1)

    def heads(col0):
        return zs3[:, :, col0:col0 + d_attn].reshape(n_dec, t_new, n_heads, HEAD_DIM)

    rows_kv = 2 * n_heads
    kvn_rows = jnp.stack([heads(d_attn), heads(2 * d_attn)], axis=3).reshape(
        n_dec, t_new * rows_kv, HEAD_DIM)
    cache2 = cache_kv[0].transpose(0, 1, 3, 2, 4).reshape(n_dec, past * rows_kv, HEAD_DIM)
    tab_s, mult_s = _sample_tables(rel_bias, t_new, past)
    assert past % NEAR_ROWS == 0 and t_new <= SUBLANES
    stream = _Stream(zs3, kvn_rows, cache2, tab_s, mult_s, d_attn, rows_kv, past // NEAR_ROWS)

    conv_row_blocks = 8
    n_in_conv = min(bsz * conv_row_blocks // stream.n_chunks, n_dec - 1)
    n_in_attn = min(bsz * n_heads // stream.n_chunks, n_dec - n_in_conv)
    n_in_proj = n_dec - n_in_attn - n_in_conv
    tm_p, tn_p = 2048, 512
    hp = _prenorm(xp2, norm_pre[0], tm=512)
    if 0 < n_in_proj * stream.n_chunks <= (bsz * seq // tm_p) * (w_in_bf.shape[1] // tn_p):
        zp, att_s0, part_cache = _inproj(hp, w_in_bf, tm_p, tn_p, stream, n_in_proj)
    else:
        zp = _inproj(hp, w_in_bf, tm_p, tn_p)
        n_in_proj = 0
        att_s0 = jnp.zeros((0, t_new, d_attn), F32)
        part_cache = jnp.zeros(cache2.shape, cache2.dtype)
    zp3 = zp.reshape(bsz, seq, -1)
    mix_att_p, att_s1, part_cache = _attn_prompt(zp3, _prompt_bias_tables(rel_bias), n_heads, stream,
                                                 n_in_proj, n_in_attn, part_cache)
    att_s0 = jnp.concatenate([att_s0, att_s1], axis=0)
    zero_prefix = jnp.zeros((bsz, hist, d_conv), F32)
    if n_in_conv > 0:
        mix_conv_p, new_conv_p, att_s2, part_cache = _conv_branch(
            zp3, zero_prefix, *conv_args, col0=conv_col0, row_blocks=conv_row_blocks,
            stream=stream, seq0=att_s0.shape[0], n_seq=n_in_conv, partial_cache=part_cache)
        att_s0 = jnp.concatenate([att_s0, att_s2], axis=0)
    else:
        mix_conv_p, new_conv_p = _conv_branch(zp3, zero_prefix, *conv_args, col0=conv_col0,
                                              row_blocks=conv_row_blocks)
    yp, kv_rows_p = _outproj(mix_att_p.reshape(bsz * seq, d_attn),
                             mix_conv_p.reshape(bsz * seq, d_conv),
                             w_out_bf, xp2, norm_post[0], tm=512, z=zp)
    win = min(MAX_WINDOW, seq)
    kv_rows_p = kv_rows_p.reshape(bsz, seq, n_heads, 2, HEAD_DIM)
    new_kv_p = kv_rows_p[:, seq - win:].transpose(0, 1, 3, 2, 4)[None]

    n_hosted = att_s0.shape[0]
    att_s, new_rows = att_s0, part_cache
    if n_hosted < n_dec:
        att_s1, new_rows = _attn_sample(stream, n_hosted, n_dec - n_hosted, part_cache)
        att_s = jnp.concatenate([att_s0, att_s1], axis=0)
    new_kv_s = new_rows.reshape(n_dec, past, n_heads, 2, HEAD_DIM).transpose(0, 1, 3, 2, 4)[None]
    mix_att_s = att_s.reshape(n_dec * t_new, d_attn).astype(BF16)
    mix_conv_s, new_conv_s = _conv_branch(zs3, cache_conv[0], *conv_args, col0=conv_col0)
    ys = _outproj(mix_att_s, mix_conv_s.reshape(n_dec * t_new, d_conv),
                  w_out_bf, xs2, norm_post[0], tm=n_dec * t_new)

    return (yp.reshape(bsz, seq, d_model), ys.reshape(n_dec, t_new, d_model),
            new_conv_p[None], new_kv_p, new_conv_s[None], new_kv_s)
```
